```python
import jax, jax.numpy as jnp
from jax import lax
import numpy as np

D_MODEL = 1024
BATCH = 4
SEQ = 4096
DEPTH = 1
DEC_BATCH = 32
DEC_SEQ = 1
PAST_LEN = 8192
PAGE_SIZE = 128

NSA_H = 8
NSA_G = 2
NSA_HPG = NSA_H // NSA_G
HD = 64
ROT_DIM = HD // 4
ROPE_THETA = 500000.0
CMP_BLOCK = 32
CMP_STRIDE = 16
CMP_HID = 128
SLC_BLOCK = 64
N_SEL = 16
SLC_QBLOCK = 64
WINDOW = 512
WIN_QBLOCK = 128
ML_H = 4
ML_D = 128
ML_W = ML_H * ML_D
CONV_W = 4
MLSTM_CHUNK = 64
MEM_T = 256
MEM_H = 4
MEM_HD = 128
MEM_W = MEM_H * MEM_HD
N_BRANCH = 3
BRANCH_W = 512
FFN_HID = -(-8 * D_MODEL // (3 * 256)) * 256
EPS = 1e-6
IN_WIDTHS = (NSA_H * HD, 3 * NSA_H, 2 * NSA_G * HD, 2 * NSA_G * HD, 2 * NSA_G * HD,
             ML_W, ML_W, ML_H, ML_H, MEM_W, N_BRANCH * D_MODEL)
IN_W = sum(IN_WIDTHS)
F32 = jnp.float32

kernel_name = 'hybrid_nsa_mlstm_memory_decoder_step'


def rmsnorm(x, g):
    xf = x.astype(F32)
    y = xf * lax.rsqrt(jnp.mean(xf * xf, axis=-1, keepdims=True) + EPS)
    return (y * g.astype(F32)).astype(x.dtype)


def rope(x, pos):
    half = ROT_DIM // 2
    freqs = ROPE_THETA ** (-jnp.arange(half, dtype=F32) / half)
    ang = pos.astype(F32)[:, None] * freqs
    ang = ang.reshape(ang.shape[:1] + (1,) * (x.ndim - 3) + (half,))
    cos, sin = jnp.cos(ang), jnp.sin(ang)
    xf = x.astype(F32)
    x1, x2, rest = xf[..., :half], xf[..., half:ROT_DIM], xf[..., ROT_DIM:]
    return jnp.concatenate([x1 * cos - x2 * sin, x2 * cos + x1 * sin, rest], -1).astype(x.dtype)


def masked_softmax(s, mask):
    s = jnp.where(mask, s.astype(F32), -jnp.inf)
    m = jnp.max(s, axis=-1, keepdims=True)
    m = jnp.where(jnp.isfinite(m), m, 0.0)
    p = jnp.exp(s - m)
    return p / jnp.maximum(p.sum(-1, keepdims=True), jnp.finfo(F32).tiny)


def dense_attend(q, k, v, mask):
    s = jnp.einsum('bsghd,btgd->bghst', q, k) * (q.shape[-1] ** -0.5)
    p = masked_softmax(s, mask)
    o = jnp.einsum('bghst,btgd->bsghd', p, v.astype(F32))
    return o.astype(q.dtype), p


def split_proj(x, norm_mix, w_in):
    offs = [int(o) for o in np.cumsum(IN_WIDTHS)[:-1]]
    return jnp.split(rmsnorm(x, norm_mix) @ w_in, offs, axis=-1)


def nsa_prep(q_raw, g_raw, kvc_raw, kvs_raw, kvw_raw, pos, q_norm, k_norm_slc, k_norm_win):
    B, S = q_raw.shape[:2]
    q = rmsnorm(q_raw.reshape(B, S, NSA_G, NSA_HPG, HD), q_norm)
    q_rot = rope(q, pos)
    gates = jax.nn.sigmoid(g_raw.reshape(B, S, 3, NSA_G, NSA_HPG).astype(F32))
    kv_cmp = kvc_raw.reshape(B, S, NSA_G, 2, HD)

    def norm_rot(raw, gain):
        kv = raw.reshape(B, S, NSA_G, 2, HD)
        k = rope(rmsnorm(kv[..., 0, :], gain), pos)
        return jnp.stack([k, kv[..., 1, :]], axis=3)

    return q, q_rot, gates, kv_cmp, norm_rot(kvs_raw, k_norm_slc), norm_rot(kvw_raw, k_norm_win)


def compress_kv(rows, cmp_pe, cmp_w1, cmp_b1, cmp_w2, k_norm_cmp):
    B, T = rows.shape[:2]
    n_ch = -(-T // CMP_STRIDE)
    rows = jnp.pad(rows, ((0, 0), (0, n_ch * CMP_STRIDE - T), (0, 0), (0, 0), (0, 0)))
    ch = rows.reshape(B, n_ch, CMP_STRIDE, NSA_G, 2, HD)
    blk = jnp.concatenate([ch[:, :-1], ch[:, 1:]], axis=2) + cmp_pe[:, None]
    n_cmp = n_ch - 1
    flat = blk.transpose(0, 1, 3, 4, 2, 5).reshape(B, n_cmp, NSA_G, 2, CMP_BLOCK * HD)
    h = jax.nn.gelu(jnp.einsum('bngcf,cfh->bngch', flat, cmp_w1) + cmp_b1)
    out = jnp.einsum('bngch,chd->bngcd', h, cmp_w2)
    end_pos = jnp.arange(n_cmp) * CMP_STRIDE + CMP_BLOCK - 1
    return rmsnorm(out[..., 0, :], k_norm_cmp), out[..., 1, :], end_pos


def select_blocks(p_cmp, pos, n_slc):
    n_cmp = p_cmp.shape[-1]
    ci = jnp.arange(n_cmp) * CMP_STRIDE
    sj = jnp.arange(n_slc) * SLC_BLOCK
    overlap = ((ci[:, None] < sj[None, :] + SLC_BLOCK) & (ci[:, None] + CMP_BLOCK > sj[None, :])).astype(F32)
    imp = jnp.einsum('bghsn,nj->bsgj', p_cmp, overlap)
    blk = jnp.arange(n_slc)
    causal = sj[None, :] <= pos[:, None]
    forced = (blk[None, :] == (pos // SLC_BLOCK)[:, None]) | (blk[None, :] == 0)
    score = jnp.where(forced[None, :, None, :], jnp.inf,
                      jnp.where(causal[None, :, None, :], imp, -jnp.inf))
    vals, idx = lax.top_k(score, min(N_SEL, n_slc))
    return idx.astype(jnp.int32), vals > -jnp.inf


def slc_attend(q, sel, idx, valid, pos):
    B, Q, G, K = idx.shape
    k = sel[..., 0, :].reshape(B, Q, G, K * SLC_BLOCK, HD)
    v = sel[..., 1, :].reshape(B, Q, G, K * SLC_BLOCK, HD)
    kpos = (idx[..., None] * SLC_BLOCK + jnp.arange(SLC_BLOCK)).reshape(B, Q, G, K * SLC_BLOCK)
    mask = jnp.repeat(valid, SLC_BLOCK, axis=-1) & (kpos <= pos[None, :, None, None])
    s = jnp.einsum('bqghd,bqgkd->bqghk', q, k) * (HD ** -0.5)
    p = masked_softmax(s, mask[:, :, :, None, :])
    return jnp.einsum('bqghk,bqgkd->bqghd', p, v.astype(F32)).astype(q.dtype)


def gather_blocks(blocks, idx):
    B, _, G, _ = idx.shape
    b_ix = jnp.arange(B)[:, None, None, None]
    g_ix = jnp.arange(G)[None, None, :, None]
    return blocks[b_ix, idx, :, g_ix]


def gather_blocks_paged(pool, page_table, kv_new, idx):
    DB, DS = kv_new.shape[:2]
    G = NSA_G
    n_past = page_table.shape[1] * PAGE_SIZE // SLC_BLOCK
    n_new = -(-DS // SLC_BLOCK)
    new_blocks = jnp.pad(kv_new, ((0, 0), (0, n_new * SLC_BLOCK - DS), (0, 0), (0, 0), (0, 0)))
    new_blocks = new_blocks.reshape(DB, n_new, SLC_BLOCK, G, 2, HD)
    b_ix = jnp.arange(DB)[:, None, None, None]
    g_ix = jnp.arange(G)[None, None, :, None]
    start = jnp.minimum(idx, n_past - 1) * SLC_BLOCK
    phys = page_table[b_ix, start // PAGE_SIZE]
    rows = (start % PAGE_SIZE)[..., None] + jnp.arange(SLC_BLOCK)
    past = pool[phys[..., None], rows, g_ix[..., None]].astype(kv_new.dtype)
    new = new_blocks[b_ix, jnp.clip(idx - n_past, 0, n_new - 1), :, g_ix]
    return jnp.where((idx < n_past)[..., None, None, None], past, new)


def slc_prompt(q_rot, kv_slc, idx, valid, pos):
    B, S = q_rot.shape[:2]
    nb = S // SLC_QBLOCK
    blocks = kv_slc.reshape(B, S // SLC_BLOCK, SLC_BLOCK, NSA_G, 2, HD)

    def to_blk(a):
        return jnp.moveaxis(a.reshape((B, nb, SLC_QBLOCK) + a.shape[2:]), 1, 0)

    def step(args):
        qb, ib, vb, pb = args
        return slc_attend(qb, gather_blocks(blocks, ib), ib, vb, pb)

    out = lax.map(step, (to_blk(q_rot), to_blk(idx), to_blk(valid), pos.reshape(nb, SLC_QBLOCK)))
    return jnp.moveaxis(out, 0, 1).reshape(q_rot.shape)


def win_prompt(q_rot, kv_win):
    B, S = q_rot.shape[:2]
    nb = S // WIN_QBLOCK
    n_band = WINDOW // WIN_QBLOCK + 1
    qb = q_rot.reshape(B, nb, WIN_QBLOCK, NSA_G, NSA_HPG, HD)
    kvb = kv_win.reshape(B, nb, WIN_QBLOCK, NSA_G, 2, HD)
    kvp = jnp.pad(kvb, ((0, 0), (n_band - 1, 0), (0, 0), (0, 0), (0, 0), (0, 0)))
    band = jnp.concatenate([kvp[:, n:n + nb] for n in range(n_band)], axis=2)
    qpos = jnp.arange(S).reshape(nb, WIN_QBLOCK)
    kpos = (jnp.arange(nb)[:, None] - (n_band - 1)) * WIN_QBLOCK + jnp.arange(n_band * WIN_QBLOCK)[None, :]
    d = qpos[:, :, None] - kpos[:, None, :]
    mask = (d >= 0) & (d <= WINDOW) & (kpos[:, None, :] >= 0)
    s = jnp.einsum('bnqghd,bnkgd->bnghqk', qb, band[..., 0, :]) * (HD ** -0.5)
    p = masked_softmax(s, mask[None, :, None, None])
    o = jnp.einsum('bnghqk,bnkgd->bnqghd', p, band[..., 1, :].astype(F32))
    return o.reshape(q_rot.shape).astype(q_rot.dtype)


def win_state_prompt(kv_win, w_buf):
    S = kv_win.shape[1]
    if S >= w_buf:
        return kv_win[:, S - w_buf:]
    return jnp.pad(kv_win, ((0, 0), (w_buf - S, 0), (0, 0), (0, 0), (0, 0)))


def win_sample(q_rot, kv_new, win_buf, pos):
    w_buf = win_buf.shape[1]
    kv = jnp.concatenate([win_buf.astype(kv_new.dtype), kv_new], axis=1)
    kpos = PAST_LEN - w_buf + jnp.arange(kv.shape[1])
    d = pos[:, None] - kpos[None, :]
    o, _ = dense_attend(q_rot, kv[..., 0, :], kv[..., 1, :], (d >= 0) & (d <= WINDOW))
    return o, kv[:, -w_buf:]


def nsa_merge(gates, o_cmp, o_slc, o_win):
    B, S = o_cmp.shape[:2]
    o = (gates[..., None] * jnp.stack([o_cmp, o_slc, o_win], axis=2).astype(F32)).sum(2)
    return o.reshape(B, S, NSA_H * HD).astype(o_cmp.dtype)


def mlstm_chunkwise(q, k, v, ig, lf, C0, n0, m0):
    B, S, H, D = q.shape
    L = MLSTM_CHUNK if S % MLSTM_CHUNK == 0 else S
    nc = S // L

    def to_chunks(a):
        return jnp.moveaxis(a.reshape((B, nc, L) + a.shape[2:]), 1, 0)

    tri = jnp.tril(jnp.ones((L, L), bool))

    def step(carry, xs):
        C, n, m = carry
        qc, kc, vc, ic, fc = xs
        b = jnp.cumsum(fc, axis=1)
        dmat = b[:, :, None, :] - b[:, None, :, :] + ic[:, None, :, :]
        dmat = jnp.where(tri[None, :, :, None], dmat, -jnp.inf)
        m_new = jnp.maximum(b + m[:, None, :], dmat.max(axis=2))
        w_in = jnp.exp(dmat - m_new[:, :, None, :]) * jnp.einsum('bthd,bshd->btsh', qc, kc)
        inter = jnp.exp(b + m[:, None, :] - m_new)
        num = inter[..., None] * jnp.einsum('bthd,bhde->bthe', qc, C) + jnp.einsum('btsh,bshe->bthe', w_in, vc)
        den = inter * jnp.einsum('bthd,bhd->bth', qc, n) + w_in.sum(2)
        h = num / jnp.maximum(jnp.abs(den), jnp.exp(-m_new))[..., None]
        m_end = m_new[:, -1]
        decay = jnp.exp(b[:, -1] + m - m_end)
        w_end = jnp.exp(b[:, -1:, :] - b + ic - m_end[:, None, :])
        C = decay[..., None, None] * C + jnp.einsum('bsh,bshd,bshe->bhde', w_end, kc, vc)
        n = decay[..., None] * n + jnp.einsum('bsh,bshd->bhd', w_end, kc)
        return (C, n, m_end), h

    (C, n, m), hs = lax.scan(step, (C0, n0, m0), (to_chunks(q), to_chunks(k), to_chunks(v), to_chunks(ig), to_chunks(lf)))
    return jnp.moveaxis(hs, 0, 1).reshape(B, S, H, D), C, n, m


def mlstm_branch(u, o_raw, i_raw, f_raw, conv_buf, C0, n0, m0,
                 conv_w, conv_b, w_ml_q, w_ml_k, w_ml_v, b_igate, b_fgate, ml_norm):
    B, S, _ = u.shape
    ext = jnp.concatenate([conv_buf.astype(u.dtype), u], axis=1)
    conv = conv_b + sum(ext[:, w:w + S] * conv_w[w] for w in range(CONV_W))
    c = jax.nn.silu(conv).reshape(B, S, ML_H, ML_D).astype(F32)
    uh = u.reshape(B, S, ML_H, ML_D).astype(F32)
    q = jnp.einsum('bshd,hde->bshe', c, w_ml_q.astype(F32)) * (ML_D ** -0.5)
    k = jnp.einsum('bshd,hde->bshe', c, w_ml_k.astype(F32))
    v = jnp.einsum('bshd,hde->bshe', uh, w_ml_v.astype(F32))
    ig = i_raw.astype(F32) + b_igate.astype(F32)
    lf = jax.nn.log_sigmoid(f_raw.astype(F32) + b_fgate.astype(F32))
    h, C, n, m = mlstm_chunkwise(q, k, v, ig, lf, C0.astype(F32), n0.astype(F32), m0.astype(F32))
    h = rmsnorm(h, ml_norm).reshape(B, S, ML_W)
    out = (jax.nn.sigmoid(o_raw.astype(F32)) * h).astype(u.dtype)
    return out, C, n, m, ext[:, -(CONV_W - 1):]


def mem_kv(mem, norm_mem, w_mem_kv, mem_k_norm):
    B, T, _ = mem.shape
    kv = (rmsnorm(mem, norm_mem) @ w_mem_kv).reshape(B, T, MEM_H, 2, MEM_HD)
    return jnp.stack([rmsnorm(kv[..., 0, :], mem_k_norm), kv[..., 1, :]], axis=3)


def mem_attend(q_raw, kv_mem, mem_q_norm):
    B, S = q_raw.shape[:2]
    q = rmsnorm(q_raw.reshape(B, S, MEM_H, 1, MEM_HD), mem_q_norm)
    kv_mem = kv_mem.astype(q.dtype)
    o, _ = dense_attend(q, kv_mem[..., 0, :], kv_mem[..., 1, :], jnp.ones((S, kv_mem.shape[1]), bool))
    return o.reshape(B, S, MEM_W)


def merge_ffn(x, gate_raw, o_nsa, o_ml, o_mem, w_branch, w_out, norm_ffn, w_ffn_in, w_ffn_out):
    B, S, _ = x.shape
    yb = jnp.einsum('bsnc,ncd->bsnd', jnp.stack([o_nsa, o_ml, o_mem], axis=2), w_branch)
    g = jax.nn.sigmoid(gate_raw.reshape(B, S, N_BRANCH, D_MODEL))
    x = x + (g * yb).sum(2) @ w_out
    a, b = jnp.split(rmsnorm(x, norm_ffn) @ w_ffn_in, 2, axis=-1)
    return x + (jax.nn.silu(a) * b) @ w_ffn_out


def setup_inputs(seed: int = 0) -> dict:
    key = jax.random.key(seed)
    ks = iter(jax.random.split(key, 48))

    def nrm(shape, scale):
        return jax.random.normal(next(ks), shape, F32) * scale

    def gain(shape):
        return 1.0 + nrm(shape, 0.02)

    n_pages = PAST_LEN // PAGE_SIZE
    n_phys = (DEC_BATCH * n_pages * 5) // 4
    w_buf = min(WINDOW, PAST_LEN)
    page_table = jax.random.permutation(next(ks), n_phys)[:DEC_BATCH * n_pages]
    page_table = page_table.reshape(DEC_BATCH, n_pages).astype(jnp.int32)
    return {
        'x_prompt': nrm((BATCH, SEQ, D_MODEL), 1.0),
        'x_sample': nrm((DEC_BATCH, DEC_SEQ, D_MODEL), 1.0),
        'cache_kv_cmp': nrm((n_phys, PAGE_SIZE, NSA_G, 2, HD), 1.0),
        'cache_kv_slc': nrm((n_phys, PAGE_SIZE, NSA_G, 2, HD), 1.0),
        'cache_kv_win': nrm((DEC_BATCH, w_buf, NSA_G, 2, HD), 1.0),
        'cache_kv_mem': nrm((DEC_BATCH, MEM_T, MEM_H, 2, MEM_HD), 1.0),
        'state_C': nrm((DEC_BATCH, ML_H, ML_D, ML_D), 0.5),
        'state_n': nrm((DEC_BATCH, ML_H, ML_D), 0.5),
        'state_m': nrm((DEC_BATCH, ML_H), 1.0),
        'state_conv': nrm((DEC_BATCH, CONV_W - 1, ML_W), 1.0),
        'page_table': page_table,
        'mem_prompt': nrm((BATCH, MEM_T, D_MODEL), 1.0),
        'norm_mix': gain((D_MODEL,)),
        'w_in': nrm((D_MODEL, IN_W), D_MODEL ** -0.5),
        'q_norm': gain((HD,)),
        'k_norm_cmp': gain((HD,)),
        'k_norm_slc': gain((HD,)),
        'k_norm_win': gain((HD,)),
        'cmp_pe': nrm((CMP_BLOCK, 2, HD), 0.5),
        'cmp_w1': nrm((2, CMP_BLOCK * HD, CMP_HID), (CMP_BLOCK * HD) ** -0.5),
        'cmp_b1': nrm((2, CMP_HID), 0.02),
        'cmp_w2': nrm((2, CMP_HID, HD), CMP_HID ** -0.5),
        'conv_w': nrm((CONV_W, ML_W), CONV_W ** -0.5),
        'conv_b': nrm((ML_W,), 0.02),
        'w_ml_q': nrm((ML_H, ML_D, ML_D), ML_D ** -0.5),
        'w_ml_k': nrm((ML_H, ML_D, ML_D), ML_D ** -0.5),
        'w_ml_v': nrm((ML_H, ML_D, ML_D), ML_D ** -0.5),
        'b_igate': nrm((ML_H,), 0.1),
        'b_fgate': 3.0 + nrm((ML_H,), 0.1),
        'ml_norm': gain((ML_H, ML_D)),
        'norm_mem': gain((D_MODEL,)),
        'w_mem_kv': nrm((D_MODEL, 2 * MEM_W), D_MODEL ** -0.5),
        'mem_q_norm': gain((MEM_HD,)),
        'mem_k_norm': gain((MEM_HD,)),
        'w_branch': nrm((N_BRANCH, BRANCH_W, D_MODEL), BRANCH_W ** -0.5),
        'w_out': nrm((D_MODEL, D_MODEL), D_MODEL ** -0.5),
        'norm_ffn': gain((D_MODEL,)),
        'w_ffn_in': nrm((D_MODEL, 2 * FFN_HID), D_MODEL ** -0.5),
        'w_ffn_out': nrm((FFN_HID, D_MODEL), FFN_HID ** -0.5),
    }


def reference(x_prompt, x_sample, cache_kv_cmp, cache_kv_slc, cache_kv_win, cache_kv_mem,
              state_C, state_n, state_m, state_conv, page_table, mem_prompt,
              norm_mix, w_in, q_norm, k_norm_cmp, k_norm_slc, k_norm_win,
              cmp_pe, cmp_w1, cmp_b1, cmp_w2, conv_w, conv_b, w_ml_q, w_ml_k, w_ml_v,
              b_igate, b_fgate, ml_norm, norm_mem, w_mem_kv, mem_q_norm, mem_k_norm,
              w_branch, w_out, norm_ffn, w_ffn_in, w_ffn_out):
    w_buf = cache_kv_win.shape[1]

    B, S, _ = x_prompt.shape
    pos_p = jnp.arange(S, dtype=jnp.int32)
    (q_raw, g_raw, kvc_raw, kvs_raw, kvw_raw, u_raw, o_raw, i_raw, f_raw,
     mq_raw, mg_raw) = split_proj(x_prompt, norm_mix, w_in)
    q_p, q_rot_p, gates_p, kv_cmp_p, kv_slc_p, kv_win_p = nsa_prep(
        q_raw, g_raw, kvc_raw, kvs_raw, kvw_raw, pos_p, q_norm, k_norm_slc, k_norm_win)
    k_c, v_c, end_c = compress_kv(kv_cmp_p, cmp_pe, cmp_w1, cmp_b1, cmp_w2, k_norm_cmp)
    o_cmp_p, p_cmp_p = dense_attend(q_p, k_c, v_c, end_c[None, :] <= pos_p[:, None])
    idx_p, valid_p = select_blocks(p_cmp_p, pos_p, -(-S // SLC_BLOCK))
    o_slc_p = slc_prompt(q_rot_p, kv_slc_p, idx_p, valid_p, pos_p)
    o_win_p = win_prompt(q_rot_p, kv_win_p)
    o_nsa_p = nsa_merge(gates_p, o_cmp_p, o_slc_p, o_win_p)
    o_ml_p, C_p, n_p, m_p, conv_p = mlstm_branch(
        u_raw, o_raw, i_raw, f_raw,
        jnp.zeros((B, CONV_W - 1, ML_W), x_prompt.dtype), jnp.zeros((B, ML_H, ML_D, ML_D), F32),
        jnp.zeros((B, ML_H, ML_D), F32), jnp.zeros((B, ML_H), F32),
        conv_w, conv_b, w_ml_q, w_ml_k, w_ml_v, b_igate, b_fgate, ml_norm)
    kv_mem_p = mem_kv(mem_prompt, norm_mem, w_mem_kv, mem_k_norm)
    o_mem_p = mem_attend(mq_raw, kv_mem_p, mem_q_norm)
    y_prompt = merge_ffn(x_prompt, mg_raw, o_nsa_p, o_ml_p, o_mem_p, w_branch, w_out, norm_ffn, w_ffn_in, w_ffn_out)
    win_p = win_state_prompt(kv_win_p, w_buf)

    DB, DS, _ = x_sample.shape
    pos_s = PAST_LEN + jnp.arange(DS, dtype=jnp.int32)
    (q_raw, g_raw, kvc_raw, kvs_raw, kvw_raw, u_raw, o_raw, i_raw, f_raw,
     mq_raw, mg_raw) = split_proj(x_sample, norm_mix, w_in)
    q_s, q_rot_s, gates_s, kv_cmp_s, kv_slc_s, kv_win_s = nsa_prep(
        q_raw, g_raw, kvc_raw, kvs_raw, kvw_raw, pos_s, q_norm, k_norm_slc, k_norm_win)
    past_cmp = cache_kv_cmp[page_table].reshape((DB, -1) + cache_kv_cmp.shape[2:]).astype(kv_cmp_s.dtype)
    rows_s = jnp.concatenate([past_cmp, kv_cmp_s], axis=1)
    k_c, v_c, end_c = compress_kv(rows_s, cmp_pe, cmp_w1, cmp_b1, cmp_w2, k_norm_cmp)
    o_cmp_s, p_cmp_s = dense_attend(q_s, k_c, v_c, end_c[None, :] <= pos_s[:, None])
    idx_s, valid_s = select_blocks(p_cmp_s, pos_s, -(-rows_s.shape[1] // SLC_BLOCK))
    sel_s = gather_blocks_paged(cache_kv_slc, page_table, kv_slc_s, idx_s)
    o_slc_s = slc_attend(q_rot_s, sel_s, idx_s, valid_s, pos_s)
    o_win_s, win_s = win_sample(q_rot_s, kv_win_s, cache_kv_win, pos_s)
    o_nsa_s = nsa_merge(gates_s, o_cmp_s, o_slc_s, o_win_s)
    o_ml_s, C_s, n_s, m_s, conv_s = mlstm_branch(
        u_raw, o_raw, i_raw, f_raw, state_conv, state_C, state_n, state_m,
        conv_w, conv_b, w_ml_q, w_ml_k, w_ml_v, b_igate, b_fgate, ml_norm)
    o_mem_s = mem_attend(mq_raw, cache_kv_mem, mem_q_norm)
    y_sample = merge_ffn(x_sample, mg_raw, o_nsa_s, o_ml_s, o_mem_s, w_branch, w_out, norm_ffn, w_ffn_in, w_ffn_out)

    return (y_prompt, y_sample, kv_cmp_p, kv_cmp_s, kv_slc_p, kv_slc_s, win_p, win_s, kv_mem_p,
            C_p, C_s, n_p, n_s, m_p, m_s, conv_p, conv_s)
```

```python
import functools

import numpy as np
import jax
import jax.numpy as jnp
from jax import lax
from jax.experimental import pallas as pl
from jax.experimental.pallas import tpu as pltpu

F32 = jnp.float32
BF16 = jnp.bfloat16

D_MODEL = 1024
PAGE_SIZE = 128
NSA_H, NSA_G, NSA_HPG, HD = 8, 2, 4, 64
ROT_DIM = HD // 4
ROPE_THETA = 500000.0
CMP_BLOCK, CMP_STRIDE, CMP_HID = 32, 16, 128
SLC_BLOCK, N_SEL = 64, 16
WINDOW = 512
ML_H, ML_D, ML_W, CONV_W = 4, 128, 512, 4
MEM_H, MEM_HD, MEM_W = 4, 128, 512
N_BRANCH, BRANCH_W = 3, 512
FFN_HID = -(-8 * D_MODEL // (3 * 256)) * 256
EPS = 1e-6

KV_W = 2 * NSA_G * HD
CHUNK_W = CMP_STRIDE * KV_W
LANES = 128
MASKED = -1e30
UNSELECTED = -32768.0
VMEM_LIMIT = 56 * 1024 * 1024

_Q0, _KVC0, _KVS0, _KVW0, _U0, _O0, _MQ0, _SLAB0, _PROJ_W = 0, 512, 768, 1024, 1280, 1792, 2304, 2816, 2944
_SLAB_I, _SLAB_F, _SLAB_G = 0, 4, 8


def _dot(a, b):
    return jnp.dot(a, b, preferred_element_type=F32)


def _dot_nt(a, b):
    return lax.dot_general(a, b, (((1,), (1,)), ((), ())), preferred_element_type=F32)


def _split2(x):
    hi = x.astype(BF16)
    return hi, (x - hi.astype(F32)).astype(BF16)


def _split3(x):
    hi = x.astype(BF16)
    r = x - hi.astype(F32)
    mid = r.astype(BF16)
    return hi, mid, (r - mid.astype(F32)).astype(BF16)


def _dot_exact_rhs(parts, m):
    out = _dot(parts[0], m)
    for p in parts[1:]:
        out = out + _dot(p, m)
    return out


def _sigmoid(x):
    return 1.0 / (1.0 + jnp.exp(-x))


def _log_sigmoid(x):
    return jnp.minimum(x, 0.0) - jnp.log(1.0 + jnp.exp(-jnp.abs(x)))


def _gelu_tanh(x):
    return x * (0.5 * (1.0 + jnp.tanh(np.sqrt(2.0 / np.pi).astype(np.float32) * (x + 0.044715 * (x * x * x)))))


def _rms_rows(x, gain):
    ms = jnp.mean(x * x, axis=-1, keepdims=True)
    return x * lax.rsqrt(ms + EPS) * gain


def _iota(shape, dim):
    return lax.broadcasted_iota(jnp.int32, shape, dim)


def _cparams(sem, vmem=None):
    return pltpu.CompilerParams(dimension_semantics=sem, vmem_limit_bytes=vmem)


def _masked_softmax(s, mask):
    s = jnp.where(mask, s, -jnp.inf)
    m = jnp.max(s, axis=-1, keepdims=True)
    m = jnp.where(m > -jnp.inf, m, 0.0)
    p = jnp.exp(s - m)
    return p / jnp.maximum(jnp.sum(p, axis=-1, keepdims=True), jnp.finfo(F32).tiny)


def _rs_heads64(a, seg_ref, segt_ref, width):
    ss = _dot_exact_rhs(_split2(a * a), seg_ref[0:width, :])
    rs = lax.rsqrt(ss * (1.0 / HD) + EPS)
    return _dot_exact_rhs(_split3(rs), segt_ref[:, 0:width])


def _rope(a, ct, sa, sb, width):
    reps = width // LANES
    if reps > 1:
        ct, sa, sb = (jnp.concatenate([t] * reps, axis=1) for t in (ct, sa, sb))
    half = ROT_DIM // 2
    return a * ct + pltpu.roll(a, width - half, 1) * sa + pltpu.roll(a, half, 1) * sb


def _proj_kernel(x_ref, nm_ref, w_ref, qg_ref, ksg_ref, kwg_ref, mqg_ref, seg_ref, segt_ref,
                 ctq_ref, saq_ref, sbq_ref, ctk_ref, sak_ref, sbk_ref,
                 qn_o, qr_o, kvc_o, kvs_o, kvw_o, u_o, o_o, mq_o, slab_o):
    xn = _rms_rows(x_ref[...], nm_ref[...]).astype(BF16)

    def seg(lo, hi):
        return _dot(xn, w_ref[:, lo:hi])

    q = seg(_Q0, _KVC0)
    qn = q * _rs_heads64(q, seg_ref, segt_ref, NSA_H * HD) * qg_ref[...]
    scale = HD ** -0.5
    qn_o[...] = (qn * scale).astype(BF16)
    qr_o[...] = (_rope(qn, ctq_ref[...], saq_ref[...], sbq_ref[...], NSA_H * HD) * scale).astype(BF16)
    kvc_o[...] = seg(_KVC0, _KVS0)
    for lo, g_ref, o_ref in ((_KVS0, ksg_ref, kvs_o), (_KVW0, kwg_ref, kvw_o)):
        a = seg(lo, lo + KV_W)
        is_k = (_iota(a.shape, 1) & HD) == 0
        an = jnp.where(is_k, a * _rs_heads64(a, seg_ref, segt_ref, KV_W) * g_ref[...], a)
        o_ref[...] = _rope(an, ctk_ref[...], sak_ref[...], sbk_ref[...], KV_W)
    u_o[...] = seg(_U0, _O0)
    o_o[...] = seg(_O0, _MQ0)
    mq = seg(_MQ0, _SLAB0)
    heads = []
    for h in range(MEM_H):
        mh = mq[:, h * MEM_HD:(h + 1) * MEM_HD]
        heads.append(mh * lax.rsqrt(jnp.mean(mh * mh, axis=-1, keepdims=True) + EPS))
    mq_o[...] = (jnp.concatenate(heads, axis=1) * mqg_ref[...] * (MEM_HD ** -0.5)).astype(BF16)
    slab_o[...] = seg(_SLAB0, _PROJ_W)


def _project(x2d, tabs, pw, tm):
    t = x2d.shape[0]
    ntab = tabs[0].shape[0] // tm
    row = lambda i: (i, 0)
    const = lambda i: (0, 0)
    tab = lambda i: (i % ntab, 0)
    full = lambda a: pl.BlockSpec(a.shape, const)
    consts = [pw['norm_mix'], pw['w_proj'], pw['q_gain'], pw['ks_gain'], pw['kw_gain'], pw['mq_gain'],
              pw['seg64'], pw['seg64_t']]
    widths = [(512, BF16), (512, BF16), (KV_W, F32), (KV_W, F32), (KV_W, F32), (ML_W, F32), (ML_W, F32),
              (MEM_W, BF16), (LANES, F32)]
    return pl.pallas_call(
        _proj_kernel,
        grid=(t // tm,),
        in_specs=[pl.BlockSpec((tm, D_MODEL), row)] + [full(a) for a in consts]
                 + [pl.BlockSpec((tm, LANES), tab)] * 6,
        out_specs=[pl.BlockSpec((tm, w), row) for w, _ in widths],
        out_shape=[jax.ShapeDtypeStruct((t, w), d) for w, d in widths],
        compiler_params=_cparams(("arbitrary",), VMEM_LIMIT),
        name="project",
    )(x2d, *consts, *tabs)


def _compress_group(yg, tail_g, pe_lo, pe_hi, wlo, whi, b1, w2, gain, n_valid):
    r = yg.shape[0]
    a = _dot((yg + pe_lo).astype(BF16), wlo)
    bh = _dot((yg + pe_hi).astype(BF16), whi)
    bs = pltpu.roll(bh, r - 1, 0)
    if tail_g is not None:
        bt = _dot((tail_g + pe_hi).astype(BF16), whi)
        bs = jnp.where(_iota(bs.shape, 0) == r - 1, bt, bs)
    hid = _gelu_tanh(a + bs + b1)
    out = _dot(hid.astype(BF16), w2)
    is_k = _iota(out.shape, 1) < HD
    ss = jnp.sum(jnp.where(is_k, out * out, 0.0), axis=-1, keepdims=True) * (1.0 / HD)
    kv = jnp.where(is_k, out * lax.rsqrt(ss + EPS) * gain, out)
    return jnp.where(_iota(kv.shape, 0) < n_valid, kv, 0.0).astype(BF16)


def _group_cols(load, g):
    return jnp.concatenate([load(j * KV_W + g * LANES, j * KV_W + (g + 1) * LANES) for j in range(CMP_STRIDE)], axis=1)


def _compress_prompt_kernel(y_ref, pe_lo, pe_hi, wlo, whi, b1, w2, gain, o_ref, *, n_valid):
    for g in range(NSA_G):
        yg = _group_cols(lambda lo, hi: y_ref[0, :, lo:hi], g)
        o_ref[0, g] = _compress_group(yg, None, pe_lo[...], pe_hi[...], wlo[...], whi[...], b1[...], w2[...],
                                      gain[...], n_valid)


def _compress_paged_kernel(pt_ref, *refs, n_pages, n_valid):
    pages = refs[:n_pages]
    tail_ref, pe_lo, pe_hi, wlo, whi, b1, w2, gain, o_ref = refs[n_pages:]
    for g in range(NSA_G):
        yg = jnp.concatenate([_group_cols(lambda lo, hi, p=p: p[0, :, lo:hi], g) for p in pages], axis=0)
        tg = _group_cols(lambda lo, hi: tail_ref[0, :, lo:hi], g)
        o_ref[0, g] = _compress_group(yg, tg, pe_lo[...], pe_hi[...], wlo[...], whi[...], b1[...], w2[...],
                                      gain[...], n_valid)


def _compress_consts(pw):
    return [pw['pe_lo'], pw['pe_hi'], pw['cmp_wlo'], pw['cmp_whi'], pw['cmp_b1'], pw['cmp_w2'], pw['kc_gain']]


def _compress_prompt(kvc, pw):
    b, s, _ = kvc.shape
    r = s // CMP_STRIDE
    y = kvc.reshape(b, r, CHUNK_W)
    consts = _compress_consts(pw)
    return pl.pallas_call(
        functools.partial(_compress_prompt_kernel, n_valid=r - 1),
        grid=(b,),
        in_specs=[pl.BlockSpec((1, r, CHUNK_W), lambda i: (i, 0, 0))]
                 + [pl.BlockSpec(a.shape, lambda i: (0, 0)) for a in consts],
        out_specs=pl.BlockSpec((1, NSA_G, r, LANES), lambda i: (i, 0, 0, 0)),
        out_shape=jax.ShapeDtypeStruct((b, NSA_G, r, LANES), BF16),
        compiler_params=_cparams(("arbitrary",), VMEM_LIMIT),
        name="compress_prompt",
    )(y, *consts)


def _compress_paged(cache, page_table, kvc_new, pw):
    db, n_pages = page_table.shape
    chunks_per_page = PAGE_SIZE // CMP_STRIDE
    r = n_pages * chunks_per_page
    pool = cache.reshape(cache.shape[0], chunks_per_page, CHUNK_W)
    tail = jnp.pad(kvc_new, ((0, 0), (0, CMP_STRIDE - kvc_new.shape[1]), (0, 0))).reshape(db, 1, CHUNK_W)
    consts = _compress_consts(pw)
    page_specs = [pl.BlockSpec((1, chunks_per_page, CHUNK_W), functools.partial(lambda i, pt, k: (pt[i, k], 0, 0), k=k))
                  for k in range(n_pages)]
    return pl.pallas_call(
        functools.partial(_compress_paged_kernel, n_pages=n_pages, n_valid=r),
        grid_spec=pltpu.PrefetchScalarGridSpec(
            num_scalar_prefetch=1,
            grid=(db,),
            in_specs=page_specs + [pl.BlockSpec((1, 1, CHUNK_W), lambda i, pt: (i, 0, 0))]
                     + [pl.BlockSpec(a.shape, lambda i, pt: (0, 0)) for a in consts],
            out_specs=pl.BlockSpec((1, NSA_G, r, LANES), lambda i, pt: (i, 0, 0, 0)),
        ),
        out_shape=jax.ShapeDtypeStruct((db, NSA_G, r, LANES), BF16),
        compiler_params=_cparams(("arbitrary",), VMEM_LIMIT),
        name="compress_paged",
    )(page_table, *([pool] * n_pages), tail, *consts)


def _rank_select(score, idx, n_blocks, axis):
    cnt = jnp.zeros(score.shape, jnp.int32)
    for i in range(n_blocks):
        other = score[i:i + 1, :] if axis == 0 else score[:, i:i + 1]
        ge = jnp.where(other >= score, 1, 0)
        gt = jnp.where(other > score, 1, 0)
        cnt = cnt + jnp.where(idx > i, ge, gt)
    return jnp.where(score > -jnp.inf, jnp.where(cnt < N_SEL, 0.0, UNSELECTED), UNSELECTED)


def _pair_lanes(even, odd):
    return jnp.where(_iota(even.shape, 1) < HD, pltpu.roll(even, HD, 1), odd)


def _cmp_prompt_kernel(q_ref, kv_ref, ovt_ref, o_ref, sel_ref, *, tq, n_slc):
    i = pl.program_id(2)
    q = q_ref[...]
    kv = kv_ref[0, 0]
    k = kv[:, 0:HD]
    r = kv.shape[0]
    pos_col = i * tq + _iota((tq, 1), 0)
    visible = (_iota((1, r), 1) * CMP_STRIDE + (CMP_BLOCK - 1)) <= pos_col
    psum = jnp.zeros((tq, r), F32)
    outs = []
    for h in range(NSA_HPG):
        p = _masked_softmax(_dot_nt(q[:, h * HD:(h + 1) * HD], k), visible)
        psum = psum + p
        outs.append(_dot(p.astype(BF16), kv))
    o_ref[...] = jnp.concatenate([_pair_lanes(outs[0], outs[1]), _pair_lanes(outs[2], outs[3])], axis=1)
    hi, lo = _split2(psum)
    imp = _dot_nt(ovt_ref[...], hi) + _dot_nt(ovt_ref[...], lo)
    blk = _iota(imp.shape, 0)
    pos_row = i * tq + _iota(imp.shape, 1)
    forced = (blk == jnp.right_shift(pos_row, 6)) | (blk == 0)
    score = jnp.where(forced, jnp.inf, jnp.where(blk * SLC_BLOCK <= pos_row, imp, -jnp.inf))
    bias = _rank_select(score, blk, n_slc, 0)
    padded = jnp.concatenate([jnp.zeros_like(bias), bias], axis=0)
    sel_ref[0, 0] = padded.T.astype(BF16)


def _cmp_prompt(qn, kvcmp, ovt, b, s, tq):
    nq = s // tq
    r = kvcmp.shape[2]
    n_slc = s // SLC_BLOCK
    assert n_slc <= HD and ovt.shape == (HD, r) and SLC_BLOCK == 64
    return pl.pallas_call(
        functools.partial(_cmp_prompt_kernel, tq=tq, n_slc=n_slc),
        grid=(b, NSA_G, nq),
        in_specs=[pl.BlockSpec((tq, NSA_HPG * HD), lambda bi, g, i: (bi * nq + i, g)),
                  pl.BlockSpec((1, 1, r, LANES), lambda bi, g, i: (bi, g, 0, 0)),
                  pl.BlockSpec(ovt.shape, lambda bi, g, i: (0, 0))],
        out_specs=[pl.BlockSpec((tq, NSA_HPG * HD), lambda bi, g, i: (bi * nq + i, g)),
                   pl.BlockSpec((1, 1, tq, LANES), lambda bi, g, i: (bi, g, i, 0))],
        out_shape=[jax.ShapeDtypeStruct((b * s, NSA_H * HD), F32),
                   jax.ShapeDtypeStruct((b, NSA_G, s, LANES), BF16)],
        compiler_params=_cparams(("arbitrary",) * 3, VMEM_LIMIT),
        name="cmp_attend_select_prompt",
    )(qn, kvcmp, ovt)


def _cmp_sample_kernel(q_ref, kv_ref, ov_ref, o_ref, sel_ref, *, pos, n_slc):
    q = q_ref[0]
    r = kv_ref.shape[2]
    row = _iota((NSA_H, 1), 0)
    visible = (_iota((1, r), 1) * CMP_STRIDE + (CMP_BLOCK - 1)) <= pos
    kvs = [kv_ref[0, g] for g in range(NSA_G)]
    s = jnp.where(row < NSA_HPG, _dot_nt(q, kvs[0][:, 0:HD]), _dot_nt(q, kvs[1][:, 0:HD]))
    p = _masked_softmax(s, visible)
    pb = p.astype(BF16)
    o = jnp.where(row < NSA_HPG, _dot(pb, kvs[0]), _dot(pb, kvs[1]))
    o_ref[0] = o[:, HD:2 * HD]
    p0 = jnp.sum(p[0:NSA_HPG], axis=0, keepdims=True)
    p1 = jnp.sum(p[NSA_HPG:NSA_H], axis=0, keepdims=True)
    psum = jnp.where(row < NSA_HPG, p0, p1)
    imp = _dot_exact_rhs(_split2(psum), ov_ref[...])
    blk = _iota(imp.shape, 1)
    forced = (blk == pos // SLC_BLOCK) | (blk == 0)
    score = jnp.where(forced, jnp.inf, jnp.where(blk * SLC_BLOCK <= pos, imp, -jnp.inf))
    sel_ref[0] = _rank_select(score, blk, n_slc, 1)


def _cmp_sample(qn3, kvcmp, ov, pos, n_slc):
    db, _, r, _ = kvcmp.shape
    nslp = ov.shape[1]
    return pl.pallas_call(
        functools.partial(_cmp_sample_kernel, pos=pos, n_slc=n_slc),
        grid=(db,),
        in_specs=[pl.BlockSpec((1, NSA_H, HD), lambda i: (i, 0, 0)),
                  pl.BlockSpec((1, NSA_G, r, LANES), lambda i: (i, 0, 0, 0)),
                  pl.BlockSpec(ov.shape, lambda i: (0, 0))],
        out_specs=[pl.BlockSpec((1, NSA_H, HD), lambda i: (i, 0, 0)),
                   pl.BlockSpec((1, NSA_H, nslp), lambda i: (i, 0, 0))],
        out_shape=[jax.ShapeDtypeStruct((db, NSA_H, HD), F32),
                   jax.ShapeDtypeStruct((db, NSA_H, nslp), F32)],
        compiler_params=_cparams(("arbitrary",), VMEM_LIMIT),
        name="cmp_attend_select_sample",
    )(qn3, kvcmp, ov)


def _flash_prompt_kernel(*refs, tq, selected):
    if selected:
        q_ref, kv_ref, sel_ref, o_ref, kop, vop = refs
    else:
        q_ref, kv_ref, o_ref, vop = refs
        kop = vop
    i = pl.program_id(2)

    @pl.when(i == 0)
    def _():
        kvf = kv_ref[0]
        vop[...] = kvf.astype(BF16)
        if selected:
            lane = _iota(kvf.shape, 1)
            onehot = (lane - HD) == jnp.right_shift(_iota(kvf.shape, 0), 6)
            kop[...] = jnp.where(lane < HD, kvf, onehot.astype(F32)).astype(BF16)

    qf = q_ref[...].astype(F32)
    extra = sel_ref[0, 0].astype(F32) if selected else jnp.zeros((tq, LANES), F32)
    lane = _iota((tq, LANES), 1)
    parts = []
    for h in range(NSA_HPG):
        col = qf[:, (h // 2) * LANES:(h // 2 + 1) * LANES]
        if h % 2:
            col = pltpu.roll(col, HD, 1)
        parts.append(jnp.where(lane < HD, col, extra))
    qa = jnp.concatenate(parts, axis=0).astype(BF16)
    pos1 = i * tq + _iota((tq, 1), 0)
    pos = jnp.concatenate([pos1] * NSA_HPG, axis=0)
    rows = NSA_HPG * tq

    def body(j, carry):
        m, l, acc = carry
        start = pl.multiple_of(j * tq, tq)
        s = _dot_nt(qa, kop[pl.ds(start, tq), :])
        kpos = j * tq + _iota((1, tq), 1)
        valid = kpos <= pos
        if not selected:
            valid = valid & ((pos - kpos) <= WINDOW)
        s = jnp.where(valid, s, MASKED)
        m_new = jnp.maximum(m, jnp.max(s, axis=-1, keepdims=True))
        p = jnp.exp(s - m_new)
        alpha = jnp.exp(m - m_new)
        l = alpha * l + jnp.sum(p, axis=-1, keepdims=True)
        acc = alpha * acc + _dot(p.astype(BF16), vop[pl.ds(start, tq), :])
        return m_new, l, acc

    first = 0 if selected else jnp.maximum(i - WINDOW // tq, 0)
    init = (jnp.full((rows, 1), MASKED, F32), jnp.zeros((rows, 1), F32), jnp.zeros((rows, LANES), F32))
    _, l, acc = lax.fori_loop(first, i + 1, body, init)
    o = acc / l
    oh = [o[h * tq:(h + 1) * tq] for h in range(NSA_HPG)]
    o_ref[...] = jnp.concatenate([_pair_lanes(oh[0], oh[1]), _pair_lanes(oh[2], oh[3])], axis=1)


def _flash_prompt(qrot, kv, sel, b, s, tq):
    nq = s // tq
    assert WINDOW % tq == 0
    selected = sel is not None
    qspec = pl.BlockSpec((tq, NSA_HPG * HD), lambda bi, g, i: (bi * nq + i, g))
    in_specs = [qspec, pl.BlockSpec((1, s, LANES), lambda bi, g, i: (bi, 0, g))]
    args = [qrot, kv]
    scratch = [pltpu.VMEM((s, LANES), BF16)]
    if selected:
        in_specs.append(pl.BlockSpec((1, 1, tq, LANES), lambda bi, g, i: (bi, g, i, 0)))
        args.append(sel)
        scratch.append(pltpu.VMEM((s, LANES), BF16))
    return pl.pallas_call(
        functools.partial(_flash_prompt_kernel, tq=tq, selected=selected),
        grid=(b, NSA_G, nq),
        in_specs=in_specs,
        out_specs=qspec,
        out_shape=jax.ShapeDtypeStruct((b * s, NSA_H * HD), F32),
        scratch_shapes=scratch,
        compiler_params=_cparams(("arbitrary",) * 3, VMEM_LIMIT),
        name="slc_attend_prompt" if selected else "win_attend_prompt",
    )(*args)


def _mem_kv_kernel(x_ref, nm_ref, w_ref, kg_ref, o_ref):
    xn = _rms_rows(x_ref[...], nm_ref[...]).astype(BF16)
    kv = _dot(xn, w_ref[...])
    cols = []
    for h in range(MEM_H):
        k = kv[:, 2 * h * MEM_HD:(2 * h + 1) * MEM_HD]
        cols.append(_rms_rows(k, kg_ref[...]))
        cols.append(kv[:, (2 * h + 1) * MEM_HD:(2 * h + 2) * MEM_HD])
    o_ref[...] = jnp.concatenate(cols, axis=1)


def _mem_kv(mem2d, pw, tm):
    t = mem2d.shape[0]
    consts = [pw['norm_mem'], pw['w_mem_kv'], pw['mem_k_gain']]
    return pl.pallas_call(
        _mem_kv_kernel,
        grid=(t // tm,),
        in_specs=[pl.BlockSpec((tm, D_MODEL), lambda i: (i, 0))] + [pl.BlockSpec(a.shape, lambda i: (0, 0)) for a in consts],
        out_specs=pl.BlockSpec((tm, 2 * MEM_W), lambda i: (i, 0)),
        out_shape=jax.ShapeDtypeStruct((t, 2 * MEM_W), F32),
        compiler_params=_cparams(("arbitrary",), VMEM_LIMIT),
        name="mem_kv",
    )(mem2d, *consts)


def _mem_prompt_kernel(q_ref, kv_ref, o_ref):
    kv = kv_ref[0].astype(BF16)
    s = _dot_nt(q_ref[...], kv[:, 0:MEM_HD])
    p = jnp.exp(s - jnp.max(s, axis=-1, keepdims=True))
    o_ref[...] = _dot(p.astype(BF16), kv[:, MEM_HD:2 * MEM_HD]) / jnp.sum(p, axis=-1, keepdims=True)


def _mem_prompt(mq, kvm, b, s, tq):
    nq = s // tq
    t_mem = kvm.shape[1]
    return pl.pallas_call(
        _mem_prompt_kernel,
        grid=(b, MEM_H, nq),
        in_specs=[pl.BlockSpec((tq, MEM_HD), lambda bi, h, i: (bi * nq + i, h)),
                  pl.BlockSpec((1, t_mem, 2 * MEM_HD), lambda bi, h, i: (bi, 0, h))],
        out_specs=pl.BlockSpec((tq, MEM_HD), lambda bi, h, i: (bi * nq + i, h)),
        out_shape=jax.ShapeDtypeStruct((b * s, MEM_W), F32),
        compiler_params=_cparams(("arbitrary",) * 3, VMEM_LIMIT),
        name="mem_attend_prompt",
    )(mq, kvm)


def _decode_attend(qa, blocks, new_row=None, new_bias=None):
    scores = []
    for load, bias in blocks:
        s = _dot_nt(qa, load())
        scores.append(s if bias is None else s + bias)
    m = functools.reduce(jnp.maximum, [jnp.max(s, axis=-1, keepdims=True) for s in scores])
    if new_row is not None:
        nb = new_row.astype(BF16).astype(F32)
        s_new = jnp.sum(qa.astype(F32) * nb, axis=-1, keepdims=True) + new_bias
        m = jnp.maximum(m, s_new)
    ps = [jnp.exp(s - m) for s in scores]
    l = functools.reduce(lambda a, c: a + c, [jnp.sum(p, axis=-1, keepdims=True) for p in ps])
    acc = functools.reduce(lambda a, c: a + c, [_dot(p.astype(BF16), load()) for p, (load, _) in zip(ps, blocks)])
    if new_row is not None:
        p_new = jnp.exp(s_new - m)
        l = l + p_new
        acc = acc + p_new.astype(BF16).astype(F32) * nb
    return acc / l


def _group_values(o):
    row = _iota((NSA_H, LANES), 0)
    both = jnp.where(row < NSA_HPG, o[:, 0:LANES], o[:, LANES:2 * LANES])
    return both[:, HD:2 * HD]


def _slc_sample_kernel(pt_ref, *refs, n_pages):
    pages = refs[:n_pages]
    q_ref, sel_ref, new_ref, o_ref = refs[n_pages:]
    sb = sel_ref[0]
    lane = _iota((NSA_H, PAGE_SIZE), 1)
    blocks = []
    for p, page in enumerate(pages):
        bias = jnp.where(lane < SLC_BLOCK, sb[:, 2 * p:2 * p + 1], sb[:, 2 * p + 1:2 * p + 2])
        blocks.append((lambda page=page: page[0].astype(BF16), bias))
    n_past = n_pages * (PAGE_SIZE // SLC_BLOCK)
    o = _decode_attend(q_ref[0], blocks, new_ref[0], sb[:, n_past:n_past + 1])
    o_ref[0] = _group_values(o)


def _slc_sample(qaug, sel, cache, page_table, kvs_new):
    db, n_pages = page_table.shape
    nslp = sel.shape[2]
    pool = cache.reshape(cache.shape[0], PAGE_SIZE, KV_W)
    page_specs = [pl.BlockSpec((1, PAGE_SIZE, KV_W), functools.partial(lambda i, pt, k: (pt[i, k], 0, 0), k=k))
                  for k in range(n_pages)]
    return pl.pallas_call(
        functools.partial(_slc_sample_kernel, n_pages=n_pages),
        grid_spec=pltpu.PrefetchScalarGridSpec(
            num_scalar_prefetch=1,
            grid=(db,),
            in_specs=page_specs + [pl.BlockSpec((1, NSA_H, KV_W), lambda i, pt: (i, 0, 0)),
                                   pl.BlockSpec((1, NSA_H, nslp), lambda i, pt: (i, 0, 0)),
                                   pl.BlockSpec((1, 1, KV_W), lambda i, pt: (i, 0, 0))],
            out_specs=pl.BlockSpec((1, NSA_H, HD), lambda i, pt: (i, 0, 0)),
        ),
        out_shape=jax.ShapeDtypeStruct((db, NSA_H, HD), F32),
        compiler_params=_cparams(("arbitrary",), VMEM_LIMIT),
        name="slc_attend_sample",
    )(page_table, *([pool] * n_pages), qaug, sel, kvs_new)


def _win_sample_kernel(q_ref, buf_ref, new_ref, o_ref, win_ref, *, pos, first_pos):
    rows = buf_ref[0]
    w_buf = rows.shape[0]
    new = new_ref[0]
    d = pos - (first_pos + _iota((1, w_buf), 1))
    bias = jnp.where((d >= 0) & (d <= WINDOW), 0.0, MASKED)
    o = _decode_attend(q_ref[0], [(lambda: rows.astype(BF16), bias)], new, 0.0)
    o_ref[0] = _group_values(o)
    shifted = pltpu.roll(rows, w_buf - 1, 0)
    win_ref[0] = jnp.where(_iota(rows.shape, 0) == w_buf - 1, new, shifted)


def _win_sample(qaug, win_buf, kvw_new, pos, first_pos):
    db, w_buf, _ = win_buf.shape
    return pl.pallas_call(
        functools.partial(_win_sample_kernel, pos=pos, first_pos=first_pos),
        grid=(db,),
        in_specs=[pl.BlockSpec((1, NSA_H, KV_W), lambda i: (i, 0, 0)),
                  pl.BlockSpec((1, w_buf, KV_W), lambda i: (i, 0, 0)),
                  pl.BlockSpec((1, 1, KV_W), lambda i: (i, 0, 0))],
        out_specs=[pl.BlockSpec((1, NSA_H, HD), lambda i: (i, 0, 0)),
                   pl.BlockSpec((1, w_buf, KV_W), lambda i: (i, 0, 0))],
        out_shape=[jax.ShapeDtypeStruct((db, NSA_H, HD), F32),
                   jax.ShapeDtypeStruct((db, w_buf, KV_W), F32)],
        compiler_params=_cparams(("arbitrary",), VMEM_LIMIT),
        name="win_attend_sample",
    )(qaug, win_buf, kvw_new)


def _mem_sample_kernel(q_ref, kv_ref, o_ref):
    o = _decode_attend(q_ref[0], [(lambda: kv_ref[0].astype(BF16), None)])
    row = _iota((8, MEM_HD), 0)
    out = jnp.zeros((8, MEM_HD), F32)
    for h in range(MEM_H):
        out = jnp.where(row == h, o[:, (2 * h + 1) * MEM_HD:(2 * h + 2) * MEM_HD], out)
    o_ref[0] = out


def _mem_sample(qaug, kvm):
    db, t_mem, w = kvm.shape
    return pl.pallas_call(
        _mem_sample_kernel,
        grid=(db,),
        in_specs=[pl.BlockSpec((1, 8, w), lambda i: (i, 0, 0)),
                  pl.BlockSpec((1, t_mem, w), lambda i: (i, 0, 0))],
        out_specs=pl.BlockSpec((1, 8, MEM_HD), lambda i: (i, 0, 0)),
        out_shape=jax.ShapeDtypeStruct((db, 8, MEM_HD), F32),
        compiler_params=_cparams(("arbitrary",), VMEM_LIMIT),
        name="mem_attend_sample",
    )(qaug, kvm)


def _mlstm_prompt_kernel(u_ref, oraw_ref, slab_ref, slabt_ref, cw_ref, cb_ref, wq_ref, wk_ref, wv_ref,
                         bgc_ref, bgr_ref, mlg_ref, tril_ref, triu_ref,
                         o_ref, c_ref, n_ref, m_ref, ubuf, *, chunk):
    L = chunk
    pad = 8

    @pl.when(pl.program_id(1) == 0)
    def _():
        c_ref[...] = jnp.zeros_like(c_ref)
        n_ref[...] = jnp.zeros_like(n_ref)
        m_ref[...] = jnp.zeros_like(m_ref)
        ubuf[0:pad, :] = jnp.zeros((pad, ML_W), F32)

    u = u_ref[0]
    ubuf[pad:pad + L, :] = u
    conv = cb_ref[...] + cw_ref[CONV_W - 1:CONV_W, :] * u
    for w in range(CONV_W - 1):
        off = pad - (CONV_W - 1) + w
        conv = conv + cw_ref[w:w + 1, :] * ubuf[off:off + L, :]
    ubuf[0:pad, :] = u[L - pad:L, :]
    cact = conv * _sigmoid(conv)

    gl = slab_ref[0] + bgc_ref[...]
    gt = slabt_ref[0] + bgr_ref[...]
    b_c = None
    for part in _split3(_log_sigmoid(gl)):
        term = _dot(tril_ref[...], part)
        b_c = term if b_c is None else b_c + term
    b_r = _dot_exact_rhs(_split3(_log_sigmoid(gt)), triu_ref[...])
    m_prev = m_ref[0]
    tri = _iota((L, L), 0) >= _iota((L, L), 1)
    lane1 = _iota((1, LANES), 1)
    m_out = m_prev
    outs = []
    for h in range(ML_H):
        sl = slice(h * ML_D, (h + 1) * ML_D)
        bc = b_c[:, _SLAB_F + h:_SLAB_F + h + 1]
        ic = gl[:, _SLAB_I + h:_SLAB_I + h + 1]
        br = b_r[_SLAB_F + h:_SLAB_F + h + 1, :]
        ir = gt[_SLAB_I + h:_SLAB_I + h + 1, :]
        m0 = m_prev[:, h:h + 1]
        dm = jnp.where(tri, bc - br + ir, -jnp.inf)
        m_new = jnp.maximum(bc + m0, jnp.max(dm, axis=-1, keepdims=True))
        ch = cact[:, sl].astype(BF16)
        qh = _dot(ch, wq_ref[h]) * (ML_D ** -0.5)
        kh = _dot(ch, wk_ref[h])
        vh = _dot(u[:, sl].astype(BF16), wv_ref[h])
        qb, kb, vb = qh.astype(BF16), kh.astype(BF16), vh.astype(BF16)
        wmat = jnp.exp(dm - m_new) * _dot_nt(qb, kb)
        inter = jnp.exp(bc + m0 - m_new)
        c_old = c_ref[0, h]
        n_old = n_ref[0, h:h + 1, :]
        num = inter * _dot(qb, c_old.astype(BF16)) + _dot(wmat.astype(BF16), vb)
        den = inter * jnp.sum(qh * n_old, axis=-1, keepdims=True) + jnp.sum(wmat, axis=-1, keepdims=True)
        hh = num / jnp.maximum(jnp.abs(den), jnp.exp(-m_new))
        m_end = m_new[L - 1:L, :]
        b_end = bc[L - 1:L, :]
        decay = jnp.exp(b_end + m0 - m_end)
        kw = kh * jnp.exp(b_end - bc + ic - m_end)
        c_ref[0, h] = decay * c_old + _dot(kw.T.astype(BF16), vb)
        n_ref[0, h:h + 1, :] = decay * n_old + jnp.sum(kw, axis=0, keepdims=True)
        m_out = jnp.where(lane1 == h, m_end, m_out)
        hn = hh * lax.rsqrt(jnp.mean(hh * hh, axis=-1, keepdims=True) + EPS) * mlg_ref[:, sl]
        outs.append(_sigmoid(oraw_ref[0, :, sl]) * hn)
    o_ref[0] = jnp.concatenate(outs, axis=1)
    m_ref[0] = m_out


def _mlstm_prompt(u, oraw, slab, pw, chunk):
    b, s, _ = u.shape
    nc = s // chunk
    slabt = jnp.swapaxes(slab[:, :, 0:8], 1, 2)
    tril = jnp.tril(jnp.ones((chunk, chunk), BF16))
    consts = [pw['conv_w'], pw['conv_b'], pw['w_ml_q'], pw['w_ml_k'], pw['w_ml_v'], pw['gate_bias_lanes'],
              pw['gate_bias_rows'], pw['ml_gain'], tril, tril.T]
    blk = lambda w: pl.BlockSpec((1, chunk, w), lambda bi, c: (bi, c, 0))

    def const_spec(a):
        nd = a.ndim
        return pl.BlockSpec(a.shape, lambda bi, c: (0,) * nd)

    return pl.pallas_call(
        functools.partial(_mlstm_prompt_kernel, chunk=chunk),
        grid=(b, nc),
        in_specs=[blk(ML_W), blk(ML_W), blk(LANES), pl.BlockSpec((1, 8, chunk), lambda bi, c: (bi, 0, c))]
                 + [const_spec(a) for a in consts],
        out_specs=[blk(ML_W),
                   pl.BlockSpec((1, ML_H, ML_D, ML_D), lambda bi, c: (bi, 0, 0, 0)),
                   pl.BlockSpec((1, ML_H, ML_D), lambda bi, c: (bi, 0, 0)),
                   pl.BlockSpec((1, 1, LANES), lambda bi, c: (bi, 0, 0))],
        out_shape=[jax.ShapeDtypeStruct((b, s, ML_W), F32),
                   jax.ShapeDtypeStruct((b, ML_H, ML_D, ML_D), F32),
                   jax.ShapeDtypeStruct((b, ML_H, ML_D), F32),
                   jax.ShapeDtypeStruct((b, 1, LANES), F32)],
        scratch_shapes=[pltpu.VMEM((chunk + 8, ML_W), F32)],
        compiler_params=_cparams(("arbitrary", "arbitrary"), VMEM_LIMIT),
        name="mlstm_prompt",
    )(u, oraw, slab, slabt, *consts)


def _mlstm_sample_kernel(ext_ref, oraw_ref, slab_ref, c_ref, n_ref, m_ref, cw_ref, cb_ref, wq_ref, wk_ref,
                         wv_ref, wkt_ref, bgc_ref, mlg_ref, o_ref, co_ref, no_ref, mo_ref):
    ext = ext_ref[0]
    conv = cb_ref[...]
    for w in range(CONV_W):
        conv = conv + cw_ref[w:w + 1, :] * ext[w:w + 1, :]
    cact = conv * _sigmoid(conv)
    u = ext[CONV_W - 1:CONV_W, :]
    gl = slab_ref[0] + bgc_ref[...]
    lf_all = _log_sigmoid(gl)
    m_prev = m_ref[0]
    lane1 = _iota((1, LANES), 1)
    m_out = m_prev
    outs = []
    for h in range(ML_H):
        sl = slice(h * ML_D, (h + 1) * ML_D)
        ch8 = jnp.broadcast_to(cact[:, sl], (8, ML_D)).astype(BF16)
        u8 = jnp.broadcast_to(u[:, sl], (8, ML_D)).astype(BF16)
        q = (_dot(ch8, wq_ref[h]) * (ML_D ** -0.5))[0:1]
        k = _dot(ch8, wk_ref[h])[0:1]
        v = _dot(u8, wv_ref[h])[0:1]
        k_col = _dot_nt(wkt_ref[h], ch8)[:, 0:1]
        ig = gl[:, _SLAB_I + h:_SLAB_I + h + 1]
        lf = lf_all[:, _SLAB_F + h:_SLAB_F + h + 1]
        m0 = m_prev[:, h:h + 1]
        m_new = jnp.maximum(lf + m0, ig)
        qb = q.astype(BF16)
        qk = jnp.sum(qb.astype(F32) * k.astype(BF16).astype(F32), axis=-1, keepdims=True)
        w_in = jnp.exp(ig - m_new) * qk
        inter = jnp.exp(lf + m0 - m_new)
        c_old = c_ref[0, h]
        n_old = n_ref[0, h:h + 1, :]
        qc = _dot(jnp.broadcast_to(qb, (8, ML_D)), c_old.astype(BF16))[0:1]
        num = inter * qc + w_in * v
        den = inter * jnp.sum(q * n_old, axis=-1, keepdims=True) + w_in
        hh = num / jnp.maximum(jnp.abs(den), jnp.exp(-m_new))
        w_end = jnp.exp(ig - m_new)
        co_ref[0, h] = inter * c_old + k_col * (w_end * v)
        no_ref[0, h:h + 1, :] = inter * n_old + w_end * k
        m_out = jnp.where(lane1 == h, m_new, m_out)
        hn = hh * lax.rsqrt(jnp.mean(hh * hh, axis=-1, keepdims=True) + EPS) * mlg_ref[:, sl]
        outs.append(_sigmoid(oraw_ref[0, :, sl]) * hn)
    o_ref[0] = jnp.concatenate(outs, axis=1)
    mo_ref[0] = m_out


def _mlstm_sample(ext, oraw, slab, state_c, state_n, state_m, pw):
    db = ext.shape[0]
    m_in = jnp.pad(state_m, ((0, 0), (0, LANES - ML_H))).reshape(db, 1, LANES)
    consts = [pw['conv_w'], pw['conv_b'], pw['w_ml_q'], pw['w_ml_k'], pw['w_ml_v'], pw['w_ml_k_t'],
              pw['gate_bias_lanes'], pw['ml_gain']]

    def const_spec(a):
        nd = a.ndim
        return pl.BlockSpec(a.shape, lambda i: (0,) * nd)

    per_seq = lambda shape: pl.BlockSpec((1,) + shape, lambda i: (i,) + (0,) * len(shape))
    return pl.pallas_call(
        _mlstm_sample_kernel,
        grid=(db,),
        in_specs=[per_seq((CONV_W, ML_W)), per_seq((1, ML_W)), per_seq((1, LANES)), per_seq((ML_H, ML_D, ML_D)),
                  per_seq((ML_H, ML_D)), per_seq((1, LANES))] + [const_spec(a) for a in consts],
        out_specs=[per_seq((1, ML_W)), per_seq((ML_H, ML_D, ML_D)), per_seq((ML_H, ML_D)), per_seq((1, LANES))],
        out_shape=[jax.ShapeDtypeStruct((db, 1, ML_W), F32),
                   jax.ShapeDtypeStruct((db, ML_H, ML_D, ML_D), F32),
                   jax.ShapeDtypeStruct((db, ML_H, ML_D), F32),
                   jax.ShapeDtypeStruct((db, 1, LANES), F32)],
        compiler_params=_cparams(("arbitrary",), VMEM_LIMIT),
        name="mlstm_sample",
    )(ext, oraw, slab, state_c, state_n, m_in, *consts)


def _merge_kernel(x_ref, nm_ref, wg_ref, slab_ref, ex_ref, ocmp_ref, oslc_ref, owin_ref, oml_ref, omem_ref,
                  wb_ref, wout_ref, o_ref):
    x = x_ref[...]
    xn = _rms_rows(x, nm_ref[...]).astype(BF16)
    gparts = _split2(_sigmoid(slab_ref[...]))
    onsa = None
    for br, br_ref in enumerate((ocmp_ref, oslc_ref, owin_ref)):
        term = _dot_exact_rhs(gparts, ex_ref[br]) * br_ref[...]
        onsa = term if onsa is None else onsa + term
    z = None
    for n, on in enumerate((onsa, oml_ref[...], omem_ref[...])):
        gate = _sigmoid(_dot(xn, wg_ref[:, n * D_MODEL:(n + 1) * D_MODEL]))
        term = gate * _dot(on.astype(BF16), wb_ref[n])
        z = term if z is None else z + term
    o_ref[...] = x + _dot(z.astype(BF16), wout_ref[...])


def _merge(x2d, slab, ocmp, oslc, owin, oml, omem, pw, tm):
    t = x2d.shape[0]
    row = lambda w: pl.BlockSpec((tm, w), lambda i: (i, 0))

    def const_spec(a):
        nd = a.ndim
        return pl.BlockSpec(a.shape, lambda i: (0,) * nd)

    return pl.pallas_call(
        _merge_kernel,
        grid=(t // tm,),
        in_specs=[row(D_MODEL), const_spec(pw['norm_mix']), const_spec(pw['w_gate']), row(LANES),
                  const_spec(pw['gate_expand'])] + [row(BRANCH_W)] * 5
                 + [const_spec(pw['w_branch']), const_spec(pw['w_out'])],
        out_specs=row(D_MODEL),
        out_shape=jax.ShapeDtypeStruct((t, D_MODEL), F32),
        compiler_params=_cparams(("arbitrary",), VMEM_LIMIT),
        name="merge",
    )(x2d, pw['norm_mix'], pw['w_gate'], slab, pw['gate_expand'], ocmp, oslc, owin, oml, omem,
      pw['w_branch'], pw['w_out'])


def _ffn_kernel(x_ref, nf_ref, win_ref, wout_ref, o_ref):
    x = x_ref[...]
    xn = _rms_rows(x, nf_ref[...]).astype(BF16)
    a = _dot(xn, win_ref[:, 0:FFN_HID])
    b = _dot(xn, win_ref[:, FFN_HID:2 * FFN_HID])
    o_ref[...] = x + _dot((a * _sigmoid(a) * b).astype(BF16), wout_ref[...])


def _ffn(x2d, pw, tm):
    t = x2d.shape[0]
    consts = [pw['norm_ffn'], pw['w_ffn_in'], pw['w_ffn_out']]
    return pl.pallas_call(
        _ffn_kernel,
        grid=(t // tm,),
        in_specs=[pl.BlockSpec((tm, D_MODEL), lambda i: (i, 0))] + [pl.BlockSpec(a.shape, lambda i: (0, 0)) for a in consts],
        out_specs=pl.BlockSpec((tm, D_MODEL), lambda i: (i, 0)),
        out_shape=jax.ShapeDtypeStruct((t, D_MODEL), F32),
        compiler_params=_cparams(("arbitrary",), VMEM_LIMIT),
        name="ffn",
    )(x2d, *consts)


def _rope_tables(pos):
    half = ROT_DIM // 2
    freqs = ROPE_THETA ** (-jnp.arange(half, dtype=F32) / half)
    ang = pos.astype(F32)[:, None] * freqs
    cos, sin = jnp.cos(ang), jnp.sin(ang)
    n = pos.shape[0]
    ones = jnp.ones((n, HD - ROT_DIM), F32)
    zeros = jnp.zeros((n, HD - ROT_DIM), F32)
    zh = jnp.zeros((n, half), F32)
    ct = jnp.concatenate([cos, cos, ones], axis=1)
    sa = jnp.concatenate([-sin, zh, zeros], axis=1)
    sb = jnp.concatenate([zh, sin, zeros], axis=1)
    one64, zero64 = jnp.ones((n, HD), F32), jnp.zeros((n, HD), F32)
    q_tabs = [jnp.concatenate([t, t], axis=1) for t in (ct, sa, sb)]
    k_tabs = [jnp.concatenate([ct, one64], axis=1), jnp.concatenate([sa, zero64], axis=1),
              jnp.concatenate([sb, zero64], axis=1)]
    return q_tabs + k_tabs


def _prepare_weights(norm_mix, w_in, q_norm, k_norm_cmp, k_norm_slc, k_norm_win, cmp_pe, cmp_w1, cmp_b1, cmp_w2,
                     conv_w, conv_b, w_ml_q, w_ml_k, w_ml_v, b_igate, b_fgate, ml_norm, norm_mem, w_mem_kv,
                     mem_q_norm, mem_k_norm, w_branch, w_out, norm_ffn, w_ffn_in, w_ffn_out):
    widths = (NSA_H * HD, 3 * NSA_H, KV_W, KV_W, KV_W, ML_W, ML_W, ML_H, ML_H, MEM_W, N_BRANCH * D_MODEL)
    offs = np.concatenate([[0], np.cumsum(widths)])
    q, g, kvc, kvs, kvw, u, o, ig, fg, mq, mg = (w_in[:, int(offs[i]):int(offs[i + 1])] for i in range(len(widths)))
    slab_pad = jnp.zeros((D_MODEL, LANES - 2 * ML_H - 3 * NSA_H), F32)
    pw = {}
    pw['w_proj'] = jnp.concatenate([q, kvc, kvs, kvw, u, o, mq, ig, fg, g, slab_pad], axis=1).astype(BF16)
    pw['w_gate'] = mg.astype(BF16)
    pw['norm_mix'] = norm_mix.reshape(1, D_MODEL)
    pw['q_gain'] = jnp.tile(q_norm, NSA_H).reshape(1, NSA_H * HD)
    ones = jnp.ones((HD,), F32)
    pw['ks_gain'] = jnp.tile(jnp.concatenate([k_norm_slc, ones]), NSA_G).reshape(1, KV_W)
    pw['kw_gain'] = jnp.tile(jnp.concatenate([k_norm_win, ones]), NSA_G).reshape(1, KV_W)
    pw['kc_gain'] = jnp.concatenate([k_norm_cmp, ones]).reshape(1, LANES)
    pw['mq_gain'] = jnp.tile(mem_q_norm, MEM_H).reshape(1, MEM_W)
    seg = (np.arange(NSA_H * HD)[:, None] // HD == np.arange(LANES)[None, :]).astype(np.float32)
    pw['seg64'] = jnp.asarray(seg, BF16)
    pw['seg64_t'] = jnp.asarray(seg.T, BF16)
    w1 = cmp_w1.reshape(2, CMP_BLOCK, HD, CMP_HID)
    eye = jnp.eye(2, dtype=F32)
    for name, part in (('cmp_wlo', w1[:, :CMP_STRIDE]), ('cmp_whi', w1[:, CMP_STRIDE:])):
        pw[name] = jnp.einsum('cjdh,ce->jcdeh', part, eye).reshape(CMP_STRIDE * 2 * HD, 2 * CMP_HID).astype(BF16)
    pw['cmp_w2'] = jnp.einsum('chd,ce->ched', cmp_w2, eye).reshape(2 * CMP_HID, 2 * HD).astype(BF16)
    pw['cmp_b1'] = cmp_b1.reshape(1, 2 * CMP_HID)
    pw['pe_lo'] = cmp_pe[:CMP_STRIDE].reshape(1, CMP_STRIDE * 2 * HD)
    pw['pe_hi'] = cmp_pe[CMP_STRIDE:].reshape(1, CMP_STRIDE * 2 * HD)
    pw['conv_w'] = conv_w
    pw['conv_b'] = conv_b.reshape(1, ML_W)
    pw['w_ml_q'], pw['w_ml_k'], pw['w_ml_v'] = (w.astype(BF16) for w in (w_ml_q, w_ml_k, w_ml_v))
    pw['w_ml_k_t'] = jnp.swapaxes(w_ml_k, 1, 2).astype(BF16)
    gate_bias = jnp.concatenate([b_igate, b_fgate])
    pw['gate_bias_lanes'] = jnp.pad(gate_bias, (0, LANES - 2 * ML_H)).reshape(1, LANES)
    pw['gate_bias_rows'] = gate_bias.reshape(2 * ML_H, 1)
    pw['ml_gain'] = ml_norm.reshape(1, ML_W)
    pw['norm_mem'] = norm_mem.reshape(1, D_MODEL)
    pw['w_mem_kv'] = w_mem_kv.astype(BF16)
    pw['mem_k_gain'] = mem_k_norm.reshape(1, MEM_HD)
    ex = np.zeros((N_BRANCH, LANES, NSA_H * HD), np.float32)
    for br in range(N_BRANCH):
        for gh in range(NSA_H):
            ex[br, _SLAB_G + br * NSA_H + gh, gh * HD:(gh + 1) * HD] = 1.0
    pw['gate_expand'] = jnp.asarray(ex, BF16)
    pw['w_branch'] = w_branch.astype(BF16)
    pw['w_out'] = w_out.astype(BF16)
    pw['norm_ffn'] = norm_ffn.reshape(1, D_MODEL)
    pw['w_ffn_in'] = w_ffn_in.astype(BF16)
    pw['w_ffn_out'] = w_ffn_out.astype(BF16)
    return pw


def _overlap(n_cmp_rows, n_slc_cols):
    ci = np.arange(n_cmp_rows)[:, None] * CMP_STRIDE
    sj = np.arange(n_slc_cols)[None, :] * SLC_BLOCK
    return ((ci < sj + SLC_BLOCK) & (ci + CMP_BLOCK > sj)).astype(np.float32)


def _pick_tile(n, pref):
    t = min(n, pref)
    while n % t:
        t //= 2
    return t


def _prompt_group(x, mem, pw):
    b, s, _ = x.shape
    t = b * s
    x2d = x.reshape(t, D_MODEL)
    tm = _pick_tile(s, 256)
    tabs = _rope_tables(jnp.arange(s, dtype=jnp.int32))
    qn, qrot, kvc, kvs, kvw, u, oraw, mq, slab = _project(x2d, tabs, pw, tm)
    kvcmp = _compress_prompt(kvc.reshape(b, s, KV_W), pw)
    r = s // CMP_STRIDE
    ovt = jnp.asarray(_overlap(r, HD).T, BF16)
    tq = _pick_tile(s, 256)
    ocmp, sel = _cmp_prompt(qn, kvcmp, ovt, b, s, tq)
    oslc = _flash_prompt(qrot, kvs.reshape(b, s, KV_W), sel, b, s, tq)
    owin = _flash_prompt(qrot, kvw.reshape(b, s, KV_W), None, b, s, tq)
    chunk = _pick_tile(s, 256)
    oml, c_new, n_new, m_new = _mlstm_prompt(u.reshape(b, s, ML_W), oraw.reshape(b, s, ML_W),
                                             slab.reshape(b, s, LANES), pw, chunk)
    t_mem = mem.shape[1]
    kvm = _mem_kv(mem.reshape(b * t_mem, D_MODEL), pw, _pick_tile(b * t_mem, 256))
    omem = _mem_prompt(mq, kvm.reshape(b, t_mem, 2 * MEM_W), b, s, _pick_tile(s, 512))
    x1 = _merge(x2d, slab, ocmp, oslc, owin, oml.reshape(t, ML_W), omem, pw, tm)
    y = _ffn(x1, pw, tm).reshape(b, s, D_MODEL)
    kv5 = lambda a, n: a.reshape(b, n, NSA_G, 2, HD)
    return dict(y=y, kv_cmp=kv5(kvc, s), kv_slc=kv5(kvs, s), kv_win=kvw.reshape(b, s, KV_W),
                kv_mem=kvm.reshape(b, t_mem, MEM_H, 2, MEM_HD), c=c_new, n=n_new, m=m_new[:, 0, :ML_H],
                u=u.reshape(b, s, ML_W))


def _sample_group(x, cache_cmp, cache_slc, cache_win, cache_mem, state_c, state_n, state_m, state_conv,
                  page_table, pw):
    db, ds, _ = x.shape
    assert ds == 1, "the sample kernels handle one new token per sequence"
    n_pages = page_table.shape[1]
    past = n_pages * PAGE_SIZE
    pos = past
    x2d = x.reshape(db, D_MODEL)
    tabs = _rope_tables(jnp.full((db,), pos, jnp.int32))
    qn, qrot, kvc, kvs, kvw, u, oraw, mq, slab = _project(x2d, tabs, pw, db)
    kvcmp = _compress_paged(cache_cmp.reshape(-1, PAGE_SIZE, KV_W), page_table, kvc.reshape(db, 1, KV_W), pw)
    n_cmp = kvcmp.shape[2]
    n_slc = -(-(past + ds) // SLC_BLOCK)
    nslp = -(-n_slc // LANES) * LANES
    ov = jnp.asarray(_overlap(n_cmp, nslp), BF16)
    ocmp, sel = _cmp_sample(qn.reshape(db, NSA_H, HD), kvcmp, ov, pos, n_slc)
    q3 = qrot.reshape(db, NSA_G, NSA_HPG, 1, HD)
    qaug = (q3 * jnp.eye(NSA_G, dtype=BF16)[None, :, None, :, None]).reshape(db, NSA_H, NSA_G, HD)
    qaug = jnp.concatenate([qaug, jnp.zeros_like(qaug)], axis=-1).reshape(db, NSA_H, KV_W)
    oslc = _slc_sample(qaug, sel, cache_slc.reshape(-1, PAGE_SIZE, KV_W), page_table, kvs.reshape(db, 1, KV_W))
    w_buf = cache_win.shape[1]
    owin, win_new = _win_sample(qaug, cache_win.reshape(db, w_buf, KV_W), kvw.reshape(db, 1, KV_W), pos, past - w_buf)
    ext = jnp.concatenate([state_conv, u.reshape(db, 1, ML_W)], axis=1)
    oml, c_new, n_new, m_new = _mlstm_sample(ext, oraw.reshape(db, 1, ML_W), slab.reshape(db, 1, LANES),
                                             state_c, state_n, state_m, pw)
    m3 = mq.reshape(db, MEM_H, 1, MEM_HD) * jnp.eye(MEM_H, dtype=BF16)[None, :, :, None]
    maug = jnp.concatenate([m3, jnp.zeros_like(m3)], axis=-1).reshape(db, MEM_H, 2 * MEM_W)
    maug = jnp.pad(maug, ((0, 0), (0, 8 - MEM_H), (0, 0)))
    t_mem = cache_mem.shape[1]
    omem = _mem_sample(maug, cache_mem.reshape(db, t_mem, 2 * MEM_W))[:, :MEM_H].reshape(db, MEM_W)
    x1 = _merge(x2d, slab, ocmp.reshape(db, NSA_H * HD), oslc.reshape(db, NSA_H * HD), owin.reshape(db, NSA_H * HD),
                oml.reshape(db, ML_W), omem, pw, db)
    y = _ffn(x1, pw, db).reshape(db, ds, D_MODEL)
    kv5 = lambda a: a.reshape(db, ds, NSA_G, 2, HD)
    return dict(y=y, kv_cmp=kv5(kvc), kv_slc=kv5(kvs), win=win_new.reshape(db, w_buf, NSA_G, 2, HD),
                c=c_new, n=n_new, m=m_new[:, 0, :ML_H], conv=ext[:, 1:])


def kernel(x_prompt, x_sample, cache_kv_cmp, cache_kv_slc, cache_kv_win, cache_kv_mem, state_C, state_n, state_m, state_conv, page_table, mem_prompt, norm_mix, w_in, q_norm, k_norm_cmp, k_norm_slc, k_norm_win, cmp_pe, cmp_w1, cmp_b1, cmp_w2, conv_w, conv_b, w_ml_q, w_ml_k, w_ml_v, b_igate, b_fgate, ml_norm, norm_mem, w_mem_kv, mem_q_norm, mem_k_norm, w_branch, w_out, norm_ffn, w_ffn_in, w_ffn_out):
    pw = _prepare_weights(norm_mix, w_in, q_norm, k_norm_cmp, k_norm_slc, k_norm_win, cmp_pe, cmp_w1, cmp_b1, cmp_w2,
                          conv_w, conv_b, w_ml_q, w_ml_k, w_ml_v, b_igate, b_fgate, ml_norm, norm_mem, w_mem_kv,
                          mem_q_norm, mem_k_norm, w_branch, w_out, norm_ffn, w_ffn_in, w_ffn_out)
    p = _prompt_group(x_prompt, mem_prompt, pw)
    s = _sample_group(x_sample, cache_kv_cmp, cache_kv_slc, cache_kv_win, cache_kv_mem, state_C, state_n, state_m,
                      state_conv, page_table, pw)
    b, seq = x_prompt.shape[:2]
    w_buf = cache_kv_win.shape[1]
    assert seq >= w_buf
    win_p = p['kv_win'][:, seq - w_buf:].reshape(b, w_buf, NSA_G, 2, HD)
    conv_p = p['u'][:, seq - (CONV_W - 1):]
    return (p['y'], s['y'], p['kv_cmp'], s['kv_cmp'], p['kv_slc'], s['kv_slc'], win_p, s['win'], p['kv_mem'],
            p['c'], s['c'], p['n'], s['n'], p['m'], s['m'], conv_p, s['conv'])
```

```python
import functools

import numpy as np
import jax
import jax.numpy as jnp
from jax import lax
from jax.experimental import pallas as pl
from jax.experimental.pallas import tpu as pltpu

F32 = jnp.float32
BF16 = jnp.bfloat16

D_MODEL = 1024
PAGE_SIZE = 128
NSA_H, NSA_G, NSA_HPG, HD = 8, 2, 4, 64
ROT_DIM = HD // 4
ROPE_THETA = 500000.0
CMP_BLOCK, CMP_STRIDE, CMP_HID = 32, 16, 128
SLC_BLOCK, N_SEL = 64, 16
WINDOW = 512
ML_H, ML_D, ML_W, CONV_W = 4, 128, 512, 4
MEM_H, MEM_HD, MEM_W = 4, 128, 512
N_BRANCH, BRANCH_W = 3, 512
FFN_HID = -(-8 * D_MODEL // (3 * 256)) * 256
EPS = 1e-6

KV_W = 2 * NSA_G * HD
CHUNK_W = CMP_STRIDE * KV_W
LANES = 128
MASKED = -1e30
UNSELECTED = -32768.0
VMEM_LIMIT = 56 * 1024 * 1024

_Q0, _KVC0, _KVS0, _KVW0, _U0, _O0, _MQ0, _SLAB0, _PROJ_W = 0, 512, 768, 1024, 1280, 1792, 2304, 2816, 2944
_SLAB_I, _SLAB_F, _SLAB_G = 0, 4, 8


def _dot(a, b):
    return jnp.dot(a, b, preferred_element_type=F32)


def _dot_nt(a, b):
    return lax.dot_general(a, b, (((1,), (1,)), ((), ())), preferred_element_type=F32)


def _split2(x):
    hi = x.astype(BF16)
    return hi, (x - hi.astype(F32)).astype(BF16)


def _split3(x):
    hi = x.astype(BF16)
    r = x - hi.astype(F32)
    mid = r.astype(BF16)
    return hi, mid, (r - mid.astype(F32)).astype(BF16)


def _dot_exact_rhs(parts, m):
    out = _dot(parts[0], m)
    for p in parts[1:]:
        out = out + _dot(p, m)
    return out


def _sigmoid(x):
    return 1.0 / (1.0 + jnp.exp(-x))


def _log_sigmoid(x):
    return jnp.minimum(x, 0.0) - jnp.log(1.0 + jnp.exp(-jnp.abs(x)))


def _gelu_tanh(x):
    return x * (0.5 * (1.0 + jnp.tanh(np.sqrt(2.0 / np.pi).astype(np.float32) * (x + 0.044715 * (x * x * x)))))


def _rms_rows(x, gain):
    ms = jnp.mean(x * x, axis=-1, keepdims=True)
    return x * lax.rsqrt(ms + EPS) * gain


def _iota(shape, dim):
    return lax.broadcasted_iota(jnp.int32, shape, dim)


def _cparams(sem, vmem=None):
    return pltpu.CompilerParams(dimension_semantics=sem, vmem_limit_bytes=vmem)


def _masked_softmax(s, mask):
    s = jnp.where(mask, s, -jnp.inf)
    m = jnp.max(s, axis=-1, keepdims=True)
    m = jnp.where(m > -jnp.inf, m, 0.0)
    p = jnp.exp(s - m)
    return p / jnp.maximum(jnp.sum(p, axis=-1, keepdims=True), jnp.finfo(F32).tiny)


def _rs_heads64(a, seg_ref, segt_ref, width):
    ss = _dot_exact_rhs(_split2(a * a), seg_ref[0:width, :])
    rs = lax.rsqrt(ss * (1.0 / HD) + EPS)
    return _dot_exact_rhs(_split3(rs), segt_ref[:, 0:width])


def _rope(a, ct, sa, sb, width):
    reps = width // LANES
    if reps > 1:
        ct, sa, sb = (jnp.concatenate([t] * reps, axis=1) for t in (ct, sa, sb))
    half = ROT_DIM // 2
    return a * ct + pltpu.roll(a, width - half, 1) * sa + pltpu.roll(a, half, 1) * sb


def _proj_kernel(x_ref, nm_ref, w_ref, qg_ref, ksg_ref, kwg_ref, mqg_ref, seg_ref, segt_ref,
                 ctq_ref, saq_ref, sbq_ref, ctk_ref, sak_ref, sbk_ref,
                 qn_o, qr_o, kvc_o, kvs_o, kvw_o, u_o, o_o, mq_o, slab_o, *fm_outs):
    xn = _rms_rows(x_ref[...], nm_ref[...]).astype(BF16)

    def seg(lo, hi):
        return _dot(xn, w_ref[:, lo:hi])

    q = seg(_Q0, _KVC0)
    qn = q * _rs_heads64(q, seg_ref, segt_ref, NSA_H * HD) * qg_ref[...]
    scale = HD ** -0.5
    qn_o[...] = (qn * scale).astype(BF16)
    qr = _rope(qn, ctq_ref[...], saq_ref[...], sbq_ref[...], NSA_H * HD) * scale
    qr_o[...] = qr.astype(BF16)
    kvc = seg(_KVC0, _KVS0)
    kvc_o[...] = kvc
    if fm_outs:
        fm_outs[0][0] = qr.T.astype(BF16)
        fm_outs[1][0] = kvc.T
    for n, (lo, g_ref, o_ref) in enumerate(((_KVS0, ksg_ref, kvs_o), (_KVW0, kwg_ref, kvw_o))):
        a = seg(lo, lo + KV_W)
        is_k = (_iota(a.shape, 1) & HD) == 0
        an = jnp.where(is_k, a * _rs_heads64(a, seg_ref, segt_ref, KV_W) * g_ref[...], a)
        ar = _rope(an, ctk_ref[...], sak_ref[...], sbk_ref[...], KV_W)
        o_ref[...] = ar
        if fm_outs:
            fm_outs[2 + n][0] = ar.T
    u_o[...] = seg(_U0, _O0)
    o_o[...] = seg(_O0, _MQ0)
    mq = seg(_MQ0, _SLAB0)
    heads = []
    for h in range(MEM_H):
        mh = mq[:, h * MEM_HD:(h + 1) * MEM_HD]
        heads.append(mh * lax.rsqrt(jnp.mean(mh * mh, axis=-1, keepdims=True) + EPS))
    mq_o[...] = (jnp.concatenate(heads, axis=1) * mqg_ref[...] * (MEM_HD ** -0.5)).astype(BF16)
    slab_o[...] = seg(_SLAB0, _PROJ_W)


def _project(x2d, tabs, pw, tm, seq_len=None):
    t = x2d.shape[0]
    ntab = tabs[0].shape[0] // tm
    row = lambda i: (i, 0)
    const = lambda i: (0, 0)
    tab = lambda i: (i % ntab, 0)
    full = lambda a: pl.BlockSpec(a.shape, const)
    consts = [pw['norm_mix'], pw['w_proj'], pw['q_gain'], pw['ks_gain'], pw['kw_gain'], pw['mq_gain'],
              pw['seg64'], pw['seg64_t']]
    widths = [(512, BF16), (512, BF16), (KV_W, F32), (KV_W, F32), (KV_W, F32), (ML_W, F32), (ML_W, F32),
              (MEM_W, BF16), (LANES, F32)]
    out_specs = [pl.BlockSpec((tm, w), row) for w, _ in widths]
    out_shape = [jax.ShapeDtypeStruct((t, w), d) for w, d in widths]
    if seq_len is not None:
        per_seq = seq_len // tm
        for w, d in ((NSA_H * HD, BF16), (KV_W, F32), (KV_W, F32), (KV_W, F32)):
            out_specs.append(pl.BlockSpec((1, w, tm), lambda i: (i // per_seq, 0, i % per_seq)))
            out_shape.append(jax.ShapeDtypeStruct((t // seq_len, w, seq_len), d))
    return pl.pallas_call(
        _proj_kernel,
        grid=(t // tm,),
        in_specs=[pl.BlockSpec((tm, D_MODEL), row)] + [full(a) for a in consts]
                 + [pl.BlockSpec((tm, LANES), tab)] * 6,
        out_specs=out_specs,
        out_shape=out_shape,
        compiler_params=_cparams(("arbitrary",), VMEM_LIMIT),
        name="project",
    )(x2d, *consts, *tabs)


def _compress_group(y_lo, y_hi, tail_hi, wlo, whi, b1, w2, gain, n_valid):
    r = y_lo.shape[0]
    a = _dot(y_lo, wlo)
    bh = _dot(y_hi, whi)
    bs = pltpu.roll(bh, r - 1, 0)
    if tail_hi is not None:
        bt = _dot(tail_hi, whi)
        bs = jnp.where(_iota(bs.shape, 0) == r - 1, bt, bs)
    hid = _gelu_tanh(a + bs + b1)
    out = _dot(hid.astype(BF16), w2)
    is_k = _iota(out.shape, 1) < HD
    ss = jnp.sum(jnp.where(is_k, out * out, 0.0), axis=-1, keepdims=True) * (1.0 / HD)
    kv = jnp.where(is_k, out * lax.rsqrt(ss + EPS) * gain, out)
    return jnp.where(_iota(kv.shape, 0) < n_valid, kv, 0.0).astype(BF16)


def _group_cols(load, g):
    return jnp.concatenate([load(j * KV_W + g * LANES, j * KV_W + (g + 1) * LANES) for j in range(CMP_STRIDE)], axis=1)


def _compress_prompt_kernel(y_ref, pe_lo, pe_hi, wlo, whi, b1, w2, gain, o_ref, *, n_valid):
    for g in range(NSA_G):
        yg = _group_cols(lambda lo, hi: y_ref[0, :, lo:hi], g)
        o_ref[0, g] = _compress_group((yg + pe_lo[...]).astype(BF16), (yg + pe_hi[...]).astype(BF16), None,
                                      wlo[...], whi[...], b1[...], w2[...], gain[...], n_valid)


def _compress_paged_kernel(pt_ref, *refs, n_pages, n_valid):
    pages = refs[:n_pages]
    tail_ref, perm_ref, pelo_t, pehi_t, pe_hi, wlo, whi, b1, w2, gain, o_ref, ylo, yhi = refs[n_pages:]
    cpp = PAGE_SIZE // CMP_STRIDE
    for k, page in enumerate(pages):
        pt = page[0]
        for table, dst in ((pelo_t, ylo), (pehi_t, yhi)):
            y = _dot_nt(perm_ref[...], (pt + table[...]).astype(BF16))
            for g in range(NSA_G):
                for j in range(CMP_STRIDE):
                    dst[g, k * cpp:(k + 1) * cpp, j * LANES:(j + 1) * LANES] = y[j * cpp:(j + 1) * cpp, g * LANES:(g + 1) * LANES]
    for g in range(NSA_G):
        tg = _group_cols(lambda lo, hi: tail_ref[0, :, lo:hi], g)
        o_ref[0, g] = _compress_group(ylo[g].astype(BF16), yhi[g].astype(BF16), (tg + pe_hi[...]).astype(BF16),
                                      wlo[...], whi[...], b1[...], w2[...], gain[...], n_valid)


def _compress_consts(pw):
    return [pw['cmp_wlo'], pw['cmp_whi'], pw['cmp_b1'], pw['cmp_w2'], pw['kc_gain']]


def _compress_prompt(kvc, pw):
    b, s, _ = kvc.shape
    r = s // CMP_STRIDE
    y = kvc.reshape(b, r, CHUNK_W)
    consts = [pw['pe_lo'], pw['pe_hi']] + _compress_consts(pw)
    return pl.pallas_call(
        functools.partial(_compress_prompt_kernel, n_valid=r - 1),
        grid=(b,),
        in_specs=[pl.BlockSpec((1, r, CHUNK_W), lambda i: (i, 0, 0))]
                 + [pl.BlockSpec(a.shape, lambda i: (0, 0)) for a in consts],
        out_specs=pl.BlockSpec((1, NSA_G, r, LANES), lambda i: (i, 0, 0, 0)),
        out_shape=jax.ShapeDtypeStruct((b, NSA_G, r, LANES), BF16),
        compiler_params=_cparams(("arbitrary",), VMEM_LIMIT),
        name="compress_prompt",
    )(y, *consts)


def _compress_paged(cache, page_table, kvc_new, pw):
    db, n_pages = page_table.shape
    chunks_per_page = PAGE_SIZE // CMP_STRIDE
    r = n_pages * chunks_per_page
    pool = _feature_major(cache)
    tail = jnp.pad(kvc_new, ((0, 0), (0, CMP_STRIDE - kvc_new.shape[1]), (0, 0))).reshape(db, 1, CHUNK_W)
    consts = [pw['chunk_perm'], pw['pe_lo_t'], pw['pe_hi_t'], pw['pe_hi']] + _compress_consts(pw)
    page_specs = [pl.BlockSpec((1, KV_W, PAGE_SIZE), functools.partial(lambda i, pt, k: (pt[i, k], 0, 0), k=k))
                  for k in range(n_pages)]
    return pl.pallas_call(
        functools.partial(_compress_paged_kernel, n_pages=n_pages, n_valid=r),
        grid_spec=pltpu.PrefetchScalarGridSpec(
            num_scalar_prefetch=1,
            grid=(db,),
            in_specs=page_specs + [pl.BlockSpec((1, 1, CHUNK_W), lambda i, pt: (i, 0, 0))]
                     + [pl.BlockSpec(a.shape, lambda i, pt: (0, 0)) for a in consts],
            out_specs=pl.BlockSpec((1, NSA_G, r, LANES), lambda i, pt: (i, 0, 0, 0)),
            scratch_shapes=[pltpu.VMEM((NSA_G, r, CMP_STRIDE * LANES), F32)] * 2,
        ),
        out_shape=jax.ShapeDtypeStruct((db, NSA_G, r, LANES), BF16),
        compiler_params=_cparams(("arbitrary",), VMEM_LIMIT),
        name="compress_paged",
    )(page_table, *([pool] * n_pages), tail, *consts)


def _rank_select(score, idx, n_blocks, axis):
    cnt = jnp.zeros(score.shape, jnp.int32)
    for i in range(n_blocks):
        other = score[i:i + 1, :] if axis == 0 else score[:, i:i + 1]
        ge = jnp.where(other >= score, 1, 0)
        gt = jnp.where(other > score, 1, 0)
        cnt = cnt + jnp.where(idx > i, ge, gt)
    return jnp.where(score > -jnp.inf, jnp.where(cnt < N_SEL, 0.0, UNSELECTED), UNSELECTED)


def _pair_lanes(even, odd):
    return jnp.where(_iota(even.shape, 1) < HD, pltpu.roll(even, HD, 1), odd)


def _cmp_prompt_kernel(q_ref, kv_ref, ovt_ref, o_ref, sel_ref, *, tq, n_slc):
    i = pl.program_id(2)
    q = q_ref[...]
    kv = kv_ref[0, 0]
    k = kv[:, 0:HD]
    r = kv.shape[0]
    pos_col = i * tq + _iota((tq, 1), 0)
    visible = (_iota((1, r), 1) * CMP_STRIDE + (CMP_BLOCK - 1)) <= pos_col
    psum = jnp.zeros((tq, r), F32)
    outs = []
    for h in range(NSA_HPG):
        p = _masked_softmax(_dot_nt(q[:, h * HD:(h + 1) * HD], k), visible)
        psum = psum + p
        outs.append(_dot(p.astype(BF16), kv))
    o_ref[...] = jnp.concatenate([_pair_lanes(outs[0], outs[1]), _pair_lanes(outs[2], outs[3])], axis=1)
    hi, lo = _split2(psum)
    imp = _dot_nt(ovt_ref[...], hi) + _dot_nt(ovt_ref[...], lo)
    blk = _iota(imp.shape, 0)
    pos_row = i * tq + _iota(imp.shape, 1)
    forced = (blk == jnp.right_shift(pos_row, 6)) | (blk == 0)
    score = jnp.where(forced, jnp.inf, jnp.where(blk * SLC_BLOCK <= pos_row, imp, -jnp.inf))
    sel_ref[0, 0] = _rank_select(score, blk, n_slc, 0).astype(BF16)


def _cmp_prompt(qn, kvcmp, ovt, b, s, tq):
    nq = s // tq
    r = kvcmp.shape[2]
    n_slc = s // SLC_BLOCK
    assert n_slc <= HD and ovt.shape == (HD, r) and SLC_BLOCK == 64
    return pl.pallas_call(
        functools.partial(_cmp_prompt_kernel, tq=tq, n_slc=n_slc),
        grid=(b, NSA_G, nq),
        in_specs=[pl.BlockSpec((tq, NSA_HPG * HD), lambda bi, g, i: (bi * nq + i, g)),
                  pl.BlockSpec((1, 1, r, LANES), lambda bi, g, i: (bi, g, 0, 0)),
                  pl.BlockSpec(ovt.shape, lambda bi, g, i: (0, 0))],
        out_specs=[pl.BlockSpec((tq, NSA_HPG * HD), lambda bi, g, i: (bi * nq + i, g)),
                   pl.BlockSpec((1, 1, HD, tq), lambda bi, g, i: (bi, g, 0, i))],
        out_shape=[jax.ShapeDtypeStruct((b * s, NSA_H * HD), F32),
                   jax.ShapeDtypeStruct((b, NSA_G, HD, s), BF16)],
        compiler_params=_cparams(("arbitrary",) * 3, VMEM_LIMIT),
        name="cmp_attend_select_prompt",
    )(qn, kvcmp, ovt)


def _cmp_sample_kernel(q_ref, kv_ref, ov_ref, o_ref, sel_ref, *, pos, n_slc):
    q = q_ref[0]
    r = kv_ref.shape[2]
    row = _iota((NSA_H, 1), 0)
    visible = (_iota((1, r), 1) * CMP_STRIDE + (CMP_BLOCK - 1)) <= pos
    kvs = [kv_ref[0, g] for g in range(NSA_G)]
    s = jnp.where(row < NSA_HPG, _dot_nt(q, kvs[0][:, 0:HD]), _dot_nt(q, kvs[1][:, 0:HD]))
    p = _masked_softmax(s, visible)
    pb = p.astype(BF16)
    o = jnp.where(row < NSA_HPG, _dot(pb, kvs[0]), _dot(pb, kvs[1]))
    o_ref[0] = o[:, HD:2 * HD]
    p0 = jnp.sum(p[0:NSA_HPG], axis=0, keepdims=True)
    p1 = jnp.sum(p[NSA_HPG:NSA_H], axis=0, keepdims=True)
    psum = jnp.where(row < NSA_HPG, p0, p1)
    imp = _dot_exact_rhs(_split2(psum), ov_ref[...])
    blk = _iota(imp.shape, 1)
    forced = (blk == pos // SLC_BLOCK) | (blk == 0)
    score = jnp.where(forced, jnp.inf, jnp.where(blk * SLC_BLOCK <= pos, imp, -jnp.inf))
    sel_ref[0] = _rank_select(score, blk, n_slc, 1)


def _cmp_sample(qn3, kvcmp, ov, pos, n_slc):
    db, _, r, _ = kvcmp.shape
    nslp = ov.shape[1]
    return pl.pallas_call(
        functools.partial(_cmp_sample_kernel, pos=pos, n_slc=n_slc),
        grid=(db,),
        in_specs=[pl.BlockSpec((1, NSA_H, HD), lambda i: (i, 0, 0)),
                  pl.BlockSpec((1, NSA_G, r, LANES), lambda i: (i, 0, 0, 0)),
                  pl.BlockSpec(ov.shape, lambda i: (0, 0))],
        out_specs=[pl.BlockSpec((1, NSA_H, HD), lambda i: (i, 0, 0)),
                   pl.BlockSpec((1, NSA_H, nslp), lambda i: (i, 0, 0))],
        out_shape=[jax.ShapeDtypeStruct((db, NSA_H, HD), F32),
                   jax.ShapeDtypeStruct((db, NSA_H, nslp), F32)],
        compiler_params=_cparams(("arbitrary",), VMEM_LIMIT),
        name="cmp_attend_select_sample",
    )(qn3, kvcmp, ov)


def _flash_prompt_kernel(*refs, tq, selected):
    if selected:
        qt_ref, kv_ref, kvt_ref, sel_ref, o_ref, kop, vop, qa_ref, m_ref, acc_ref = refs
    else:
        qt_ref, kv_ref, kvt_ref, o_ref, kop, vop, qa_ref, m_ref, acc_ref = refs
    i = pl.program_id(2)
    cols = NSA_HPG * tq
    n_tiles = vop.shape[0]

    @pl.when(i == 0)
    def _():
        kvf = kv_ref[0]
        if selected:
            lane = _iota(kvf.shape, 1)
            onehot = (lane - HD) == jnp.right_shift(_iota(kvf.shape, 0), 6)
            kop[...] = jnp.where(lane < HD, kvf, onehot.astype(F32)).astype(BF16)
        else:
            kop[...] = kvf.astype(BF16)
        for t in range(n_tiles):
            tile = kvt_ref[0, :, t * tq:(t + 1) * tq]
            vop[t] = jnp.where(_iota(tile.shape, 0) < HD, 1.0, tile).astype(BF16)

    extra = sel_ref[0, 0] if selected else jnp.zeros((HD, tq), BF16)
    for h in range(NSA_HPG):
        qa_ref[0:HD, h * tq:(h + 1) * tq] = qt_ref[0, h * HD:(h + 1) * HD, :]
        qa_ref[HD:2 * HD, h * tq:(h + 1) * tq] = extra
    m_ref[...] = jnp.full((1, cols), MASKED, F32)
    acc_ref[...] = jnp.zeros((LANES, cols), F32)

    def tile_step(j, masked):
        start = pl.multiple_of(j * tq, tq)
        for cb in range(cols // LANES):
            sl = slice(cb * LANES, (cb + 1) * LANES)
            s = _dot(kop[pl.ds(start, tq), :], qa_ref[:, sl])
            if masked:
                pos = i * tq + (cb * LANES) % tq + _iota((1, LANES), 1)
                kpos = j * tq + _iota((tq, 1), 0)
                valid = kpos <= pos
                if not selected:
                    valid = valid & ((pos - kpos) <= WINDOW)
                s = jnp.where(valid, s, MASKED)
            m_old = m_ref[:, sl]
            m_new = jnp.maximum(m_old, jnp.max(s, axis=0, keepdims=True))
            p = jnp.exp(s - m_new).astype(BF16)
            m_ref[:, sl] = m_new
            acc_ref[:, sl] = jnp.exp(m_old - m_new) * acc_ref[:, sl] + _dot(vop[j], p)

    if selected:
        def interior(j, carry):
            tile_step(j, False)
            return carry
        lax.fori_loop(0, i, interior, 0)
        tile_step(i, True)
    else:
        def banded(j, carry):
            tile_step(j, True)
            return carry
        lax.fori_loop(jnp.maximum(i - WINDOW // tq, 0), i + 1, banded, 0)
    oh = []
    for h in range(NSA_HPG):
        t = acc_ref[:, h * tq:(h + 1) * tq].T
        oh.append(t / t[:, 0:1])
    o_ref[...] = jnp.concatenate([_pair_lanes(oh[0], oh[1]), _pair_lanes(oh[2], oh[3])], axis=1)


def _flash_prompt(qrot_t, kv, kv_t, sel_t, b, s, tq):
    nq = s // tq
    assert WINDOW % tq == 0 and tq % LANES == 0
    selected = sel_t is not None
    in_specs = [pl.BlockSpec((1, NSA_HPG * HD, tq), lambda bi, g, i: (bi, g, i)),
                pl.BlockSpec((1, s, LANES), lambda bi, g, i: (bi, 0, g)),
                pl.BlockSpec((1, LANES, s), lambda bi, g, i: (bi, g, 0))]
    args = [qrot_t, kv, kv_t]
    cols = NSA_HPG * tq
    scratch = [pltpu.VMEM((s, LANES), BF16), pltpu.VMEM((s // tq, LANES, tq), BF16),
               pltpu.VMEM((LANES, cols), BF16), pltpu.VMEM((1, cols), F32), pltpu.VMEM((LANES, cols), F32)]
    if selected:
        in_specs.append(pl.BlockSpec((1, 1, HD, tq), lambda bi, g, i: (bi, g, 0, i)))
        args.append(sel_t)
    return pl.pallas_call(
        functools.partial(_flash_prompt_kernel, tq=tq, selected=selected),
        grid=(b, NSA_G, nq),
        in_specs=in_specs,
        out_specs=pl.BlockSpec((tq, NSA_HPG * HD), lambda bi, g, i: (bi * nq + i, g)),
        out_shape=jax.ShapeDtypeStruct((b * s, NSA_H * HD), F32),
        scratch_shapes=scratch,
        compiler_params=_cparams(("arbitrary",) * 3, VMEM_LIMIT),
        name="slc_attend_prompt" if selected else "win_attend_prompt",
    )(*args)


def _mem_kv_kernel(x_ref, nm_ref, w_ref, kg_ref, o_ref):
    xn = _rms_rows(x_ref[...], nm_ref[...]).astype(BF16)
    kv = _dot(xn, w_ref[...])
    cols = []
    for h in range(MEM_H):
        k = kv[:, 2 * h * MEM_HD:(2 * h + 1) * MEM_HD]
        cols.append(_rms_rows(k, kg_ref[...]))
        cols.append(kv[:, (2 * h + 1) * MEM_HD:(2 * h + 2) * MEM_HD])
    o_ref[...] = jnp.concatenate(cols, axis=1)


def _mem_kv(mem2d, pw, tm):
    t = mem2d.shape[0]
    consts = [pw['norm_mem'], pw['w_mem_kv'], pw['mem_k_gain']]
    return pl.pallas_call(
        _mem_kv_kernel,
        grid=(t // tm,),
        in_specs=[pl.BlockSpec((tm, D_MODEL), lambda i: (i, 0))] + [pl.BlockSpec(a.shape, lambda i: (0, 0)) for a in consts],
        out_specs=pl.BlockSpec((tm, 2 * MEM_W), lambda i: (i, 0)),
        out_shape=jax.ShapeDtypeStruct((t, 2 * MEM_W), F32),
        compiler_params=_cparams(("arbitrary",), VMEM_LIMIT),
        name="mem_kv",
    )(mem2d, *consts)


def _mem_prompt_kernel(q_ref, kv_ref, o_ref):
    kv = kv_ref[0].astype(BF16)
    s = _dot_nt(q_ref[...], kv[:, 0:MEM_HD])
    p = jnp.exp(s - jnp.max(s, axis=-1, keepdims=True))
    o_ref[...] = _dot(p.astype(BF16), kv[:, MEM_HD:2 * MEM_HD]) / jnp.sum(p, axis=-1, keepdims=True)


def _mem_prompt(mq, kvm, b, s, tq):
    nq = s // tq
    t_mem = kvm.shape[1]
    return pl.pallas_call(
        _mem_prompt_kernel,
        grid=(b, MEM_H, nq),
        in_specs=[pl.BlockSpec((tq, MEM_HD), lambda bi, h, i: (bi * nq + i, h)),
                  pl.BlockSpec((1, t_mem, 2 * MEM_HD), lambda bi, h, i: (bi, 0, h))],
        out_specs=pl.BlockSpec((tq, MEM_HD), lambda bi, h, i: (bi * nq + i, h)),
        out_shape=jax.ShapeDtypeStruct((b * s, MEM_W), F32),
        compiler_params=_cparams(("arbitrary",) * 3, VMEM_LIMIT),
        name="mem_attend_prompt",
    )(mq, kvm)


def _decode_attend(qa, blocks, new_row=None, new_bias=None, feature_major=False):
    qk, pv = (_dot, _dot_nt) if feature_major else (_dot_nt, _dot)
    scores = []
    for load, bias in blocks:
        s = qk(qa, load())
        scores.append(s if bias is None else s + bias)
    m = functools.reduce(jnp.maximum, [jnp.max(s, axis=-1, keepdims=True) for s in scores])
    if new_row is not None:
        nb = new_row.astype(BF16).astype(F32)
        s_new = jnp.sum(qa.astype(F32) * nb, axis=-1, keepdims=True) + new_bias
        m = jnp.maximum(m, s_new)
    ps = [jnp.exp(s - m) for s in scores]
    l = functools.reduce(lambda a, c: a + c, [jnp.sum(p, axis=-1, keepdims=True) for p in ps])
    acc = functools.reduce(lambda a, c: a + c, [pv(p.astype(BF16), load()) for p, (load, _) in zip(ps, blocks)])
    if new_row is not None:
        p_new = jnp.exp(s_new - m)
        l = l + p_new
        acc = acc + p_new.astype(BF16).astype(F32) * nb
    return acc / l


def _group_values(o):
    row = _iota((NSA_H, LANES), 0)
    both = jnp.where(row < NSA_HPG, o[:, 0:LANES], o[:, LANES:2 * LANES])
    return both[:, HD:2 * HD]


def _slc_sample_kernel(pt_ref, *refs, n_pages):
    pages = refs[:n_pages]
    q_ref, sel_ref, new_ref, o_ref = refs[n_pages:]
    sb = sel_ref[0]
    lane = _iota((NSA_H, PAGE_SIZE), 1)
    blocks = []
    for p, page in enumerate(pages):
        bias = jnp.where(lane < SLC_BLOCK, sb[:, 2 * p:2 * p + 1], sb[:, 2 * p + 1:2 * p + 2])
        blocks.append((lambda page=page: page[0].astype(BF16), bias))
    n_past = n_pages * (PAGE_SIZE // SLC_BLOCK)
    o = _decode_attend(q_ref[0], blocks, new_ref[0], sb[:, n_past:n_past + 1], feature_major=True)
    o_ref[0] = _group_values(o)


def _feature_major(cache):
    n, rows = cache.shape[:2]
    return jnp.transpose(cache, (0, 2, 3, 4, 1)).reshape(n, KV_W, rows)


def _slc_sample(qaug, sel, cache, page_table, kvs_new):
    db, n_pages = page_table.shape
    nslp = sel.shape[2]
    pool = _feature_major(cache)
    page_specs = [pl.BlockSpec((1, KV_W, PAGE_SIZE), functools.partial(lambda i, pt, k: (pt[i, k], 0, 0), k=k))
                  for k in range(n_pages)]
    return pl.pallas_call(
        functools.partial(_slc_sample_kernel, n_pages=n_pages),
        grid_spec=pltpu.PrefetchScalarGridSpec(
            num_scalar_prefetch=1,
            grid=(db,),
            in_specs=page_specs + [pl.BlockSpec((1, NSA_H, KV_W), lambda i, pt: (i, 0, 0)),
                                   pl.BlockSpec((1, NSA_H, nslp), lambda i, pt: (i, 0, 0)),
                                   pl.BlockSpec((1, 1, KV_W), lambda i, pt: (i, 0, 0))],
            out_specs=pl.BlockSpec((1, NSA_H, HD), lambda i, pt: (i, 0, 0)),
        ),
        out_shape=jax.ShapeDtypeStruct((db, NSA_H, HD), F32),
        compiler_params=_cparams(("arbitrary",), VMEM_LIMIT),
        name="slc_attend_sample",
    )(page_table, *([pool] * n_pages), qaug, sel, kvs_new)


def _win_sample_kernel(q_ref, buf_ref, new_ref, newcol_ref, o_ref, win_ref, *, pos, first_pos):
    buf = buf_ref[0]
    w_buf = buf.shape[1]
    d = pos - (first_pos + _iota((1, w_buf), 1))
    bias = jnp.where((d >= 0) & (d <= WINDOW), 0.0, MASKED)
    o = _decode_attend(q_ref[0], [(lambda: buf.astype(BF16), bias)], new_ref[0], 0.0, feature_major=True)
    o_ref[0] = _group_values(o)
    shifted = pltpu.roll(buf, w_buf - 1, 1)
    win_ref[0] = jnp.where(_iota(buf.shape, 1) == w_buf - 1, newcol_ref[0], shifted)


def _win_sample(qaug, win_buf_t, kvw_new, pos, first_pos):
    db, _, w_buf = win_buf_t.shape
    return pl.pallas_call(
        functools.partial(_win_sample_kernel, pos=pos, first_pos=first_pos),
        grid=(db,),
        in_specs=[pl.BlockSpec((1, NSA_H, KV_W), lambda i: (i, 0, 0)),
                  pl.BlockSpec((1, KV_W, w_buf), lambda i: (i, 0, 0)),
                  pl.BlockSpec((1, 1, KV_W), lambda i: (i, 0, 0)),
                  pl.BlockSpec((1, KV_W, 1), lambda i: (i, 0, 0))],
        out_specs=[pl.BlockSpec((1, NSA_H, HD), lambda i: (i, 0, 0)),
                   pl.BlockSpec((1, KV_W, w_buf), lambda i: (i, 0, 0))],
        out_shape=[jax.ShapeDtypeStruct((db, NSA_H, HD), F32),
                   jax.ShapeDtypeStruct((db, KV_W, w_buf), F32)],
        compiler_params=_cparams(("arbitrary",), VMEM_LIMIT),
        name="win_attend_sample",
    )(qaug, win_buf_t, kvw_new, kvw_new.reshape(db, KV_W, 1))


def _mem_sample_kernel(q_ref, kv_ref, o_ref):
    o = _decode_attend(q_ref[0], [(lambda: kv_ref[0].astype(BF16), None)])
    row = _iota((8, MEM_HD), 0)
    out = jnp.zeros((8, MEM_HD), F32)
    for h in range(MEM_H):
        out = jnp.where(row == h, o[:, (2 * h + 1) * MEM_HD:(2 * h + 2) * MEM_HD], out)
    o_ref[0] = out


def _mem_sample(qaug, kvm):
    db, t_mem, w = kvm.shape
    return pl.pallas_call(
        _mem_sample_kernel,
        grid=(db,),
        in_specs=[pl.BlockSpec((1, 8, w), lambda i: (i, 0, 0)),
                  pl.BlockSpec((1, t_mem, w), lambda i: (i, 0, 0))],
        out_specs=pl.BlockSpec((1, 8, MEM_HD), lambda i: (i, 0, 0)),
        out_shape=jax.ShapeDtypeStruct((db, 8, MEM_HD), F32),
        compiler_params=_cparams(("arbitrary",), VMEM_LIMIT),
        name="mem_attend_sample",
    )(qaug, kvm)


def _mlstm_prompt_kernel(u_ref, oraw_ref, slab_ref, slabt_ref, cw_ref, cb_ref, wq_ref, wk_ref, wv_ref,
                         bgc_ref, bgr_ref, mlg_ref, tril_ref, triu_ref,
                         o_ref, c_ref, n_ref, m_ref, ubuf, *, chunk):
    L = chunk
    pad = 8

    @pl.when(pl.program_id(1) == 0)
    def _():
        c_ref[...] = jnp.zeros_like(c_ref)
        n_ref[...] = jnp.zeros_like(n_ref)
        m_ref[...] = jnp.zeros_like(m_ref)
        ubuf[0:pad, :] = jnp.zeros((pad, ML_W), F32)

    u = u_ref[0]
    ubuf[pad:pad + L, :] = u
    conv = cb_ref[...] + cw_ref[CONV_W - 1:CONV_W, :] * u
    for w in range(CONV_W - 1):
        off = pad - (CONV_W - 1) + w
        conv = conv + cw_ref[w:w + 1, :] * ubuf[off:off + L, :]
    ubuf[0:pad, :] = u[L - pad:L, :]
    cact = conv * _sigmoid(conv)

    gl = slab_ref[0] + bgc_ref[...]
    gt = slabt_ref[0] + bgr_ref[...]
    b_c = None
    for part in _split3(_log_sigmoid(gl)):
        term = _dot(tril_ref[...], part)
        b_c = term if b_c is None else b_c + term
    b_r = _dot_exact_rhs(_split3(_log_sigmoid(gt)), triu_ref[...])
    m_prev = m_ref[0]
    tri = _iota((L, L), 0) >= _iota((L, L), 1)
    lane1 = _iota((1, LANES), 1)
    m_out = m_prev
    outs = []
    for h in range(ML_H):
        sl = slice(h * ML_D, (h + 1) * ML_D)
        bc = b_c[:, _SLAB_F + h:_SLAB_F + h + 1]
        ic = gl[:, _SLAB_I + h:_SLAB_I + h + 1]
        br = b_r[_SLAB_F + h:_SLAB_F + h + 1, :]
        ir = gt[_SLAB_I + h:_SLAB_I + h + 1, :]
        m0 = m_prev[:, h:h + 1]
        dm = jnp.where(tri, bc - br + ir, -jnp.inf)
        m_new = jnp.maximum(bc + m0, jnp.max(dm, axis=-1, keepdims=True))
        ch = cact[:, sl].astype(BF16)
        qh = _dot(ch, wq_ref[h]) * (ML_D ** -0.5)
        kh = _dot(ch, wk_ref[h])
        vh = _dot(u[:, sl].astype(BF16), wv_ref[h])
        qb, kb, vb = qh.astype(BF16), kh.astype(BF16), vh.astype(BF16)
        wmat = jnp.exp(dm - m_new) * _dot_nt(qb, kb)
        inter = jnp.exp(bc + m0 - m_new)
        c_old = c_ref[0, h]
        n_old = n_ref[0, h:h + 1, :]
        num = inter * _dot(qb, c_old.astype(BF16)) + _dot(wmat.astype(BF16), vb)
        den = inter * jnp.sum(qh * n_old, axis=-1, keepdims=True) + jnp.sum(wmat, axis=-1, keepdims=True)
        hh = num / jnp.maximum(jnp.abs(den), jnp.exp(-m_new))
        m_end = m_new[L - 1:L, :]
        b_end = bc[L - 1:L, :]
        decay = jnp.exp(b_end + m0 - m_end)
        kw = kh * jnp.exp(b_end - bc + ic - m_end)
        c_ref[0, h] = decay * c_old + _dot(kw.T.astype(BF16), vb)
        n_ref[0, h:h + 1, :] = decay * n_old + jnp.sum(kw, axis=0, keepdims=True)
        m_out = jnp.where(lane1 == h, m_end, m_out)
        hn = hh * lax.rsqrt(jnp.mean(hh * hh, axis=-1, keepdims=True) + EPS) * mlg_ref[:, sl]
        outs.append(_sigmoid(oraw_ref[0, :, sl]) * hn)
    o_ref[0] = jnp.concatenate(outs, axis=1)
    m_ref[0] = m_out


def _mlstm_prompt(u, oraw, slab, pw, chunk):
    b, s, _ = u.shape
    nc = s // chunk
    slabt = jnp.swapaxes(slab[:, :, 0:8], 1, 2)
    tril = jnp.tril(jnp.ones((chunk, chunk), BF16))
    consts = [pw['conv_w'], pw['conv_b'], pw['w_ml_q'], pw['w_ml_k'], pw['w_ml_v'], pw['gate_bias_lanes'],
              pw['gate_bias_rows'], pw['ml_gain'], tril, tril.T]
    blk = lambda w: pl.BlockSpec((1, chunk, w), lambda bi, c: (bi, c, 0))

    def const_spec(a):
        nd = a.ndim
        return pl.BlockSpec(a.shape, lambda bi, c: (0,) * nd)

    return pl.pallas_call(
        functools.partial(_mlstm_prompt_kernel, chunk=chunk),
        grid=(b, nc),
        in_specs=[blk(ML_W), blk(ML_W), blk(LANES), pl.BlockSpec((1, 8, chunk), lambda bi, c: (bi, 0, c))]
                 + [const_spec(a) for a in consts],
        out_specs=[blk(ML_W),
                   pl.BlockSpec((1, ML_H, ML_D, ML_D), lambda bi, c: (bi, 0, 0, 0)),
                   pl.BlockSpec((1, ML_H, ML_D), lambda bi, c: (bi, 0, 0)),
                   pl.BlockSpec((1, 1, LANES), lambda bi, c: (bi, 0, 0))],
        out_shape=[jax.ShapeDtypeStruct((b, s, ML_W), F32),
                   jax.ShapeDtypeStruct((b, ML_H, ML_D, ML_D), F32),
                   jax.ShapeDtypeStruct((b, ML_H, ML_D), F32),
                   jax.ShapeDtypeStruct((b, 1, LANES), F32)],
        scratch_shapes=[pltpu.VMEM((chunk + 8, ML_W), F32)],
        compiler_params=_cparams(("arbitrary", "arbitrary"), VMEM_LIMIT),
        name="mlstm_prompt",
    )(u, oraw, slab, slabt, *consts)


def _mlstm_sample_kernel(ext_ref, oraw_ref, slab_ref, c_ref, n_ref, m_ref, cw_ref, cb_ref, wq_ref, wk_ref,
                         wv_ref, wkt_ref, bgc_ref, mlg_ref, o_ref, co_ref, no_ref, mo_ref):
    ext = ext_ref[0]
    conv = cb_ref[...]
    for w in range(CONV_W):
        conv = conv + cw_ref[w:w + 1, :] * ext[w:w + 1, :]
    cact = conv * _sigmoid(conv)
    u = ext[CONV_W - 1:CONV_W, :]
    gl = slab_ref[0] + bgc_ref[...]
    lf_all = _log_sigmoid(gl)
    m_prev = m_ref[0]
    lane1 = _iota((1, LANES), 1)
    m_out = m_prev
    outs = []
    for h in range(ML_H):
        sl = slice(h * ML_D, (h + 1) * ML_D)
        ch8 = jnp.broadcast_to(cact[:, sl], (8, ML_D)).astype(BF16)
        u8 = jnp.broadcast_to(u[:, sl], (8, ML_D)).astype(BF16)
        q = (_dot(ch8, wq_ref[h]) * (ML_D ** -0.5))[0:1]
        k = _dot(ch8, wk_ref[h])[0:1]
        v = _dot(u8, wv_ref[h])[0:1]
        k_col = _dot_nt(wkt_ref[h], ch8)[:, 0:1]
        ig = gl[:, _SLAB_I + h:_SLAB_I + h + 1]
        lf = lf_all[:, _SLAB_F + h:_SLAB_F + h + 1]
        m0 = m_prev[:, h:h + 1]
        m_new = jnp.maximum(lf + m0, ig)
        qb = q.astype(BF16)
        qk = jnp.sum(qb.astype(F32) * k.astype(BF16).astype(F32), axis=-1, keepdims=True)
        w_in = jnp.exp(ig - m_new) * qk
        inter = jnp.exp(lf + m0 - m_new)
        c_old = c_ref[0, h]
        n_old = n_ref[0, h:h + 1, :]
        qc = _dot(jnp.broadcast_to(qb, (8, ML_D)), c_old.astype(BF16))[0:1]
        num = inter * qc + w_in * v
        den = inter * jnp.sum(q * n_old, axis=-1, keepdims=True) + w_in
        hh = num / jnp.maximum(jnp.abs(den), jnp.exp(-m_new))
        w_end = jnp.exp(ig - m_new)
        co_ref[0, h] = inter * c_old + k_col * (w_end * v)
        no_ref[0, h:h + 1, :] = inter * n_old + w_end * k
        m_out = jnp.where(lane1 == h, m_new, m_out)
        hn = hh * lax.rsqrt(jnp.mean(hh * hh, axis=-1, keepdims=True) + EPS) * mlg_ref[:, sl]
        outs.append(_sigmoid(oraw_ref[0, :, sl]) * hn)
    o_ref[0] = jnp.concatenate(outs, axis=1)
    mo_ref[0] = m_out


def _mlstm_sample(ext, oraw, slab, state_c, state_n, state_m, pw):
    db = ext.shape[0]
    m_in = jnp.pad(state_m, ((0, 0), (0, LANES - ML_H))).reshape(db, 1, LANES)
    consts = [pw['conv_w'], pw['conv_b'], pw['w_ml_q'], pw['w_ml_k'], pw['w_ml_v'], pw['w_ml_k_t'],
              pw['gate_bias_lanes'], pw['ml_gain']]

    def const_spec(a):
        nd = a.ndim
        return pl.BlockSpec(a.shape, lambda i: (0,) * nd)

    per_seq = lambda shape: pl.BlockSpec((1,) + shape, lambda i: (i,) + (0,) * len(shape))
    return pl.pallas_call(
        _mlstm_sample_kernel,
        grid=(db,),
        in_specs=[per_seq((CONV_W, ML_W)), per_seq((1, ML_W)), per_seq((1, LANES)), per_seq((ML_H, ML_D, ML_D)),
                  per_seq((ML_H, ML_D)), per_seq((1, LANES))] + [const_spec(a) for a in consts],
        out_specs=[per_seq((1, ML_W)), per_seq((ML_H, ML_D, ML_D)), per_seq((ML_H, ML_D)), per_seq((1, LANES))],
        out_shape=[jax.ShapeDtypeStruct((db, 1, ML_W), F32),
                   jax.ShapeDtypeStruct((db, ML_H, ML_D, ML_D), F32),
                   jax.ShapeDtypeStruct((db, ML_H, ML_D), F32),
                   jax.ShapeDtypeStruct((db, 1, LANES), F32)],
        compiler_params=_cparams(("arbitrary",), VMEM_LIMIT),
        name="mlstm_sample",
    )(ext, oraw, slab, state_c, state_n, m_in, *consts)


def _merge_kernel(x_ref, nm_ref, wg_ref, slab_ref, ex_ref, ocmp_ref, oslc_ref, owin_ref, oml_ref, omem_ref,
                  wb_ref, wout_ref, o_ref):
    x = x_ref[...]
    xn = _rms_rows(x, nm_ref[...]).astype(BF16)
    gparts = _split2(_sigmoid(slab_ref[...]))
    onsa = None
    for br, br_ref in enumerate((ocmp_ref, oslc_ref, owin_ref)):
        term = _dot_exact_rhs(gparts, ex_ref[br]) * br_ref[...]
        onsa = term if onsa is None else onsa + term
    z = None
    for n, on in enumerate((onsa, oml_ref[...], omem_ref[...])):
        gate = _sigmoid(_dot(xn, wg_ref[:, n * D_MODEL:(n + 1) * D_MODEL]))
        term = gate * _dot(on.astype(BF16), wb_ref[n])
        z = term if z is None else z + term
    o_ref[...] = x + _dot(z.astype(BF16), wout_ref[...])


def _merge(x2d, slab, ocmp, oslc, owin, oml, omem, pw, tm):
    t = x2d.shape[0]
    row = lambda w: pl.BlockSpec((tm, w), lambda i: (i, 0))

    def const_spec(a):
        nd = a.ndim
        return pl.BlockSpec(a.shape, lambda i: (0,) * nd)

    return pl.pallas_call(
        _merge_kernel,
        grid=(t // tm,),
        in_specs=[row(D_MODEL), const_spec(pw['norm_mix']), const_spec(pw['w_gate']), row(LANES),
                  const_spec(pw['gate_expand'])] + [row(BRANCH_W)] * 5
                 + [const_spec(pw['w_branch']), const_spec(pw['w_out'])],
        out_specs=row(D_MODEL),
        out_shape=jax.ShapeDtypeStruct((t, D_MODEL), F32),
        compiler_params=_cparams(("arbitrary",), VMEM_LIMIT),
        name="merge",
    )(x2d, pw['norm_mix'], pw['w_gate'], slab, pw['gate_expand'], ocmp, oslc, owin, oml, omem,
      pw['w_branch'], pw['w_out'])


def _ffn_kernel(x_ref, nf_ref, win_ref, wout_ref, o_ref):
    x = x_ref[...]
    xn = _rms_rows(x, nf_ref[...]).astype(BF16)
    a = _dot(xn, win_ref[:, 0:FFN_HID])
    b = _dot(xn, win_ref[:, FFN_HID:2 * FFN_HID])
    o_ref[...] = x + _dot((a * _sigmoid(a) * b).astype(BF16), wout_ref[...])


def _ffn(x2d, pw, tm):
    t = x2d.shape[0]
    consts = [pw['norm_ffn'], pw['w_ffn_in'], pw['w_ffn_out']]
    return pl.pallas_call(
        _ffn_kernel,
        grid=(t // tm,),
        in_specs=[pl.BlockSpec((tm, D_MODEL), lambda i: (i, 0))] + [pl.BlockSpec(a.shape, lambda i: (0, 0)) for a in consts],
        out_specs=pl.BlockSpec((tm, D_MODEL), lambda i: (i, 0)),
        out_shape=jax.ShapeDtypeStruct((t, D_MODEL), F32),
        compiler_params=_cparams(("arbitrary",), VMEM_LIMIT),
        name="ffn",
    )(x2d, *consts)


def _rope_tables(pos):
    half = ROT_DIM // 2
    freqs = ROPE_THETA ** (-jnp.arange(half, dtype=F32) / half)
    ang = pos.astype(F32)[:, None] * freqs
    cos, sin = jnp.cos(ang), jnp.sin(ang)
    n = pos.shape[0]
    ones = jnp.ones((n, HD - ROT_DIM), F32)
    zeros = jnp.zeros((n, HD - ROT_DIM), F32)
    zh = jnp.zeros((n, half), F32)
    ct = jnp.concatenate([cos, cos, ones], axis=1)
    sa = jnp.concatenate([-sin, zh, zeros], axis=1)
    sb = jnp.concatenate([zh, sin, zeros], axis=1)
    one64, zero64 = jnp.ones((n, HD), F32), jnp.zeros((n, HD), F32)
    q_tabs = [jnp.concatenate([t, t], axis=1) for t in (ct, sa, sb)]
    k_tabs = [jnp.concatenate([ct, one64], axis=1), jnp.concatenate([sa, zero64], axis=1),
              jnp.concatenate([sb, zero64], axis=1)]
    return q_tabs + k_tabs


def _prepare_weights(norm_mix, w_in, q_norm, k_norm_cmp, k_norm_slc, k_norm_win, cmp_pe, cmp_w1, cmp_b1, cmp_w2,
                     conv_w, conv_b, w_ml_q, w_ml_k, w_ml_v, b_igate, b_fgate, ml_norm, norm_mem, w_mem_kv,
                     mem_q_norm, mem_k_norm, w_branch, w_out, norm_ffn, w_ffn_in, w_ffn_out):
    widths = (NSA_H * HD, 3 * NSA_H, KV_W, KV_W, KV_W, ML_W, ML_W, ML_H, ML_H, MEM_W, N_BRANCH * D_MODEL)
    offs = np.concatenate([[0], np.cumsum(widths)])
    q, g, kvc, kvs, kvw, u, o, ig, fg, mq, mg = (w_in[:, int(offs[i]):int(offs[i + 1])] for i in range(len(widths)))
    slab_pad = jnp.zeros((D_MODEL, LANES - 2 * ML_H - 3 * NSA_H), F32)
    pw = {}
    pw['w_proj'] = jnp.concatenate([q, kvc, kvs, kvw, u, o, mq, ig, fg, g, slab_pad], axis=1).astype(BF16)
    pw['w_gate'] = mg.astype(BF16)
    pw['norm_mix'] = norm_mix.reshape(1, D_MODEL)
    pw['q_gain'] = jnp.tile(q_norm, NSA_H).reshape(1, NSA_H * HD)
    ones = jnp.ones((HD,), F32)
    pw['ks_gain'] = jnp.tile(jnp.concatenate([k_norm_slc, ones]), NSA_G).reshape(1, KV_W)
    pw['kw_gain'] = jnp.tile(jnp.concatenate([k_norm_win, ones]), NSA_G).reshape(1, KV_W)
    pw['kc_gain'] = jnp.concatenate([k_norm_cmp, ones]).reshape(1, LANES)
    pw['mq_gain'] = jnp.tile(mem_q_norm, MEM_H).reshape(1, MEM_W)
    seg = (np.arange(NSA_H * HD)[:, None] // HD == np.arange(LANES)[None, :]).astype(np.float32)
    pw['seg64'] = jnp.asarray(seg, BF16)
    pw['seg64_t'] = jnp.asarray(seg.T, BF16)
    w1 = cmp_w1.reshape(2, CMP_BLOCK, HD, CMP_HID)
    eye = jnp.eye(2, dtype=F32)
    for name, part in (('cmp_wlo', w1[:, :CMP_STRIDE]), ('cmp_whi', w1[:, CMP_STRIDE:])):
        pw[name] = jnp.einsum('cjdh,ce->jcdeh', part, eye).reshape(CMP_STRIDE * 2 * HD, 2 * CMP_HID).astype(BF16)
    pw['cmp_w2'] = jnp.einsum('chd,ce->ched', cmp_w2, eye).reshape(2 * CMP_HID, 2 * HD).astype(BF16)
    pw['cmp_b1'] = cmp_b1.reshape(1, 2 * CMP_HID)
    pw['pe_lo'] = cmp_pe[:CMP_STRIDE].reshape(1, CMP_STRIDE * 2 * HD)
    pw['pe_hi'] = cmp_pe[CMP_STRIDE:].reshape(1, CMP_STRIDE * 2 * HD)
    for name, part in (('pe_lo_t', cmp_pe[:CMP_STRIDE]), ('pe_hi_t', cmp_pe[CMP_STRIDE:])):
        cols = jnp.transpose(part, (1, 2, 0)).reshape(2 * HD, CMP_STRIDE)
        pw[name] = jnp.tile(cols, (NSA_G, PAGE_SIZE // CMP_STRIDE))
    cpp = PAGE_SIZE // CMP_STRIDE
    dst = np.arange(PAGE_SIZE)
    perm = (np.arange(PAGE_SIZE)[None, :] == ((dst % cpp) * CMP_STRIDE + dst // cpp)[:, None]).astype(np.float32)
    pw['chunk_perm'] = jnp.asarray(perm, BF16)
    pw['conv_w'] = conv_w
    pw['conv_b'] = conv_b.reshape(1, ML_W)
    pw['w_ml_q'], pw['w_ml_k'], pw['w_ml_v'] = (w.astype(BF16) for w in (w_ml_q, w_ml_k, w_ml_v))
    pw['w_ml_k_t'] = jnp.swapaxes(w_ml_k, 1, 2).astype(BF16)
    gate_bias = jnp.concatenate([b_igate, b_fgate])
    pw['gate_bias_lanes'] = jnp.pad(gate_bias, (0, LANES - 2 * ML_H)).reshape(1, LANES)
    pw['gate_bias_rows'] = gate_bias.reshape(2 * ML_H, 1)
    pw['ml_gain'] = ml_norm.reshape(1, ML_W)
    pw['norm_mem'] = norm_mem.reshape(1, D_MODEL)
    pw['w_mem_kv'] = w_mem_kv.astype(BF16)
    pw['mem_k_gain'] = mem_k_norm.reshape(1, MEM_HD)
    ex = np.zeros((N_BRANCH, LANES, NSA_H * HD), np.float32)
    for br in range(N_BRANCH):
        for gh in range(NSA_H):
            ex[br, _SLAB_G + br * NSA_H + gh, gh * HD:(gh + 1) * HD] = 1.0
    pw['gate_expand'] = jnp.asarray(ex, BF16)
    pw['w_branch'] = w_branch.astype(BF16)
    pw['w_out'] = w_out.astype(BF16)
    pw['norm_ffn'] = norm_ffn.reshape(1, D_MODEL)
    pw['w_ffn_in'] = w_ffn_in.astype(BF16)
    pw['w_ffn_out'] = w_ffn_out.astype(BF16)
    return pw


def _overlap(n_cmp_rows, n_slc_cols):
    ci = np.arange(n_cmp_rows)[:, None] * CMP_STRIDE
    sj = np.arange(n_slc_cols)[None, :] * SLC_BLOCK
    return ((ci < sj + SLC_BLOCK) & (ci + CMP_BLOCK > sj)).astype(np.float32)


def _pick_tile(n, pref):
    t = min(n, pref)
    while n % t:
        t //= 2
    return t


def _prompt_group(x, mem, pw):
    b, s, _ = x.shape
    t = b * s
    x2d = x.reshape(t, D_MODEL)
    tm = _pick_tile(s, 256)
    tabs = _rope_tables(jnp.arange(s, dtype=jnp.int32))
    qn, _, kvc, kvs, kvw, u, oraw, mq, slab, qrot_t, kvc_t, kvs_t, kvw_t = _project(x2d, tabs, pw, tm, seq_len=s)
    kvcmp = _compress_prompt(kvc.reshape(b, s, KV_W), pw)
    r = s // CMP_STRIDE
    ovt = jnp.asarray(_overlap(r, HD).T, BF16)
    tq = _pick_tile(s, 256)
    ocmp, sel_t = _cmp_prompt(qn, kvcmp, ovt, b, s, tq)
    oslc = _flash_prompt(qrot_t, kvs.reshape(b, s, KV_W), kvs_t, sel_t, b, s, tq)
    owin = _flash_prompt(qrot_t, kvw.reshape(b, s, KV_W), kvw_t, None, b, s, tq)
    chunk = _pick_tile(s, 256)
    oml, c_new, n_new, m_new = _mlstm_prompt(u.reshape(b, s, ML_W), oraw.reshape(b, s, ML_W),
                                             slab.reshape(b, s, LANES), pw, chunk)
    t_mem = mem.shape[1]
    kvm = _mem_kv(mem.reshape(b * t_mem, D_MODEL), pw, _pick_tile(b * t_mem, 256))
    omem = _mem_prompt(mq, kvm.reshape(b, t_mem, 2 * MEM_W), b, s, _pick_tile(s, 512))
    x1 = _merge(x2d, slab, ocmp, oslc, owin, oml.reshape(t, ML_W), omem, pw, tm)
    y = _ffn(x1, pw, tm).reshape(b, s, D_MODEL)
    kv5 = lambda a_t: jnp.transpose(a_t.reshape(b, NSA_G, 2, HD, a_t.shape[-1]), (0, 4, 1, 2, 3))
    return dict(y=y, kv_cmp=kv5(kvc_t), kv_slc=kv5(kvs_t), kv_win_t=kvw_t, kv5=kv5,
                kv_mem=kvm.reshape(b, t_mem, MEM_H, 2, MEM_HD), c=c_new, n=n_new, m=m_new[:, 0, :ML_H],
                u=u.reshape(b, s, ML_W))


def _sample_group(x, cache_cmp, cache_slc, cache_win, cache_mem, state_c, state_n, state_m, state_conv,
                  page_table, pw):
    db, ds, _ = x.shape
    assert ds == 1, "the sample kernels handle one new token per sequence"
    n_pages = page_table.shape[1]
    past = n_pages * PAGE_SIZE
    pos = past
    x2d = x.reshape(db, D_MODEL)
    tabs = _rope_tables(jnp.full((db,), pos, jnp.int32))
    qn, qrot, kvc, kvs, kvw, u, oraw, mq, slab = _project(x2d, tabs, pw, db)
    kvcmp = _compress_paged(cache_cmp, page_table, kvc.reshape(db, 1, KV_W), pw)
    n_cmp = kvcmp.shape[2]
    n_slc = -(-(past + ds) // SLC_BLOCK)
    nslp = -(-n_slc // LANES) * LANES
    ov = jnp.asarray(_overlap(n_cmp, nslp), BF16)
    ocmp, sel = _cmp_sample(qn.reshape(db, NSA_H, HD), kvcmp, ov, pos, n_slc)
    q3 = qrot.reshape(db, NSA_G, NSA_HPG, 1, HD)
    qaug = (q3 * jnp.eye(NSA_G, dtype=BF16)[None, :, None, :, None]).reshape(db, NSA_H, NSA_G, HD)
    qaug = jnp.concatenate([qaug, jnp.zeros_like(qaug)], axis=-1).reshape(db, NSA_H, KV_W)
    oslc = _slc_sample(qaug, sel, cache_slc, page_table, kvs.reshape(db, 1, KV_W))
    w_buf = cache_win.shape[1]
    owin, win_new_t = _win_sample(qaug, _feature_major(cache_win), kvw.reshape(db, 1, KV_W), pos, past - w_buf)
    win_new = jnp.transpose(win_new_t.reshape(db, NSA_G, 2, HD, w_buf), (0, 4, 1, 2, 3))
    ext = jnp.concatenate([state_conv, u.reshape(db, 1, ML_W)], axis=1)
    oml, c_new, n_new, m_new = _mlstm_sample(ext, oraw.reshape(db, 1, ML_W), slab.reshape(db, 1, LANES),
                                             state_c, state_n, state_m, pw)
    m3 = mq.reshape(db, MEM_H, 1, MEM_HD) * jnp.eye(MEM_H, dtype=BF16)[None, :, :, None]
    maug = jnp.concatenate([m3, jnp.zeros_like(m3)], axis=-1).reshape(db, MEM_H, 2 * MEM_W)
    maug = jnp.pad(maug, ((0, 0), (0, 8 - MEM_H), (0, 0)))
    t_mem = cache_mem.shape[1]
    omem = _mem_sample(maug, cache_mem.reshape(db, t_mem, 2 * MEM_W))[:, :MEM_H].reshape(db, MEM_W)
    x1 = _merge(x2d, slab, ocmp.reshape(db, NSA_H * HD), oslc.reshape(db, NSA_H * HD), owin.reshape(db, NSA_H * HD),
                oml.reshape(db, ML_W), omem, pw, db)
    y = _ffn(x1, pw, db).reshape(db, ds, D_MODEL)
    kv5 = lambda a: a.reshape(db, ds, NSA_G, 2, HD)
    return dict(y=y, kv_cmp=kv5(kvc), kv_slc=kv5(kvs), win=win_new,
                c=c_new, n=n_new, m=m_new[:, 0, :ML_H], conv=ext[:, 1:])


def kernel(x_prompt, x_sample, cache_kv_cmp, cache_kv_slc, cache_kv_win, cache_kv_mem, state_C, state_n, state_m, state_conv, page_table, mem_prompt, norm_mix, w_in, q_norm, k_norm_cmp, k_norm_slc, k_norm_win, cmp_pe, cmp_w1, cmp_b1, cmp_w2, conv_w, conv_b, w_ml_q, w_ml_k, w_ml_v, b_igate, b_fgate, ml_norm, norm_mem, w_mem_kv, mem_q_norm, mem_k_norm, w_branch, w_out, norm_ffn, w_ffn_in, w_ffn_out):
    pw = _prepare_weights(norm_mix, w_in, q_norm, k_norm_cmp, k_norm_slc, k_norm_win, cmp_pe, cmp_w1, cmp_b1, cmp_w2,
                          conv_w, conv_b, w_ml_q, w_ml_k, w_ml_v, b_igate, b_fgate, ml_norm, norm_mem, w_mem_kv,
                          mem_q_norm, mem_k_norm, w_branch, w_out, norm_ffn, w_ffn_in, w_ffn_out)
    p = _prompt_group(x_prompt, mem_prompt, pw)
    s = _sample_group(x_sample, cache_kv_cmp, cache_kv_slc, cache_kv_win, cache_kv_mem, state_C, state_n, state_m,
                      state_conv, page_table, pw)
    b, seq = x_prompt.shape[:2]
    w_buf = cache_kv_win.shape[1]
    assert seq >= w_buf
    win_p = p['kv5'](p['kv_win_t'][:, :, seq - w_buf:])
    conv_p = p['u'][:, seq - (CONV_W - 1):]
    return (p['y'], s['y'], p['kv_cmp'], s['kv_cmp'], p['kv_slc'], s['kv_slc'], win_p, s['win'], p['kv_mem'],
            p['c'], s['c'], p['n'], s['n'], p['m'], s['m'], conv_p, s['conv'])
```

```python
import functools

import numpy as np
import jax
import jax.numpy as jnp
from jax import lax
from jax.experimental import pallas as pl
from jax.experimental.pallas import tpu as pltpu

F32 = jnp.float32
BF16 = jnp.bfloat16

D_MODEL = 1024
PAGE_SIZE = 128
NSA_H, NSA_G, NSA_HPG, HD = 8, 2, 4, 64
ROT_DIM = HD // 4
ROPE_THETA = 500000.0
CMP_BLOCK, CMP_STRIDE, CMP_HID = 32, 16, 128
SLC_BLOCK, N_SEL = 64, 16
WINDOW = 512
ML_H, ML_D, ML_W, CONV_W = 4, 128, 512, 4
MEM_H, MEM_HD, MEM_W = 4, 128, 512
N_BRANCH, BRANCH_W = 3, 512
FFN_HID = -(-8 * D_MODEL // (3 * 256)) * 256
EPS = 1e-6

KV_W = 2 * NSA_G * HD
CHUNK_W = CMP_STRIDE * KV_W
LANES = 128
MASKED = -1e30
UNSELECTED = -32768.0
VMEM_LIMIT = 56 * 1024 * 1024
FLASH_TQ, FLASH_TK = 512, 256

_Q0, _KVC0, _KVS0, _KVW0, _U0, _O0, _MQ0, _SLAB0, _PROJ_W = 0, 512, 768, 1024, 1280, 1792, 2304, 2816, 2944
_SLAB_I, _SLAB_F, _SLAB_G = 0, 4, 8


def _dot(a, b):
    return jnp.dot(a, b, preferred_element_type=F32)


def _dot_nt(a, b):
    return lax.dot_general(a, b, (((1,), (1,)), ((), ())), preferred_element_type=F32)


def _split2(x):
    hi = x.astype(BF16)
    return hi, (x - hi.astype(F32)).astype(BF16)


def _split3(x):
    hi = x.astype(BF16)
    r = x - hi.astype(F32)
    mid = r.astype(BF16)
    return hi, mid, (r - mid.astype(F32)).astype(BF16)


def _dot_exact_rhs(parts, m):
    out = _dot(parts[0], m)
    for p in parts[1:]:
        out = out + _dot(p, m)
    return out


def _sigmoid(x):
    return 1.0 / (1.0 + jnp.exp(-x))


def _log_sigmoid(x):
    return jnp.minimum(x, 0.0) - jnp.log(1.0 + jnp.exp(-jnp.abs(x)))


def _gelu_tanh(x):
    return x * (0.5 * (1.0 + jnp.tanh(np.sqrt(2.0 / np.pi).astype(np.float32) * (x + 0.044715 * (x * x * x)))))


def _rms_rows(x, gain):
    ms = jnp.mean(x * x, axis=-1, keepdims=True)
    return x * lax.rsqrt(ms + EPS) * gain


def _iota(shape, dim):
    return lax.broadcasted_iota(jnp.int32, shape, dim)


def _cparams(sem, vmem=None):
    return pltpu.CompilerParams(dimension_semantics=sem, vmem_limit_bytes=vmem)


def _masked_softmax(s, mask):
    s = jnp.where(mask, s, -jnp.inf)
    m = jnp.max(s, axis=-1, keepdims=True)
    m = jnp.where(m > -jnp.inf, m, 0.0)
    p = jnp.exp(s - m)
    return p / jnp.maximum(jnp.sum(p, axis=-1, keepdims=True), jnp.finfo(F32).tiny)


def _rs_heads64(a, seg_ref, width):
    ss = _dot_exact_rhs(_split2(a * a), seg_ref[0:width, 0:width])
    return lax.rsqrt(ss * (1.0 / HD) + EPS)


def _rope(a, ct, sa, sb, width):
    reps = width // LANES
    if reps > 1:
        ct, sa, sb = (jnp.concatenate([t] * reps, axis=1) for t in (ct, sa, sb))
    half = ROT_DIM // 2
    return a * ct + pltpu.roll(a, width - half, 1) * sa + pltpu.roll(a, half, 1) * sb


def _proj_kernel(x_ref, nm_ref, w_ref, qg_ref, ksg_ref, kwg_ref, mqg_ref, seg_ref,
                 ctq_ref, saq_ref, sbq_ref, ctk_ref, sak_ref, sbk_ref,
                 qn_o, qr_o, kvc_o, kvs_o, kvw_o, u_o, o_o, mq_o, slab_o, *fm_outs):
    xn = _rms_rows(x_ref[...], nm_ref[...]).astype(BF16)

    def seg(lo, hi):
        return _dot(xn, w_ref[:, lo:hi])

    q = seg(_Q0, _KVC0)
    qn = q * _rs_heads64(q, seg_ref, NSA_H * HD) * qg_ref[...]
    scale = HD ** -0.5
    qn_o[...] = (qn * scale).astype(BF16)
    qr = _rope(qn, ctq_ref[...], saq_ref[...], sbq_ref[...], NSA_H * HD) * scale
    qr_o[...] = qr.astype(BF16)
    kvc = seg(_KVC0, _KVS0)
    kvc_o[...] = kvc
    if fm_outs:
        fm_outs[0][0] = qr.T.astype(BF16)
        fm_outs[1][0] = kvc.T
    for n, (lo, g_ref, o_ref) in enumerate(((_KVS0, ksg_ref, kvs_o), (_KVW0, kwg_ref, kvw_o))):
        a = seg(lo, lo + KV_W)
        is_k = (_iota(a.shape, 1) & HD) == 0
        an = jnp.where(is_k, a * _rs_heads64(a, seg_ref, KV_W) * g_ref[...], a)
        ar = _rope(an, ctk_ref[...], sak_ref[...], sbk_ref[...], KV_W)
        o_ref[...] = ar
        if fm_outs:
            fm_outs[2 + n][0] = ar.T
    u_o[...] = seg(_U0, _O0)
    o_o[...] = seg(_O0, _MQ0)
    mq = seg(_MQ0, _SLAB0)
    heads = []
    for h in range(MEM_H):
        mh = mq[:, h * MEM_HD:(h + 1) * MEM_HD]
        heads.append(mh * lax.rsqrt(jnp.mean(mh * mh, axis=-1, keepdims=True) + EPS))
    mq_o[...] = (jnp.concatenate(heads, axis=1) * mqg_ref[...] * (MEM_HD ** -0.5)).astype(BF16)
    slab_o[...] = seg(_SLAB0, _PROJ_W)


def _project(x2d, tabs, pw, tm, seq_len=None):
    t = x2d.shape[0]
    ntab = tabs[0].shape[0] // tm
    row = lambda i: (i, 0)
    const = lambda i: (0, 0)
    tab = lambda i: (i % ntab, 0)
    full = lambda a: pl.BlockSpec(a.shape, const)
    consts = [pw['norm_mix'], pw['w_proj'], pw['q_gain'], pw['ks_gain'], pw['kw_gain'], pw['mq_gain'],
              pw['seg64']]
    widths = [(512, BF16), (512, BF16), (KV_W, F32), (KV_W, F32), (KV_W, F32), (ML_W, F32), (ML_W, F32),
              (MEM_W, BF16), (LANES, F32)]
    out_specs = [pl.BlockSpec((tm, w), row) for w, _ in widths]
    out_shape = [jax.ShapeDtypeStruct((t, w), d) for w, d in widths]
    if seq_len is not None:
        per_seq = seq_len // tm
        for w, d in ((NSA_H * HD, BF16), (KV_W, F32), (KV_W, F32), (KV_W, F32)):
            out_specs.append(pl.BlockSpec((1, w, tm), lambda i: (i // per_seq, 0, i % per_seq)))
            out_shape.append(jax.ShapeDtypeStruct((t // seq_len, w, seq_len), d))
    return pl.pallas_call(
        _proj_kernel,
        grid=(t // tm,),
        in_specs=[pl.BlockSpec((tm, D_MODEL), row)] + [full(a) for a in consts]
                 + [pl.BlockSpec((tm, LANES), tab)] * 6,
        out_specs=out_specs,
        out_shape=out_shape,
        compiler_params=_cparams(("arbitrary",), VMEM_LIMIT),
        name="project",
    )(x2d, *consts, *tabs)


def _compress_group(y_lo, y_hi, tail_hi, wlo, whi, b1, w2, gain, n_valid):
    r = y_lo.shape[0]
    a = _dot(y_lo, wlo)
    bh = _dot(y_hi, whi)
    bs = pltpu.roll(bh, r - 1, 0)
    if tail_hi is not None:
        bt = _dot(tail_hi, whi)
        bs = jnp.where(_iota(bs.shape, 0) == r - 1, bt, bs)
    hid = _gelu_tanh(a + bs + b1)
    out = _dot(hid.astype(BF16), w2)
    is_k = _iota(out.shape, 1) < HD
    ss = jnp.sum(jnp.where(is_k, out * out, 0.0), axis=-1, keepdims=True) * (1.0 / HD)
    kv = jnp.where(is_k, out * lax.rsqrt(ss + EPS) * gain, out)
    return jnp.where(_iota(kv.shape, 0) < n_valid, kv, 0.0).astype(BF16)


def _group_cols(load, g):
    return jnp.concatenate([load(j * KV_W + g * LANES, j * KV_W + (g + 1) * LANES) for j in range(CMP_STRIDE)], axis=1)


def _compress_prompt_kernel(y_ref, pe_lo, pe_hi, wlo, whi, b1, w2, gain, o_ref, *, n_valid):
    for g in range(NSA_G):
        yg = _group_cols(lambda lo, hi: y_ref[0, :, lo:hi], g)
        o_ref[0, g] = _compress_group((yg + pe_lo[...]).astype(BF16), (yg + pe_hi[...]).astype(BF16), None,
                                      wlo[...], whi[...], b1[...], w2[...], gain[...], n_valid)


def _compress_paged_kernel(pt_ref, *refs, n_pages, n_valid):
    pages = refs[:n_pages]
    tail_ref, pe_lo, pe_hi, wlo, whi, b1, w2, gain, o_ref, y = refs[n_pages:]
    cpp = PAGE_SIZE // CMP_STRIDE
    for k, page in enumerate(pages):
        for g in range(NSA_G):
            nat = page[0, g * LANES:(g + 1) * LANES, :].T
            by_row = jnp.swapaxes(nat.reshape(cpp, CMP_STRIDE, LANES), 0, 1)
            for j in range(CMP_STRIDE):
                y[g, k * cpp:(k + 1) * cpp, j * LANES:(j + 1) * LANES] = by_row[j]
    for g in range(NSA_G):
        yg = y[g]
        tg = _group_cols(lambda lo, hi: tail_ref[0, :, lo:hi], g)
        o_ref[0, g] = _compress_group((yg + pe_lo[...]).astype(BF16), (yg + pe_hi[...]).astype(BF16),
                                      (tg + pe_hi[...]).astype(BF16), wlo[...], whi[...], b1[...], w2[...],
                                      gain[...], n_valid)


def _compress_consts(pw):
    return [pw['cmp_wlo'], pw['cmp_whi'], pw['cmp_b1'], pw['cmp_w2'], pw['kc_gain']]


def _compress_prompt(kvc, pw):
    b, s, _ = kvc.shape
    r = s // CMP_STRIDE
    y = kvc.reshape(b, r, CHUNK_W)
    consts = [pw['pe_lo'], pw['pe_hi']] + _compress_consts(pw)
    return pl.pallas_call(
        functools.partial(_compress_prompt_kernel, n_valid=r - 1),
        grid=(b,),
        in_specs=[pl.BlockSpec((1, r, CHUNK_W), lambda i: (i, 0, 0))]
                 + [pl.BlockSpec(a.shape, lambda i: (0, 0)) for a in consts],
        out_specs=pl.BlockSpec((1, NSA_G, r, LANES), lambda i: (i, 0, 0, 0)),
        out_shape=jax.ShapeDtypeStruct((b, NSA_G, r, LANES), BF16),
        compiler_params=_cparams(("arbitrary",), VMEM_LIMIT),
        name="compress_prompt",
    )(y, *consts)


def _compress_paged(cache, page_table, kvc_new, pw):
    db, n_pages = page_table.shape
    chunks_per_page = PAGE_SIZE // CMP_STRIDE
    r = n_pages * chunks_per_page
    pool = _feature_major(cache)
    tail = jnp.pad(kvc_new, ((0, 0), (0, CMP_STRIDE - kvc_new.shape[1]), (0, 0))).reshape(db, 1, CHUNK_W)
    consts = [pw['pe_lo'], pw['pe_hi']] + _compress_consts(pw)
    page_specs = [pl.BlockSpec((1, KV_W, PAGE_SIZE), functools.partial(lambda i, pt, k: (pt[i, k], 0, 0), k=k))
                  for k in range(n_pages)]
    return pl.pallas_call(
        functools.partial(_compress_paged_kernel, n_pages=n_pages, n_valid=r),
        grid_spec=pltpu.PrefetchScalarGridSpec(
            num_scalar_prefetch=1,
            grid=(db,),
            in_specs=page_specs + [pl.BlockSpec((1, 1, CHUNK_W), lambda i, pt: (i, 0, 0))]
                     + [pl.BlockSpec(a.shape, lambda i, pt: (0, 0)) for a in consts],
            out_specs=pl.BlockSpec((1, NSA_G, r, LANES), lambda i, pt: (i, 0, 0, 0)),
            scratch_shapes=[pltpu.VMEM((NSA_G, r, CMP_STRIDE * LANES), F32)],
        ),
        out_shape=jax.ShapeDtypeStruct((db, NSA_G, r, LANES), BF16),
        compiler_params=_cparams(("arbitrary",), VMEM_LIMIT),
        name="compress_paged",
    )(page_table, *([pool] * n_pages), tail, *consts)


def _rank_select(score, idx, n_blocks):
    cnt = jnp.zeros(score.shape, jnp.int32)
    for i in range(n_blocks):
        other = score[:, i:i + 1]
        ge = jnp.where(other >= score, 1, 0)
        gt = jnp.where(other > score, 1, 0)
        cnt = cnt + jnp.where(idx > i, ge, gt)
    return jnp.where(score > -jnp.inf, jnp.where(cnt < N_SEL, 0.0, UNSELECTED), UNSELECTED)


def _rank_select_rows(score, n_blocks):
    sub = _iota((8, score.shape[1]), 0)
    n_groups = -(-n_blocks // 8)
    groups = [score[g * 8:(g + 1) * 8, :] for g in range(n_groups)]
    cnts = [jnp.zeros(groups[0].shape, jnp.int32) for _ in groups]
    for i in range(n_blocks):
        other = score[i:i + 1, :]
        for g, sg in enumerate(groups):
            if g > i // 8:
                one = jnp.where(other >= sg, 1, 0)
            elif g < i // 8:
                one = jnp.where(other > sg, 1, 0)
            else:
                one = jnp.where(sub > i % 8, jnp.where(other >= sg, 1, 0), jnp.where(other > sg, 1, 0))
            cnts[g] = cnts[g] + one
    out = [jnp.where(sg > -jnp.inf, jnp.where(c < N_SEL, 0.0, UNSELECTED), UNSELECTED) for sg, c in zip(groups, cnts)]
    pad = score.shape[0] - 8 * n_groups
    if pad:
        out.append(jnp.full((pad, score.shape[1]), UNSELECTED, F32))
    return jnp.concatenate(out, axis=0)


def _pair_lanes(even, odd):
    return jnp.where(_iota(even.shape, 1) < HD, pltpu.roll(even, HD, 1), odd)


def _cmp_prompt_kernel(q_ref, kv_ref, ovt_ref, o_ref, sel_ref, *, tq, n_slc):
    i = pl.program_id(2)
    q = q_ref[...]
    kv = kv_ref[0, 0]
    k = kv[:, 0:HD]
    r = kv.shape[0]
    pos_col = i * tq + _iota((tq, 1), 0)
    visible = (_iota((1, r), 1) * CMP_STRIDE + (CMP_BLOCK - 1)) <= pos_col
    psum = jnp.zeros((tq, r), F32)
    outs = []
    for h in range(NSA_HPG):
        p = _masked_softmax(_dot_nt(q[:, h * HD:(h + 1) * HD], k), visible)
        psum = psum + p
        outs.append(_dot(p.astype(BF16), kv))
    o_ref[...] = jnp.concatenate([_pair_lanes(outs[0], outs[1]), _pair_lanes(outs[2], outs[3])], axis=1)
    hi, lo = _split2(psum)
    imp = _dot_nt(ovt_ref[...], hi) + _dot_nt(ovt_ref[...], lo)
    blk = _iota(imp.shape, 0)
    pos_row = i * tq + _iota(imp.shape, 1)
    forced = (blk == jnp.right_shift(pos_row, 6)) | (blk == 0)
    score = jnp.where(forced, jnp.inf, jnp.where(blk * SLC_BLOCK <= pos_row, imp, -jnp.inf))
    sel_ref[0, 0] = _rank_select_rows(score, n_slc).astype(BF16)


def _cmp_prompt(qn, kvcmp, ovt, b, s, tq):
    nq = s // tq
    r = kvcmp.shape[2]
    n_slc = s // SLC_BLOCK
    assert n_slc <= HD and ovt.shape == (HD, r) and SLC_BLOCK == 64
    return pl.pallas_call(
        functools.partial(_cmp_prompt_kernel, tq=tq, n_slc=n_slc),
        grid=(b, NSA_G, nq),
        in_specs=[pl.BlockSpec((tq, NSA_HPG * HD), lambda bi, g, i: (bi * nq + i, g)),
                  pl.BlockSpec((1, 1, r, LANES), lambda bi, g, i: (bi, g, 0, 0)),
                  pl.BlockSpec(ovt.shape, lambda bi, g, i: (0, 0))],
        out_specs=[pl.BlockSpec((tq, NSA_HPG * HD), lambda bi, g, i: (bi * nq + i, g)),
                   pl.BlockSpec((1, 1, HD, tq), lambda bi, g, i: (bi, g, 0, i))],
        out_shape=[jax.ShapeDtypeStruct((b * s, NSA_H * HD), F32),
                   jax.ShapeDtypeStruct((b, NSA_G, HD, s), BF16)],
        compiler_params=_cparams(("arbitrary",) * 3, VMEM_LIMIT),
        name="cmp_attend_select_prompt",
    )(qn, kvcmp, ovt)


def _cmp_sample_kernel(q_ref, kv_ref, ov_ref, o_ref, sel_ref, *, pos, n_slc):
    q = q_ref[0]
    r = kv_ref.shape[2]
    row = _iota((NSA_H, 1), 0)
    visible = (_iota((1, r), 1) * CMP_STRIDE + (CMP_BLOCK - 1)) <= pos
    kvs = [kv_ref[0, g] for g in range(NSA_G)]
    s = jnp.where(row < NSA_HPG, _dot_nt(q, kvs[0][:, 0:HD]), _dot_nt(q, kvs[1][:, 0:HD]))
    p = _masked_softmax(s, visible)
    pb = p.astype(BF16)
    o = jnp.where(row < NSA_HPG, _dot(pb, kvs[0]), _dot(pb, kvs[1]))
    o_ref[0] = o[:, HD:2 * HD]
    p0 = jnp.sum(p[0:NSA_HPG], axis=0, keepdims=True)
    p1 = jnp.sum(p[NSA_HPG:NSA_H], axis=0, keepdims=True)
    psum = jnp.where(row < NSA_HPG, p0, p1)
    imp = _dot_exact_rhs(_split2(psum), ov_ref[...])
    blk = _iota(imp.shape, 1)
    forced = (blk == pos // SLC_BLOCK) | (blk == 0)
    score = jnp.where(forced, jnp.inf, jnp.where(blk * SLC_BLOCK <= pos, imp, -jnp.inf))
    sel_ref[0] = _rank_select(score, blk, n_slc)


def _cmp_sample(qn3, kvcmp, ov, pos, n_slc):
    db, _, r, _ = kvcmp.shape
    nslp = ov.shape[1]
    return pl.pallas_call(
        functools.partial(_cmp_sample_kernel, pos=pos, n_slc=n_slc),
        grid=(db,),
        in_specs=[pl.BlockSpec((1, NSA_H, HD), lambda i: (i, 0, 0)),
                  pl.BlockSpec((1, NSA_G, r, LANES), lambda i: (i, 0, 0, 0)),
                  pl.BlockSpec(ov.shape, lambda i: (0, 0))],
        out_specs=[pl.BlockSpec((1, NSA_H, HD), lambda i: (i, 0, 0)),
                   pl.BlockSpec((1, NSA_H, nslp), lambda i: (i, 0, 0))],
        out_shape=[jax.ShapeDtypeStruct((db, NSA_H, HD), F32),
                   jax.ShapeDtypeStruct((db, NSA_H, nslp), F32)],
        compiler_params=_cparams(("arbitrary",), VMEM_LIMIT),
        name="cmp_attend_select_sample",
    )(qn3, kvcmp, ov)


def _flash_prompt_kernel(*refs, tq, selected):
    if selected:
        qt_ref, kv_ref, kvt_ref, sel_ref, o_ref, kop, vop, qa_ref, m_ref, acc_ref = refs
    else:
        qt_ref, kv_ref, kvt_ref, o_ref, kop, vop, qa_ref, m_ref, acc_ref = refs
    i = pl.program_id(2)
    cols = NSA_HPG * tq
    n_tiles, _, tk = vop.shape

    @pl.when(i == 0)
    def _():
        kvf = kv_ref[0]
        if selected:
            lane = _iota(kvf.shape, 1)
            onehot = (lane - HD) == jnp.right_shift(_iota(kvf.shape, 0), 6)
            kop[...] = jnp.where(lane < HD, kvf, onehot.astype(F32)).astype(BF16)
        else:
            kop[...] = kvf.astype(BF16)
        for t in range(n_tiles):
            tile = kvt_ref[0, :, t * tk:(t + 1) * tk]
            vop[t] = jnp.where(_iota(tile.shape, 0) < HD, 1.0, tile).astype(BF16)

    extra = sel_ref[0, 0] if selected else jnp.zeros((HD, tq), BF16)
    for h in range(NSA_HPG):
        qa_ref[0:HD, h * tq:(h + 1) * tq] = qt_ref[0, h * HD:(h + 1) * HD, :]
        qa_ref[HD:2 * HD, h * tq:(h + 1) * tq] = extra
    m_ref[...] = jnp.full((1, cols), MASKED, F32)
    acc_ref[...] = jnp.zeros((LANES, cols), F32)

    def tile_step(j, masked):
        start = pl.multiple_of(j * tk, tk)
        for cb in range(cols // LANES):
            sl = slice(cb * LANES, (cb + 1) * LANES)
            s = _dot(kop[pl.ds(start, tk), :], qa_ref[:, sl])
            if masked:
                pos = i * tq + (cb * LANES) % tq + _iota((1, LANES), 1)
                kpos = j * tk + _iota((tk, 1), 0)
                valid = kpos <= pos
                if not selected:
                    valid = valid & ((pos - kpos) <= WINDOW)
                s = jnp.where(valid, s, MASKED)
            m_old = m_ref[:, sl]
            m_new = jnp.maximum(m_old, jnp.max(s, axis=0, keepdims=True))
            p = jnp.exp(s - m_new).astype(BF16)
            m_ref[:, sl] = m_new
            acc_ref[:, sl] = jnp.exp(m_old - m_new) * acc_ref[:, sl] + _dot(vop[j], p)

    per_q = tq // tk

    def run(lo, hi, masked):
        def body(t, carry):
            for u in range(per_q):
                tile_step(t * per_q + u, masked)
            return carry
        lax.fori_loop(lo, hi, body, 0)

    if selected:
        run(0, i, False)
        run(i, i + 1, True)
    else:
        run(jnp.maximum(i - WINDOW // tq, 0), i + 1, True)
    oh = []
    for h in range(NSA_HPG):
        t = acc_ref[:, h * tq:(h + 1) * tq].T
        oh.append(t / t[:, 0:1])
    o_ref[...] = jnp.concatenate([_pair_lanes(oh[0], oh[1]), _pair_lanes(oh[2], oh[3])], axis=1)


def _flash_prompt(qrot_t, kv, kv_t, sel_t, b, s, tq, tk):
    nq = s // tq
    assert WINDOW % tq == 0 and tq % tk == 0 and tk % LANES == 0 and s % tq == 0
    selected = sel_t is not None
    in_specs = [pl.BlockSpec((1, NSA_HPG * HD, tq), lambda bi, g, i: (bi, g, i)),
                pl.BlockSpec((1, s, LANES), lambda bi, g, i: (bi, 0, g)),
                pl.BlockSpec((1, LANES, s), lambda bi, g, i: (bi, g, 0))]
    args = [qrot_t, kv, kv_t]
    cols = NSA_HPG * tq
    scratch = [pltpu.VMEM((s, LANES), BF16), pltpu.VMEM((s // tk, LANES, tk), BF16),
               pltpu.VMEM((LANES, cols), BF16), pltpu.VMEM((1, cols), F32), pltpu.VMEM((LANES, cols), F32)]
    if selected:
        in_specs.append(pl.BlockSpec((1, 1, HD, tq), lambda bi, g, i: (bi, g, 0, i)))
        args.append(sel_t)
    return pl.pallas_call(
        functools.partial(_flash_prompt_kernel, tq=tq, selected=selected),
        grid=(b, NSA_G, nq),
        in_specs=in_specs,
        out_specs=pl.BlockSpec((tq, NSA_HPG * HD), lambda bi, g, i: (bi * nq + i, g)),
        out_shape=jax.ShapeDtypeStruct((b * s, NSA_H * HD), F32),
        scratch_shapes=scratch,
        compiler_params=_cparams(("arbitrary",) * 3, VMEM_LIMIT),
        name="slc_attend_prompt" if selected else "win_attend_prompt",
    )(*args)


def _mem_kv_kernel(x_ref, nm_ref, w_ref, kg_ref, o_ref):
    xn = _rms_rows(x_ref[...], nm_ref[...]).astype(BF16)
    kv = _dot(xn, w_ref[...])
    cols = []
    for h in range(MEM_H):
        k = kv[:, 2 * h * MEM_HD:(2 * h + 1) * MEM_HD]
        cols.append(_rms_rows(k, kg_ref[...]))
        cols.append(kv[:, (2 * h + 1) * MEM_HD:(2 * h + 2) * MEM_HD])
    o_ref[...] = jnp.concatenate(cols, axis=1)


def _mem_kv(mem2d, pw, tm):
    t = mem2d.shape[0]
    consts = [pw['norm_mem'], pw['w_mem_kv'], pw['mem_k_gain']]
    return pl.pallas_call(
        _mem_kv_kernel,
        grid=(t // tm,),
        in_specs=[pl.BlockSpec((tm, D_MODEL), lambda i: (i, 0))] + [pl.BlockSpec(a.shape, lambda i: (0, 0)) for a in consts],
        out_specs=pl.BlockSpec((tm, 2 * MEM_W), lambda i: (i, 0)),
        out_shape=jax.ShapeDtypeStruct((t, 2 * MEM_W), F32),
        compiler_params=_cparams(("arbitrary",), VMEM_LIMIT),
        name="mem_kv",
    )(mem2d, *consts)


def _mem_prompt_kernel(q_ref, kv_ref, o_ref):
    kv = kv_ref[0].astype(BF16)
    s = _dot_nt(q_ref[...], kv[:, 0:MEM_HD])
    p = jnp.exp(s - jnp.max(s, axis=-1, keepdims=True))
    o_ref[...] = _dot(p.astype(BF16), kv[:, MEM_HD:2 * MEM_HD]) / jnp.sum(p, axis=-1, keepdims=True)


def _mem_prompt(mq, kvm, b, s, tq):
    nq = s // tq
    t_mem = kvm.shape[1]
    return pl.pallas_call(
        _mem_prompt_kernel,
        grid=(b, MEM_H, nq),
        in_specs=[pl.BlockSpec((tq, MEM_HD), lambda bi, h, i: (bi * nq + i, h)),
                  pl.BlockSpec((1, t_mem, 2 * MEM_HD), lambda bi, h, i: (bi, 0, h))],
        out_specs=pl.BlockSpec((tq, MEM_HD), lambda bi, h, i: (bi * nq + i, h)),
        out_shape=jax.ShapeDtypeStruct((b * s, MEM_W), F32),
        compiler_params=_cparams(("arbitrary",) * 3, VMEM_LIMIT),
        name="mem_attend_prompt",
    )(mq, kvm)


def _decode_attend(qa, blocks, new_row=None, new_bias=None, feature_major=False):
    qk, pv = (_dot, _dot_nt) if feature_major else (_dot_nt, _dot)
    scores = []
    for load, bias in blocks:
        s = qk(qa, load())
        scores.append(s if bias is None else s + bias)
    m = functools.reduce(jnp.maximum, [jnp.max(s, axis=-1, keepdims=True) for s in scores])
    if new_row is not None:
        nb = new_row.astype(BF16).astype(F32)
        s_new = jnp.sum(qa.astype(F32) * nb, axis=-1, keepdims=True) + new_bias
        m = jnp.maximum(m, s_new)
    ps = [jnp.exp(s - m) for s in scores]
    l = functools.reduce(lambda a, c: a + c, [jnp.sum(p, axis=-1, keepdims=True) for p in ps])
    acc = functools.reduce(lambda a, c: a + c, [pv(p.astype(BF16), load()) for p, (load, _) in zip(ps, blocks)])
    if new_row is not None:
        p_new = jnp.exp(s_new - m)
        l = l + p_new
        acc = acc + p_new.astype(BF16).astype(F32) * nb
    return acc / l


def _group_values(o):
    row = _iota((NSA_H, LANES), 0)
    both = jnp.where(row < NSA_HPG, o[:, 0:LANES], o[:, LANES:2 * LANES])
    return both[:, HD:2 * HD]


def _slc_sample_kernel(pt_ref, *refs, n_pages):
    pages = refs[:n_pages]
    q_ref, sel_ref, new_ref, o_ref = refs[n_pages:]
    sb = sel_ref[0]
    lane = _iota((NSA_H, PAGE_SIZE), 1)
    blocks = []
    for p, page in enumerate(pages):
        bias = jnp.where(lane < SLC_BLOCK, sb[:, 2 * p:2 * p + 1], sb[:, 2 * p + 1:2 * p + 2])
        blocks.append((lambda page=page: page[0].astype(BF16), bias))
    n_past = n_pages * (PAGE_SIZE // SLC_BLOCK)
    o = _decode_attend(q_ref[0], blocks, new_ref[0], sb[:, n_past:n_past + 1], feature_major=True)
    o_ref[0] = _group_values(o)


def _feature_major(cache):
    n, rows = cache.shape[:2]
    return jnp.transpose(cache, (0, 2, 3, 4, 1)).reshape(n, KV_W, rows)


def _slc_sample(qaug, sel, cache, page_table, kvs_new):
    db, n_pages = page_table.shape
    nslp = sel.shape[2]
    pool = _feature_major(cache)
    page_specs = [pl.BlockSpec((1, KV_W, PAGE_SIZE), functools.partial(lambda i, pt, k: (pt[i, k], 0, 0), k=k))
                  for k in range(n_pages)]
    return pl.pallas_call(
        functools.partial(_slc_sample_kernel, n_pages=n_pages),
        grid_spec=pltpu.PrefetchScalarGridSpec(
            num_scalar_prefetch=1,
            grid=(db,),
            in_specs=page_specs + [pl.BlockSpec((1, NSA_H, KV_W), lambda i, pt: (i, 0, 0)),
                                   pl.BlockSpec((1, NSA_H, nslp), lambda i, pt: (i, 0, 0)),
                                   pl.BlockSpec((1, 1, KV_W), lambda i, pt: (i, 0, 0))],
            out_specs=pl.BlockSpec((1, NSA_H, HD), lambda i, pt: (i, 0, 0)),
        ),
        out_shape=jax.ShapeDtypeStruct((db, NSA_H, HD), F32),
        compiler_params=_cparams(("arbitrary",), VMEM_LIMIT),
        name="slc_attend_sample",
    )(page_table, *([pool] * n_pages), qaug, sel, kvs_new)


def _win_sample_kernel(q_ref, buf_ref, new_ref, newcol_ref, o_ref, win_ref, *, pos, first_pos):
    buf = buf_ref[0]
    w_buf = buf.shape[1]
    d = pos - (first_pos + _iota((1, w_buf), 1))
    bias = jnp.where((d >= 0) & (d <= WINDOW), 0.0, MASKED)
    o = _decode_attend(q_ref[0], [(lambda: buf.astype(BF16), bias)], new_ref[0], 0.0, feature_major=True)
    o_ref[0] = _group_values(o)
    shifted = pltpu.roll(buf, w_buf - 1, 1)
    win_ref[0] = jnp.where(_iota(buf.shape, 1) == w_buf - 1, newcol_ref[0], shifted)


def _win_sample(qaug, win_buf_t, kvw_new, pos, first_pos):
    db, _, w_buf = win_buf_t.shape
    return pl.pallas_call(
        functools.partial(_win_sample_kernel, pos=pos, first_pos=first_pos),
        grid=(db,),
        in_specs=[pl.BlockSpec((1, NSA_H, KV_W), lambda i: (i, 0, 0)),
                  pl.BlockSpec((1, KV_W, w_buf), lambda i: (i, 0, 0)),
                  pl.BlockSpec((1, 1, KV_W), lambda i: (i, 0, 0)),
                  pl.BlockSpec((1, KV_W, 1), lambda i: (i, 0, 0))],
        out_specs=[pl.BlockSpec((1, NSA_H, HD), lambda i: (i, 0, 0)),
                   pl.BlockSpec((1, KV_W, w_buf), lambda i: (i, 0, 0))],
        out_shape=[jax.ShapeDtypeStruct((db, NSA_H, HD), F32),
                   jax.ShapeDtypeStruct((db, KV_W, w_buf), F32)],
        compiler_params=_cparams(("arbitrary",), VMEM_LIMIT),
        name="win_attend_sample",
    )(qaug, win_buf_t, kvw_new, kvw_new.reshape(db, KV_W, 1))


def _mem_sample_kernel(q_ref, kv_ref, o_ref):
    o = _decode_attend(q_ref[0], [(lambda: kv_ref[0].astype(BF16), None)])
    row = _iota((8, MEM_HD), 0)
    out = jnp.zeros((8, MEM_HD), F32)
    for h in range(MEM_H):
        out = jnp.where(row == h, o[:, (2 * h + 1) * MEM_HD:(2 * h + 2) * MEM_HD], out)
    o_ref[0] = out


def _mem_sample(qaug, kvm):
    db, t_mem, w = kvm.shape
    return pl.pallas_call(
        _mem_sample_kernel,
        grid=(db,),
        in_specs=[pl.BlockSpec((1, 8, w), lambda i: (i, 0, 0)),
                  pl.BlockSpec((1, t_mem, w), lambda i: (i, 0, 0))],
        out_specs=pl.BlockSpec((1, 8, MEM_HD), lambda i: (i, 0, 0)),
        out_shape=jax.ShapeDtypeStruct((db, 8, MEM_HD), F32),
        compiler_params=_cparams(("arbitrary",), VMEM_LIMIT),
        name="mem_attend_sample",
    )(qaug, kvm)


def _mlstm_prompt_kernel(u_ref, oraw_ref, slab_ref, slabt_ref, cw_ref, cb_ref, wq_ref, wk_ref, wv_ref,
                         bgc_ref, bgr_ref, mlg_ref, tril_ref, triu_ref,
                         o_ref, c_ref, n_ref, m_ref, ubuf, *, chunk):
    L = chunk
    pad = 8

    @pl.when(pl.program_id(1) == 0)
    def _():
        c_ref[...] = jnp.zeros_like(c_ref)
        n_ref[...] = jnp.zeros_like(n_ref)
        m_ref[...] = jnp.zeros_like(m_ref)
        ubuf[0:pad, :] = jnp.zeros((pad, ML_W), F32)

    u = u_ref[0]
    ubuf[pad:pad + L, :] = u
    conv = cb_ref[...] + cw_ref[CONV_W - 1:CONV_W, :] * u
    for w in range(CONV_W - 1):
        off = pad - (CONV_W - 1) + w
        conv = conv + cw_ref[w:w + 1, :] * ubuf[off:off + L, :]
    ubuf[0:pad, :] = u[L - pad:L, :]
    cact = conv * _sigmoid(conv)

    gl = slab_ref[0] + bgc_ref[...]
    gt = slabt_ref[0] + bgr_ref[...]
    b_c = None
    for part in _split3(_log_sigmoid(gl)):
        term = _dot(tril_ref[...], part)
        b_c = term if b_c is None else b_c + term
    b_r = _dot_exact_rhs(_split3(_log_sigmoid(gt)), triu_ref[...])
    m_prev = m_ref[0]
    tri = _iota((L, L), 0) >= _iota((L, L), 1)
    lane1 = _iota((1, LANES), 1)
    m_out = m_prev
    outs = []
    for h in range(ML_H):
        sl = slice(h * ML_D, (h + 1) * ML_D)
        bc = b_c[:, _SLAB_F + h:_SLAB_F + h + 1]
        ic = gl[:, _SLAB_I + h:_SLAB_I + h + 1]
        br = b_r[_SLAB_F + h:_SLAB_F + h + 1, :]
        ir = gt[_SLAB_I + h:_SLAB_I + h + 1, :]
        m0 = m_prev[:, h:h + 1]
        dm = jnp.where(tri, bc - br + ir, -jnp.inf)
        m_new = jnp.maximum(bc + m0, jnp.max(dm, axis=-1, keepdims=True))
        ch = cact[:, sl].astype(BF16)
        qh = _dot(ch, wq_ref[h]) * (ML_D ** -0.5)
        kh = _dot(ch, wk_ref[h])
        vh = _dot(u[:, sl].astype(BF16), wv_ref[h])
        qb, kb, vb = qh.astype(BF16), kh.astype(BF16), vh.astype(BF16)
        wmat = jnp.exp(dm - m_new) * _dot_nt(qb, kb)
        inter = jnp.exp(bc + m0 - m_new)
        c_old = c_ref[0, h]
        n_old = n_ref[0, h:h + 1, :]
        num = inter * _dot(qb, c_old.astype(BF16)) + _dot(wmat.astype(BF16), vb)
        den = inter * jnp.sum(qh * n_old, axis=-1, keepdims=True) + jnp.sum(wmat, axis=-1, keepdims=True)
        hh = num / jnp.maximum(jnp.abs(den), jnp.exp(-m_new))
        m_end = m_new[L - 1:L, :]
        b_end = bc[L - 1:L, :]
        decay = jnp.exp(b_end + m0 - m_end)
        kw = kh * jnp.exp(b_end - bc + ic - m_end)
        c_ref[0, h] = decay * c_old + _dot(kw.T.astype(BF16), vb)
        n_ref[0, h:h + 1, :] = decay * n_old + jnp.sum(kw, axis=0, keepdims=True)
        m_out = jnp.where(lane1 == h, m_end, m_out)
        hn = hh * lax.rsqrt(jnp.mean(hh * hh, axis=-1, keepdims=True) + EPS) * mlg_ref[:, sl]
        outs.append(_sigmoid(oraw_ref[0, :, sl]) * hn)
    o_ref[0] = jnp.concatenate(outs, axis=1)
    m_ref[0] = m_out


def _mlstm_prompt(u, oraw, slab, pw, chunk):
    b, s, _ = u.shape
    nc = s // chunk
    slabt = jnp.swapaxes(slab[:, :, 0:8], 1, 2)
    tril = jnp.tril(jnp.ones((chunk, chunk), BF16))
    consts = [pw['conv_w'], pw['conv_b'], pw['w_ml_q'], pw['w_ml_k'], pw['w_ml_v'], pw['gate_bias_lanes'],
              pw['gate_bias_rows'], pw['ml_gain'], tril, tril.T]
    blk = lambda w: pl.BlockSpec((1, chunk, w), lambda bi, c: (bi, c, 0))

    def const_spec(a):
        nd = a.ndim
        return pl.BlockSpec(a.shape, lambda bi, c: (0,) * nd)

    return pl.pallas_call(
        functools.partial(_mlstm_prompt_kernel, chunk=chunk),
        grid=(b, nc),
        in_specs=[blk(ML_W), blk(ML_W), blk(LANES), pl.BlockSpec((1, 8, chunk), lambda bi, c: (bi, 0, c))]
                 + [const_spec(a) for a in consts],
        out_specs=[blk(ML_W),
                   pl.BlockSpec((1, ML_H, ML_D, ML_D), lambda bi, c: (bi, 0, 0, 0)),
                   pl.BlockSpec((1, ML_H, ML_D), lambda bi, c: (bi, 0, 0)),
                   pl.BlockSpec((1, 1, LANES), lambda bi, c: (bi, 0, 0))],
        out_shape=[jax.ShapeDtypeStruct((b, s, ML_W), F32),
                   jax.ShapeDtypeStruct((b, ML_H, ML_D, ML_D), F32),
                   jax.ShapeDtypeStruct((b, ML_H, ML_D), F32),
                   jax.ShapeDtypeStruct((b, 1, LANES), F32)],
        scratch_shapes=[pltpu.VMEM((chunk + 8, ML_W), F32)],
        compiler_params=_cparams(("arbitrary", "arbitrary"), VMEM_LIMIT),
        name="mlstm_prompt",
    )(u, oraw, slab, slabt, *consts)


def _mlstm_sample_kernel(ext_ref, oraw_ref, slab_ref, c_ref, n_ref, m_ref, cw_ref, cb_ref, wq_ref, wk_ref,
                         wv_ref, wkt_ref, bgc_ref, mlg_ref, o_ref, co_ref, no_ref, mo_ref):
    ext = ext_ref[0]
    conv = cb_ref[...]
    for w in range(CONV_W):
        conv = conv + cw_ref[w:w + 1, :] * ext[w:w + 1, :]
    cact = conv * _sigmoid(conv)
    u = ext[CONV_W - 1:CONV_W, :]
    gl = slab_ref[0] + bgc_ref[...]
    lf_all = _log_sigmoid(gl)
    m_prev = m_ref[0]
    lane1 = _iota((1, LANES), 1)
    m_out = m_prev
    outs = []
    for h in range(ML_H):
        sl = slice(h * ML_D, (h + 1) * ML_D)
        ch8 = jnp.broadcast_to(cact[:, sl], (8, ML_D)).astype(BF16)
        u8 = jnp.broadcast_to(u[:, sl], (8, ML_D)).astype(BF16)
        q = (_dot(ch8, wq_ref[h]) * (ML_D ** -0.5))[0:1]
        k = _dot(ch8, wk_ref[h])[0:1]
        v = _dot(u8, wv_ref[h])[0:1]
        k_col = _dot_nt(wkt_ref[h], ch8)[:, 0:1]
        ig = gl[:, _SLAB_I + h:_SLAB_I + h + 1]
        lf = lf_all[:, _SLAB_F + h:_SLAB_F + h + 1]
        m0 = m_prev[:, h:h + 1]
        m_new = jnp.maximum(lf + m0, ig)
        qb = q.astype(BF16)
        qk = jnp.sum(qb.astype(F32) * k.astype(BF16).astype(F32), axis=-1, keepdims=True)
        w_in = jnp.exp(ig - m_new) * qk
        inter = jnp.exp(lf + m0 - m_new)
        c_old = c_ref[0, h]
        n_old = n_ref[0, h:h + 1, :]
        qc = _dot(jnp.broadcast_to(qb, (8, ML_D)), c_old.astype(BF16))[0:1]
        num = inter * qc + w_in * v
        den = inter * jnp.sum(q * n_old, axis=-1, keepdims=True) + w_in
        hh = num / jnp.maximum(jnp.abs(den), jnp.exp(-m_new))
        w_end = jnp.exp(ig - m_new)
        co_ref[0, h] = inter * c_old + k_col * (w_end * v)
        no_ref[0, h:h + 1, :] = inter * n_old + w_end * k
        m_out = jnp.where(lane1 == h, m_new, m_out)
        hn = hh * lax.rsqrt(jnp.mean(hh * hh, axis=-1, keepdims=True) + EPS) * mlg_ref[:, sl]
        outs.append(_sigmoid(oraw_ref[0, :, sl]) * hn)
    o_ref[0] = jnp.concatenate(outs, axis=1)
    mo_ref[0] = m_out


def _mlstm_sample(ext, oraw, slab, state_c, state_n, state_m, pw):
    db = ext.shape[0]
    m_in = jnp.pad(state_m, ((0, 0), (0, LANES - ML_H))).reshape(db, 1, LANES)
    consts = [pw['conv_w'], pw['conv_b'], pw['w_ml_q'], pw['w_ml_k'], pw['w_ml_v'], pw['w_ml_k_t'],
              pw['gate_bias_lanes'], pw['ml_gain']]

    def const_spec(a):
        nd = a.ndim
        return pl.BlockSpec(a.shape, lambda i: (0,) * nd)

    per_seq = lambda shape: pl.BlockSpec((1,) + shape, lambda i: (i,) + (0,) * len(shape))
    return pl.pallas_call(
        _mlstm_sample_kernel,
        grid=(db,),
        in_specs=[per_seq((CONV_W, ML_W)), per_seq((1, ML_W)), per_seq((1, LANES)), per_seq((ML_H, ML_D, ML_D)),
                  per_seq((ML_H, ML_D)), per_seq((1, LANES))] + [const_spec(a) for a in consts],
        out_specs=[per_seq((1, ML_W)), per_seq((ML_H, ML_D, ML_D)), per_seq((ML_H, ML_D)), per_seq((1, LANES))],
        out_shape=[jax.ShapeDtypeStruct((db, 1, ML_W), F32),
                   jax.ShapeDtypeStruct((db, ML_H, ML_D, ML_D), F32),
                   jax.ShapeDtypeStruct((db, ML_H, ML_D), F32),
                   jax.ShapeDtypeStruct((db, 1, LANES), F32)],
        compiler_params=_cparams(("arbitrary",), VMEM_LIMIT),
        name="mlstm_sample",
    )(ext, oraw, slab, state_c, state_n, m_in, *consts)


def _merge_kernel(x_ref, nm_ref, wg_ref, slab_ref, ex_ref, ocmp_ref, oslc_ref, owin_ref, oml_ref, omem_ref,
                  wb_ref, wout_ref, o_ref):
    x = x_ref[...]
    xn = _rms_rows(x, nm_ref[...]).astype(BF16)
    gparts = _split2(_sigmoid(slab_ref[...]))
    onsa = None
    for br, br_ref in enumerate((ocmp_ref, oslc_ref, owin_ref)):
        term = _dot_exact_rhs(gparts, ex_ref[br]) * br_ref[...]
        onsa = term if onsa is None else onsa + term
    z = None
    for n, on in enumerate((onsa, oml_ref[...], omem_ref[...])):
        gate = _sigmoid(_dot(xn, wg_ref[:, n * D_MODEL:(n + 1) * D_MODEL]))
        term = gate * _dot(on.astype(BF16), wb_ref[n])
        z = term if z is None else z + term
    o_ref[...] = x + _dot(z.astype(BF16), wout_ref[...])


def _merge(x2d, slab, ocmp, oslc, owin, oml, omem, pw, tm):
    t = x2d.shape[0]
    row = lambda w: pl.BlockSpec((tm, w), lambda i: (i, 0))

    def const_spec(a):
        nd = a.ndim
        return pl.BlockSpec(a.shape, lambda i: (0,) * nd)

    return pl.pallas_call(
        _merge_kernel,
        grid=(t // tm,),
        in_specs=[row(D_MODEL), const_spec(pw['norm_mix']), const_spec(pw['w_gate']), row(LANES),
                  const_spec(pw['gate_expand'])] + [row(BRANCH_W)] * 5
                 + [const_spec(pw['w_branch']), const_spec(pw['w_out'])],
        out_specs=row(D_MODEL),
        out_shape=jax.ShapeDtypeStruct((t, D_MODEL), F32),
        compiler_params=_cparams(("arbitrary",), VMEM_LIMIT),
        name="merge",
    )(x2d, pw['norm_mix'], pw['w_gate'], slab, pw['gate_expand'], ocmp, oslc, owin, oml, omem,
      pw['w_branch'], pw['w_out'])


def _ffn_kernel(x_ref, nf_ref, win_ref, wout_ref, o_ref):
    x = x_ref[...]
    xn = _rms_rows(x, nf_ref[...]).astype(BF16)
    a = _dot(xn, win_ref[:, 0:FFN_HID])
    b = _dot(xn, win_ref[:, FFN_HID:2 * FFN_HID])
    o_ref[...] = x + _dot((a * _sigmoid(a) * b).astype(BF16), wout_ref[...])


def _ffn(x2d, pw, tm):
    t = x2d.shape[0]
    consts = [pw['norm_ffn'], pw['w_ffn_in'], pw['w_ffn_out']]
    return pl.pallas_call(
        _ffn_kernel,
        grid=(t // tm,),
        in_specs=[pl.BlockSpec((tm, D_MODEL), lambda i: (i, 0))] + [pl.BlockSpec(a.shape, lambda i: (0, 0)) for a in consts],
        out_specs=pl.BlockSpec((tm, D_MODEL), lambda i: (i, 0)),
        out_shape=jax.ShapeDtypeStruct((t, D_MODEL), F32),
        compiler_params=_cparams(("arbitrary",), VMEM_LIMIT),
        name="ffn",
    )(x2d, *consts)


def _rope_tables(pos):
    half = ROT_DIM // 2
    freqs = ROPE_THETA ** (-jnp.arange(half, dtype=F32) / half)
    ang = pos.astype(F32)[:, None] * freqs
    cos, sin = jnp.cos(ang), jnp.sin(ang)
    n = pos.shape[0]
    ones = jnp.ones((n, HD - ROT_DIM), F32)
    zeros = jnp.zeros((n, HD - ROT_DIM), F32)
    zh = jnp.zeros((n, half), F32)
    ct = jnp.concatenate([cos, cos, ones], axis=1)
    sa = jnp.concatenate([-sin, zh, zeros], axis=1)
    sb = jnp.concatenate([zh, sin, zeros], axis=1)
    one64, zero64 = jnp.ones((n, HD), F32), jnp.zeros((n, HD), F32)
    q_tabs = [jnp.concatenate([t, t], axis=1) for t in (ct, sa, sb)]
    k_tabs = [jnp.concatenate([ct, one64], axis=1), jnp.concatenate([sa, zero64], axis=1),
              jnp.concatenate([sb, zero64], axis=1)]
    return q_tabs + k_tabs


def _prepare_weights(norm_mix, w_in, q_norm, k_norm_cmp, k_norm_slc, k_norm_win, cmp_pe, cmp_w1, cmp_b1, cmp_w2,
                     conv_w, conv_b, w_ml_q, w_ml_k, w_ml_v, b_igate, b_fgate, ml_norm, norm_mem, w_mem_kv,
                     mem_q_norm, mem_k_norm, w_branch, w_out, norm_ffn, w_ffn_in, w_ffn_out):
    widths = (NSA_H * HD, 3 * NSA_H, KV_W, KV_W, KV_W, ML_W, ML_W, ML_H, ML_H, MEM_W, N_BRANCH * D_MODEL)
    offs = np.concatenate([[0], np.cumsum(widths)])
    q, g, kvc, kvs, kvw, u, o, ig, fg, mq, mg = (w_in[:, int(offs[i]):int(offs[i + 1])] for i in range(len(widths)))
    slab_pad = jnp.zeros((D_MODEL, LANES - 2 * ML_H - 3 * NSA_H), F32)
    pw = {}
    pw['w_proj'] = jnp.concatenate([q, kvc, kvs, kvw, u, o, mq, ig, fg, g, slab_pad], axis=1).astype(BF16)
    pw['w_gate'] = mg.astype(BF16)
    pw['norm_mix'] = norm_mix.reshape(1, D_MODEL)
    pw['q_gain'] = jnp.tile(q_norm, NSA_H).reshape(1, NSA_H * HD)
    ones = jnp.ones((HD,), F32)
    pw['ks_gain'] = jnp.tile(jnp.concatenate([k_norm_slc, ones]), NSA_G).reshape(1, KV_W)
    pw['kw_gain'] = jnp.tile(jnp.concatenate([k_norm_win, ones]), NSA_G).reshape(1, KV_W)
    pw['kc_gain'] = jnp.concatenate([k_norm_cmp, ones]).reshape(1, LANES)
    pw['mq_gain'] = jnp.tile(mem_q_norm, MEM_H).reshape(1, MEM_W)
    head_of = np.arange(NSA_H * HD) // HD
    pw['seg64'] = jnp.asarray(head_of[:, None] == head_of[None, :], BF16)
    w1 = cmp_w1.reshape(2, CMP_BLOCK, HD, CMP_HID)
    eye = jnp.eye(2, dtype=F32)
    for name, part in (('cmp_wlo', w1[:, :CMP_STRIDE]), ('cmp_whi', w1[:, CMP_STRIDE:])):
        pw[name] = jnp.einsum('cjdh,ce->jcdeh', part, eye).reshape(CMP_STRIDE * 2 * HD, 2 * CMP_HID).astype(BF16)
    pw['cmp_w2'] = jnp.einsum('chd,ce->ched', cmp_w2, eye).reshape(2 * CMP_HID, 2 * HD).astype(BF16)
    pw['cmp_b1'] = cmp_b1.reshape(1, 2 * CMP_HID)
    pw['pe_lo'] = cmp_pe[:CMP_STRIDE].reshape(1, CMP_STRIDE * 2 * HD)
    pw['pe_hi'] = cmp_pe[CMP_STRIDE:].reshape(1, CMP_STRIDE * 2 * HD)
    pw['conv_w'] = conv_w
    pw['conv_b'] = conv_b.reshape(1, ML_W)
    pw['w_ml_q'], pw['w_ml_k'], pw['w_ml_v'] = (w.astype(BF16) for w in (w_ml_q, w_ml_k, w_ml_v))
    pw['w_ml_k_t'] = jnp.swapaxes(w_ml_k, 1, 2).astype(BF16)
    gate_bias = jnp.concatenate([b_igate, b_fgate])
    pw['gate_bias_lanes'] = jnp.pad(gate_bias, (0, LANES - 2 * ML_H)).reshape(1, LANES)
    pw['gate_bias_rows'] = gate_bias.reshape(2 * ML_H, 1)
    pw['ml_gain'] = ml_norm.reshape(1, ML_W)
    pw['norm_mem'] = norm_mem.reshape(1, D_MODEL)
    pw['w_mem_kv'] = w_mem_kv.astype(BF16)
    pw['mem_k_gain'] = mem_k_norm.reshape(1, MEM_HD)
    ex = np.zeros((N_BRANCH, LANES, NSA_H * HD), np.float32)
    for br in range(N_BRANCH):
        for gh in range(NSA_H):
            ex[br, _SLAB_G + br * NSA_H + gh, gh * HD:(gh + 1) * HD] = 1.0
    pw['gate_expand'] = jnp.asarray(ex, BF16)
    pw['w_branch'] = w_branch.astype(BF16)
    pw['w_out'] = w_out.astype(BF16)
    pw['norm_ffn'] = norm_ffn.reshape(1, D_MODEL)
    pw['w_ffn_in'] = w_ffn_in.astype(BF16)
    pw['w_ffn_out'] = w_ffn_out.astype(BF16)
    return pw


def _overlap(n_cmp_rows, n_slc_cols):
    ci = np.arange(n_cmp_rows)[:, None] * CMP_STRIDE
    sj = np.arange(n_slc_cols)[None, :] * SLC_BLOCK
    return ((ci < sj + SLC_BLOCK) & (ci + CMP_BLOCK > sj)).astype(np.float32)


def _pick_tile(n, pref):
    t = min(n, pref)
    while n % t:
        t //= 2
    return t


def _prompt_group(x, mem, pw):
    b, s, _ = x.shape
    t = b * s
    x2d = x.reshape(t, D_MODEL)
    tm = _pick_tile(s, 256)
    tabs = _rope_tables(jnp.arange(s, dtype=jnp.int32))
    qn, _, kvc, kvs, kvw, u, oraw, mq, slab, qrot_t, kvc_t, kvs_t, kvw_t = _project(x2d, tabs, pw, tm, seq_len=s)
    kvcmp = _compress_prompt(kvc.reshape(b, s, KV_W), pw)
    r = s // CMP_STRIDE
    ovt = jnp.asarray(_overlap(r, HD).T, BF16)
    tq = _pick_tile(s, 256)
    ocmp, sel_t = _cmp_prompt(qn, kvcmp, ovt, b, s, tq)
    fq, fk = _pick_tile(s, FLASH_TQ), _pick_tile(s, FLASH_TK)
    oslc = _flash_prompt(qrot_t, kvs.reshape(b, s, KV_W), kvs_t, sel_t, b, s, fq, fk)
    owin = _flash_prompt(qrot_t, kvw.reshape(b, s, KV_W), kvw_t, None, b, s, fq, fk)
    chunk = _pick_tile(s, 256)
    oml, c_new, n_new, m_new = _mlstm_prompt(u.reshape(b, s, ML_W), oraw.reshape(b, s, ML_W),
                                             slab.reshape(b, s, LANES), pw, chunk)
    t_mem = mem.shape[1]
    kvm = _mem_kv(mem.reshape(b * t_mem, D_MODEL), pw, _pick_tile(b * t_mem, 256))
    omem = _mem_prompt(mq, kvm.reshape(b, t_mem, 2 * MEM_W), b, s, _pick_tile(s, 512))
    x1 = _merge(x2d, slab, ocmp, oslc, owin, oml.reshape(t, ML_W), omem, pw, tm)
    y = _ffn(x1, pw, tm).reshape(b, s, D_MODEL)
    kv5 = lambda a_t: jnp.transpose(a_t.reshape(b, NSA_G, 2, HD, a_t.shape[-1]), (0, 4, 1, 2, 3))
    return dict(y=y, kv_cmp=kv5(kvc_t), kv_slc=kv5(kvs_t), kv_win_t=kvw_t, kv5=kv5,
                kv_mem=kvm.reshape(b, t_mem, MEM_H, 2, MEM_HD), c=c_new, n=n_new, m=m_new[:, 0, :ML_H],
                u=u.reshape(b, s, ML_W))


def _sample_group(x, cache_cmp, cache_slc, cache_win, cache_mem, state_c, state_n, state_m, state_conv,
                  page_table, pw):
    db, ds, _ = x.shape
    assert ds == 1, "the sample kernels handle one new token per sequence"
    n_pages = page_table.shape[1]
    past = n_pages * PAGE_SIZE
    pos = past
    x2d = x.reshape(db, D_MODEL)
    tabs = _rope_tables(jnp.full((db,), pos, jnp.int32))
    qn, qrot, kvc, kvs, kvw, u, oraw, mq, slab = _project(x2d, tabs, pw, db)
    kvcmp = _compress_paged(cache_cmp, page_table, kvc.reshape(db, 1, KV_W), pw)
    n_cmp = kvcmp.shape[2]
    n_slc = -(-(past + ds) // SLC_BLOCK)
    nslp = -(-n_slc // LANES) * LANES
    ov = jnp.asarray(_overlap(n_cmp, nslp), BF16)
    ocmp, sel = _cmp_sample(qn.reshape(db, NSA_H, HD), kvcmp, ov, pos, n_slc)
    q3 = qrot.reshape(db, NSA_G, NSA_HPG, 1, HD)
    qaug = (q3 * jnp.eye(NSA_G, dtype=BF16)[None, :, None, :, None]).reshape(db, NSA_H, NSA_G, HD)
    qaug = jnp.concatenate([qaug, jnp.zeros_like(qaug)], axis=-1).reshape(db, NSA_H, KV_W)
    oslc = _slc_sample(qaug, sel, cache_slc, page_table, kvs.reshape(db, 1, KV_W))
    w_buf = cache_win.shape[1]
    owin, win_new_t = _win_sample(qaug, _feature_major(cache_win), kvw.reshape(db, 1, KV_W), pos, past - w_buf)
    win_new = jnp.transpose(win_new_t.reshape(db, NSA_G, 2, HD, w_buf), (0, 4, 1, 2, 3))
    ext = jnp.concatenate([state_conv, u.reshape(db, 1, ML_W)], axis=1)
    oml, c_new, n_new, m_new = _mlstm_sample(ext, oraw.reshape(db, 1, ML_W), slab.reshape(db, 1, LANES),
                                             state_c, state_n, state_m, pw)
    m3 = mq.reshape(db, MEM_H, 1, MEM_HD) * jnp.eye(MEM_H, dtype=BF16)[None, :, :, None]
    maug = jnp.concatenate([m3, jnp.zeros_like(m3)], axis=-1).reshape(db, MEM_H, 2 * MEM_W)
    maug = jnp.pad(maug, ((0, 0), (0, 8 - MEM_H), (0, 0)))
    t_mem = cache_mem.shape[1]
    omem = _mem_sample(maug, cache_mem.reshape(db, t_mem, 2 * MEM_W))[:, :MEM_H].reshape(db, MEM_W)
    x1 = _merge(x2d, slab, ocmp.reshape(db, NSA_H * HD), oslc.reshape(db, NSA_H * HD), owin.reshape(db, NSA_H * HD),
                oml.reshape(db, ML_W), omem, pw, db)
    y = _ffn(x1, pw, db).reshape(db, ds, D_MODEL)
    kv5 = lambda a: a.reshape(db, ds, NSA_G, 2, HD)
    return dict(y=y, kv_cmp=kv5(kvc), kv_slc=kv5(kvs), win=win_new,
                c=c_new, n=n_new, m=m_new[:, 0, :ML_H], conv=ext[:, 1:])


def kernel(x_prompt, x_sample, cache_kv_cmp, cache_kv_slc, cache_kv_win, cache_kv_mem, state_C, state_n, state_m, state_conv, page_table, mem_prompt, norm_mix, w_in, q_norm, k_norm_cmp, k_norm_slc, k_norm_win, cmp_pe, cmp_w1, cmp_b1, cmp_w2, conv_w, conv_b, w_ml_q, w_ml_k, w_ml_v, b_igate, b_fgate, ml_norm, norm_mem, w_mem_kv, mem_q_norm, mem_k_norm, w_branch, w_out, norm_ffn, w_ffn_in, w_ffn_out):
    pw = _prepare_weights(norm_mix, w_in, q_norm, k_norm_cmp, k_norm_slc, k_norm_win, cmp_pe, cmp_w1, cmp_b1, cmp_w2,
                          conv_w, conv_b, w_ml_q, w_ml_k, w_ml_v, b_igate, b_fgate, ml_norm, norm_mem, w_mem_kv,
                          mem_q_norm, mem_k_norm, w_branch, w_out, norm_ffn, w_ffn_in, w_ffn_out)
    p = _prompt_group(x_prompt, mem_prompt, pw)
    s = _sample_group(x_sample, cache_kv_cmp, cache_kv_slc, cache_kv_win, cache_kv_mem, state_C, state_n, state_m,
                      state_conv, page_table, pw)
    b, seq = x_prompt.shape[:2]
    w_buf = cache_kv_win.shape[1]
    assert seq >= w_buf
    win_p = p['kv5'](p['kv_win_t'][:, :, seq - w_buf:])
    conv_p = p['u'][:, seq - (CONV_W - 1):]
    return (p['y'], s['y'], p['kv_cmp'], s['kv_cmp'], p['kv_slc'], s['kv_slc'], win_p, s['win'], p['kv_mem'],
            p['c'], s['c'], p['n'], s['n'], p['m'], s['m'], conv_p, s['conv'])
```

```python
import functools

import numpy as np
import jax
import jax.numpy as jnp
from jax import lax
from jax.experimental import pallas as pl
from jax.experimental.pallas import tpu as pltpu

F32 = jnp.float32
BF16 = jnp.bfloat16

D_MODEL = 1024
PAGE_SIZE = 128
NSA_H, NSA_G, NSA_HPG, HD = 8, 2, 4, 64
ROT_DIM = HD // 4
ROPE_THETA = 500000.0
CMP_BLOCK, CMP_STRIDE, CMP_HID = 32, 16, 128
SLC_BLOCK, N_SEL = 64, 16
WINDOW = 512
ML_H, ML_D, ML_W, CONV_W = 4, 128, 512, 4
MEM_H, MEM_HD, MEM_W = 4, 128, 512
N_BRANCH, BRANCH_W = 3, 512
FFN_HID = -(-8 * D_MODEL // (3 * 256)) * 256
EPS = 1e-6

KV_W = 2 * NSA_G * HD
CHUNK_W = CMP_STRIDE * KV_W
LANES = 128
MASKED = -1e30
UNSELECTED = -32768.0
VMEM_LIMIT = 56 * 1024 * 1024
TOKEN_TILE = 512
CMP_TQ = 512
FLASH_TQ, FLASH_TK = 512, 256
MLSTM_CHUNK = 256
MLSTM_SEQS_PER_STEP = 1

_Q0, _KVC0, _KVS0, _KVW0, _U0, _O0, _MQ0, _SLAB0, _PROJ_W = 0, 512, 768, 1024, 1280, 1792, 2304, 2816, 2944
_SLAB_I, _SLAB_F, _SLAB_G = 0, 4, 8


def _dot(a, b):
    return jnp.dot(a, b, preferred_element_type=F32)


def _dot_nt(a, b):
    return lax.dot_general(a, b, (((1,), (1,)), ((), ())), preferred_element_type=F32)


def _split2(x):
    hi = x.astype(BF16)
    return hi, (x - hi.astype(F32)).astype(BF16)


def _split3(x):
    hi = x.astype(BF16)
    r = x - hi.astype(F32)
    mid = r.astype(BF16)
    return hi, mid, (r - mid.astype(F32)).astype(BF16)


def _dot_exact_rhs(parts, m):
    out = _dot(parts[0], m)
    for p in parts[1:]:
        out = out + _dot(p, m)
    return out


def _dot_exact_rhs_left(m, parts):
    out = _dot(m, parts[0])
    for p in parts[1:]:
        out = out + _dot(m, p)
    return out


def _sigmoid(x):
    return 1.0 / (1.0 + jnp.exp(-x))


def _log_sigmoid(x):
    return jnp.minimum(x, 0.0) - jnp.log(1.0 + jnp.exp(-jnp.abs(x)))


def _gelu_tanh(x):
    return x * (0.5 * (1.0 + jnp.tanh(np.sqrt(2.0 / np.pi).astype(np.float32) * (x + 0.044715 * (x * x * x)))))


def _rms_rows(x, gain):
    ms = jnp.mean(x * x, axis=-1, keepdims=True)
    return x * lax.rsqrt(ms + EPS) * gain


def _iota(shape, dim):
    return lax.broadcasted_iota(jnp.int32, shape, dim)


def _cparams(sem, vmem=None):
    return pltpu.CompilerParams(dimension_semantics=sem, vmem_limit_bytes=vmem)


def _resident_spec(a):
    nd = a.ndim
    return pl.BlockSpec(a.shape, lambda *_: (0,) * nd, pipeline_mode=pl.Buffered(1))


def _masked_softmax(s, mask):
    s = jnp.where(mask, s, -jnp.inf)
    m = jnp.max(s, axis=-1, keepdims=True)
    m = jnp.where(m > -jnp.inf, m, 0.0)
    p = jnp.exp(s - m)
    return p / jnp.maximum(jnp.sum(p, axis=-1, keepdims=True), jnp.finfo(F32).tiny)


def _rs_heads64(a, seg_ref, width):
    ss = _dot_exact_rhs(_split2(a * a), seg_ref[0:width, 0:width])
    return lax.rsqrt(ss * (1.0 / HD) + EPS)


def _rope(a, ct, sa, sb, width):
    reps = width // LANES
    if reps > 1:
        ct, sa, sb = (jnp.concatenate([t] * reps, axis=1) for t in (ct, sa, sb))
    half = ROT_DIM // 2
    return a * ct + pltpu.roll(a, width - half, 1) * sa + pltpu.roll(a, half, 1) * sb


def _proj_kernel(x_ref, nm_ref, w_ref, qg_ref, ksg_ref, kwg_ref, mqg_ref, seg_ref,
                 ctq_ref, saq_ref, sbq_ref, ctk_ref, sak_ref, sbk_ref,
                 qn_o, qr_o, kvc_o, kvs_o, kvw_o, u_o, o_o, mq_o, slab_o, *fm_outs):
    xn = _rms_rows(x_ref[...], nm_ref[...]).astype(BF16)

    def seg(lo, hi):
        return _dot(xn, w_ref[:, lo:hi])

    q = seg(_Q0, _KVC0)
    qn = q * _rs_heads64(q, seg_ref, NSA_H * HD) * qg_ref[...]
    scale = HD ** -0.5
    qn_o[...] = (qn * scale).astype(BF16)
    qr = _rope(qn, ctq_ref[...], saq_ref[...], sbq_ref[...], NSA_H * HD) * scale
    qr_o[...] = qr.astype(BF16)
    kvc = seg(_KVC0, _KVS0)
    kvc_o[...] = kvc
    if fm_outs:
        fm_outs[0][0] = qr.T.astype(BF16)
        fm_outs[1][0] = kvc.T
        fm_outs[4][0] = (qn * scale).T.astype(BF16)
    for n, (lo, g_ref, o_ref) in enumerate(((_KVS0, ksg_ref, kvs_o), (_KVW0, kwg_ref, kvw_o))):
        a = seg(lo, lo + KV_W)
        is_k = (_iota(a.shape, 1) & HD) == 0
        an = jnp.where(is_k, a * _rs_heads64(a, seg_ref, KV_W) * g_ref[...], a)
        ar = _rope(an, ctk_ref[...], sak_ref[...], sbk_ref[...], KV_W)
        o_ref[...] = ar
        if fm_outs:
            fm_outs[2 + n][0] = ar.T
    u_o[...] = seg(_U0, _O0)
    o_o[...] = seg(_O0, _MQ0)
    mq = seg(_MQ0, _SLAB0)
    heads = []
    for h in range(MEM_H):
        mh = mq[:, h * MEM_HD:(h + 1) * MEM_HD]
        heads.append(mh * lax.rsqrt(jnp.mean(mh * mh, axis=-1, keepdims=True) + EPS))
    mqn = jnp.concatenate(heads, axis=1) * mqg_ref[...] * (MEM_HD ** -0.5)
    mq_o[...] = mqn.astype(BF16)
    if fm_outs:
        fm_outs[5][0] = mqn.T.astype(BF16)
    slab_o[...] = seg(_SLAB0, _PROJ_W)


def _project(x2d, tabs, pw, tm, seq_len=None):
    t = x2d.shape[0]
    ntab = tabs[0].shape[0] // tm
    row = lambda i: (i, 0)
    const = lambda i: (0, 0)
    tab = lambda i: (i % ntab, 0)
    full = lambda a: _resident_spec(a)
    consts = [pw['norm_mix'], pw['w_proj'], pw['q_gain'], pw['ks_gain'], pw['kw_gain'], pw['mq_gain'],
              pw['seg64']]
    widths = [(512, BF16), (512, BF16), (KV_W, F32), (KV_W, F32), (KV_W, F32), (ML_W, F32), (ML_W, F32),
              (MEM_W, BF16), (LANES, F32)]
    out_specs = [pl.BlockSpec((tm, w), row) for w, _ in widths]
    out_shape = [jax.ShapeDtypeStruct((t, w), d) for w, d in widths]
    if seq_len is not None:
        per_seq = seq_len // tm
        for w, d in ((NSA_H * HD, BF16), (KV_W, F32), (KV_W, F32), (KV_W, F32), (NSA_H * HD, BF16), (MEM_W, BF16)):
            out_specs.append(pl.BlockSpec((1, w, tm), lambda i: (i // per_seq, 0, i % per_seq)))
            out_shape.append(jax.ShapeDtypeStruct((t // seq_len, w, seq_len), d))
    return pl.pallas_call(
        _proj_kernel,
        grid=(t // tm,),
        in_specs=[pl.BlockSpec((tm, D_MODEL), row)] + [full(a) for a in consts]
                 + [pl.BlockSpec((tm, LANES), tab)] * 6,
        out_specs=out_specs,
        out_shape=out_shape,
        compiler_params=_cparams(("arbitrary",), VMEM_LIMIT),
        name="project",
    )(x2d, *consts, *tabs)


def _compress_group(y_lo, y_hi, tail_hi, wlo, whi, b1, w2, gain, n_valid):
    r = y_lo.shape[0]
    a = _dot(y_lo, wlo)
    bh = _dot(y_hi, whi)
    bs = pltpu.roll(bh, r - 1, 0)
    if tail_hi is not None:
        bt = _dot(tail_hi, whi)
        bs = jnp.where(_iota(bs.shape, 0) == r - 1, bt, bs)
    hid = _gelu_tanh(a + bs + b1)
    out = _dot(hid.astype(BF16), w2)
    is_k = _iota(out.shape, 1) < HD
    ss = jnp.sum(jnp.where(is_k, out * out, 0.0), axis=-1, keepdims=True) * (1.0 / HD)
    kv = jnp.where(is_k, out * lax.rsqrt(ss + EPS) * gain, out)
    return jnp.where(_iota(kv.shape, 0) < n_valid, kv, 0.0)


def _group_cols(load, g):
    return jnp.concatenate([load(j * KV_W + g * LANES, j * KV_W + (g + 1) * LANES) for j in range(CMP_STRIDE)], axis=1)


def _compress_prompt_kernel(y_ref, pe_lo, pe_hi, wlo, whi, b1, w2, gain, o_ref, ot_ref, *, n_valid):
    for g in range(NSA_G):
        yg = _group_cols(lambda lo, hi: y_ref[0, :, lo:hi], g)
        kv = _compress_group((yg + pe_lo[...]).astype(BF16), (yg + pe_hi[...]).astype(BF16), None,
                             wlo[...], whi[...], b1[...], w2[...], gain[...], n_valid)
        o_ref[0, g] = kv.astype(BF16)
        ot_ref[0, g] = kv.T.astype(BF16)


def _compress_paged_kernel(pt_ref, *refs, n_pages, n_valid):
    pages = refs[:n_pages]
    tail_ref, pe_lo, pe_hi, wlo, whi, b1, w2, gain, o_ref, y = refs[n_pages:]
    cpp = PAGE_SIZE // CMP_STRIDE
    for k, page in enumerate(pages):
        for g in range(NSA_G):
            nat = page[0, g * LANES:(g + 1) * LANES, :].T
            by_row = jnp.swapaxes(nat.reshape(cpp, CMP_STRIDE, LANES), 0, 1)
            for j in range(CMP_STRIDE):
                y[g, k * cpp:(k + 1) * cpp, j * LANES:(j + 1) * LANES] = by_row[j]
    for g in range(NSA_G):
        yg = y[g]
        tg = _group_cols(lambda lo, hi: tail_ref[0, :, lo:hi], g)
        o_ref[0, g] = _compress_group((yg + pe_lo[...]).astype(BF16), (yg + pe_hi[...]).astype(BF16),
                                      (tg + pe_hi[...]).astype(BF16), wlo[...], whi[...], b1[...], w2[...],
                                      gain[...], n_valid).astype(BF16)


def _compress_consts(pw):
    return [pw['cmp_wlo'], pw['cmp_whi'], pw['cmp_b1'], pw['cmp_w2'], pw['kc_gain']]


def _compress_prompt(kvc, pw):
    b, s, _ = kvc.shape
    r = s // CMP_STRIDE
    y = kvc.reshape(b, r, CHUNK_W)
    consts = [pw['pe_lo'], pw['pe_hi']] + _compress_consts(pw)
    return pl.pallas_call(
        functools.partial(_compress_prompt_kernel, n_valid=r - 1),
        grid=(b,),
        in_specs=[pl.BlockSpec((1, r, CHUNK_W), lambda i: (i, 0, 0))]
                 + [pl.BlockSpec(a.shape, lambda i: (0, 0)) for a in consts],
        out_specs=[pl.BlockSpec((1, NSA_G, r, LANES), lambda i: (i, 0, 0, 0)),
                   pl.BlockSpec((1, NSA_G, LANES, r), lambda i: (i, 0, 0, 0))],
        out_shape=[jax.ShapeDtypeStruct((b, NSA_G, r, LANES), BF16),
                   jax.ShapeDtypeStruct((b, NSA_G, LANES, r), BF16)],
        compiler_params=_cparams(("arbitrary",), VMEM_LIMIT),
        name="compress_prompt",
    )(y, *consts)


def _compress_paged(cache, page_table, kvc_new, pw):
    db, n_pages = page_table.shape
    chunks_per_page = PAGE_SIZE // CMP_STRIDE
    r = n_pages * chunks_per_page
    pool = _feature_major(cache)
    tail = jnp.pad(kvc_new, ((0, 0), (0, CMP_STRIDE - kvc_new.shape[1]), (0, 0))).reshape(db, 1, CHUNK_W)
    consts = [pw['pe_lo'], pw['pe_hi']] + _compress_consts(pw)
    page_specs = [pl.BlockSpec((1, KV_W, PAGE_SIZE), functools.partial(lambda i, pt, k: (pt[i, k], 0, 0), k=k))
                  for k in range(n_pages)]
    return pl.pallas_call(
        functools.partial(_compress_paged_kernel, n_pages=n_pages, n_valid=r),
        grid_spec=pltpu.PrefetchScalarGridSpec(
            num_scalar_prefetch=1,
            grid=(db,),
            in_specs=page_specs + [pl.BlockSpec((1, 1, CHUNK_W), lambda i, pt: (i, 0, 0))]
                     + [pl.BlockSpec(a.shape, lambda i, pt: (0, 0)) for a in consts],
            out_specs=pl.BlockSpec((1, NSA_G, r, LANES), lambda i, pt: (i, 0, 0, 0)),
            scratch_shapes=[pltpu.VMEM((NSA_G, r, CMP_STRIDE * LANES), F32)],
        ),
        out_shape=jax.ShapeDtypeStruct((db, NSA_G, r, LANES), BF16),
        compiler_params=_cparams(("arbitrary",), VMEM_LIMIT),
        name="compress_paged",
    )(page_table, *([pool] * n_pages), tail, *consts)


def _rank_select(score, idx, n_blocks):
    cnt = jnp.zeros(score.shape, jnp.int32)
    for i in range(n_blocks):
        other = score[:, i:i + 1]
        ge = jnp.where(other >= score, 1, 0)
        gt = jnp.where(other > score, 1, 0)
        cnt = cnt + jnp.where(idx > i, ge, gt)
    return jnp.where(score > -jnp.inf, jnp.where(cnt < N_SEL, 0.0, UNSELECTED), UNSELECTED)


def _rank_select_rows(score, n_blocks):
    sub = _iota((8, score.shape[1]), 0)
    n_groups = -(-n_blocks // 8)
    groups = [score[g * 8:(g + 1) * 8, :] for g in range(n_groups)]
    cnts = [jnp.zeros(groups[0].shape, jnp.int32) for _ in groups]
    for i in range(n_blocks):
        other = score[i:i + 1, :]
        for g, sg in enumerate(groups):
            if g > i // 8:
                one = jnp.where(other >= sg, 1, 0)
            elif g < i // 8:
                one = jnp.where(other > sg, 1, 0)
            else:
                one = jnp.where(sub > i % 8, jnp.where(other >= sg, 1, 0), jnp.where(other > sg, 1, 0))
            cnts[g] = cnts[g] + one
    out = [jnp.where(sg > -jnp.inf, jnp.where(c < N_SEL, 0.0, UNSELECTED), UNSELECTED) for sg, c in zip(groups, cnts)]
    pad = score.shape[0] - 8 * n_groups
    if pad:
        out.append(jnp.full((pad, score.shape[1]), UNSELECTED, F32))
    return jnp.concatenate(out, axis=0)


def _cmp_prompt_kernel(qt_ref, kv_ref, kvt_ref, ovt_ref, o_ref, sel_ref, *, tq, n_slc):
    i = pl.program_id(2)
    kv = kv_ref[0, 0]
    kvt = kvt_ref[0, 0]
    r = kv.shape[0]
    pos = i * tq + _iota((1, tq), 1)
    visible = (_iota((r, 1), 0) * CMP_STRIDE + (CMP_BLOCK - 1)) <= pos
    zeros = jnp.zeros((HD, tq), BF16)
    psum = jnp.zeros((r, tq), F32)
    outs = []
    for h in range(NSA_HPG):
        qa = jnp.concatenate([qt_ref[0, h * HD:(h + 1) * HD, :], zeros], axis=0)
        s = jnp.where(visible, _dot(kv, qa), -jnp.inf)
        m = jnp.max(s, axis=0, keepdims=True)
        e = jnp.exp(s - jnp.where(m > -jnp.inf, m, 0.0))
        inv_l = 1.0 / jnp.maximum(jnp.sum(e, axis=0, keepdims=True), jnp.finfo(F32).tiny)
        psum = psum + e * inv_l
        outs.append(_dot(kvt[HD:2 * HD, :], e.astype(BF16)) * inv_l)
    o_ref[...] = jnp.concatenate([jnp.concatenate(outs[0:2], axis=0).T, jnp.concatenate(outs[2:4], axis=0).T], axis=1)
    imp = _dot_exact_rhs_left(ovt_ref[...], _split2(psum))
    blk = _iota(imp.shape, 0)
    pos_row = i * tq + _iota(imp.shape, 1)
    forced = (blk == jnp.right_shift(pos_row, 6)) | (blk == 0)
    score = jnp.where(forced, jnp.inf, jnp.where(blk * SLC_BLOCK <= pos_row, imp, -jnp.inf))
    sel_ref[0, 0] = _rank_select_rows(score, n_slc).astype(BF16)


def _cmp_prompt(qn_t, kvcmp, kvcmp_t, ovt, b, s, tq):
    nq = s // tq
    r = kvcmp.shape[2]
    n_slc = s // SLC_BLOCK
    assert n_slc <= HD and ovt.shape == (HD, r) and SLC_BLOCK == 64
    return pl.pallas_call(
        functools.partial(_cmp_prompt_kernel, tq=tq, n_slc=n_slc),
        grid=(b, NSA_G, nq),
        in_specs=[pl.BlockSpec((1, NSA_HPG * HD, tq), lambda bi, g, i: (bi, g, i)),
                  pl.BlockSpec((1, 1, r, LANES), lambda bi, g, i: (bi, g, 0, 0)),
                  pl.BlockSpec((1, 1, LANES, r), lambda bi, g, i: (bi, g, 0, 0)),
                  pl.BlockSpec(ovt.shape, lambda bi, g, i: (0, 0))],
        out_specs=[pl.BlockSpec((tq, NSA_HPG * HD), lambda bi, g, i: (bi * nq + i, g)),
                   pl.BlockSpec((1, 1, HD, tq), lambda bi, g, i: (bi, g, 0, i))],
        out_shape=[jax.ShapeDtypeStruct((b * s, NSA_H * HD), F32),
                   jax.ShapeDtypeStruct((b, NSA_G, HD, s), BF16)],
        compiler_params=_cparams(("arbitrary",) * 3, VMEM_LIMIT),
        name="cmp_attend_select_prompt",
    )(qn_t, kvcmp, kvcmp_t, ovt)


def _cmp_sample_kernel(q_ref, kv_ref, ov_ref, o_ref, sel_ref, *, pos, n_slc):
    q = q_ref[0]
    r = kv_ref.shape[2]
    row = _iota((NSA_H, 1), 0)
    visible = (_iota((1, r), 1) * CMP_STRIDE + (CMP_BLOCK - 1)) <= pos
    kvs = [kv_ref[0, g] for g in range(NSA_G)]
    s = jnp.where(row < NSA_HPG, _dot_nt(q, kvs[0][:, 0:HD]), _dot_nt(q, kvs[1][:, 0:HD]))
    p = _masked_softmax(s, visible)
    pb = p.astype(BF16)
    o = jnp.where(row < NSA_HPG, _dot(pb, kvs[0]), _dot(pb, kvs[1]))
    o_ref[0] = o[:, HD:2 * HD]
    p0 = jnp.sum(p[0:NSA_HPG], axis=0, keepdims=True)
    p1 = jnp.sum(p[NSA_HPG:NSA_H], axis=0, keepdims=True)
    psum = jnp.where(row < NSA_HPG, p0, p1)
    imp = _dot_exact_rhs(_split2(psum), ov_ref[...])
    blk = _iota(imp.shape, 1)
    forced = (blk == pos // SLC_BLOCK) | (blk == 0)
    score = jnp.where(forced, jnp.inf, jnp.where(blk * SLC_BLOCK <= pos, imp, -jnp.inf))
    sel_ref[0] = _rank_select(score, blk, n_slc)


def _cmp_sample(qn3, kvcmp, ov, pos, n_slc):
    db, _, r, _ = kvcmp.shape
    nslp = ov.shape[1]
    return pl.pallas_call(
        functools.partial(_cmp_sample_kernel, pos=pos, n_slc=n_slc),
        grid=(db,),
        in_specs=[pl.BlockSpec((1, NSA_H, HD), lambda i: (i, 0, 0)),
                  pl.BlockSpec((1, NSA_G, r, LANES), lambda i: (i, 0, 0, 0)),
                  pl.BlockSpec(ov.shape, lambda i: (0, 0))],
        out_specs=[pl.BlockSpec((1, NSA_H, HD), lambda i: (i, 0, 0)),
                   pl.BlockSpec((1, NSA_H, nslp), lambda i: (i, 0, 0))],
        out_shape=[jax.ShapeDtypeStruct((db, NSA_H, HD), F32),
                   jax.ShapeDtypeStruct((db, NSA_H, nslp), F32)],
        compiler_params=_cparams(("arbitrary",), VMEM_LIMIT),
        name="cmp_attend_select_sample",
    )(qn3, kvcmp, ov)


def _flash_prompt_kernel(*refs, tq, selected):
    if selected:
        qt_ref, kv_ref, kvt_ref, sel_ref, o_ref, kop, vop, qa_ref, m_ref, acc_ref = refs
    else:
        qt_ref, kv_ref, kvt_ref, o_ref, kop, vop, qa_ref, m_ref, acc_ref = refs
    i = pl.program_id(2)
    cols = NSA_HPG * tq
    n_tiles, _, tk = vop.shape

    @pl.when(i == 0)
    def _():
        kvf = kv_ref[0]
        if selected:
            lane = _iota(kvf.shape, 1)
            onehot = (lane - HD) == jnp.right_shift(_iota(kvf.shape, 0), 6)
            kop[...] = jnp.where(lane < HD, kvf, onehot.astype(F32)).astype(BF16)
        else:
            kop[...] = kvf.astype(BF16)
        for t in range(n_tiles):
            tile = kvt_ref[0, :, t * tk:(t + 1) * tk]
            vop[t] = jnp.where(_iota(tile.shape, 0) < HD, 1.0, tile).astype(BF16)

    extra = sel_ref[0, 0] if selected else jnp.zeros((HD, tq), BF16)
    for h in range(NSA_HPG):
        qa_ref[0:HD, h * tq:(h + 1) * tq] = qt_ref[0, h * HD:(h + 1) * HD, :]
        qa_ref[HD:2 * HD, h * tq:(h + 1) * tq] = extra
    m_ref[...] = jnp.full((1, cols), MASKED, F32)
    acc_ref[...] = jnp.zeros((LANES, cols), F32)

    def tile_step(j, masked):
        start = pl.multiple_of(j * tk, tk)
        for cb in range(cols // LANES):
            sl = slice(cb * LANES, (cb + 1) * LANES)
            s = _dot(kop[pl.ds(start, tk), :], qa_ref[:, sl])
            if masked:
                pos = i * tq + (cb * LANES) % tq + _iota((1, LANES), 1)
                kpos = j * tk + _iota((tk, 1), 0)
                valid = kpos <= pos
                if not selected:
                    valid = valid & ((pos - kpos) <= WINDOW)
                s = jnp.where(valid, s, MASKED)
            m_old = m_ref[:, sl]
            m_new = jnp.maximum(m_old, jnp.max(s, axis=0, keepdims=True))
            p = jnp.exp(s - m_new).astype(BF16)
            m_ref[:, sl] = m_new
            acc_ref[:, sl] = jnp.exp(m_old - m_new) * acc_ref[:, sl] + _dot(vop[j], p)

    per_q = tq // tk

    def run(lo, hi, masked):
        def body(t, carry):
            for u in range(per_q):
                tile_step(t * per_q + u, masked)
            return carry
        lax.fori_loop(lo, hi, body, 0)

    if selected:
        run(0, i, False)
        run(i, i + 1, True)
    else:
        run(jnp.maximum(i - WINDOW // tq, 0), i + 1, True)
    normed = []
    for h in range(NSA_HPG):
        a = acc_ref[:, h * tq:(h + 1) * tq]
        normed.append(a[HD:2 * HD, :] * (1.0 / a[0:1, :]))
    o_ref[...] = jnp.concatenate([jnp.concatenate(normed[0:2], axis=0).T, jnp.concatenate(normed[2:4], axis=0).T], axis=1)


def _flash_prompt(qrot_t, kv, kv_t, sel_t, b, s, tq, tk):
    nq = s // tq
    assert WINDOW % tq == 0 and tq % tk == 0 and tk % LANES == 0 and s % tq == 0
    selected = sel_t is not None
    in_specs = [pl.BlockSpec((1, NSA_HPG * HD, tq), lambda bi, g, i: (bi, g, i)),
                pl.BlockSpec((1, s, LANES), lambda bi, g, i: (bi, 0, g)),
                pl.BlockSpec((1, LANES, s), lambda bi, g, i: (bi, g, 0))]
    args = [qrot_t, kv, kv_t]
    cols = NSA_HPG * tq
    scratch = [pltpu.VMEM((s, LANES), BF16), pltpu.VMEM((s // tk, LANES, tk), BF16),
               pltpu.VMEM((LANES, cols), BF16), pltpu.VMEM((1, cols), F32), pltpu.VMEM((LANES, cols), F32)]
    if selected:
        in_specs.append(pl.BlockSpec((1, 1, HD, tq), lambda bi, g, i: (bi, g, 0, i)))
        args.append(sel_t)
    return pl.pallas_call(
        functools.partial(_flash_prompt_kernel, tq=tq, selected=selected),
        grid=(b, NSA_G, nq),
        in_specs=in_specs,
        out_specs=pl.BlockSpec((tq, NSA_HPG * HD), lambda bi, g, i: (bi * nq + i, g)),
        out_shape=jax.ShapeDtypeStruct((b * s, NSA_H * HD), F32),
        scratch_shapes=scratch,
        compiler_params=_cparams(("arbitrary",) * 3, VMEM_LIMIT),
        name="slc_attend_prompt" if selected else "win_attend_prompt",
    )(*args)


def _mem_kv_kernel(x_ref, nm_ref, w_ref, kg_ref, o_ref, ot_ref):
    xn = _rms_rows(x_ref[0], nm_ref[...]).astype(BF16)
    kv = _dot(xn, w_ref[...])
    cols = []
    for h in range(MEM_H):
        k = kv[:, 2 * h * MEM_HD:(2 * h + 1) * MEM_HD]
        cols.append(_rms_rows(k, kg_ref[...]))
        cols.append(kv[:, (2 * h + 1) * MEM_HD:(2 * h + 2) * MEM_HD])
    out = jnp.concatenate(cols, axis=1)
    o_ref[0] = out
    ot_ref[0] = out.T.astype(BF16)


def _mem_kv(mem, pw):
    b, t_mem, _ = mem.shape
    consts = [pw['norm_mem'], pw['w_mem_kv'], pw['mem_k_gain']]
    return pl.pallas_call(
        _mem_kv_kernel,
        grid=(b,),
        in_specs=[pl.BlockSpec((1, t_mem, D_MODEL), lambda i: (i, 0, 0))]
                 + [pl.BlockSpec(a.shape, lambda i: (0, 0)) for a in consts],
        out_specs=[pl.BlockSpec((1, t_mem, 2 * MEM_W), lambda i: (i, 0, 0)),
                   pl.BlockSpec((1, 2 * MEM_W, t_mem), lambda i: (i, 0, 0))],
        out_shape=[jax.ShapeDtypeStruct((b, t_mem, 2 * MEM_W), F32),
                   jax.ShapeDtypeStruct((b, 2 * MEM_W, t_mem), BF16)],
        compiler_params=_cparams(("arbitrary",), VMEM_LIMIT),
        name="mem_kv",
    )(mem, *consts)


def _mem_prompt_kernel(qt_ref, kv_ref, kvt_ref, o_ref):
    for h in range(MEM_H):
        k = kv_ref[0, :, 2 * h * MEM_HD:(2 * h + 1) * MEM_HD].astype(BF16)
        s = _dot(k, qt_ref[0, h * MEM_HD:(h + 1) * MEM_HD, :])
        p = jnp.exp(s - jnp.max(s, axis=0, keepdims=True))
        vt = kvt_ref[0, (2 * h + 1) * MEM_HD:(2 * h + 2) * MEM_HD, :]
        ot = _dot(vt, p.astype(BF16)) * (1.0 / jnp.sum(p, axis=0, keepdims=True))
        o_ref[:, h * MEM_HD:(h + 1) * MEM_HD] = ot.T


def _mem_prompt(mq_t, kvm, kvm_t, b, s, tq):
    nq = s // tq
    t_mem = kvm.shape[1]
    return pl.pallas_call(
        _mem_prompt_kernel,
        grid=(b, nq),
        in_specs=[pl.BlockSpec((1, MEM_W, tq), lambda bi, i: (bi, 0, i)),
                  pl.BlockSpec((1, t_mem, 2 * MEM_W), lambda bi, i: (bi, 0, 0)),
                  pl.BlockSpec((1, 2 * MEM_W, t_mem), lambda bi, i: (bi, 0, 0))],
        out_specs=pl.BlockSpec((tq, MEM_W), lambda bi, i: (bi * nq + i, 0)),
        out_shape=jax.ShapeDtypeStruct((b * s, MEM_W), F32),
        compiler_params=_cparams(("arbitrary",) * 2, VMEM_LIMIT),
        name="mem_attend_prompt",
    )(mq_t, kvm, kvm_t)


def _decode_attend(qa, blocks, new_row=None, new_bias=None, feature_major=False):
    qk, pv = (_dot, _dot_nt) if feature_major else (_dot_nt, _dot)
    scores = []
    for load, bias in blocks:
        s = qk(qa, load())
        scores.append(s if bias is None else s + bias)
    m = functools.reduce(jnp.maximum, [jnp.max(s, axis=-1, keepdims=True) for s in scores])
    if new_row is not None:
        nb = new_row.astype(BF16).astype(F32)
        s_new = jnp.sum(qa.astype(F32) * nb, axis=-1, keepdims=True) + new_bias
        m = jnp.maximum(m, s_new)
    ps = [jnp.exp(s - m) for s in scores]
    l = functools.reduce(lambda a, c: a + c, [jnp.sum(p, axis=-1, keepdims=True) for p in ps])
    acc = functools.reduce(lambda a, c: a + c, [pv(p.astype(BF16), load()) for p, (load, _) in zip(ps, blocks)])
    if new_row is not None:
        p_new = jnp.exp(s_new - m)
        l = l + p_new
        acc = acc + p_new.astype(BF16).astype(F32) * nb
    return acc / l


def _group_values(o):
    row = _iota((NSA_H, LANES), 0)
    both = jnp.where(row < NSA_HPG, o[:, 0:LANES], o[:, LANES:2 * LANES])
    return both[:, HD:2 * HD]


def _slc_sample_kernel(pt_ref, *refs, n_pages):
    pages = refs[:n_pages]
    q_ref, sel_ref, new_ref, o_ref = refs[n_pages:]
    sb = sel_ref[0]
    lane = _iota((NSA_H, PAGE_SIZE), 1)
    blocks = []
    for p, page in enumerate(pages):
        bias = jnp.where(lane < SLC_BLOCK, sb[:, 2 * p:2 * p + 1], sb[:, 2 * p + 1:2 * p + 2])
        blocks.append((lambda page=page: page[0].astype(BF16), bias))
    n_past = n_pages * (PAGE_SIZE // SLC_BLOCK)
    o = _decode_attend(q_ref[0], blocks, new_ref[0], sb[:, n_past:n_past + 1], feature_major=True)
    o_ref[0] = _group_values(o)


def _feature_major(cache):
    n, rows = cache.shape[:2]
    return jnp.transpose(cache, (0, 2, 3, 4, 1)).reshape(n, KV_W, rows)


def _slc_sample(qaug, sel, cache, page_table, kvs_new):
    db, n_pages = page_table.shape
    nslp = sel.shape[2]
    pool = _feature_major(cache)
    page_specs = [pl.BlockSpec((1, KV_W, PAGE_SIZE), functools.partial(lambda i, pt, k: (pt[i, k], 0, 0), k=k))
                  for k in range(n_pages)]
    return pl.pallas_call(
        functools.partial(_slc_sample_kernel, n_pages=n_pages),
        grid_spec=pltpu.PrefetchScalarGridSpec(
            num_scalar_prefetch=1,
            grid=(db,),
            in_specs=page_specs + [pl.BlockSpec((1, NSA_H, KV_W), lambda i, pt: (i, 0, 0)),
                                   pl.BlockSpec((1, NSA_H, nslp), lambda i, pt: (i, 0, 0)),
                                   pl.BlockSpec((1, 1, KV_W), lambda i, pt: (i, 0, 0))],
            out_specs=pl.BlockSpec((1, NSA_H, HD), lambda i, pt: (i, 0, 0)),
        ),
        out_shape=jax.ShapeDtypeStruct((db, NSA_H, HD), F32),
        compiler_params=_cparams(("arbitrary",), VMEM_LIMIT),
        name="slc_attend_sample",
    )(page_table, *([pool] * n_pages), qaug, sel, kvs_new)


def _win_sample_kernel(q_ref, buf_ref, new_ref, newcol_ref, o_ref, win_ref, *, pos, first_pos):
    buf = buf_ref[0]
    w_buf = buf.shape[1]
    d = pos - (first_pos + _iota((1, w_buf), 1))
    bias = jnp.where((d >= 0) & (d <= WINDOW), 0.0, MASKED)
    o = _decode_attend(q_ref[0], [(lambda: buf.astype(BF16), bias)], new_ref[0], 0.0, feature_major=True)
    o_ref[0] = _group_values(o)
    shifted = pltpu.roll(buf, w_buf - 1, 1)
    win_ref[0] = jnp.where(_iota(buf.shape, 1) == w_buf - 1, newcol_ref[0], shifted)


def _win_sample(qaug, win_buf_t, kvw_new, pos, first_pos):
    db, _, w_buf = win_buf_t.shape
    return pl.pallas_call(
        functools.partial(_win_sample_kernel, pos=pos, first_pos=first_pos),
        grid=(db,),
        in_specs=[pl.BlockSpec((1, NSA_H, KV_W), lambda i: (i, 0, 0)),
                  pl.BlockSpec((1, KV_W, w_buf), lambda i: (i, 0, 0)),
                  pl.BlockSpec((1, 1, KV_W), lambda i: (i, 0, 0)),
                  pl.BlockSpec((1, KV_W, 1), lambda i: (i, 0, 0))],
        out_specs=[pl.BlockSpec((1, NSA_H, HD), lambda i: (i, 0, 0)),
                   pl.BlockSpec((1, KV_W, w_buf), lambda i: (i, 0, 0))],
        out_shape=[jax.ShapeDtypeStruct((db, NSA_H, HD), F32),
                   jax.ShapeDtypeStruct((db, KV_W, w_buf), F32)],
        compiler_params=_cparams(("arbitrary",), VMEM_LIMIT),
        name="win_attend_sample",
    )(qaug, win_buf_t, kvw_new, kvw_new.reshape(db, KV_W, 1))


def _mem_sample_kernel(q_ref, kv_ref, o_ref):
    o = _decode_attend(q_ref[0], [(lambda: kv_ref[0].astype(BF16), None)])
    row = _iota((8, MEM_HD), 0)
    out = jnp.zeros((8, MEM_HD), F32)
    for h in range(MEM_H):
        out = jnp.where(row == h, o[:, (2 * h + 1) * MEM_HD:(2 * h + 2) * MEM_HD], out)
    o_ref[0] = out


def _mem_sample(qaug, kvm):
    db, t_mem, w = kvm.shape
    return pl.pallas_call(
        _mem_sample_kernel,
        grid=(db,),
        in_specs=[pl.BlockSpec((1, 8, w), lambda i: (i, 0, 0)),
                  pl.BlockSpec((1, t_mem, w), lambda i: (i, 0, 0))],
        out_specs=pl.BlockSpec((1, 8, MEM_HD), lambda i: (i, 0, 0)),
        out_shape=jax.ShapeDtypeStruct((db, 8, MEM_HD), F32),
        compiler_params=_cparams(("arbitrary",), VMEM_LIMIT),
        name="mem_attend_sample",
    )(qaug, kvm)


def _mlstm_prompt_kernel(u_ref, oraw_ref, slab_ref, slabt_ref, cw_ref, cb_ref, wq_ref, wk_ref, wv_ref,
                         bgc_ref, bgr_ref, mlg_ref, tril_ref, triu_ref,
                         o_ref, c_ref, n_ref, m_ref, ubuf, *, chunk):
    for bi in range(u_ref.shape[0]):
        _mlstm_prompt_chunk(bi, u_ref, oraw_ref, slab_ref, slabt_ref, cw_ref, cb_ref, wq_ref, wk_ref, wv_ref,
                            bgc_ref, bgr_ref, mlg_ref, tril_ref, triu_ref, o_ref, c_ref, n_ref, m_ref, ubuf, chunk)


def _mlstm_prompt_chunk(bi, u_ref, oraw_ref, slab_ref, slabt_ref, cw_ref, cb_ref, wq_ref, wk_ref, wv_ref,
                        bgc_ref, bgr_ref, mlg_ref, tril_ref, triu_ref, o_ref, c_ref, n_ref, m_ref, ubuf, chunk):
    L = chunk
    pad = 8

    @pl.when(pl.program_id(1) == 0)
    def _():
        c_ref[bi] = jnp.zeros(c_ref.shape[1:], F32)
        n_ref[bi] = jnp.zeros(n_ref.shape[1:], F32)
        m_ref[bi] = jnp.zeros(m_ref.shape[1:], F32)
        ubuf[bi, 0:pad, :] = jnp.zeros((pad, ML_W), F32)

    u = u_ref[bi]
    ubuf[bi, pad:pad + L, :] = u
    conv = cb_ref[...] + cw_ref[CONV_W - 1:CONV_W, :] * u
    for w in range(CONV_W - 1):
        off = pad - (CONV_W - 1) + w
        conv = conv + cw_ref[w:w + 1, :] * ubuf[bi, off:off + L, :]
    ubuf[bi, 0:pad, :] = u[L - pad:L, :]
    cact = conv * _sigmoid(conv)

    gl = slab_ref[bi] + bgc_ref[...]
    gt = slabt_ref[bi] + bgr_ref[...]
    b_c = None
    for part in _split3(_log_sigmoid(gl)):
        term = _dot(tril_ref[...], part)
        b_c = term if b_c is None else b_c + term
    b_r = _dot_exact_rhs(_split3(_log_sigmoid(gt)), triu_ref[...])
    m_prev = m_ref[bi]
    tri = _iota((L, L), 0) >= _iota((L, L), 1)
    lane1 = _iota((1, LANES), 1)
    m_out = m_prev
    outs = []
    for h in range(ML_H):
        sl = slice(h * ML_D, (h + 1) * ML_D)
        bc = b_c[:, _SLAB_F + h:_SLAB_F + h + 1]
        ic = gl[:, _SLAB_I + h:_SLAB_I + h + 1]
        br = b_r[_SLAB_F + h:_SLAB_F + h + 1, :]
        ir = gt[_SLAB_I + h:_SLAB_I + h + 1, :]
        m0 = m_prev[:, h:h + 1]
        dm = jnp.where(tri, bc - br + ir, -jnp.inf)
        m_new = jnp.maximum(bc + m0, jnp.max(dm, axis=-1, keepdims=True))
        ch = cact[:, sl].astype(BF16)
        qh = _dot(ch, wq_ref[h]) * (ML_D ** -0.5)
        kh = _dot(ch, wk_ref[h])
        vh = _dot(u[:, sl].astype(BF16), wv_ref[h])
        qb, kb, vb = qh.astype(BF16), kh.astype(BF16), vh.astype(BF16)
        wmat = jnp.exp(dm - m_new) * _dot_nt(qb, kb)
        inter = jnp.exp(bc + m0 - m_new)
        c_old = c_ref[bi, h]
        n_old = n_ref[bi, h:h + 1, :]
        num = inter * _dot(qb, c_old.astype(BF16)) + _dot(wmat.astype(BF16), vb)
        den = inter * jnp.sum(qh * n_old, axis=-1, keepdims=True) + jnp.sum(wmat, axis=-1, keepdims=True)
        hh = num / jnp.maximum(jnp.abs(den), jnp.exp(-m_new))
        m_end = m_new[L - 1:L, :]
        b_end = bc[L - 1:L, :]
        decay = jnp.exp(b_end + m0 - m_end)
        kw = kh * jnp.exp(b_end - bc + ic - m_end)
        c_ref[bi, h] = decay * c_old + _dot(kw.T.astype(BF16), vb)
        n_ref[bi, h:h + 1, :] = decay * n_old + jnp.sum(kw, axis=0, keepdims=True)
        m_out = jnp.where(lane1 == h, m_end, m_out)
        hn = hh * lax.rsqrt(jnp.mean(hh * hh, axis=-1, keepdims=True) + EPS) * mlg_ref[:, sl]
        outs.append(_sigmoid(oraw_ref[bi, :, sl]) * hn)
    o_ref[bi] = jnp.concatenate(outs, axis=1)
    m_ref[bi] = m_out


def _mlstm_prompt(u, oraw, slab, pw, chunk):
    b, s, _ = u.shape
    nc = s // chunk
    bb = _pick_tile(b, MLSTM_SEQS_PER_STEP)
    slabt = jnp.swapaxes(slab[:, :, 0:8], 1, 2)
    tril = jnp.tril(jnp.ones((chunk, chunk), BF16))
    consts = [pw['conv_w'], pw['conv_b'], pw['w_ml_q'], pw['w_ml_k'], pw['w_ml_v'], pw['gate_bias_lanes'],
              pw['gate_bias_rows'], pw['ml_gain'], tril, tril.T]
    blk = lambda w: pl.BlockSpec((bb, chunk, w), lambda bi, c: (bi, c, 0))

    def const_spec(a):
        nd = a.ndim
        return pl.BlockSpec(a.shape, lambda bi, c: (0,) * nd)

    return pl.pallas_call(
        functools.partial(_mlstm_prompt_kernel, chunk=chunk),
        grid=(b // bb, nc),
        in_specs=[blk(ML_W), blk(ML_W), blk(LANES), pl.BlockSpec((bb, 8, chunk), lambda bi, c: (bi, 0, c))]
                 + [const_spec(a) for a in consts],
        out_specs=[blk(ML_W),
                   pl.BlockSpec((bb, ML_H, ML_D, ML_D), lambda bi, c: (bi, 0, 0, 0)),
                   pl.BlockSpec((bb, ML_H, ML_D), lambda bi, c: (bi, 0, 0)),
                   pl.BlockSpec((bb, 1, LANES), lambda bi, c: (bi, 0, 0))],
        out_shape=[jax.ShapeDtypeStruct((b, s, ML_W), F32),
                   jax.ShapeDtypeStruct((b, ML_H, ML_D, ML_D), F32),
                   jax.ShapeDtypeStruct((b, ML_H, ML_D), F32),
                   jax.ShapeDtypeStruct((b, 1, LANES), F32)],
        scratch_shapes=[pltpu.VMEM((bb, chunk + 8, ML_W), F32)],
        compiler_params=_cparams(("arbitrary", "arbitrary"), VMEM_LIMIT),
        name="mlstm_prompt",
    )(u, oraw, slab, slabt, *consts)


def _mlstm_sample_kernel(ext_ref, oraw_ref, slab_ref, c_ref, n_ref, m_ref, cw_ref, cb_ref, wq_ref, wk_ref,
                         wv_ref, wkt_ref, bgc_ref, mlg_ref, o_ref, co_ref, no_ref, mo_ref):
    ext = ext_ref[0]
    conv = cb_ref[...]
    for w in range(CONV_W):
        conv = conv + cw_ref[w:w + 1, :] * ext[w:w + 1, :]
    cact = conv * _sigmoid(conv)
    u = ext[CONV_W - 1:CONV_W, :]
    gl = slab_ref[0] + bgc_ref[...]
    lf_all = _log_sigmoid(gl)
    m_prev = m_ref[0]
    lane1 = _iota((1, LANES), 1)
    m_out = m_prev
    outs = []
    for h in range(ML_H):
        sl = slice(h * ML_D, (h + 1) * ML_D)
        ch8 = jnp.broadcast_to(cact[:, sl], (8, ML_D)).astype(BF16)
        u8 = jnp.broadcast_to(u[:, sl], (8, ML_D)).astype(BF16)
        q = (_dot(ch8, wq_ref[h]) * (ML_D ** -0.5))[0:1]
        k = _dot(ch8, wk_ref[h])[0:1]
        v = _dot(u8, wv_ref[h])[0:1]
        k_col = _dot_nt(wkt_ref[h], ch8)[:, 0:1]
        ig = gl[:, _SLAB_I + h:_SLAB_I + h + 1]
        lf = lf_all[:, _SLAB_F + h:_SLAB_F + h + 1]
        m0 = m_prev[:, h:h + 1]
        m_new = jnp.maximum(lf + m0, ig)
        qb = q.astype(BF16)
        qk = jnp.sum(qb.astype(F32) * k.astype(BF16).astype(F32), axis=-1, keepdims=True)
        w_in = jnp.exp(ig - m_new) * qk
        inter = jnp.exp(lf + m0 - m_new)
        c_old = c_ref[0, h]
        n_old = n_ref[0, h:h + 1, :]
        qc = _dot(jnp.broadcast_to(qb, (8, ML_D)), c_old.astype(BF16))[0:1]
        num = inter * qc + w_in * v
        den = inter * jnp.sum(q * n_old, axis=-1, keepdims=True) + w_in
        hh = num / jnp.maximum(jnp.abs(den), jnp.exp(-m_new))
        w_end = jnp.exp(ig - m_new)
        co_ref[0, h] = inter * c_old + k_col * (w_end * v)
        no_ref[0, h:h + 1, :] = inter * n_old + w_end * k
        m_out = jnp.where(lane1 == h, m_new, m_out)
        hn = hh * lax.rsqrt(jnp.mean(hh * hh, axis=-1, keepdims=True) + EPS) * mlg_ref[:, sl]
        outs.append(_sigmoid(oraw_ref[0, :, sl]) * hn)
    o_ref[0] = jnp.concatenate(outs, axis=1)
    mo_ref[0] = m_out


def _mlstm_sample(ext, oraw, slab, state_c, state_n, state_m, pw):
    db = ext.shape[0]
    m_in = jnp.pad(state_m, ((0, 0), (0, LANES - ML_H))).reshape(db, 1, LANES)
    consts = [pw['conv_w'], pw['conv_b'], pw['w_ml_q'], pw['w_ml_k'], pw['w_ml_v'], pw['w_ml_k_t'],
              pw['gate_bias_lanes'], pw['ml_gain']]

    def const_spec(a):
        nd = a.ndim
        return pl.BlockSpec(a.shape, lambda i: (0,) * nd)

    per_seq = lambda shape: pl.BlockSpec((1,) + shape, lambda i: (i,) + (0,) * len(shape))
    return pl.pallas_call(
        _mlstm_sample_kernel,
        grid=(db,),
        in_specs=[per_seq((CONV_W, ML_W)), per_seq((1, ML_W)), per_seq((1, LANES)), per_seq((ML_H, ML_D, ML_D)),
                  per_seq((ML_H, ML_D)), per_seq((1, LANES))] + [const_spec(a) for a in consts],
        out_specs=[per_seq((1, ML_W)), per_seq((ML_H, ML_D, ML_D)), per_seq((ML_H, ML_D)), per_seq((1, LANES))],
        out_shape=[jax.ShapeDtypeStruct((db, 1, ML_W), F32),
                   jax.ShapeDtypeStruct((db, ML_H, ML_D, ML_D), F32),
                   jax.ShapeDtypeStruct((db, ML_H, ML_D), F32),
                   jax.ShapeDtypeStruct((db, 1, LANES), F32)],
        compiler_params=_cparams(("arbitrary",), VMEM_LIMIT),
        name="mlstm_sample",
    )(ext, oraw, slab, state_c, state_n, m_in, *consts)


def _merge_kernel(x_ref, nm_ref, wg_ref, slab_ref, ex_ref, ocmp_ref, oslc_ref, owin_ref, oml_ref, omem_ref,
                  wb_ref, wout_ref, o_ref):
    x = x_ref[...]
    xn = _rms_rows(x, nm_ref[...]).astype(BF16)
    gparts = _split2(_sigmoid(slab_ref[...]))
    onsa = None
    for br, br_ref in enumerate((ocmp_ref, oslc_ref, owin_ref)):
        term = _dot_exact_rhs(gparts, ex_ref[br]) * br_ref[...]
        onsa = term if onsa is None else onsa + term
    z = None
    for n, on in enumerate((onsa, oml_ref[...], omem_ref[...])):
        gate = _sigmoid(_dot(xn, wg_ref[:, n * D_MODEL:(n + 1) * D_MODEL]))
        term = gate * _dot(on.astype(BF16), wb_ref[n])
        z = term if z is None else z + term
    o_ref[...] = x + _dot(z.astype(BF16), wout_ref[...])


def _merge(x2d, slab, ocmp, oslc, owin, oml, omem, pw, tm):
    t = x2d.shape[0]
    row = lambda w: pl.BlockSpec((tm, w), lambda i: (i, 0))
    const_spec = lambda a: _resident_spec(a)
    return pl.pallas_call(
        _merge_kernel,
        grid=(t // tm,),
        in_specs=[row(D_MODEL), const_spec(pw['norm_mix']), const_spec(pw['w_gate']), row(LANES),
                  const_spec(pw['gate_expand'])] + [row(BRANCH_W)] * 5
                 + [const_spec(pw['w_branch']), const_spec(pw['w_out'])],
        out_specs=row(D_MODEL),
        out_shape=jax.ShapeDtypeStruct((t, D_MODEL), F32),
        compiler_params=_cparams(("arbitrary",), VMEM_LIMIT),
        name="merge",
    )(x2d, pw['norm_mix'], pw['w_gate'], slab, pw['gate_expand'], ocmp, oslc, owin, oml, omem,
      pw['w_branch'], pw['w_out'])


def _ffn_kernel(x_ref, nf_ref, win_ref, wout_ref, o_ref):
    x = x_ref[...]
    xn = _rms_rows(x, nf_ref[...]).astype(BF16)
    a = _dot(xn, win_ref[:, 0:FFN_HID])
    b = _dot(xn, win_ref[:, FFN_HID:2 * FFN_HID])
    o_ref[...] = x + _dot((a * _sigmoid(a) * b).astype(BF16), wout_ref[...])


def _ffn(x2d, pw, tm):
    t = x2d.shape[0]
    consts = [pw['norm_ffn'], pw['w_ffn_in'], pw['w_ffn_out']]
    return pl.pallas_call(
        _ffn_kernel,
        grid=(t // tm,),
        in_specs=[pl.BlockSpec((tm, D_MODEL), lambda i: (i, 0))] + [_resident_spec(a) for a in consts],
        out_specs=pl.BlockSpec((tm, D_MODEL), lambda i: (i, 0)),
        out_shape=jax.ShapeDtypeStruct((t, D_MODEL), F32),
        compiler_params=_cparams(("arbitrary",), VMEM_LIMIT),
        name="ffn",
    )(x2d, *consts)


def _rope_tables(pos):
    half = ROT_DIM // 2
    freqs = ROPE_THETA ** (-jnp.arange(half, dtype=F32) / half)
    ang = pos.astype(F32)[:, None] * freqs
    cos, sin = jnp.cos(ang), jnp.sin(ang)
    n = pos.shape[0]
    ones = jnp.ones((n, HD - ROT_DIM), F32)
    zeros = jnp.zeros((n, HD - ROT_DIM), F32)
    zh = jnp.zeros((n, half), F32)
    ct = jnp.concatenate([cos, cos, ones], axis=1)
    sa = jnp.concatenate([-sin, zh, zeros], axis=1)
    sb = jnp.concatenate([zh, sin, zeros], axis=1)
    one64, zero64 = jnp.ones((n, HD), F32), jnp.zeros((n, HD), F32)
    q_tabs = [jnp.concatenate([t, t], axis=1) for t in (ct, sa, sb)]
    k_tabs = [jnp.concatenate([ct, one64], axis=1), jnp.concatenate([sa, zero64], axis=1),
              jnp.concatenate([sb, zero64], axis=1)]
    return q_tabs + k_tabs


def _prepare_weights(norm_mix, w_in, q_norm, k_norm_cmp, k_norm_slc, k_norm_win, cmp_pe, cmp_w1, cmp_b1, cmp_w2,
                     conv_w, conv_b, w_ml_q, w_ml_k, w_ml_v, b_igate, b_fgate, ml_norm, norm_mem, w_mem_kv,
                     mem_q_norm, mem_k_norm, w_branch, w_out, norm_ffn, w_ffn_in, w_ffn_out):
    widths = (NSA_H * HD, 3 * NSA_H, KV_W, KV_W, KV_W, ML_W, ML_W, ML_H, ML_H, MEM_W, N_BRANCH * D_MODEL)
    offs = np.concatenate([[0], np.cumsum(widths)])
    q, g, kvc, kvs, kvw, u, o, ig, fg, mq, mg = (w_in[:, int(offs[i]):int(offs[i + 1])] for i in range(len(widths)))
    slab_pad = jnp.zeros((D_MODEL, LANES - 2 * ML_H - 3 * NSA_H), F32)
    pw = {}
    pw['w_proj'] = jnp.concatenate([q, kvc, kvs, kvw, u, o, mq, ig, fg, g, slab_pad], axis=1).astype(BF16)
    pw['w_gate'] = mg.astype(BF16)
    pw['norm_mix'] = norm_mix.reshape(1, D_MODEL)
    pw['q_gain'] = jnp.tile(q_norm, NSA_H).reshape(1, NSA_H * HD)
    ones = jnp.ones((HD,), F32)
    pw['ks_gain'] = jnp.tile(jnp.concatenate([k_norm_slc, ones]), NSA_G).reshape(1, KV_W)
    pw['kw_gain'] = jnp.tile(jnp.concatenate([k_norm_win, ones]), NSA_G).reshape(1, KV_W)
    pw['kc_gain'] = jnp.concatenate([k_norm_cmp, ones]).reshape(1, LANES)
    pw['mq_gain'] = jnp.tile(mem_q_norm, MEM_H).reshape(1, MEM_W)
    head_of = np.arange(NSA_H * HD) // HD
    pw['seg64'] = jnp.asarray(head_of[:, None] == head_of[None, :], BF16)
    w1 = cmp_w1.reshape(2, CMP_BLOCK, HD, CMP_HID)
    eye = jnp.eye(2, dtype=F32)
    for name, part in (('cmp_wlo', w1[:, :CMP_STRIDE]), ('cmp_whi', w1[:, CMP_STRIDE:])):
        pw[name] = jnp.einsum('cjdh,ce->jcdeh', part, eye).reshape(CMP_STRIDE * 2 * HD, 2 * CMP_HID).astype(BF16)
    pw['cmp_w2'] = jnp.einsum('chd,ce->ched', cmp_w2, eye).reshape(2 * CMP_HID, 2 * HD).astype(BF16)
    pw['cmp_b1'] = cmp_b1.reshape(1, 2 * CMP_HID)
    pw['pe_lo'] = cmp_pe[:CMP_STRIDE].reshape(1, CMP_STRIDE * 2 * HD)
    pw['pe_hi'] = cmp_pe[CMP_STRIDE:].reshape(1, CMP_STRIDE * 2 * HD)
    pw['conv_w'] = conv_w
    pw['conv_b'] = conv_b.reshape(1, ML_W)
    pw['w_ml_q'], pw['w_ml_k'], pw['w_ml_v'] = (w.astype(BF16) for w in (w_ml_q, w_ml_k, w_ml_v))
    pw['w_ml_k_t'] = jnp.swapaxes(w_ml_k, 1, 2).astype(BF16)
    gate_bias = jnp.concatenate([b_igate, b_fgate])
    pw['gate_bias_lanes'] = jnp.pad(gate_bias, (0, LANES - 2 * ML_H)).reshape(1, LANES)
    pw['gate_bias_rows'] = gate_bias.reshape(2 * ML_H, 1)
    pw['ml_gain'] = ml_norm.reshape(1, ML_W)
    pw['norm_mem'] = norm_mem.reshape(1, D_MODEL)
    pw['w_mem_kv'] = w_mem_kv.astype(BF16)
    pw['mem_k_gain'] = mem_k_norm.reshape(1, MEM_HD)
    ex = np.zeros((N_BRANCH, LANES, NSA_H * HD), np.float32)
    for br in range(N_BRANCH):
        for gh in range(NSA_H):
            ex[br, _SLAB_G + br * NSA_H + gh, gh * HD:(gh + 1) * HD] = 1.0
    pw['gate_expand'] = jnp.asarray(ex, BF16)
    pw['w_branch'] = w_branch.astype(BF16)
    pw['w_out'] = w_out.astype(BF16)
    pw['norm_ffn'] = norm_ffn.reshape(1, D_MODEL)
    pw['w_ffn_in'] = w_ffn_in.astype(BF16)
    pw['w_ffn_out'] = w_ffn_out.astype(BF16)
    return pw


def _overlap(n_cmp_rows, n_slc_cols):
    ci = np.arange(n_cmp_rows)[:, None] * CMP_STRIDE
    sj = np.arange(n_slc_cols)[None, :] * SLC_BLOCK
    return ((ci < sj + SLC_BLOCK) & (ci + CMP_BLOCK > sj)).astype(np.float32)


def _pick_tile(n, pref):
    t = min(n, pref)
    while n % t:
        t //= 2
    return t


def _prompt_group(x, mem, pw):
    b, s, _ = x.shape
    t = b * s
    x2d = x.reshape(t, D_MODEL)
    tm = _pick_tile(s, TOKEN_TILE)
    tabs = _rope_tables(jnp.arange(s, dtype=jnp.int32))
    (_, _, kvc, kvs, kvw, u, oraw, _, slab,
     qrot_t, kvc_t, kvs_t, kvw_t, qn_t, mq_t) = _project(x2d, tabs, pw, tm, seq_len=s)
    kvcmp, kvcmp_t = _compress_prompt(kvc.reshape(b, s, KV_W), pw)
    r = s // CMP_STRIDE
    ovt = jnp.asarray(_overlap(r, HD).T, BF16)
    ocmp, sel_t = _cmp_prompt(qn_t, kvcmp, kvcmp_t, ovt, b, s, _pick_tile(s, CMP_TQ))
    fq, fk = _pick_tile(s, FLASH_TQ), _pick_tile(s, FLASH_TK)
    oslc = _flash_prompt(qrot_t, kvs.reshape(b, s, KV_W), kvs_t, sel_t, b, s, fq, fk)
    owin = _flash_prompt(qrot_t, kvw.reshape(b, s, KV_W), kvw_t, None, b, s, fq, fk)
    chunk = _pick_tile(s, MLSTM_CHUNK)
    oml, c_new, n_new, m_new = _mlstm_prompt(u.reshape(b, s, ML_W), oraw.reshape(b, s, ML_W),
                                             slab.reshape(b, s, LANES), pw, chunk)
    t_mem = mem.shape[1]
    kvm, kvm_t = _mem_kv(mem, pw)
    omem = _mem_prompt(mq_t, kvm, kvm_t, b, s, _pick_tile(s, 512))
    x1 = _merge(x2d, slab, ocmp, oslc, owin, oml.reshape(t, ML_W), omem, pw, tm)
    y = _ffn(x1, pw, tm).reshape(b, s, D_MODEL)
    kv5 = lambda a_t: jnp.transpose(a_t.reshape(b, NSA_G, 2, HD, a_t.shape[-1]), (0, 4, 1, 2, 3))
    return dict(y=y, kv_cmp=kv5(kvc_t), kv_slc=kv5(kvs_t), kv_win_t=kvw_t, kv5=kv5,
                kv_mem=kvm.reshape(b, t_mem, MEM_H, 2, MEM_HD), c=c_new, n=n_new, m=m_new[:, 0, :ML_H],
                u=u.reshape(b, s, ML_W))


def _sample_group(x, cache_cmp, cache_slc, cache_win, cache_mem, state_c, state_n, state_m, state_conv,
                  page_table, pw):
    db, ds, _ = x.shape
    assert ds == 1, "the sample kernels handle one new token per sequence"
    n_pages = page_table.shape[1]
    past = n_pages * PAGE_SIZE
    pos = past
    x2d = x.reshape(db, D_MODEL)
    tabs = _rope_tables(jnp.full((db,), pos, jnp.int32))
    qn, qrot, kvc, kvs, kvw, u, oraw, mq, slab = _project(x2d, tabs, pw, db)
    kvcmp = _compress_paged(cache_cmp, page_table, kvc.reshape(db, 1, KV_W), pw)
    n_cmp = kvcmp.shape[2]
    n_slc = -(-(past + ds) // SLC_BLOCK)
    nslp = -(-n_slc // LANES) * LANES
    ov = jnp.asarray(_overlap(n_cmp, nslp), BF16)
    ocmp, sel = _cmp_sample(qn.reshape(db, NSA_H, HD), kvcmp, ov, pos, n_slc)
    q3 = qrot.reshape(db, NSA_G, NSA_HPG, 1, HD)
    qaug = (q3 * jnp.eye(NSA_G, dtype=BF16)[None, :, None, :, None]).reshape(db, NSA_H, NSA_G, HD)
    qaug = jnp.concatenate([qaug, jnp.zeros_like(qaug)], axis=-1).reshape(db, NSA_H, KV_W)
    oslc = _slc_sample(qaug, sel, cache_slc, page_table, kvs.reshape(db, 1, KV_W))
    w_buf = cache_win.shape[1]
    owin, win_new_t = _win_sample(qaug, _feature_major(cache_win), kvw.reshape(db, 1, KV_W), pos, past - w_buf)
    win_new = jnp.transpose(win_new_t.reshape(db, NSA_G, 2, HD, w_buf), (0, 4, 1, 2, 3))
    ext = jnp.concatenate([state_conv, u.reshape(db, 1, ML_W)], axis=1)
    oml, c_new, n_new, m_new = _mlstm_sample(ext, oraw.reshape(db, 1, ML_W), slab.reshape(db, 1, LANES),
                                             state_c, state_n, state_m, pw)
    m3 = mq.reshape(db, MEM_H, 1, MEM_HD) * jnp.eye(MEM_H, dtype=BF16)[None, :, :, None]
    maug = jnp.concatenate([m3, jnp.zeros_like(m3)], axis=-1).reshape(db, MEM_H, 2 * MEM_W)
    maug = jnp.pad(maug, ((0, 0), (0, 8 - MEM_H), (0, 0)))
    t_mem = cache_mem.shape[1]
    omem = _mem_sample(maug, cache_mem.reshape(db, t_mem, 2 * MEM_W))[:, :MEM_H].reshape(db, MEM_W)
    x1 = _merge(x2d, slab, ocmp.reshape(db, NSA_H * HD), oslc.reshape(db, NSA_H * HD), owin.reshape(db, NSA_H * HD),
                oml.reshape(db, ML_W), omem, pw, db)
    y = _ffn(x1, pw, db).reshape(db, ds, D_MODEL)
    kv5 = lambda a: a.reshape(db, ds, NSA_G, 2, HD)
    return dict(y=y, kv_cmp=kv5(kvc), kv_slc=kv5(kvs), win=win_new,
                c=c_new, n=n_new, m=m_new[:, 0, :ML_H], conv=ext[:, 1:])


def kernel(x_prompt, x_sample, cache_kv_cmp, cache_kv_slc, cache_kv_win, cache_kv_mem, state_C, state_n, state_m, state_conv, page_table, mem_prompt, norm_mix, w_in, q_norm, k_norm_cmp, k_norm_slc, k_norm_win, cmp_pe, cmp_w1, cmp_b1, cmp_w2, conv_w, conv_b, w_ml_q, w_ml_k, w_ml_v, b_igate, b_fgate, ml_norm, norm_mem, w_mem_kv, mem_q_norm, mem_k_norm, w_branch, w_out, norm_ffn, w_ffn_in, w_ffn_out):
    pw = _prepare_weights(norm_mix, w_in, q_norm, k_norm_cmp, k_norm_slc, k_norm_win, cmp_pe, cmp_w1, cmp_b1, cmp_w2,
                          conv_w, conv_b, w_ml_q, w_ml_k, w_ml_v, b_igate, b_fgate, ml_norm, norm_mem, w_mem_kv,
                          mem_q_norm, mem_k_norm, w_branch, w_out, norm_ffn, w_ffn_in, w_ffn_out)
    p = _prompt_group(x_prompt, mem_prompt, pw)
    s = _sample_group(x_sample, cache_kv_cmp, cache_kv_slc, cache_kv_win, cache_kv_mem, state_C, state_n, state_m,
                      state_conv, page_table, pw)
    b, seq = x_prompt.shape[:2]
    w_buf = cache_kv_win.shape[1]
    assert seq >= w_buf
    win_p = p['kv5'](p['kv_win_t'][:, :, seq - w_buf:])
    conv_p = p['u'][:, seq - (CONV_W - 1):]
    return (p['y'], s['y'], p['kv_cmp'], s['kv_cmp'], p['kv_slc'], s['kv_slc'], win_p, s['win'], p['kv_mem'],
            p['c'], s['c'], p['n'], s['n'], p['m'], s['m'], conv_p, s['conv'])
```

```python
import functools

import numpy as np
import jax
import jax.numpy as jnp
from jax import lax
from jax.experimental import pallas as pl
from jax.experimental.pallas import tpu as pltpu

F32 = jnp.float32
BF16 = jnp.bfloat16

D_MODEL = 1024
PAGE_SIZE = 128
NSA_H, NSA_G, NSA_HPG, HD = 8, 2, 4, 64
ROT_DIM = HD // 4
ROPE_THETA = 500000.0
CMP_BLOCK, CMP_STRIDE, CMP_HID = 32, 16, 128
SLC_BLOCK, N_SEL = 64, 16
WINDOW = 512
ML_H, ML_D, ML_W, CONV_W = 4, 128, 512, 4
MEM_H, MEM_HD, MEM_W = 4, 128, 512
N_BRANCH, BRANCH_W = 3, 512
FFN_HID = -(-8 * D_MODEL // (3 * 256)) * 256
EPS = 1e-6

KV_W = 2 * NSA_G * HD
CHUNK_W = CMP_STRIDE * KV_W
LANES = 128
MASKED = -1e30
UNSELECTED = -32768.0
VMEM_LIMIT = 56 * 1024 * 1024
TOKEN_TILE = 512
CMP_TQ = 512
FLASH_TQ, FLASH_TK = 512, 256
FLASH_COLS = 128
FLASH_ONES_ROWS = 64
MLSTM_CHUNK = 256

_Q0, _KVC0, _KVS0, _KVW0, _U0, _O0, _MQ0, _SLAB0, _PROJ_W = 0, 512, 768, 1024, 1280, 1792, 2304, 2816, 2944
_SLAB_I, _SLAB_F, _SLAB_G = 0, 4, 8


def _dot(a, b):
    return jnp.dot(a, b, preferred_element_type=F32)


def _dot_nt(a, b):
    return lax.dot_general(a, b, (((1,), (1,)), ((), ())), preferred_element_type=F32)


def _split2(x):
    hi = x.astype(BF16)
    return hi, (x - hi.astype(F32)).astype(BF16)


def _split3(x):
    hi = x.astype(BF16)
    r = x - hi.astype(F32)
    mid = r.astype(BF16)
    return hi, mid, (r - mid.astype(F32)).astype(BF16)


def _dot_exact_rhs(parts, m):
    out = _dot(parts[0], m)
    for p in parts[1:]:
        out = out + _dot(p, m)
    return out


def _dot_exact_rhs_left(m, parts):
    out = _dot(m, parts[0])
    for p in parts[1:]:
        out = out + _dot(m, p)
    return out


def _sigmoid(x):
    return 1.0 / (1.0 + jnp.exp(-x))


def _log_sigmoid(x):
    return jnp.minimum(x, 0.0) - jnp.log(1.0 + jnp.exp(-jnp.abs(x)))


def _gelu_tanh(x):
    return x * (0.5 * (1.0 + jnp.tanh(np.sqrt(2.0 / np.pi).astype(np.float32) * (x + 0.044715 * (x * x * x)))))


def _rms_rows(x, gain):
    ms = jnp.mean(x * x, axis=-1, keepdims=True)
    return x * lax.rsqrt(ms + EPS) * gain


def _iota(shape, dim):
    return lax.broadcasted_iota(jnp.int32, shape, dim)


def _cparams(sem, vmem=None):
    return pltpu.CompilerParams(dimension_semantics=sem, vmem_limit_bytes=vmem)


def _resident_spec(a):
    nd = a.ndim
    return pl.BlockSpec(a.shape, lambda *_: (0,) * nd, pipeline_mode=pl.Buffered(1))


def _masked_softmax(s, mask):
    s = jnp.where(mask, s, -jnp.inf)
    m = jnp.max(s, axis=-1, keepdims=True)
    m = jnp.where(m > -jnp.inf, m, 0.0)
    p = jnp.exp(s - m)
    return p / jnp.maximum(jnp.sum(p, axis=-1, keepdims=True), jnp.finfo(F32).tiny)


def _rs_heads64(a, seg_ref, width):
    ss = _dot_exact_rhs(_split2(a * a), seg_ref[0:width, 0:width])
    return lax.rsqrt(ss * (1.0 / HD) + EPS)


def _rope(a, ct, sa, sb, width):
    reps = width // LANES
    if reps > 1:
        ct, sa, sb = (jnp.concatenate([t] * reps, axis=1) for t in (ct, sa, sb))
    half = ROT_DIM // 2
    return a * ct + pltpu.roll(a, width - half, 1) * sa + pltpu.roll(a, half, 1) * sb


def _proj_kernel(x_ref, nm_ref, w_ref, qg_ref, ksg_ref, kwg_ref, mqg_ref, seg_ref,
                 ctq_ref, saq_ref, sbq_ref, ctk_ref, sak_ref, sbk_ref,
                 qn_o, qr_o, kvc_o, kvs_o, kvw_o, u_o, o_o, mq_o, slab_o, *fm_outs):
    xn = _rms_rows(x_ref[...], nm_ref[...]).astype(BF16)

    def seg(lo, hi):
        return _dot(xn, w_ref[:, lo:hi])

    q = seg(_Q0, _KVC0)
    qn = q * _rs_heads64(q, seg_ref, NSA_H * HD) * qg_ref[...]
    scale = HD ** -0.5
    qn_o[...] = (qn * scale).astype(BF16)
    qr = _rope(qn, ctq_ref[...], saq_ref[...], sbq_ref[...], NSA_H * HD) * scale
    qr_o[...] = qr.astype(BF16)
    kvc = seg(_KVC0, _KVS0)
    kvc_o[...] = kvc
    if fm_outs:
        fm_outs[0][0] = qr.T.astype(BF16)
        fm_outs[1][0] = kvc.T
        fm_outs[4][0] = (qn * scale).T.astype(BF16)
    for n, (lo, g_ref, o_ref) in enumerate(((_KVS0, ksg_ref, kvs_o), (_KVW0, kwg_ref, kvw_o))):
        a = seg(lo, lo + KV_W)
        is_k = (_iota(a.shape, 1) & HD) == 0
        an = jnp.where(is_k, a * _rs_heads64(a, seg_ref, KV_W) * g_ref[...], a)
        ar = _rope(an, ctk_ref[...], sak_ref[...], sbk_ref[...], KV_W)
        o_ref[...] = ar
        if fm_outs:
            fm_outs[2 + n][0] = ar.T
    u_o[...] = seg(_U0, _O0)
    o_o[...] = seg(_O0, _MQ0)
    mq = seg(_MQ0, _SLAB0)
    heads = []
    for h in range(MEM_H):
        mh = mq[:, h * MEM_HD:(h + 1) * MEM_HD]
        heads.append(mh * lax.rsqrt(jnp.mean(mh * mh, axis=-1, keepdims=True) + EPS))
    mqn = jnp.concatenate(heads, axis=1) * mqg_ref[...] * (MEM_HD ** -0.5)
    mq_o[...] = mqn.astype(BF16)
    if fm_outs:
        fm_outs[5][0] = mqn.T.astype(BF16)
    slab_o[...] = seg(_SLAB0, _PROJ_W)


def _project(x2d, tabs, pw, tm, seq_len=None):
    t = x2d.shape[0]
    ntab = tabs[0].shape[0] // tm
    row = lambda i: (i, 0)
    const = lambda i: (0, 0)
    tab = lambda i: (i % ntab, 0)
    full = lambda a: _resident_spec(a)
    consts = [pw['norm_mix'], pw['w_proj'], pw['q_gain'], pw['ks_gain'], pw['kw_gain'], pw['mq_gain'],
              pw['seg64']]
    widths = [(512, BF16), (512, BF16), (KV_W, F32), (KV_W, F32), (KV_W, F32), (ML_W, F32), (ML_W, F32),
              (MEM_W, BF16), (LANES, F32)]
    out_specs = [pl.BlockSpec((tm, w), row) for w, _ in widths]
    out_shape = [jax.ShapeDtypeStruct((t, w), d) for w, d in widths]
    if seq_len is not None:
        per_seq = seq_len // tm
        for w, d in ((NSA_H * HD, BF16), (KV_W, F32), (KV_W, F32), (KV_W, F32), (NSA_H * HD, BF16), (MEM_W, BF16)):
            out_specs.append(pl.BlockSpec((1, w, tm), lambda i: (i // per_seq, 0, i % per_seq)))
            out_shape.append(jax.ShapeDtypeStruct((t // seq_len, w, seq_len), d))
    return pl.pallas_call(
        _proj_kernel,
        grid=(t // tm,),
        in_specs=[pl.BlockSpec((tm, D_MODEL), row)] + [full(a) for a in consts]
                 + [pl.BlockSpec((tm, LANES), tab)] * 6,
        out_specs=out_specs,
        out_shape=out_shape,
        compiler_params=_cparams(("arbitrary",), VMEM_LIMIT),
        name="project",
    )(x2d, *consts, *tabs)


def _compress_group(y_lo, y_hi, tail_hi, wlo, whi, b1, w2, gain, n_valid):
    r = y_lo.shape[0]
    a = _dot(y_lo, wlo)
    bh = _dot(y_hi, whi)
    bs = pltpu.roll(bh, r - 1, 0)
    if tail_hi is not None:
        bt = _dot(tail_hi, whi)
        bs = jnp.where(_iota(bs.shape, 0) == r - 1, bt, bs)
    hid = _gelu_tanh(a + bs + b1)
    out = _dot(hid.astype(BF16), w2)
    is_k = _iota(out.shape, 1) < HD
    ss = jnp.sum(jnp.where(is_k, out * out, 0.0), axis=-1, keepdims=True) * (1.0 / HD)
    kv = jnp.where(is_k, out * lax.rsqrt(ss + EPS) * gain, out)
    return jnp.where(_iota(kv.shape, 0) < n_valid, kv, 0.0)


def _group_cols(load, g):
    return jnp.concatenate([load(j * KV_W + g * LANES, j * KV_W + (g + 1) * LANES) for j in range(CMP_STRIDE)], axis=1)


def _compress_prompt_kernel(y_ref, pe_lo, pe_hi, wlo, whi, b1, w2, gain, o_ref, ot_ref, *, n_valid):
    for g in range(NSA_G):
        yg = _group_cols(lambda lo, hi: y_ref[0, :, lo:hi], g)
        kv = _compress_group((yg + pe_lo[...]).astype(BF16), (yg + pe_hi[...]).astype(BF16), None,
                             wlo[...], whi[...], b1[...], w2[...], gain[...], n_valid)
        o_ref[0, g] = kv.astype(BF16)
        ot_ref[0, g] = kv.T.astype(BF16)


def _compress_paged_kernel(pt_ref, *refs, n_pages, n_valid):
    pages = refs[:n_pages]
    tail_ref, pe_lo, pe_hi, wlo, whi, b1, w2, gain, o_ref, y = refs[n_pages:]
    cpp = PAGE_SIZE // CMP_STRIDE
    for k, page in enumerate(pages):
        for g in range(NSA_G):
            nat = page[0, g * LANES:(g + 1) * LANES, :].T
            by_row = jnp.swapaxes(nat.reshape(cpp, CMP_STRIDE, LANES), 0, 1)
            for j in range(CMP_STRIDE):
                y[g, k * cpp:(k + 1) * cpp, j * LANES:(j + 1) * LANES] = by_row[j]
    for g in range(NSA_G):
        yg = y[g]
        tg = _group_cols(lambda lo, hi: tail_ref[0, :, lo:hi], g)
        o_ref[0, g] = _compress_group((yg + pe_lo[...]).astype(BF16), (yg + pe_hi[...]).astype(BF16),
                                      (tg + pe_hi[...]).astype(BF16), wlo[...], whi[...], b1[...], w2[...],
                                      gain[...], n_valid).astype(BF16)


def _compress_consts(pw):
    return [pw['cmp_wlo'], pw['cmp_whi'], pw['cmp_b1'], pw['cmp_w2'], pw['kc_gain']]


def _compress_prompt(kvc, pw):
    b, s, _ = kvc.shape
    r = s // CMP_STRIDE
    y = kvc.reshape(b, r, CHUNK_W)
    consts = [pw['pe_lo'], pw['pe_hi']] + _compress_consts(pw)
    return pl.pallas_call(
        functools.partial(_compress_prompt_kernel, n_valid=r - 1),
        grid=(b,),
        in_specs=[pl.BlockSpec((1, r, CHUNK_W), lambda i: (i, 0, 0))]
                 + [pl.BlockSpec(a.shape, lambda i: (0, 0)) for a in consts],
        out_specs=[pl.BlockSpec((1, NSA_G, r, LANES), lambda i: (i, 0, 0, 0)),
                   pl.BlockSpec((1, NSA_G, LANES, r), lambda i: (i, 0, 0, 0))],
        out_shape=[jax.ShapeDtypeStruct((b, NSA_G, r, LANES), BF16),
                   jax.ShapeDtypeStruct((b, NSA_G, LANES, r), BF16)],
        compiler_params=_cparams(("arbitrary",), VMEM_LIMIT),
        name="compress_prompt",
    )(y, *consts)


def _compress_paged(cache, page_table, kvc_new, pw):
    db, n_pages = page_table.shape
    chunks_per_page = PAGE_SIZE // CMP_STRIDE
    r = n_pages * chunks_per_page
    pool = _feature_major(cache)
    tail = jnp.pad(kvc_new, ((0, 0), (0, CMP_STRIDE - kvc_new.shape[1]), (0, 0))).reshape(db, 1, CHUNK_W)
    consts = [pw['pe_lo'], pw['pe_hi']] + _compress_consts(pw)
    page_specs = [pl.BlockSpec((1, KV_W, PAGE_SIZE), functools.partial(lambda i, pt, k: (pt[i, k], 0, 0), k=k))
                  for k in range(n_pages)]
    return pl.pallas_call(
        functools.partial(_compress_paged_kernel, n_pages=n_pages, n_valid=r),
        grid_spec=pltpu.PrefetchScalarGridSpec(
            num_scalar_prefetch=1,
            grid=(db,),
            in_specs=page_specs + [pl.BlockSpec((1, 1, CHUNK_W), lambda i, pt: (i, 0, 0))]
                     + [pl.BlockSpec(a.shape, lambda i, pt: (0, 0)) for a in consts],
            out_specs=pl.BlockSpec((1, NSA_G, r, LANES), lambda i, pt: (i, 0, 0, 0)),
            scratch_shapes=[pltpu.VMEM((NSA_G, r, CMP_STRIDE * LANES), F32)],
        ),
        out_shape=jax.ShapeDtypeStruct((db, NSA_G, r, LANES), BF16),
        compiler_params=_cparams(("arbitrary",), VMEM_LIMIT),
        name="compress_paged",
    )(page_table, *([pool] * n_pages), tail, *consts)


def _rank_select(score, idx, n_blocks):
    cnt = jnp.zeros(score.shape, jnp.int32)
    for i in range(n_blocks):
        other = score[:, i:i + 1]
        ge = jnp.where(other >= score, 1, 0)
        gt = jnp.where(other > score, 1, 0)
        cnt = cnt + jnp.where(idx > i, ge, gt)
    return jnp.where(score > -jnp.inf, jnp.where(cnt < N_SEL, 0.0, UNSELECTED), UNSELECTED)


def _rank_select_rows(score, n_blocks):
    sub = _iota((8, score.shape[1]), 0)
    n_groups = -(-n_blocks // 8)
    groups = [score[g * 8:(g + 1) * 8, :] for g in range(n_groups)]
    cnts = [jnp.zeros(groups[0].shape, jnp.int32) for _ in groups]
    for i in range(n_blocks):
        other = score[i:i + 1, :]
        for g, sg in enumerate(groups):
            if g > i // 8:
                one = jnp.where(other >= sg, 1, 0)
            elif g < i // 8:
                one = jnp.where(other > sg, 1, 0)
            else:
                one = jnp.where(sub > i % 8, jnp.where(other >= sg, 1, 0), jnp.where(other > sg, 1, 0))
            cnts[g] = cnts[g] + one
    out = [jnp.where(sg > -jnp.inf, jnp.where(c < N_SEL, 0.0, UNSELECTED), UNSELECTED) for sg, c in zip(groups, cnts)]
    pad = score.shape[0] - 8 * n_groups
    if pad:
        out.append(jnp.full((pad, score.shape[1]), UNSELECTED, F32))
    return jnp.concatenate(out, axis=0)


def _cmp_prompt_kernel(qt_ref, kv_ref, kvt_ref, ovt_ref, o_ref, sel_ref, *, tq, n_slc):
    i = pl.program_id(2)
    kv = kv_ref[0, 0]
    kvt = kvt_ref[0, 0]
    r = kv.shape[0]
    pos = i * tq + _iota((1, tq), 1)
    visible = (_iota((r, 1), 0) * CMP_STRIDE + (CMP_BLOCK - 1)) <= pos
    zeros = jnp.zeros((HD, tq), BF16)
    psum = jnp.zeros((r, tq), F32)
    outs = []
    for h in range(NSA_HPG):
        qa = jnp.concatenate([qt_ref[0, h * HD:(h + 1) * HD, :], zeros], axis=0)
        s = jnp.where(visible, _dot(kv, qa), -jnp.inf)
        m = jnp.max(s, axis=0, keepdims=True)
        e = jnp.exp(s - jnp.where(m > -jnp.inf, m, 0.0))
        inv_l = 1.0 / jnp.maximum(jnp.sum(e, axis=0, keepdims=True), jnp.finfo(F32).tiny)
        psum = psum + e * inv_l
        outs.append(_dot(kvt[HD:2 * HD, :], e.astype(BF16)) * inv_l)
    o_ref[...] = jnp.concatenate([jnp.concatenate(outs[0:2], axis=0).T, jnp.concatenate(outs[2:4], axis=0).T], axis=1)
    imp = _dot_exact_rhs_left(ovt_ref[...], _split2(psum))
    blk = _iota(imp.shape, 0)
    pos_row = i * tq + _iota(imp.shape, 1)
    forced = (blk == jnp.right_shift(pos_row, 6)) | (blk == 0)
    score = jnp.where(forced, jnp.inf, jnp.where(blk * SLC_BLOCK <= pos_row, imp, -jnp.inf))
    sel_ref[0, 0] = _rank_select_rows(score, n_slc).astype(BF16)


def _cmp_prompt(qn_t, kvcmp, kvcmp_t, ovt, b, s, tq):
    nq = s // tq
    r = kvcmp.shape[2]
    n_slc = s // SLC_BLOCK
    assert n_slc <= HD and ovt.shape == (HD, r) and SLC_BLOCK == 64
    return pl.pallas_call(
        functools.partial(_cmp_prompt_kernel, tq=tq, n_slc=n_slc),
        grid=(b, NSA_G, nq),
        in_specs=[pl.BlockSpec((1, NSA_HPG * HD, tq), lambda bi, g, i: (bi, g, i)),
                  pl.BlockSpec((1, 1, r, LANES), lambda bi, g, i: (bi, g, 0, 0)),
                  pl.BlockSpec((1, 1, LANES, r), lambda bi, g, i: (bi, g, 0, 0)),
                  pl.BlockSpec(ovt.shape, lambda bi, g, i: (0, 0))],
        out_specs=[pl.BlockSpec((tq, NSA_HPG * HD), lambda bi, g, i: (bi * nq + i, g)),
                   pl.BlockSpec((1, 1, HD, tq), lambda bi, g, i: (bi, g, 0, i))],
        out_shape=[jax.ShapeDtypeStruct((b * s, NSA_H * HD), F32),
                   jax.ShapeDtypeStruct((b, NSA_G, HD, s), BF16)],
        compiler_params=_cparams(("arbitrary",) * 3, VMEM_LIMIT),
        name="cmp_attend_select_prompt",
    )(qn_t, kvcmp, kvcmp_t, ovt)


def _cmp_sample_kernel(q_ref, kv_ref, ov_ref, o_ref, sel_ref, *, pos, n_slc):
    q = q_ref[0]
    r = kv_ref.shape[2]
    row = _iota((NSA_H, 1), 0)
    visible = (_iota((1, r), 1) * CMP_STRIDE + (CMP_BLOCK - 1)) <= pos
    kvs = [kv_ref[0, g] for g in range(NSA_G)]
    s = jnp.where(row < NSA_HPG, _dot_nt(q, kvs[0][:, 0:HD]), _dot_nt(q, kvs[1][:, 0:HD]))
    p = _masked_softmax(s, visible)
    pb = p.astype(BF16)
    o = jnp.where(row < NSA_HPG, _dot(pb, kvs[0]), _dot(pb, kvs[1]))
    o_ref[0] = o[:, HD:2 * HD]
    p0 = jnp.sum(p[0:NSA_HPG], axis=0, keepdims=True)
    p1 = jnp.sum(p[NSA_HPG:NSA_H], axis=0, keepdims=True)
    psum = jnp.where(row < NSA_HPG, p0, p1)
    imp = _dot_exact_rhs(_split2(psum), ov_ref[...])
    blk = _iota(imp.shape, 1)
    forced = (blk == pos // SLC_BLOCK) | (blk == 0)
    score = jnp.where(forced, jnp.inf, jnp.where(blk * SLC_BLOCK <= pos, imp, -jnp.inf))
    sel_ref[0] = _rank_select(score, blk, n_slc)


def _cmp_sample(qn3, kvcmp, ov, pos, n_slc):
    db, _, r, _ = kvcmp.shape
    nslp = ov.shape[1]
    return pl.pallas_call(
        functools.partial(_cmp_sample_kernel, pos=pos, n_slc=n_slc),
        grid=(db,),
        in_specs=[pl.BlockSpec((1, NSA_H, HD), lambda i: (i, 0, 0)),
                  pl.BlockSpec((1, NSA_G, r, LANES), lambda i: (i, 0, 0, 0)),
                  pl.BlockSpec(ov.shape, lambda i: (0, 0))],
        out_specs=[pl.BlockSpec((1, NSA_H, HD), lambda i: (i, 0, 0)),
                   pl.BlockSpec((1, NSA_H, nslp), lambda i: (i, 0, 0))],
        out_shape=[jax.ShapeDtypeStruct((db, NSA_H, HD), F32),
                   jax.ShapeDtypeStruct((db, NSA_H, nslp), F32)],
        compiler_params=_cparams(("arbitrary",), VMEM_LIMIT),
        name="cmp_attend_select_sample",
    )(qn3, kvcmp, ov)


def _flash_prompt_kernel(*refs, tq, selected):
    if selected:
        qt_ref, kv_ref, kvt_ref, sel_ref, o_ref, kop, vop, qa_ref, m_ref, acc_ref = refs
    else:
        qt_ref, kv_ref, kvt_ref, o_ref, kop, vop, qa_ref, m_ref, acc_ref = refs
    i = pl.program_id(2)
    cols = NSA_HPG * tq
    n_tiles, v_rows, tk = vop.shape
    ones_rows = v_rows - HD
    wide = FLASH_COLS

    @pl.when(i == 0)
    def _():
        kvf = kv_ref[0]
        if selected:
            lane = _iota(kvf.shape, 1)
            onehot = (lane - HD) == jnp.right_shift(_iota(kvf.shape, 0), 6)
            kop[...] = jnp.where(lane < HD, kvf, onehot.astype(F32)).astype(BF16)
        else:
            kop[...] = kvf.astype(BF16)
        for t in range(n_tiles):
            vop[t, 0:ones_rows, :] = jnp.ones((ones_rows, tk), BF16)
            vop[t, ones_rows:v_rows, :] = kvt_ref[0, HD:2 * HD, t * tk:(t + 1) * tk].astype(BF16)

    extra = sel_ref[0, 0] if selected else jnp.zeros((HD, tq), BF16)
    for h in range(NSA_HPG):
        qa_ref[0:HD, h * tq:(h + 1) * tq] = qt_ref[0, h * HD:(h + 1) * HD, :]
        qa_ref[HD:2 * HD, h * tq:(h + 1) * tq] = extra
    m_ref[...] = jnp.full((1, cols), MASKED, F32)
    acc_ref[...] = jnp.zeros((v_rows, cols), F32)

    def tile_step(j, masked):
        start = pl.multiple_of(j * tk, tk)
        for cb in range(cols // wide):
            sl = slice(cb * wide, (cb + 1) * wide)
            s = _dot(kop[pl.ds(start, tk), :], qa_ref[:, sl])
            if masked:
                pos = i * tq + (cb * wide) % tq + _iota((1, wide), 1)
                kpos = j * tk + _iota((tk, 1), 0)
                valid = kpos <= pos
                if not selected:
                    valid = valid & ((pos - kpos) <= WINDOW)
                s = jnp.where(valid, s, MASKED)
            m_old = m_ref[:, sl]
            m_new = jnp.maximum(m_old, jnp.max(s, axis=0, keepdims=True))
            p = jnp.exp(s - m_new).astype(BF16)
            m_ref[:, sl] = m_new
            acc_ref[:, sl] = jnp.exp(m_old - m_new) * acc_ref[:, sl] + _dot(vop[j], p)

    per_q = tq // tk

    def run(lo, hi, masked):
        def body(t, carry):
            for u in range(per_q):
                tile_step(t * per_q + u, masked)
            return carry
        lax.fori_loop(lo, hi, body, 0)

    if selected:
        run(0, i, False)
        run(i, i + 1, True)
    else:
        run(jnp.maximum(i - WINDOW // tq, 0), i + 1, True)
    normed = []
    for h in range(NSA_HPG):
        a = acc_ref[:, h * tq:(h + 1) * tq]
        normed.append(a[ones_rows:v_rows, :] * (1.0 / a[0:1, :]))
    o_ref[...] = jnp.concatenate([jnp.concatenate(normed[0:2], axis=0).T, jnp.concatenate(normed[2:4], axis=0).T], axis=1)


def _flash_prompt(qrot_t, kv, kv_t, sel_t, b, s, tq, tk):
    nq = s // tq
    assert WINDOW % tq == 0 and tq % tk == 0 and tk % LANES == 0 and s % tq == 0 and tq % FLASH_COLS == 0
    v_rows = FLASH_ONES_ROWS + HD
    selected = sel_t is not None
    in_specs = [pl.BlockSpec((1, NSA_HPG * HD, tq), lambda bi, g, i: (bi, g, i)),
                pl.BlockSpec((1, s, LANES), lambda bi, g, i: (bi, 0, g)),
                pl.BlockSpec((1, LANES, s), lambda bi, g, i: (bi, g, 0))]
    args = [qrot_t, kv, kv_t]
    cols = NSA_HPG * tq
    scratch = [pltpu.VMEM((s, LANES), BF16), pltpu.VMEM((s // tk, v_rows, tk), BF16),
               pltpu.VMEM((LANES, cols), BF16), pltpu.VMEM((1, cols), F32), pltpu.VMEM((v_rows, cols), F32)]
    if selected:
        in_specs.append(pl.BlockSpec((1, 1, HD, tq), lambda bi, g, i: (bi, g, 0, i)))
        args.append(sel_t)
    return pl.pallas_call(
        functools.partial(_flash_prompt_kernel, tq=tq, selected=selected),
        grid=(b, NSA_G, nq),
        in_specs=in_specs,
        out_specs=pl.BlockSpec((tq, NSA_HPG * HD), lambda bi, g, i: (bi * nq + i, g)),
        out_shape=jax.ShapeDtypeStruct((b * s, NSA_H * HD), F32),
        scratch_shapes=scratch,
        compiler_params=_cparams(("arbitrary",) * 3, VMEM_LIMIT),
        name="slc_attend_prompt" if selected else "win_attend_prompt",
    )(*args)


def _mem_kv_kernel(x_ref, nm_ref, w_ref, kg_ref, o_ref, ot_ref):
    xn = _rms_rows(x_ref[0], nm_ref[...]).astype(BF16)
    kv = _dot(xn, w_ref[...])
    cols = []
    for h in range(MEM_H):
        k = kv[:, 2 * h * MEM_HD:(2 * h + 1) * MEM_HD]
        cols.append(_rms_rows(k, kg_ref[...]))
        cols.append(kv[:, (2 * h + 1) * MEM_HD:(2 * h + 2) * MEM_HD])
    out = jnp.concatenate(cols, axis=1)
    o_ref[0] = out
    ot_ref[0] = out.T.astype(BF16)


def _mem_kv(mem, pw):
    b, t_mem, _ = mem.shape
    consts = [pw['norm_mem'], pw['w_mem_kv'], pw['mem_k_gain']]
    return pl.pallas_call(
        _mem_kv_kernel,
        grid=(b,),
        in_specs=[pl.BlockSpec((1, t_mem, D_MODEL), lambda i: (i, 0, 0))]
                 + [pl.BlockSpec(a.shape, lambda i: (0, 0)) for a in consts],
        out_specs=[pl.BlockSpec((1, t_mem, 2 * MEM_W), lambda i: (i, 0, 0)),
                   pl.BlockSpec((1, 2 * MEM_W, t_mem), lambda i: (i, 0, 0))],
        out_shape=[jax.ShapeDtypeStruct((b, t_mem, 2 * MEM_W), F32),
                   jax.ShapeDtypeStruct((b, 2 * MEM_W, t_mem), BF16)],
        compiler_params=_cparams(("arbitrary",), VMEM_LIMIT),
        name="mem_kv",
    )(mem, *consts)


def _mem_prompt_kernel(qt_ref, kv_ref, kvt_ref, o_ref):
    for h in range(MEM_H):
        k = kv_ref[0, :, 2 * h * MEM_HD:(2 * h + 1) * MEM_HD].astype(BF16)
        s = _dot(k, qt_ref[0, h * MEM_HD:(h + 1) * MEM_HD, :])
        p = jnp.exp(s - jnp.max(s, axis=0, keepdims=True))
        vt = kvt_ref[0, (2 * h + 1) * MEM_HD:(2 * h + 2) * MEM_HD, :]
        ot = _dot(vt, p.astype(BF16)) * (1.0 / jnp.sum(p, axis=0, keepdims=True))
        o_ref[:, h * MEM_HD:(h + 1) * MEM_HD] = ot.T


def _mem_prompt(mq_t, kvm, kvm_t, b, s, tq):
    nq = s // tq
    t_mem = kvm.shape[1]
    return pl.pallas_call(
        _mem_prompt_kernel,
        grid=(b, nq),
        in_specs=[pl.BlockSpec((1, MEM_W, tq), lambda bi, i: (bi, 0, i)),
                  pl.BlockSpec((1, t_mem, 2 * MEM_W), lambda bi, i: (bi, 0, 0)),
                  pl.BlockSpec((1, 2 * MEM_W, t_mem), lambda bi, i: (bi, 0, 0))],
        out_specs=pl.BlockSpec((tq, MEM_W), lambda bi, i: (bi * nq + i, 0)),
        out_shape=jax.ShapeDtypeStruct((b * s, MEM_W), F32),
        compiler_params=_cparams(("arbitrary",) * 2, VMEM_LIMIT),
        name="mem_attend_prompt",
    )(mq_t, kvm, kvm_t)


def _decode_attend(qa, blocks, new_row=None, new_bias=None, feature_major=False):
    qk, pv = (_dot, _dot_nt) if feature_major else (_dot_nt, _dot)
    scores = []
    for load, bias in blocks:
        s = qk(qa, load())
        scores.append(s if bias is None else s + bias)
    m = functools.reduce(jnp.maximum, [jnp.max(s, axis=-1, keepdims=True) for s in scores])
    if new_row is not None:
        nb = new_row.astype(BF16).astype(F32)
        s_new = jnp.sum(qa.astype(F32) * nb, axis=-1, keepdims=True) + new_bias
        m = jnp.maximum(m, s_new)
    ps = [jnp.exp(s - m) for s in scores]
    l = functools.reduce(lambda a, c: a + c, [jnp.sum(p, axis=-1, keepdims=True) for p in ps])
    acc = functools.reduce(lambda a, c: a + c, [pv(p.astype(BF16), load()) for p, (load, _) in zip(ps, blocks)])
    if new_row is not None:
        p_new = jnp.exp(s_new - m)
        l = l + p_new
        acc = acc + p_new.astype(BF16).astype(F32) * nb
    return acc / l


def _group_values(o):
    row = _iota((NSA_H, LANES), 0)
    both = jnp.where(row < NSA_HPG, o[:, 0:LANES], o[:, LANES:2 * LANES])
    return both[:, HD:2 * HD]


def _slc_sample_kernel(pt_ref, *refs, n_pages):
    pages = refs[:n_pages]
    q_ref, sel_ref, new_ref, o_ref = refs[n_pages:]
    sb = sel_ref[0]
    lane = _iota((NSA_H, PAGE_SIZE), 1)
    blocks = []
    for p, page in enumerate(pages):
        bias = jnp.where(lane < SLC_BLOCK, sb[:, 2 * p:2 * p + 1], sb[:, 2 * p + 1:2 * p + 2])
        blocks.append((lambda page=page: page[0].astype(BF16), bias))
    n_past = n_pages * (PAGE_SIZE // SLC_BLOCK)
    o = _decode_attend(q_ref[0], blocks, new_ref[0], sb[:, n_past:n_past + 1], feature_major=True)
    o_ref[0] = _group_values(o)


def _feature_major(cache):
    n, rows = cache.shape[:2]
    return jnp.transpose(cache, (0, 2, 3, 4, 1)).reshape(n, KV_W, rows)


def _slc_sample(qaug, sel, cache, page_table, kvs_new):
    db, n_pages = page_table.shape
    nslp = sel.shape[2]
    pool = _feature_major(cache)
    page_specs = [pl.BlockSpec((1, KV_W, PAGE_SIZE), functools.partial(lambda i, pt, k: (pt[i, k], 0, 0), k=k))
                  for k in range(n_pages)]
    return pl.pallas_call(
        functools.partial(_slc_sample_kernel, n_pages=n_pages),
        grid_spec=pltpu.PrefetchScalarGridSpec(
            num_scalar_prefetch=1,
            grid=(db,),
            in_specs=page_specs + [pl.BlockSpec((1, NSA_H, KV_W), lambda i, pt: (i, 0, 0)),
                                   pl.BlockSpec((1, NSA_H, nslp), lambda i, pt: (i, 0, 0)),
                                   pl.BlockSpec((1, 1, KV_W), lambda i, pt: (i, 0, 0))],
            out_specs=pl.BlockSpec((1, NSA_H, HD), lambda i, pt: (i, 0, 0)),
        ),
        out_shape=jax.ShapeDtypeStruct((db, NSA_H, HD), F32),
        compiler_params=_cparams(("arbitrary",), VMEM_LIMIT),
        name="slc_attend_sample",
    )(page_table, *([pool] * n_pages), qaug, sel, kvs_new)


def _win_sample_kernel(q_ref, buf_ref, new_ref, newcol_ref, o_ref, win_ref, *, pos, first_pos):
    buf = buf_ref[0]
    w_buf = buf.shape[1]
    d = pos - (first_pos + _iota((1, w_buf), 1))
    bias = jnp.where((d >= 0) & (d <= WINDOW), 0.0, MASKED)
    o = _decode_attend(q_ref[0], [(lambda: buf.astype(BF16), bias)], new_ref[0], 0.0, feature_major=True)
    o_ref[0] = _group_values(o)
    shifted = pltpu.roll(buf, w_buf - 1, 1)
    win_ref[0] = jnp.where(_iota(buf.shape, 1) == w_buf - 1, newcol_ref[0], shifted)


def _win_sample(qaug, win_buf_t, kvw_new, pos, first_pos):
    db, _, w_buf = win_buf_t.shape
    return pl.pallas_call(
        functools.partial(_win_sample_kernel, pos=pos, first_pos=first_pos),
        grid=(db,),
        in_specs=[pl.BlockSpec((1, NSA_H, KV_W), lambda i: (i, 0, 0)),
                  pl.BlockSpec((1, KV_W, w_buf), lambda i: (i, 0, 0)),
                  pl.BlockSpec((1, 1, KV_W), lambda i: (i, 0, 0)),
                  pl.BlockSpec((1, KV_W, 1), lambda i: (i, 0, 0))],
        out_specs=[pl.BlockSpec((1, NSA_H, HD), lambda i: (i, 0, 0)),
                   pl.BlockSpec((1, KV_W, w_buf), lambda i: (i, 0, 0))],
        out_shape=[jax.ShapeDtypeStruct((db, NSA_H, HD), F32),
                   jax.ShapeDtypeStruct((db, KV_W, w_buf), F32)],
        compiler_params=_cparams(("arbitrary",), VMEM_LIMIT),
        name="win_attend_sample",
    )(qaug, win_buf_t, kvw_new, kvw_new.reshape(db, KV_W, 1))


def _mem_sample_kernel(q_ref, kv_ref, o_ref):
    q = q_ref[0]
    row = _iota((8, MEM_HD), 0)
    out = jnp.zeros((8, MEM_HD), F32)
    for h in range(MEM_H):
        k = kv_ref[0, :, h, 0, :].astype(BF16)
        v = kv_ref[0, :, h, 1, :].astype(BF16)
        s = _dot_nt(q, k)
        p = jnp.exp(s - jnp.max(s, axis=-1, keepdims=True))
        o = _dot(p.astype(BF16), v) / jnp.sum(p, axis=-1, keepdims=True)
        out = jnp.where(row == h, o, out)
    o_ref[0] = out


def _mem_sample(q8, cache_mem):
    db, t_mem = cache_mem.shape[:2]
    return pl.pallas_call(
        _mem_sample_kernel,
        grid=(db,),
        in_specs=[pl.BlockSpec((1, 8, MEM_HD), lambda i: (i, 0, 0)),
                  pl.BlockSpec((1, t_mem, MEM_H, 2, MEM_HD), lambda i: (i, 0, 0, 0, 0))],
        out_specs=pl.BlockSpec((1, 8, MEM_HD), lambda i: (i, 0, 0)),
        out_shape=jax.ShapeDtypeStruct((db, 8, MEM_HD), F32),
        compiler_params=_cparams(("arbitrary",), VMEM_LIMIT),
        name="mem_attend_sample",
    )(q8, cache_mem)


def _mlstm_prompt_kernel(u_ref, oraw_ref, slab_ref, slabt_ref, cw_ref, cb_ref, wqt_ref, wk_ref, wvt_ref,
                         bgc_ref, bgr_ref, mlgt_ref, tril_ref, triu_ref,
                         o_ref, ct_ref, n_ref, m_ref, ubuf, *, chunk):
    L = chunk
    pad = 8

    @pl.when(pl.program_id(1) == 0)
    def _():
        ct_ref[...] = jnp.zeros_like(ct_ref)
        n_ref[...] = jnp.zeros_like(n_ref)
        m_ref[...] = jnp.zeros_like(m_ref)
        ubuf[0:pad, :] = jnp.zeros((pad, ML_W), F32)

    u = u_ref[0]
    ubuf[pad:pad + L, :] = u
    conv = cb_ref[...] + cw_ref[CONV_W - 1:CONV_W, :] * u
    for w in range(CONV_W - 1):
        off = pad - (CONV_W - 1) + w
        conv = conv + cw_ref[w:w + 1, :] * ubuf[off:off + L, :]
    ubuf[0:pad, :] = u[L - pad:L, :]
    cact = conv * _sigmoid(conv)

    gl = slab_ref[0] + bgc_ref[...]
    gt = slabt_ref[0] + bgr_ref[...]
    b_c = None
    for part in _split3(_log_sigmoid(gl)):
        term = _dot(tril_ref[...], part)
        b_c = term if b_c is None else b_c + term
    b_r = _dot_exact_rhs(_split3(_log_sigmoid(gt)), triu_ref[...])
    m_prev = m_ref[0]
    not_after = _iota((L, L), 0) <= _iota((L, L), 1)
    lane1 = _iota((1, LANES), 1)
    gain_t = jnp.concatenate([mlgt_ref[...]] * (L // LANES), axis=1)
    m_out = m_prev
    outs = []
    for h in range(ML_H):
        sl = slice(h * ML_D, (h + 1) * ML_D)
        br = b_r[_SLAB_F + h:_SLAB_F + h + 1, :]
        ir = gt[_SLAB_I + h:_SLAB_I + h + 1, :]
        src = gl[:, _SLAB_I + h:_SLAB_I + h + 1] - b_c[:, _SLAB_F + h:_SLAB_F + h + 1]
        m0 = m_prev[:, h:h + 1]
        dm = jnp.where(not_after, br + src, -jnp.inf)
        m_new = jnp.maximum(br + m0, jnp.max(dm, axis=0, keepdims=True))
        ch = cact[:, sl].astype(BF16)
        kb = _dot(ch, wk_ref[h]).astype(BF16)
        qt = _dot_nt(wqt_ref[h], ch) * (ML_D ** -0.5)
        vt = _dot_nt(wvt_ref[h], u[:, sl].astype(BF16))
        qtb = qt.astype(BF16)
        wmat = jnp.exp(dm - m_new) * _dot(kb, qtb)
        inter = jnp.exp(br + m0 - m_new)
        ct_old = ct_ref[0, h]
        n_old = n_ref[0, h:h + 1, :]
        num = inter * _dot(ct_old.astype(BF16), qtb) + _dot(vt.astype(BF16), wmat.astype(BF16))
        nq = _dot_exact_rhs(_split2(jnp.broadcast_to(n_old, (8, ML_D))), qtb)[0:1]
        den = inter * nq + jnp.sum(wmat, axis=0, keepdims=True)
        hh = num / jnp.maximum(jnp.abs(den), jnp.exp(-m_new))
        m_end = m_new[:, L - 1:L]
        b_end = br[:, L - 1:L]
        decay = jnp.exp(b_end + m0 - m_end)
        wend = jnp.exp(b_end - br + ir - m_end)
        ct_ref[0, h] = decay * ct_old + _dot((vt * wend).astype(BF16), kb)
        n_ref[0, h:h + 1, :] = decay * n_old + _dot_exact_rhs(_split2(jnp.broadcast_to(wend, (8, L))), kb)[0:1]
        m_out = jnp.where(lane1 == h, m_end, m_out)
        hn = hh * lax.rsqrt(jnp.mean(hh * hh, axis=0, keepdims=True) + EPS) * gain_t[sl, :]
        outs.append(_sigmoid(oraw_ref[0, :, sl]) * hn.T)
    o_ref[0] = jnp.concatenate(outs, axis=1)
    m_ref[0] = m_out


def _mlstm_prompt(u, oraw, slab, pw, chunk):
    b, s, _ = u.shape
    nc = s // chunk
    slabt = jnp.swapaxes(slab[:, :, 0:8], 1, 2)
    tril = jnp.tril(jnp.ones((chunk, chunk), BF16))
    consts = [pw['conv_w'], pw['conv_b'], pw['w_ml_q_t'], pw['w_ml_k'], pw['w_ml_v_t'], pw['gate_bias_lanes'],
              pw['gate_bias_rows'], pw['ml_gain_t'], tril, tril.T]
    blk = lambda w: pl.BlockSpec((1, chunk, w), lambda bi, c: (bi, c, 0))

    def const_spec(a):
        nd = a.ndim
        return pl.BlockSpec(a.shape, lambda bi, c: (0,) * nd)

    return pl.pallas_call(
        functools.partial(_mlstm_prompt_kernel, chunk=chunk),
        grid=(b, nc),
        in_specs=[blk(ML_W), blk(ML_W), blk(LANES), pl.BlockSpec((1, 8, chunk), lambda bi, c: (bi, 0, c))]
                 + [const_spec(a) for a in consts],
        out_specs=[blk(ML_W),
                   pl.BlockSpec((1, ML_H, ML_D, ML_D), lambda bi, c: (bi, 0, 0, 0)),
                   pl.BlockSpec((1, ML_H, ML_D), lambda bi, c: (bi, 0, 0)),
                   pl.BlockSpec((1, 1, LANES), lambda bi, c: (bi, 0, 0))],
        out_shape=[jax.ShapeDtypeStruct((b, s, ML_W), F32),
                   jax.ShapeDtypeStruct((b, ML_H, ML_D, ML_D), F32),
                   jax.ShapeDtypeStruct((b, ML_H, ML_D), F32),
                   jax.ShapeDtypeStruct((b, 1, LANES), F32)],
        scratch_shapes=[pltpu.VMEM((chunk + 8, ML_W), F32)],
        compiler_params=_cparams(("arbitrary", "arbitrary"), VMEM_LIMIT),
        name="mlstm_prompt",
    )(u, oraw, slab, slabt, *consts)


def _mlstm_sample_kernel(ext_ref, oraw_ref, slab_ref, c_ref, n_ref, m_ref, cw_ref, cb_ref, wq_ref, wk_ref,
                         wv_ref, wkt_ref, bgc_ref, mlg_ref, o_ref, co_ref, no_ref, mo_ref):
    ext = ext_ref[0]
    conv = cb_ref[...]
    for w in range(CONV_W):
        conv = conv + cw_ref[w:w + 1, :] * ext[w:w + 1, :]
    cact = conv * _sigmoid(conv)
    u = ext[CONV_W - 1:CONV_W, :]
    gl = slab_ref[0] + bgc_ref[...]
    lf_all = _log_sigmoid(gl)
    m_prev = m_ref[0]
    lane1 = _iota((1, LANES), 1)
    m_out = m_prev
    outs = []
    for h in range(ML_H):
        sl = slice(h * ML_D, (h + 1) * ML_D)
        ch8 = jnp.broadcast_to(cact[:, sl], (8, ML_D)).astype(BF16)
        u8 = jnp.broadcast_to(u[:, sl], (8, ML_D)).astype(BF16)
        q = (_dot(ch8, wq_ref[h]) * (ML_D ** -0.5))[0:1]
        k = _dot(ch8, wk_ref[h])[0:1]
        v = _dot(u8, wv_ref[h])[0:1]
        k_col = _dot_nt(wkt_ref[h], ch8)[:, 0:1]
        ig = gl[:, _SLAB_I + h:_SLAB_I + h + 1]
        lf = lf_all[:, _SLAB_F + h:_SLAB_F + h + 1]
        m0 = m_prev[:, h:h + 1]
        m_new = jnp.maximum(lf + m0, ig)
        qb = q.astype(BF16)
        qk = jnp.sum(qb.astype(F32) * k.astype(BF16).astype(F32), axis=-1, keepdims=True)
        w_in = jnp.exp(ig - m_new) * qk
        inter = jnp.exp(lf + m0 - m_new)
        c_old = c_ref[0, h]
        n_old = n_ref[0, h:h + 1, :]
        qc = _dot(jnp.broadcast_to(qb, (8, ML_D)), c_old.astype(BF16))[0:1]
        num = inter * qc + w_in * v
        den = inter * jnp.sum(q * n_old, axis=-1, keepdims=True) + w_in
        hh = num / jnp.maximum(jnp.abs(den), jnp.exp(-m_new))
        w_end = jnp.exp(ig - m_new)
        co_ref[0, h] = inter * c_old + k_col * (w_end * v)
        no_ref[0, h:h + 1, :] = inter * n_old + w_end * k
        m_out = jnp.where(lane1 == h, m_new, m_out)
        hn = hh * lax.rsqrt(jnp.mean(hh * hh, axis=-1, keepdims=True) + EPS) * mlg_ref[:, sl]
        outs.append(_sigmoid(oraw_ref[0, :, sl]) * hn)
    o_ref[0] = jnp.concatenate(outs, axis=1)
    mo_ref[0] = m_out


def _mlstm_sample(ext, oraw, slab, state_c, state_n, state_m, pw):
    db = ext.shape[0]
    m_in = jnp.pad(state_m, ((0, 0), (0, LANES - ML_H))).reshape(db, 1, LANES)
    consts = [pw['conv_w'], pw['conv_b'], pw['w_ml_q'], pw['w_ml_k'], pw['w_ml_v'], pw['w_ml_k_t'],
              pw['gate_bias_lanes'], pw['ml_gain']]

    def const_spec(a):
        nd = a.ndim
        return pl.BlockSpec(a.shape, lambda i: (0,) * nd)

    per_seq = lambda shape: pl.BlockSpec((1,) + shape, lambda i: (i,) + (0,) * len(shape))
    return pl.pallas_call(
        _mlstm_sample_kernel,
        grid=(db,),
        in_specs=[per_seq((CONV_W, ML_W)), per_seq((1, ML_W)), per_seq((1, LANES)), per_seq((ML_H, ML_D, ML_D)),
                  per_seq((ML_H, ML_D)), per_seq((1, LANES))] + [const_spec(a) for a in consts],
        out_specs=[per_seq((1, ML_W)), per_seq((ML_H, ML_D, ML_D)), per_seq((ML_H, ML_D)), per_seq((1, LANES))],
        out_shape=[jax.ShapeDtypeStruct((db, 1, ML_W), F32),
                   jax.ShapeDtypeStruct((db, ML_H, ML_D, ML_D), F32),
                   jax.ShapeDtypeStruct((db, ML_H, ML_D), F32),
                   jax.ShapeDtypeStruct((db, 1, LANES), F32)],
        compiler_params=_cparams(("arbitrary",), VMEM_LIMIT),
        name="mlstm_sample",
    )(ext, oraw, slab, state_c, state_n, m_in, *consts)


def _merge_kernel(x_ref, nm_ref, wg_ref, slab_ref, ex_ref, ocmp_ref, oslc_ref, owin_ref, oml_ref, omem_ref,
                  wb_ref, wout_ref, o_ref):
    x = x_ref[...]
    xn = _rms_rows(x, nm_ref[...]).astype(BF16)
    gparts = _split2(_sigmoid(slab_ref[...]))
    onsa = None
    for br, br_ref in enumerate((ocmp_ref, oslc_ref, owin_ref)):
        term = _dot_exact_rhs(gparts, ex_ref[br]) * br_ref[...]
        onsa = term if onsa is None else onsa + term
    z = None
    for n, on in enumerate((onsa, oml_ref[...], omem_ref[...])):
        gate = _sigmoid(_dot(xn, wg_ref[:, n * D_MODEL:(n + 1) * D_MODEL]))
        term = gate * _dot(on.astype(BF16), wb_ref[n])
        z = term if z is None else z + term
    o_ref[...] = x + _dot(z.astype(BF16), wout_ref[...])


def _merge(x2d, slab, ocmp, oslc, owin, oml, omem, pw, tm):
    t = x2d.shape[0]
    row = lambda w: pl.BlockSpec((tm, w), lambda i: (i, 0))
    const_spec = lambda a: _resident_spec(a)
    return pl.pallas_call(
        _merge_kernel,
        grid=(t // tm,),
        in_specs=[row(D_MODEL), const_spec(pw['norm_mix']), const_spec(pw['w_gate']), row(LANES),
                  const_spec(pw['gate_expand'])] + [row(BRANCH_W)] * 5
                 + [const_spec(pw['w_branch']), const_spec(pw['w_out'])],
        out_specs=row(D_MODEL),
        out_shape=jax.ShapeDtypeStruct((t, D_MODEL), F32),
        compiler_params=_cparams(("arbitrary",), VMEM_LIMIT),
        name="merge",
    )(x2d, pw['norm_mix'], pw['w_gate'], slab, pw['gate_expand'], ocmp, oslc, owin, oml, omem,
      pw['w_branch'], pw['w_out'])


def _ffn_kernel(x_ref, nf_ref, win_ref, wout_ref, o_ref):
    x = x_ref[...]
    xn = _rms_rows(x, nf_ref[...]).astype(BF16)
    a = _dot(xn, win_ref[:, 0:FFN_HID])
    b = _dot(xn, win_ref[:, FFN_HID:2 * FFN_HID])
    o_ref[...] = x + _dot((a * _sigmoid(a) * b).astype(BF16), wout_ref[...])


def _ffn(x2d, pw, tm):
    t = x2d.shape[0]
    consts = [pw['norm_ffn'], pw['w_ffn_in'], pw['w_ffn_out']]
    return pl.pallas_call(
        _ffn_kernel,
        grid=(t // tm,),
        in_specs=[pl.BlockSpec((tm, D_MODEL), lambda i: (i, 0))] + [_resident_spec(a) for a in consts],
        out_specs=pl.BlockSpec((tm, D_MODEL), lambda i: (i, 0)),
        out_shape=jax.ShapeDtypeStruct((t, D_MODEL), F32),
        compiler_params=_cparams(("arbitrary",), VMEM_LIMIT),
        name="ffn",
    )(x2d, *consts)


def _rope_tables(pos):
    half = ROT_DIM // 2
    freqs = ROPE_THETA ** (-jnp.arange(half, dtype=F32) / half)
    ang = pos.astype(F32)[:, None] * freqs
    cos, sin = jnp.cos(ang), jnp.sin(ang)
    n = pos.shape[0]
    ones = jnp.ones((n, HD - ROT_DIM), F32)
    zeros = jnp.zeros((n, HD - ROT_DIM), F32)
    zh = jnp.zeros((n, half), F32)
    ct = jnp.concatenate([cos, cos, ones], axis=1)
    sa = jnp.concatenate([-sin, zh, zeros], axis=1)
    sb = jnp.concatenate([zh, sin, zeros], axis=1)
    one64, zero64 = jnp.ones((n, HD), F32), jnp.zeros((n, HD), F32)
    q_tabs = [jnp.concatenate([t, t], axis=1) for t in (ct, sa, sb)]
    k_tabs = [jnp.concatenate([ct, one64], axis=1), jnp.concatenate([sa, zero64], axis=1),
              jnp.concatenate([sb, zero64], axis=1)]
    return q_tabs + k_tabs


def _prepare_weights(norm_mix, w_in, q_norm, k_norm_cmp, k_norm_slc, k_norm_win, cmp_pe, cmp_w1, cmp_b1, cmp_w2,
                     conv_w, conv_b, w_ml_q, w_ml_k, w_ml_v, b_igate, b_fgate, ml_norm, norm_mem, w_mem_kv,
                     mem_q_norm, mem_k_norm, w_branch, w_out, norm_ffn, w_ffn_in, w_ffn_out):
    widths = (NSA_H * HD, 3 * NSA_H, KV_W, KV_W, KV_W, ML_W, ML_W, ML_H, ML_H, MEM_W, N_BRANCH * D_MODEL)
    offs = np.concatenate([[0], np.cumsum(widths)])
    q, g, kvc, kvs, kvw, u, o, ig, fg, mq, mg = (w_in[:, int(offs[i]):int(offs[i + 1])] for i in range(len(widths)))
    slab_pad = jnp.zeros((D_MODEL, LANES - 2 * ML_H - 3 * NSA_H), F32)
    pw = {}
    pw['w_proj'] = jnp.concatenate([q, kvc, kvs, kvw, u, o, mq, ig, fg, g, slab_pad], axis=1).astype(BF16)
    pw['w_gate'] = mg.astype(BF16)
    pw['norm_mix'] = norm_mix.reshape(1, D_MODEL)
    pw['q_gain'] = jnp.tile(q_norm, NSA_H).reshape(1, NSA_H * HD)
    ones = jnp.ones((HD,), F32)
    pw['ks_gain'] = jnp.tile(jnp.concatenate([k_norm_slc, ones]), NSA_G).reshape(1, KV_W)
    pw['kw_gain'] = jnp.tile(jnp.concatenate([k_norm_win, ones]), NSA_G).reshape(1, KV_W)
    pw['kc_gain'] = jnp.concatenate([k_norm_cmp, ones]).reshape(1, LANES)
    pw['mq_gain'] = jnp.tile(mem_q_norm, MEM_H).reshape(1, MEM_W)
    head_of = np.arange(NSA_H * HD) // HD
    pw['seg64'] = jnp.asarray(head_of[:, None] == head_of[None, :], BF16)
    w1 = cmp_w1.reshape(2, CMP_BLOCK, HD, CMP_HID)
    eye = jnp.eye(2, dtype=F32)
    for name, part in (('cmp_wlo', w1[:, :CMP_STRIDE]), ('cmp_whi', w1[:, CMP_STRIDE:])):
        pw[name] = jnp.einsum('cjdh,ce->jcdeh', part, eye).reshape(CMP_STRIDE * 2 * HD, 2 * CMP_HID).astype(BF16)
    pw['cmp_w2'] = jnp.einsum('chd,ce->ched', cmp_w2, eye).reshape(2 * CMP_HID, 2 * HD).astype(BF16)
    pw['cmp_b1'] = cmp_b1.reshape(1, 2 * CMP_HID)
    pw['pe_lo'] = cmp_pe[:CMP_STRIDE].reshape(1, CMP_STRIDE * 2 * HD)
    pw['pe_hi'] = cmp_pe[CMP_STRIDE:].reshape(1, CMP_STRIDE * 2 * HD)
    pw['conv_w'] = conv_w
    pw['conv_b'] = conv_b.reshape(1, ML_W)
    pw['w_ml_q'], pw['w_ml_k'], pw['w_ml_v'] = (w.astype(BF16) for w in (w_ml_q, w_ml_k, w_ml_v))
    pw['w_ml_q_t'], pw['w_ml_k_t'], pw['w_ml_v_t'] = (jnp.swapaxes(w, 1, 2).astype(BF16) for w in (w_ml_q, w_ml_k, w_ml_v))
    pw['ml_gain_t'] = jnp.broadcast_to(ml_norm.reshape(ML_W, 1), (ML_W, LANES))
    gate_bias = jnp.concatenate([b_igate, b_fgate])
    pw['gate_bias_lanes'] = jnp.pad(gate_bias, (0, LANES - 2 * ML_H)).reshape(1, LANES)
    pw['gate_bias_rows'] = gate_bias.reshape(2 * ML_H, 1)
    pw['ml_gain'] = ml_norm.reshape(1, ML_W)
    pw['norm_mem'] = norm_mem.reshape(1, D_MODEL)
    pw['w_mem_kv'] = w_mem_kv.astype(BF16)
    pw['mem_k_gain'] = mem_k_norm.reshape(1, MEM_HD)
    ex = np.zeros((N_BRANCH, LANES, NSA_H * HD), np.float32)
    for br in range(N_BRANCH):
        for gh in range(NSA_H):
            ex[br, _SLAB_G + br * NSA_H + gh, gh * HD:(gh + 1) * HD] = 1.0
    pw['gate_expand'] = jnp.asarray(ex, BF16)
    pw['w_branch'] = w_branch.astype(BF16)
    pw['w_out'] = w_out.astype(BF16)
    pw['norm_ffn'] = norm_ffn.reshape(1, D_MODEL)
    pw['w_ffn_in'] = w_ffn_in.astype(BF16)
    pw['w_ffn_out'] = w_ffn_out.astype(BF16)
    return pw


def _overlap(n_cmp_rows, n_slc_cols):
    ci = np.arange(n_cmp_rows)[:, None] * CMP_STRIDE
    sj = np.arange(n_slc_cols)[None, :] * SLC_BLOCK
    return ((ci < sj + SLC_BLOCK) & (ci + CMP_BLOCK > sj)).astype(np.float32)


def _pick_tile(n, pref):
    t = min(n, pref)
    while n % t:
        t //= 2
    return t


def _prompt_group(x, mem, pw):
    b, s, _ = x.shape
    t = b * s
    x2d = x.reshape(t, D_MODEL)
    tm = _pick_tile(s, TOKEN_TILE)
    tabs = _rope_tables(jnp.arange(s, dtype=jnp.int32))
    (_, _, kvc, kvs, kvw, u, oraw, _, slab,
     qrot_t, kvc_t, kvs_t, kvw_t, qn_t, mq_t) = _project(x2d, tabs, pw, tm, seq_len=s)
    kvcmp, kvcmp_t = _compress_prompt(kvc.reshape(b, s, KV_W), pw)
    r = s // CMP_STRIDE
    ovt = jnp.asarray(_overlap(r, HD).T, BF16)
    ocmp, sel_t = _cmp_prompt(qn_t, kvcmp, kvcmp_t, ovt, b, s, _pick_tile(s, CMP_TQ))
    fq, fk = _pick_tile(s, FLASH_TQ), _pick_tile(s, FLASH_TK)
    oslc = _flash_prompt(qrot_t, kvs.reshape(b, s, KV_W), kvs_t, sel_t, b, s, fq, fk)
    owin = _flash_prompt(qrot_t, kvw.reshape(b, s, KV_W), kvw_t, None, b, s, fq, fk)
    chunk = _pick_tile(s, MLSTM_CHUNK)
    oml, c_t, n_new, m_new = _mlstm_prompt(u.reshape(b, s, ML_W), oraw.reshape(b, s, ML_W),
                                             slab.reshape(b, s, LANES), pw, chunk)
    t_mem = mem.shape[1]
    kvm, kvm_t = _mem_kv(mem, pw)
    omem = _mem_prompt(mq_t, kvm, kvm_t, b, s, _pick_tile(s, 512))
    x1 = _merge(x2d, slab, ocmp, oslc, owin, oml.reshape(t, ML_W), omem, pw, tm)
    y = _ffn(x1, pw, tm).reshape(b, s, D_MODEL)
    kv5 = lambda a_t: jnp.transpose(a_t.reshape(b, NSA_G, 2, HD, a_t.shape[-1]), (0, 4, 1, 2, 3))
    return dict(y=y, kv_cmp=kv5(kvc_t), kv_slc=kv5(kvs_t), kv_win_t=kvw_t, kv5=kv5,
                kv_mem=kvm.reshape(b, t_mem, MEM_H, 2, MEM_HD), c=jnp.swapaxes(c_t, 2, 3), n=n_new,
                m=m_new[:, 0, :ML_H],
                u=u.reshape(b, s, ML_W))


def _sample_group(x, cache_cmp, cache_slc, cache_win, cache_mem, state_c, state_n, state_m, state_conv,
                  page_table, pw):
    db, ds, _ = x.shape
    assert ds == 1, "the sample kernels handle one new token per sequence"
    n_pages = page_table.shape[1]
    past = n_pages * PAGE_SIZE
    pos = past
    x2d = x.reshape(db, D_MODEL)
    tabs = _rope_tables(jnp.full((db,), pos, jnp.int32))
    qn, qrot, kvc, kvs, kvw, u, oraw, mq, slab = _project(x2d, tabs, pw, db)
    kvcmp = _compress_paged(cache_cmp, page_table, kvc.reshape(db, 1, KV_W), pw)
    n_cmp = kvcmp.shape[2]
    n_slc = -(-(past + ds) // SLC_BLOCK)
    nslp = -(-n_slc // LANES) * LANES
    ov = jnp.asarray(_overlap(n_cmp, nslp), BF16)
    ocmp, sel = _cmp_sample(qn.reshape(db, NSA_H, HD), kvcmp, ov, pos, n_slc)
    q3 = qrot.reshape(db, NSA_G, NSA_HPG, 1, HD)
    qaug = (q3 * jnp.eye(NSA_G, dtype=BF16)[None, :, None, :, None]).reshape(db, NSA_H, NSA_G, HD)
    qaug = jnp.concatenate([qaug, jnp.zeros_like(qaug)], axis=-1).reshape(db, NSA_H, KV_W)
    oslc = _slc_sample(qaug, sel, cache_slc, page_table, kvs.reshape(db, 1, KV_W))
    w_buf = cache_win.shape[1]
    owin, win_new_t = _win_sample(qaug, _feature_major(cache_win), kvw.reshape(db, 1, KV_W), pos, past - w_buf)
    win_new = jnp.transpose(win_new_t.reshape(db, NSA_G, 2, HD, w_buf), (0, 4, 1, 2, 3))
    ext = jnp.concatenate([state_conv, u.reshape(db, 1, ML_W)], axis=1)
    oml, c_new, n_new, m_new = _mlstm_sample(ext, oraw.reshape(db, 1, ML_W), slab.reshape(db, 1, LANES),
                                             state_c, state_n, state_m, pw)
    q8 = jnp.pad(mq.reshape(db, MEM_H, MEM_HD), ((0, 0), (0, 8 - MEM_H), (0, 0)))
    omem = _mem_sample(q8, cache_mem)[:, :MEM_H].reshape(db, MEM_W)
    x1 = _merge(x2d, slab, ocmp.reshape(db, NSA_H * HD), oslc.reshape(db, NSA_H * HD), owin.reshape(db, NSA_H * HD),
                oml.reshape(db, ML_W), omem, pw, db)
    y = _ffn(x1, pw, db).reshape(db, ds, D_MODEL)
    kv5 = lambda a: a.reshape(db, ds, NSA_G, 2, HD)
    return dict(y=y, kv_cmp=kv5(kvc), kv_slc=kv5(kvs), win=win_new,
                c=c_new, n=n_new, m=m_new[:, 0, :ML_H], conv=ext[:, 1:])


def kernel(x_prompt, x_sample, cache_kv_cmp, cache_kv_slc, cache_kv_win, cache_kv_mem, state_C, state_n, state_m, state_conv, page_table, mem_prompt, norm_mix, w_in, q_norm, k_norm_cmp, k_norm_slc, k_norm_win, cmp_pe, cmp_w1, cmp_b1, cmp_w2, conv_w, conv_b, w_ml_q, w_ml_k, w_ml_v, b_igate, b_fgate, ml_norm, norm_mem, w_mem_kv, mem_q_norm, mem_k_norm, w_branch, w_out, norm_ffn, w_ffn_in, w_ffn_out):
    pw = _prepare_weights(norm_mix, w_in, q_norm, k_norm_cmp, k_norm_slc, k_norm_win, cmp_pe, cmp_w1, cmp_b1, cmp_w2,
                          conv_w, conv_b, w_ml_q, w_ml_k, w_ml_v, b_igate, b_fgate, ml_norm, norm_mem, w_mem_kv,
                          mem_q_norm, mem_k_norm, w_branch, w_out, norm_ffn, w_ffn_in, w_ffn_out)
    p = _prompt_group(x_prompt, mem_prompt, pw)
    s = _sample_group(x_sample, cache_kv_cmp, cache_kv_slc, cache_kv_win, cache_kv_mem, state_C, state_n, state_m,
                      state_conv, page_table, pw)
    b, seq = x_prompt.shape[:2]
    w_buf = cache_kv_win.shape[1]
    assert seq >= w_buf
    win_p = p['kv5'](p['kv_win_t'][:, :, seq - w_buf:])
    conv_p = p['u'][:, seq - (CONV_W - 1):]
    return (p['y'], s['y'], p['kv_cmp'], s['kv_cmp'], p['kv_slc'], s['kv_slc'], win_p, s['win'], p['kv_mem'],
            p['c'], s['c'], p['n'], s['n'], p['m'], s['m'], conv_p, s['conv'])
```

```python
import functools

import numpy as np
import jax
import jax.numpy as jnp
from jax import lax
from jax.experimental import pallas as pl
from jax.experimental.pallas import tpu as pltpu

F32 = jnp.float32
BF16 = jnp.bfloat16

D_MODEL = 1024
PAGE_SIZE = 128
NSA_H, NSA_G, NSA_HPG, HD = 8, 2, 4, 64
ROT_DIM = HD // 4
ROPE_THETA = 500000.0
CMP_BLOCK, CMP_STRIDE, CMP_HID = 32, 16, 128
SLC_BLOCK, N_SEL = 64, 16
WINDOW = 512
ML_H, ML_D, ML_W, CONV_W = 4, 128, 512, 4
MEM_H, MEM_HD, MEM_W = 4, 128, 512
N_BRANCH, BRANCH_W = 3, 512
FFN_HID = -(-8 * D_MODEL // (3 * 256)) * 256
EPS = 1e-6

KV_W = 2 * NSA_G * HD
CHUNK_W = CMP_STRIDE * KV_W
LANES = 128
MASKED = -1e30
UNSELECTED = -32768.0
VMEM_LIMIT = 56 * 1024 * 1024
TOKEN_TILE = 512
CMP_TQ = 512
FLASH_TQ, FLASH_TK = 512, 256
FLASH_COLS = 128
FLASH_ONES_ROWS = 64
MLSTM_CHUNK = 256
SAMPLE_SEQS_PER_STEP = 4

_Q0, _KVC0, _KVS0, _KVW0, _U0, _O0, _MQ0, _SLAB0, _PROJ_W = 0, 512, 768, 1024, 1280, 1792, 2304, 2816, 2944
_SLAB_I, _SLAB_F, _SLAB_G = 0, 4, 8


def _dot(a, b):
    return jnp.dot(a, b, preferred_element_type=F32)


def _dot_nt(a, b):
    return lax.dot_general(a, b, (((1,), (1,)), ((), ())), preferred_element_type=F32)


def _split2(x):
    hi = x.astype(BF16)
    return hi, (x - hi.astype(F32)).astype(BF16)


def _split3(x):
    hi = x.astype(BF16)
    r = x - hi.astype(F32)
    mid = r.astype(BF16)
    return hi, mid, (r - mid.astype(F32)).astype(BF16)


def _dot_exact_rhs(parts, m):
    out = _dot(parts[0], m)
    for p in parts[1:]:
        out = out + _dot(p, m)
    return out


def _dot_exact_rhs_left(m, parts):
    out = _dot(m, parts[0])
    for p in parts[1:]:
        out = out + _dot(m, p)
    return out


def _sigmoid(x):
    return 1.0 / (1.0 + jnp.exp(-x))


def _log_sigmoid(x):
    return jnp.minimum(x, 0.0) - jnp.log(1.0 + jnp.exp(-jnp.abs(x)))


def _gelu_tanh(x):
    return x * (0.5 * (1.0 + jnp.tanh(np.sqrt(2.0 / np.pi).astype(np.float32) * (x + 0.044715 * (x * x * x)))))


def _rms_rows(x, gain):
    ms = jnp.mean(x * x, axis=-1, keepdims=True)
    return x * lax.rsqrt(ms + EPS) * gain


def _iota(shape, dim):
    return lax.broadcasted_iota(jnp.int32, shape, dim)


def _cparams(sem, vmem=None):
    return pltpu.CompilerParams(dimension_semantics=sem, vmem_limit_bytes=vmem)


def _resident_spec(a):
    nd = a.ndim
    return pl.BlockSpec(a.shape, lambda *_: (0,) * nd, pipeline_mode=pl.Buffered(1))


def _masked_softmax(s, mask):
    s = jnp.where(mask, s, -jnp.inf)
    m = jnp.max(s, axis=-1, keepdims=True)
    m = jnp.where(m > -jnp.inf, m, 0.0)
    p = jnp.exp(s - m)
    return p / jnp.maximum(jnp.sum(p, axis=-1, keepdims=True), jnp.finfo(F32).tiny)


def _rs_heads64(a, seg_ref, width):
    ss = _dot_exact_rhs(_split2(a * a), seg_ref[0:width, 0:width])
    return lax.rsqrt(ss * (1.0 / HD) + EPS)


def _rope(a, ct, sa, sb, width):
    reps = width // LANES
    if reps > 1:
        ct, sa, sb = (jnp.concatenate([t] * reps, axis=1) for t in (ct, sa, sb))
    half = ROT_DIM // 2
    return a * ct + pltpu.roll(a, width - half, 1) * sa + pltpu.roll(a, half, 1) * sb


def _proj_kernel(x_ref, nm_ref, w_ref, qg_ref, ksg_ref, kwg_ref, mqg_ref, seg_ref,
                 ctq_ref, saq_ref, sbq_ref, ctk_ref, sak_ref, sbk_ref,
                 qn_o, qr_o, kvc_o, kvs_o, kvw_o, u_o, o_o, mq_o, slab_o, *fm_outs):
    xn = _rms_rows(x_ref[...], nm_ref[...]).astype(BF16)

    def seg(lo, hi):
        return _dot(xn, w_ref[:, lo:hi])

    q = seg(_Q0, _KVC0)
    qn = q * _rs_heads64(q, seg_ref, NSA_H * HD) * qg_ref[...]
    scale = HD ** -0.5
    qn_o[...] = (qn * scale).astype(BF16)
    qr = _rope(qn, ctq_ref[...], saq_ref[...], sbq_ref[...], NSA_H * HD) * scale
    qr_o[...] = qr.astype(BF16)
    kvc = seg(_KVC0, _KVS0)
    kvc_o[...] = kvc
    if fm_outs:
        fm_outs[0][0] = qr.T.astype(BF16)
        fm_outs[1][0] = kvc.T
        fm_outs[4][0] = (qn * scale).T.astype(BF16)
    for n, (lo, g_ref, o_ref) in enumerate(((_KVS0, ksg_ref, kvs_o), (_KVW0, kwg_ref, kvw_o))):
        a = seg(lo, lo + KV_W)
        is_k = (_iota(a.shape, 1) & HD) == 0
        an = jnp.where(is_k, a * _rs_heads64(a, seg_ref, KV_W) * g_ref[...], a)
        ar = _rope(an, ctk_ref[...], sak_ref[...], sbk_ref[...], KV_W)
        o_ref[...] = ar
        if fm_outs:
            fm_outs[2 + n][0] = ar.T
    u_o[...] = seg(_U0, _O0)
    o_o[...] = seg(_O0, _MQ0)
    mq = seg(_MQ0, _SLAB0)
    heads = []
    for h in range(MEM_H):
        mh = mq[:, h * MEM_HD:(h + 1) * MEM_HD]
        heads.append(mh * lax.rsqrt(jnp.mean(mh * mh, axis=-1, keepdims=True) + EPS))
    mqn = jnp.concatenate(heads, axis=1) * mqg_ref[...] * (MEM_HD ** -0.5)
    mq_o[...] = mqn.astype(BF16)
    if fm_outs:
        fm_outs[5][0] = mqn.T.astype(BF16)
    slab_o[...] = seg(_SLAB0, _PROJ_W)


def _project(x2d, tabs, pw, tm, seq_len=None):
    t = x2d.shape[0]
    ntab = tabs[0].shape[0] // tm
    row = lambda i: (i, 0)
    const = lambda i: (0, 0)
    tab = lambda i: (i % ntab, 0)
    full = lambda a: _resident_spec(a)
    consts = [pw['norm_mix'], pw['w_proj'], pw['q_gain'], pw['ks_gain'], pw['kw_gain'], pw['mq_gain'],
              pw['seg64']]
    widths = [(512, BF16), (512, BF16), (KV_W, F32), (KV_W, F32), (KV_W, F32), (ML_W, F32), (ML_W, F32),
              (MEM_W, BF16), (LANES, F32)]
    out_specs = [pl.BlockSpec((tm, w), row) for w, _ in widths]
    out_shape = [jax.ShapeDtypeStruct((t, w), d) for w, d in widths]
    if seq_len is not None:
        per_seq = seq_len // tm
        for w, d in ((NSA_H * HD, BF16), (KV_W, F32), (KV_W, F32), (KV_W, F32), (NSA_H * HD, BF16), (MEM_W, BF16)):
            out_specs.append(pl.BlockSpec((1, w, tm), lambda i: (i // per_seq, 0, i % per_seq)))
            out_shape.append(jax.ShapeDtypeStruct((t // seq_len, w, seq_len), d))
    return pl.pallas_call(
        _proj_kernel,
        grid=(t // tm,),
        in_specs=[pl.BlockSpec((tm, D_MODEL), row)] + [full(a) for a in consts]
                 + [pl.BlockSpec((tm, LANES), tab)] * 6,
        out_specs=out_specs,
        out_shape=out_shape,
        compiler_params=_cparams(("arbitrary",), VMEM_LIMIT),
        name="project",
    )(x2d, *consts, *tabs)


def _compress_group(y_lo, y_hi, tail_hi, wlo, whi, b1, w2, gain, n_valid):
    r = y_lo.shape[0]
    a = _dot(y_lo, wlo)
    bh = _dot(y_hi, whi)
    bs = pltpu.roll(bh, r - 1, 0)
    if tail_hi is not None:
        bt = _dot(tail_hi, whi)
        bs = jnp.where(_iota(bs.shape, 0) == r - 1, bt, bs)
    hid = _gelu_tanh(a + bs + b1)
    out = _dot(hid.astype(BF16), w2)
    is_k = _iota(out.shape, 1) < HD
    ss = jnp.sum(jnp.where(is_k, out * out, 0.0), axis=-1, keepdims=True) * (1.0 / HD)
    kv = jnp.where(is_k, out * lax.rsqrt(ss + EPS) * gain, out)
    return jnp.where(_iota(kv.shape, 0) < n_valid, kv, 0.0)


def _group_cols(load, g):
    return jnp.concatenate([load(j * KV_W + g * LANES, j * KV_W + (g + 1) * LANES) for j in range(CMP_STRIDE)], axis=1)


def _compress_prompt_kernel(y_ref, pe_lo, pe_hi, wlo, whi, b1, w2, gain, o_ref, ot_ref, *, n_valid):
    for g in range(NSA_G):
        yg = _group_cols(lambda lo, hi: y_ref[0, :, lo:hi], g)
        kv = _compress_group((yg + pe_lo[...]).astype(BF16), (yg + pe_hi[...]).astype(BF16), None,
                             wlo[...], whi[...], b1[...], w2[...], gain[...], n_valid)
        o_ref[0, g] = kv.astype(BF16)
        ot_ref[0, g] = kv.T.astype(BF16)


def _compress_paged_kernel(pt_ref, *refs, n_pages, n_valid):
    pages = refs[:n_pages]
    tail_ref, pe_lo, pe_hi, wlo, whi, b1, w2, gain, o_ref, y = refs[n_pages:]
    cpp = PAGE_SIZE // CMP_STRIDE
    for k, page in enumerate(pages):
        for g in range(NSA_G):
            nat = page[0, g * LANES:(g + 1) * LANES, :].T
            by_row = jnp.swapaxes(nat.reshape(cpp, CMP_STRIDE, LANES), 0, 1)
            for j in range(CMP_STRIDE):
                y[g, k * cpp:(k + 1) * cpp, j * LANES:(j + 1) * LANES] = by_row[j]
    for g in range(NSA_G):
        yg = y[g]
        tg = _group_cols(lambda lo, hi: tail_ref[0, :, lo:hi], g)
        o_ref[0, g] = _compress_group((yg + pe_lo[...]).astype(BF16), (yg + pe_hi[...]).astype(BF16),
                                      (tg + pe_hi[...]).astype(BF16), wlo[...], whi[...], b1[...], w2[...],
                                      gain[...], n_valid).astype(BF16)


def _compress_consts(pw):
    return [pw['cmp_wlo'], pw['cmp_whi'], pw['cmp_b1'], pw['cmp_w2'], pw['kc_gain']]


def _compress_prompt(kvc, pw):
    b, s, _ = kvc.shape
    r = s // CMP_STRIDE
    y = kvc.reshape(b, r, CHUNK_W)
    consts = [pw['pe_lo'], pw['pe_hi']] + _compress_consts(pw)
    return pl.pallas_call(
        functools.partial(_compress_prompt_kernel, n_valid=r - 1),
        grid=(b,),
        in_specs=[pl.BlockSpec((1, r, CHUNK_W), lambda i: (i, 0, 0))]
                 + [pl.BlockSpec(a.shape, lambda i: (0, 0)) for a in consts],
        out_specs=[pl.BlockSpec((1, NSA_G, r, LANES), lambda i: (i, 0, 0, 0)),
                   pl.BlockSpec((1, NSA_G, LANES, r), lambda i: (i, 0, 0, 0))],
        out_shape=[jax.ShapeDtypeStruct((b, NSA_G, r, LANES), BF16),
                   jax.ShapeDtypeStruct((b, NSA_G, LANES, r), BF16)],
        compiler_params=_cparams(("arbitrary",), VMEM_LIMIT),
        name="compress_prompt",
    )(y, *consts)


def _compress_paged(cache, page_table, kvc_new, pw):
    db, n_pages = page_table.shape
    chunks_per_page = PAGE_SIZE // CMP_STRIDE
    r = n_pages * chunks_per_page
    pool = _feature_major(cache)
    tail = jnp.pad(kvc_new, ((0, 0), (0, CMP_STRIDE - kvc_new.shape[1]), (0, 0))).reshape(db, 1, CHUNK_W)
    consts = [pw['pe_lo'], pw['pe_hi']] + _compress_consts(pw)
    page_specs = [pl.BlockSpec((1, KV_W, PAGE_SIZE), functools.partial(lambda i, pt, k: (pt[i, k], 0, 0), k=k))
                  for k in range(n_pages)]
    return pl.pallas_call(
        functools.partial(_compress_paged_kernel, n_pages=n_pages, n_valid=r),
        grid_spec=pltpu.PrefetchScalarGridSpec(
            num_scalar_prefetch=1,
            grid=(db,),
            in_specs=page_specs + [pl.BlockSpec((1, 1, CHUNK_W), lambda i, pt: (i, 0, 0))]
                     + [pl.BlockSpec(a.shape, lambda i, pt: (0, 0)) for a in consts],
            out_specs=pl.BlockSpec((1, NSA_G, r, LANES), lambda i, pt: (i, 0, 0, 0)),
            scratch_shapes=[pltpu.VMEM((NSA_G, r, CMP_STRIDE * LANES), F32)],
        ),
        out_shape=jax.ShapeDtypeStruct((db, NSA_G, r, LANES), BF16),
        compiler_params=_cparams(("arbitrary",), VMEM_LIMIT),
        name="compress_paged",
    )(page_table, *([pool] * n_pages), tail, *consts)


def _rank_select(score, idx, n_blocks):
    cnt = jnp.zeros(score.shape, jnp.int32)
    for i in range(n_blocks):
        other = score[:, i:i + 1]
        ge = jnp.where(other >= score, 1, 0)
        gt = jnp.where(other > score, 1, 0)
        cnt = cnt + jnp.where(idx > i, ge, gt)
    return jnp.where(score > -jnp.inf, jnp.where(cnt < N_SEL, 0.0, UNSELECTED), UNSELECTED)


def _rank_select_rows(score, n_blocks, last_pos, cnt_ref):
    sub = _iota((8, score.shape[1]), 0)
    n_groups = -(-n_blocks // 8)
    groups = [score[g * 8:(g + 1) * 8, :] for g in range(n_groups)]
    cnt_ref[...] = jnp.zeros(cnt_ref.shape, jnp.int32)
    for og in range(n_groups):
        @pl.when(og * 8 * SLC_BLOCK <= last_pos)
        def _():
            part = [None] * n_groups
            for i in range(og * 8, min(og * 8 + 8, n_blocks)):
                other = score[i:i + 1, :]
                for g, sg in enumerate(groups):
                    if g > og:
                        one = jnp.where(other >= sg, 1, 0)
                    elif g < og:
                        one = jnp.where(other > sg, 1, 0)
                    else:
                        one = jnp.where(sub > i % 8, jnp.where(other >= sg, 1, 0), jnp.where(other > sg, 1, 0))
                    part[g] = one if part[g] is None else part[g] + one
            for g in range(n_groups):
                cnt_ref[g * 8:(g + 1) * 8, :] += part[g]
    cnts = [cnt_ref[g * 8:(g + 1) * 8, :] for g in range(n_groups)]
    out = [jnp.where(sg > -jnp.inf, jnp.where(c < N_SEL, 0.0, UNSELECTED), UNSELECTED) for sg, c in zip(groups, cnts)]
    pad = score.shape[0] - 8 * n_groups
    if pad:
        out.append(jnp.full((pad, score.shape[1]), UNSELECTED, F32))
    return jnp.concatenate(out, axis=0)


def _cmp_prompt_kernel(qt_ref, kv_ref, kvt_ref, ovt_ref, o_ref, sel_ref, cnt_ref, *, tq, n_slc):
    i = pl.program_id(2)
    kv = kv_ref[0, 0]
    kvt = kvt_ref[0, 0]
    r = kv.shape[0]
    pos = i * tq + _iota((1, tq), 1)
    visible = (_iota((r, 1), 0) * CMP_STRIDE + (CMP_BLOCK - 1)) <= pos
    zeros = jnp.zeros((HD, tq), BF16)
    psum = jnp.zeros((r, tq), F32)
    outs = []
    for h in range(NSA_HPG):
        qa = jnp.concatenate([qt_ref[0, h * HD:(h + 1) * HD, :], zeros], axis=0)
        s = jnp.where(visible, _dot(kv, qa), -jnp.inf)
        m = jnp.max(s, axis=0, keepdims=True)
        e = jnp.exp(s - jnp.where(m > -jnp.inf, m, 0.0))
        inv_l = 1.0 / jnp.maximum(jnp.sum(e, axis=0, keepdims=True), jnp.finfo(F32).tiny)
        psum = psum + e * inv_l
        outs.append(_dot(kvt[HD:2 * HD, :], e.astype(BF16)) * inv_l)
    o_ref[...] = jnp.concatenate([jnp.concatenate(outs[0:2], axis=0).T, jnp.concatenate(outs[2:4], axis=0).T], axis=1)
    imp = _dot_exact_rhs_left(ovt_ref[...], _split2(psum))
    blk = _iota(imp.shape, 0)
    pos_row = i * tq + _iota(imp.shape, 1)
    forced = (blk == jnp.right_shift(pos_row, 6)) | (blk == 0)
    score = jnp.where(forced, jnp.inf, jnp.where(blk * SLC_BLOCK <= pos_row, imp, -jnp.inf))
    sel_ref[0, 0] = _rank_select_rows(score, n_slc, i * tq + tq - 1, cnt_ref).astype(BF16)


def _cmp_prompt(qn_t, kvcmp, kvcmp_t, ovt, b, s, tq):
    nq = s // tq
    r = kvcmp.shape[2]
    n_slc = s // SLC_BLOCK
    assert n_slc <= HD and ovt.shape == (HD, r) and SLC_BLOCK == 64
    return pl.pallas_call(
        functools.partial(_cmp_prompt_kernel, tq=tq, n_slc=n_slc),
        grid=(b, NSA_G, nq),
        in_specs=[pl.BlockSpec((1, NSA_HPG * HD, tq), lambda bi, g, i: (bi, g, i)),
                  pl.BlockSpec((1, 1, r, LANES), lambda bi, g, i: (bi, g, 0, 0)),
                  pl.BlockSpec((1, 1, LANES, r), lambda bi, g, i: (bi, g, 0, 0)),
                  pl.BlockSpec(ovt.shape, lambda bi, g, i: (0, 0))],
        out_specs=[pl.BlockSpec((tq, NSA_HPG * HD), lambda bi, g, i: (bi * nq + i, g)),
                   pl.BlockSpec((1, 1, HD, tq), lambda bi, g, i: (bi, g, 0, i))],
        out_shape=[jax.ShapeDtypeStruct((b * s, NSA_H * HD), F32),
                   jax.ShapeDtypeStruct((b, NSA_G, HD, s), BF16)],
        scratch_shapes=[pltpu.VMEM((HD, tq), jnp.int32)],
        compiler_params=_cparams(("arbitrary",) * 3, VMEM_LIMIT),
        name="cmp_attend_select_prompt",
    )(qn_t, kvcmp, kvcmp_t, ovt)


def _cmp_sample_kernel(q_ref, kv_ref, ov_ref, o_ref, sel_ref, *, pos, n_slc):
    r = kv_ref.shape[2]
    row = _iota((NSA_H, 1), 0)
    visible = (_iota((1, r), 1) * CMP_STRIDE + (CMP_BLOCK - 1)) <= pos
    n_seq = q_ref.shape[0]
    scores = []
    for bi in range(n_seq):
        q = q_ref[bi]
        kvs = [kv_ref[bi, g] for g in range(NSA_G)]
        s = jnp.where(row < NSA_HPG, _dot_nt(q, kvs[0][:, 0:HD]), _dot_nt(q, kvs[1][:, 0:HD]))
        p = _masked_softmax(s, visible)
        pb = p.astype(BF16)
        o = jnp.where(row < NSA_HPG, _dot(pb, kvs[0]), _dot(pb, kvs[1]))
        o_ref[bi] = o[:, HD:2 * HD]
        p0 = jnp.sum(p[0:NSA_HPG], axis=0, keepdims=True)
        p1 = jnp.sum(p[NSA_HPG:NSA_H], axis=0, keepdims=True)
        psum = jnp.where(row < NSA_HPG, p0, p1)
        imp = _dot_exact_rhs(_split2(psum), ov_ref[...])
        blk = _iota(imp.shape, 1)
        forced = (blk == pos // SLC_BLOCK) | (blk == 0)
        scores.append(jnp.where(forced, jnp.inf, jnp.where(blk * SLC_BLOCK <= pos, imp, -jnp.inf)))
    stacked = jnp.concatenate(scores, axis=0)
    sel = _rank_select(stacked, _iota(stacked.shape, 1), n_slc)
    for bi in range(n_seq):
        sel_ref[bi] = sel[bi * NSA_H:(bi + 1) * NSA_H]


def _cmp_sample(qn3, kvcmp, ov, pos, n_slc):
    db, _, r, _ = kvcmp.shape
    nslp = ov.shape[1]
    bb = _pick_tile(db, SAMPLE_SEQS_PER_STEP)
    return pl.pallas_call(
        functools.partial(_cmp_sample_kernel, pos=pos, n_slc=n_slc),
        grid=(db // bb,),
        in_specs=[pl.BlockSpec((bb, NSA_H, HD), lambda i: (i, 0, 0)),
                  pl.BlockSpec((bb, NSA_G, r, LANES), lambda i: (i, 0, 0, 0)),
                  pl.BlockSpec(ov.shape, lambda i: (0, 0))],
        out_specs=[pl.BlockSpec((bb, NSA_H, HD), lambda i: (i, 0, 0)),
                   pl.BlockSpec((bb, NSA_H, nslp), lambda i: (i, 0, 0))],
        out_shape=[jax.ShapeDtypeStruct((db, NSA_H, HD), F32),
                   jax.ShapeDtypeStruct((db, NSA_H, nslp), F32)],
        compiler_params=_cparams(("arbitrary",), VMEM_LIMIT),
        name="cmp_attend_select_sample",
    )(qn3, kvcmp, ov)


def _flash_prompt_kernel(*refs, tq, selected):
    if selected:
        qt_ref, kv_ref, kvt_ref, sel_ref, o_ref, kop, vop, qa_ref, m_ref, acc_ref = refs
    else:
        qt_ref, kv_ref, kvt_ref, o_ref, kop, vop, qa_ref, m_ref, acc_ref = refs
    i = pl.program_id(2)
    cols = NSA_HPG * tq
    n_tiles, v_rows, tk = vop.shape
    ones_rows = v_rows - HD
    wide = FLASH_COLS

    @pl.when(i == 0)
    def _():
        kvf = kv_ref[0]
        if selected:
            lane = _iota(kvf.shape, 1)
            onehot = (lane - HD) == jnp.right_shift(_iota(kvf.shape, 0), 6)
            kop[...] = jnp.where(lane < HD, kvf, onehot.astype(F32)).astype(BF16)
        else:
            kop[...] = kvf.astype(BF16)
        for t in range(n_tiles):
            vop[t, 0:ones_rows, :] = jnp.ones((ones_rows, tk), BF16)
            vop[t, ones_rows:v_rows, :] = kvt_ref[0, HD:2 * HD, t * tk:(t + 1) * tk].astype(BF16)

    extra = sel_ref[0, 0] if selected else jnp.zeros((HD, tq), BF16)
    for h in range(NSA_HPG):
        qa_ref[0:HD, h * tq:(h + 1) * tq] = qt_ref[0, h * HD:(h + 1) * HD, :]
        qa_ref[HD:2 * HD, h * tq:(h + 1) * tq] = extra
    m_ref[...] = jnp.full((1, cols), MASKED, F32)
    acc_ref[...] = jnp.zeros((v_rows, cols), F32)

    def tile_step(j, masked):
        start = pl.multiple_of(j * tk, tk)
        for cb in range(cols // wide):
            sl = slice(cb * wide, (cb + 1) * wide)
            s = _dot(kop[pl.ds(start, tk), :], qa_ref[:, sl])
            if masked:
                pos = i * tq + (cb * wide) % tq + _iota((1, wide), 1)
                kpos = j * tk + _iota((tk, 1), 0)
                valid = kpos <= pos
                if not selected:
                    valid = valid & ((pos - kpos) <= WINDOW)
                s = jnp.where(valid, s, MASKED)
            m_old = m_ref[:, sl]
            m_new = jnp.maximum(m_old, jnp.max(s, axis=0, keepdims=True))
            p = jnp.exp(s - m_new).astype(BF16)
            m_ref[:, sl] = m_new
            acc_ref[:, sl] = jnp.exp(m_old - m_new) * acc_ref[:, sl] + _dot(vop[j], p)

    per_q = tq // tk

    def run(lo, hi, masked):
        def body(t, carry):
            for u in range(per_q):
                tile_step(t * per_q + u, masked)
            return carry
        lax.fori_loop(lo, hi, body, 0)

    if selected:
        run(0, i, False)
        run(i, i + 1, True)
    else:
        run(jnp.maximum(i - WINDOW // tq, 0), i + 1, True)
    normed = []
    for h in range(NSA_HPG):
        a = acc_ref[:, h * tq:(h + 1) * tq]
        normed.append(a[ones_rows:v_rows, :] * (1.0 / a[0:1, :]))
    o_ref[...] = jnp.concatenate([jnp.concatenate(normed[0:2], axis=0).T, jnp.concatenate(normed[2:4], axis=0).T], axis=1)


def _flash_prompt(qrot_t, kv, kv_t, sel_t, b, s, tq, tk):
    nq = s // tq
    assert WINDOW % tq == 0 and tq % tk == 0 and tk % LANES == 0 and s % tq == 0 and tq % FLASH_COLS == 0
    v_rows = FLASH_ONES_ROWS + HD
    selected = sel_t is not None
    in_specs = [pl.BlockSpec((1, NSA_HPG * HD, tq), lambda bi, g, i: (bi, g, i)),
                pl.BlockSpec((1, s, LANES), lambda bi, g, i: (bi, 0, g)),
                pl.BlockSpec((1, LANES, s), lambda bi, g, i: (bi, g, 0))]
    args = [qrot_t, kv, kv_t]
    cols = NSA_HPG * tq
    scratch = [pltpu.VMEM((s, LANES), BF16), pltpu.VMEM((s // tk, v_rows, tk), BF16),
               pltpu.VMEM((LANES, cols), BF16), pltpu.VMEM((1, cols), F32), pltpu.VMEM((v_rows, cols), F32)]
    if selected:
        in_specs.append(pl.BlockSpec((1, 1, HD, tq), lambda bi, g, i: (bi, g, 0, i)))
        args.append(sel_t)
    return pl.pallas_call(
        functools.partial(_flash_prompt_kernel, tq=tq, selected=selected),
        grid=(b, NSA_G, nq),
        in_specs=in_specs,
        out_specs=pl.BlockSpec((tq, NSA_HPG * HD), lambda bi, g, i: (bi * nq + i, g)),
        out_shape=jax.ShapeDtypeStruct((b * s, NSA_H * HD), F32),
        scratch_shapes=scratch,
        compiler_params=_cparams(("arbitrary",) * 3, VMEM_LIMIT),
        name="slc_attend_prompt" if selected else "win_attend_prompt",
    )(*args)


def _mem_kv_kernel(x_ref, nm_ref, w_ref, kg_ref, o_ref, ot_ref):
    xn = _rms_rows(x_ref[0], nm_ref[...]).astype(BF16)
    kv = _dot(xn, w_ref[...])
    cols = []
    for h in range(MEM_H):
        k = kv[:, 2 * h * MEM_HD:(2 * h + 1) * MEM_HD]
        cols.append(_rms_rows(k, kg_ref[...]))
        cols.append(kv[:, (2 * h + 1) * MEM_HD:(2 * h + 2) * MEM_HD])
    out = jnp.concatenate(cols, axis=1)
    o_ref[0] = out
    ot_ref[0] = out.T.astype(BF16)


def _mem_kv(mem, pw):
    b, t_mem, _ = mem.shape
    consts = [pw['norm_mem'], pw['w_mem_kv'], pw['mem_k_gain']]
    return pl.pallas_call(
        _mem_kv_kernel,
        grid=(b,),
        in_specs=[pl.BlockSpec((1, t_mem, D_MODEL), lambda i: (i, 0, 0))]
                 + [pl.BlockSpec(a.shape, lambda i: (0, 0)) for a in consts],
        out_specs=[pl.BlockSpec((1, t_mem, 2 * MEM_W), lambda i: (i, 0, 0)),
                   pl.BlockSpec((1, 2 * MEM_W, t_mem), lambda i: (i, 0, 0))],
        out_shape=[jax.ShapeDtypeStruct((b, t_mem, 2 * MEM_W), F32),
                   jax.ShapeDtypeStruct((b, 2 * MEM_W, t_mem), BF16)],
        compiler_params=_cparams(("arbitrary",), VMEM_LIMIT),
        name="mem_kv",
    )(mem, *consts)


def _mem_prompt_kernel(qt_ref, kv_ref, kvt_ref, o_ref):
    for h in range(MEM_H):
        k = kv_ref[0, :, 2 * h * MEM_HD:(2 * h + 1) * MEM_HD].astype(BF16)
        s = _dot(k, qt_ref[0, h * MEM_HD:(h + 1) * MEM_HD, :])
        p = jnp.exp(s - jnp.max(s, axis=0, keepdims=True))
        vt = kvt_ref[0, (2 * h + 1) * MEM_HD:(2 * h + 2) * MEM_HD, :]
        ot = _dot(vt, p.astype(BF16)) * (1.0 / jnp.sum(p, axis=0, keepdims=True))
        o_ref[:, h * MEM_HD:(h + 1) * MEM_HD] = ot.T


def _mem_prompt(mq_t, kvm, kvm_t, b, s, tq):
    nq = s // tq
    t_mem = kvm.shape[1]
    return pl.pallas_call(
        _mem_prompt_kernel,
        grid=(b, nq),
        in_specs=[pl.BlockSpec((1, MEM_W, tq), lambda bi, i: (bi, 0, i)),
                  pl.BlockSpec((1, t_mem, 2 * MEM_W), lambda bi, i: (bi, 0, 0)),
                  pl.BlockSpec((1, 2 * MEM_W, t_mem), lambda bi, i: (bi, 0, 0))],
        out_specs=pl.BlockSpec((tq, MEM_W), lambda bi, i: (bi * nq + i, 0)),
        out_shape=jax.ShapeDtypeStruct((b * s, MEM_W), F32),
        compiler_params=_cparams(("arbitrary",) * 2, VMEM_LIMIT),
        name="mem_attend_prompt",
    )(mq_t, kvm, kvm_t)


def _decode_attend(qa, blocks, new_row=None, new_bias=None, feature_major=False):
    qk, pv = (_dot, _dot_nt) if feature_major else (_dot_nt, _dot)
    scores = []
    for load, bias in blocks:
        s = qk(qa, load())
        scores.append(s if bias is None else s + bias)
    m = functools.reduce(jnp.maximum, [jnp.max(s, axis=-1, keepdims=True) for s in scores])
    if new_row is not None:
        nb = new_row.astype(BF16).astype(F32)
        s_new = jnp.sum(qa.astype(F32) * nb, axis=-1, keepdims=True) + new_bias
        m = jnp.maximum(m, s_new)
    ps = [jnp.exp(s - m) for s in scores]
    l = functools.reduce(lambda a, c: a + c, [jnp.sum(p, axis=-1, keepdims=True) for p in ps])
    acc = functools.reduce(lambda a, c: a + c, [pv(p.astype(BF16), load()) for p, (load, _) in zip(ps, blocks)])
    if new_row is not None:
        p_new = jnp.exp(s_new - m)
        l = l + p_new
        acc = acc + p_new.astype(BF16).astype(F32) * nb
    return acc / l


def _group_values(o):
    row = _iota((NSA_H, LANES), 0)
    both = jnp.where(row < NSA_HPG, o[:, 0:LANES], o[:, LANES:2 * LANES])
    return both[:, HD:2 * HD]


def _slc_sample_kernel(pt_ref, *refs, n_pages):
    pages = refs[:n_pages]
    q_ref, sel_ref, new_ref, o_ref = refs[n_pages:]
    sb = sel_ref[0]
    lane = _iota((NSA_H, PAGE_SIZE), 1)
    blocks = []
    for p, page in enumerate(pages):
        bias = jnp.where(lane < SLC_BLOCK, sb[:, 2 * p:2 * p + 1], sb[:, 2 * p + 1:2 * p + 2])
        blocks.append((lambda page=page: page[0].astype(BF16), bias))
    n_past = n_pages * (PAGE_SIZE // SLC_BLOCK)
    o = _decode_attend(q_ref[0], blocks, new_ref[0], sb[:, n_past:n_past + 1], feature_major=True)
    o_ref[0] = _group_values(o)


def _feature_major(cache):
    n, rows = cache.shape[:2]
    return jnp.transpose(cache, (0, 2, 3, 4, 1)).reshape(n, KV_W, rows)


def _slc_sample(qaug, sel, cache, page_table, kvs_new):
    db, n_pages = page_table.shape
    nslp = sel.shape[2]
    pool = _feature_major(cache)
    page_specs = [pl.BlockSpec((1, KV_W, PAGE_SIZE), functools.partial(lambda i, pt, k: (pt[i, k], 0, 0), k=k))
                  for k in range(n_pages)]
    return pl.pallas_call(
        functools.partial(_slc_sample_kernel, n_pages=n_pages),
        grid_spec=pltpu.PrefetchScalarGridSpec(
            num_scalar_prefetch=1,
            grid=(db,),
            in_specs=page_specs + [pl.BlockSpec((1, NSA_H, KV_W), lambda i, pt: (i, 0, 0)),
                                   pl.BlockSpec((1, NSA_H, nslp), lambda i, pt: (i, 0, 0)),
                                   pl.BlockSpec((1, 1, KV_W), lambda i, pt: (i, 0, 0))],
            out_specs=pl.BlockSpec((1, NSA_H, HD), lambda i, pt: (i, 0, 0)),
        ),
        out_shape=jax.ShapeDtypeStruct((db, NSA_H, HD), F32),
        compiler_params=_cparams(("arbitrary",), VMEM_LIMIT),
        name="slc_attend_sample",
    )(page_table, *([pool] * n_pages), qaug, sel, kvs_new)


def _win_sample_kernel(q_ref, buf_ref, new_ref, newcol_ref, o_ref, win_ref, *, pos, first_pos):
    w_buf = buf_ref.shape[2]
    d = pos - (first_pos + _iota((1, w_buf), 1))
    bias = jnp.where((d >= 0) & (d <= WINDOW), 0.0, MASKED)
    for bi in range(q_ref.shape[0]):
        buf = buf_ref[bi]
        o = _decode_attend(q_ref[bi], [(lambda buf=buf: buf.astype(BF16), bias)], new_ref[bi], 0.0, feature_major=True)
        o_ref[bi] = _group_values(o)
        shifted = pltpu.roll(buf, w_buf - 1, 1)
        win_ref[bi] = jnp.where(_iota(buf.shape, 1) == w_buf - 1, newcol_ref[bi], shifted)


def _win_sample(qaug, win_buf_t, kvw_new, pos, first_pos):
    db, _, w_buf = win_buf_t.shape
    bb = _pick_tile(db, SAMPLE_SEQS_PER_STEP)
    return pl.pallas_call(
        functools.partial(_win_sample_kernel, pos=pos, first_pos=first_pos),
        grid=(db // bb,),
        in_specs=[pl.BlockSpec((bb, NSA_H, KV_W), lambda i: (i, 0, 0)),
                  pl.BlockSpec((bb, KV_W, w_buf), lambda i: (i, 0, 0)),
                  pl.BlockSpec((bb, 1, KV_W), lambda i: (i, 0, 0)),
                  pl.BlockSpec((bb, KV_W, 1), lambda i: (i, 0, 0))],
        out_specs=[pl.BlockSpec((bb, NSA_H, HD), lambda i: (i, 0, 0)),
                   pl.BlockSpec((bb, KV_W, w_buf), lambda i: (i, 0, 0))],
        out_shape=[jax.ShapeDtypeStruct((db, NSA_H, HD), F32),
                   jax.ShapeDtypeStruct((db, KV_W, w_buf), F32)],
        compiler_params=_cparams(("arbitrary",), VMEM_LIMIT),
        name="win_attend_sample",
    )(qaug, win_buf_t, kvw_new, kvw_new.reshape(db, KV_W, 1))


def _mem_sample_kernel(q_ref, kv_ref, o_ref):
    row = _iota((8, MEM_HD), 0)
    for bi in range(q_ref.shape[0]):
        q = q_ref[bi]
        out = jnp.zeros((8, MEM_HD), F32)
        for h in range(MEM_H):
            k = kv_ref[bi, :, h, 0, :].astype(BF16)
            v = kv_ref[bi, :, h, 1, :].astype(BF16)
            s = _dot_nt(q, k)
            p = jnp.exp(s - jnp.max(s, axis=-1, keepdims=True))
            o = _dot(p.astype(BF16), v) / jnp.sum(p, axis=-1, keepdims=True)
            out = jnp.where(row == h, o, out)
        o_ref[bi] = out


def _mem_sample(q8, cache_mem):
    db, t_mem = cache_mem.shape[:2]
    bb = _pick_tile(db, SAMPLE_SEQS_PER_STEP)
    return pl.pallas_call(
        _mem_sample_kernel,
        grid=(db // bb,),
        in_specs=[pl.BlockSpec((bb, 8, MEM_HD), lambda i: (i, 0, 0)),
                  pl.BlockSpec((bb, t_mem, MEM_H, 2, MEM_HD), lambda i: (i, 0, 0, 0, 0))],
        out_specs=pl.BlockSpec((bb, 8, MEM_HD), lambda i: (i, 0, 0)),
        out_shape=jax.ShapeDtypeStruct((db, 8, MEM_HD), F32),
        compiler_params=_cparams(("arbitrary",), VMEM_LIMIT),
        name="mem_attend_sample",
    )(q8, cache_mem)


def _mlstm_prompt_kernel(u_ref, oraw_ref, slab_ref, slabt_ref, cw_ref, cb_ref, wqt_ref, wk_ref, wvt_ref,
                         bgc_ref, bgr_ref, mlgt_ref, tril_ref, triu_ref,
                         o_ref, ct_ref, n_ref, m_ref, ubuf, *, chunk):
    L = chunk
    pad = 8

    @pl.when(pl.program_id(1) == 0)
    def _():
        ct_ref[...] = jnp.zeros_like(ct_ref)
        n_ref[...] = jnp.zeros_like(n_ref)
        m_ref[...] = jnp.zeros_like(m_ref)
        ubuf[0:pad, :] = jnp.zeros((pad, ML_W), F32)

    u = u_ref[0]
    ubuf[pad:pad + L, :] = u
    conv = cb_ref[...] + cw_ref[CONV_W - 1:CONV_W, :] * u
    for w in range(CONV_W - 1):
        off = pad - (CONV_W - 1) + w
        conv = conv + cw_ref[w:w + 1, :] * ubuf[off:off + L, :]
    ubuf[0:pad, :] = u[L - pad:L, :]
    cact = conv * _sigmoid(conv)

    gl = slab_ref[0] + bgc_ref[...]
    gt = slabt_ref[0] + bgr_ref[...]
    b_c = None
    for part in _split3(_log_sigmoid(gl)):
        term = _dot(tril_ref[...], part)
        b_c = term if b_c is None else b_c + term
    b_r = _dot_exact_rhs(_split3(_log_sigmoid(gt)), triu_ref[...])
    m_prev = m_ref[0]
    not_after = _iota((L, L), 0) <= _iota((L, L), 1)
    lane1 = _iota((1, LANES), 1)
    gain_t = jnp.concatenate([mlgt_ref[...]] * (L // LANES), axis=1)
    m_out = m_prev
    outs = []
    for h in range(ML_H):
        sl = slice(h * ML_D, (h + 1) * ML_D)
        br = b_r[_SLAB_F + h:_SLAB_F + h + 1, :]
        ir = gt[_SLAB_I + h:_SLAB_I + h + 1, :]
        src = gl[:, _SLAB_I + h:_SLAB_I + h + 1] - b_c[:, _SLAB_F + h:_SLAB_F + h + 1]
        m0 = m_prev[:, h:h + 1]
        dm = jnp.where(not_after, br + src, -jnp.inf)
        m_new = jnp.maximum(br + m0, jnp.max(dm, axis=0, keepdims=True))
        ch = cact[:, sl].astype(BF16)
        kb = _dot(ch, wk_ref[h]).astype(BF16)
        qt = _dot_nt(wqt_ref[h], ch) * (ML_D ** -0.5)
        vt = _dot_nt(wvt_ref[h], u[:, sl].astype(BF16))
        qtb = qt.astype(BF16)
        wmat = jnp.exp(dm - m_new) * _dot(kb, qtb)
        inter = jnp.exp(br + m0 - m_new)
        ct_old = ct_ref[0, h]
        n_old = n_ref[0, h:h + 1, :]
        num = inter * _dot(ct_old.astype(BF16), qtb) + _dot(vt.astype(BF16), wmat.astype(BF16))
        nq = _dot_exact_rhs(_split2(jnp.broadcast_to(n_old, (8, ML_D))), qtb)[0:1]
        den = inter * nq + jnp.sum(wmat, axis=0, keepdims=True)
        hh = num / jnp.maximum(jnp.abs(den), jnp.exp(-m_new))
        m_end = m_new[:, L - 1:L]
        b_end = br[:, L - 1:L]
        decay = jnp.exp(b_end + m0 - m_end)
        wend = jnp.exp(b_end - br + ir - m_end)
        ct_ref[0, h] = decay * ct_old + _dot((vt * wend).astype(BF16), kb)
        n_ref[0, h:h + 1, :] = decay * n_old + _dot_exact_rhs(_split2(jnp.broadcast_to(wend, (8, L))), kb)[0:1]
        m_out = jnp.where(lane1 == h, m_end, m_out)
        hn = hh * lax.rsqrt(jnp.mean(hh * hh, axis=0, keepdims=True) + EPS) * gain_t[sl, :]
        outs.append(_sigmoid(oraw_ref[0, :, sl]) * hn.T)
    o_ref[0] = jnp.concatenate(outs, axis=1)
    m_ref[0] = m_out


def _mlstm_prompt(u, oraw, slab, pw, chunk):
    b, s, _ = u.shape
    nc = s // chunk
    slabt = jnp.swapaxes(slab[:, :, 0:8], 1, 2)
    tril = jnp.tril(jnp.ones((chunk, chunk), BF16))
    consts = [pw['conv_w'], pw['conv_b'], pw['w_ml_q_t'], pw['w_ml_k'], pw['w_ml_v_t'], pw['gate_bias_lanes'],
              pw['gate_bias_rows'], pw['ml_gain_t'], tril, tril.T]
    blk = lambda w: pl.BlockSpec((1, chunk, w), lambda bi, c: (bi, c, 0))

    def const_spec(a):
        nd = a.ndim
        return pl.BlockSpec(a.shape, lambda bi, c: (0,) * nd)

    return pl.pallas_call(
        functools.partial(_mlstm_prompt_kernel, chunk=chunk),
        grid=(b, nc),
        in_specs=[blk(ML_W), blk(ML_W), blk(LANES), pl.BlockSpec((1, 8, chunk), lambda bi, c: (bi, 0, c))]
                 + [const_spec(a) for a in consts],
        out_specs=[blk(ML_W),
                   pl.BlockSpec((1, ML_H, ML_D, ML_D), lambda bi, c: (bi, 0, 0, 0)),
                   pl.BlockSpec((1, ML_H, ML_D), lambda bi, c: (bi, 0, 0)),
                   pl.BlockSpec((1, 1, LANES), lambda bi, c: (bi, 0, 0))],
        out_shape=[jax.ShapeDtypeStruct((b, s, ML_W), F32),
                   jax.ShapeDtypeStruct((b, ML_H, ML_D, ML_D), F32),
                   jax.ShapeDtypeStruct((b, ML_H, ML_D), F32),
                   jax.ShapeDtypeStruct((b, 1, LANES), F32)],
        scratch_shapes=[pltpu.VMEM((chunk + 8, ML_W), F32)],
        compiler_params=_cparams(("arbitrary", "arbitrary"), VMEM_LIMIT),
        name="mlstm_prompt",
    )(u, oraw, slab, slabt, *consts)


def _mlstm_sample_kernel(ext_ref, oraw_ref, slab_ref, c_ref, n_ref, m_ref, cw_ref, cb_ref, wq_ref, wk_ref,
                         wv_ref, wkt_ref, bgc_ref, mlg_ref, o_ref, co_ref, no_ref, mo_ref):
    for bi in range(ext_ref.shape[0]):
        _mlstm_sample_step(bi, ext_ref, oraw_ref, slab_ref, c_ref, n_ref, m_ref, cw_ref, cb_ref, wq_ref, wk_ref,
                           wv_ref, wkt_ref, bgc_ref, mlg_ref, o_ref, co_ref, no_ref, mo_ref)


def _mlstm_sample_step(bi, ext_ref, oraw_ref, slab_ref, c_ref, n_ref, m_ref, cw_ref, cb_ref, wq_ref, wk_ref,
                       wv_ref, wkt_ref, bgc_ref, mlg_ref, o_ref, co_ref, no_ref, mo_ref):
    ext = ext_ref[bi]
    conv = cb_ref[...]
    for w in range(CONV_W):
        conv = conv + cw_ref[w:w + 1, :] * ext[w:w + 1, :]
    cact = conv * _sigmoid(conv)
    u = ext[CONV_W - 1:CONV_W, :]
    gl = slab_ref[bi] + bgc_ref[...]
    lf_all = _log_sigmoid(gl)
    m_prev = m_ref[bi]
    lane1 = _iota((1, LANES), 1)
    m_out = m_prev
    outs = []
    for h in range(ML_H):
        sl = slice(h * ML_D, (h + 1) * ML_D)
        ch8 = jnp.broadcast_to(cact[:, sl], (8, ML_D)).astype(BF16)
        u8 = jnp.broadcast_to(u[:, sl], (8, ML_D)).astype(BF16)
        q = (_dot(ch8, wq_ref[h]) * (ML_D ** -0.5))[0:1]
        k = _dot(ch8, wk_ref[h])[0:1]
        v = _dot(u8, wv_ref[h])[0:1]
        k_col = _dot_nt(wkt_ref[h], ch8)[:, 0:1]
        ig = gl[:, _SLAB_I + h:_SLAB_I + h + 1]
        lf = lf_all[:, _SLAB_F + h:_SLAB_F + h + 1]
        m0 = m_prev[:, h:h + 1]
        m_new = jnp.maximum(lf + m0, ig)
        qb = q.astype(BF16)
        qk = jnp.sum(qb.astype(F32) * k.astype(BF16).astype(F32), axis=-1, keepdims=True)
        w_in = jnp.exp(ig - m_new) * qk
        inter = jnp.exp(lf + m0 - m_new)
        c_old = c_ref[bi, h]
        n_old = n_ref[bi, h:h + 1, :]
        qc = _dot(jnp.broadcast_to(qb, (8, ML_D)), c_old.astype(BF16))[0:1]
        num = inter * qc + w_in * v
        den = inter * jnp.sum(q * n_old, axis=-1, keepdims=True) + w_in
        hh = num / jnp.maximum(jnp.abs(den), jnp.exp(-m_new))
        w_end = jnp.exp(ig - m_new)
        co_ref[bi, h] = inter * c_old + k_col * (w_end * v)
        no_ref[bi, h:h + 1, :] = inter * n_old + w_end * k
        m_out = jnp.where(lane1 == h, m_new, m_out)
        hn = hh * lax.rsqrt(jnp.mean(hh * hh, axis=-1, keepdims=True) + EPS) * mlg_ref[:, sl]
        outs.append(_sigmoid(oraw_ref[bi, :, sl]) * hn)
    o_ref[bi] = jnp.concatenate(outs, axis=1)
    mo_ref[bi] = m_out


def _mlstm_sample(ext, oraw, slab, state_c, state_n, state_m, pw):
    db = ext.shape[0]
    m_in = jnp.pad(state_m, ((0, 0), (0, LANES - ML_H))).reshape(db, 1, LANES)
    consts = [pw['conv_w'], pw['conv_b'], pw['w_ml_q'], pw['w_ml_k'], pw['w_ml_v'], pw['w_ml_k_t'],
              pw['gate_bias_lanes'], pw['ml_gain']]

    def const_spec(a):
        nd = a.ndim
        return pl.BlockSpec(a.shape, lambda i: (0,) * nd)

    bb = _pick_tile(db, SAMPLE_SEQS_PER_STEP)
    per_seq = lambda shape: pl.BlockSpec((bb,) + shape, lambda i: (i,) + (0,) * len(shape))
    return pl.pallas_call(
        _mlstm_sample_kernel,
        grid=(db // bb,),
        in_specs=[per_seq((CONV_W, ML_W)), per_seq((1, ML_W)), per_seq((1, LANES)), per_seq((ML_H, ML_D, ML_D)),
                  per_seq((ML_H, ML_D)), per_seq((1, LANES))] + [const_spec(a) for a in consts],
        out_specs=[per_seq((1, ML_W)), per_seq((ML_H, ML_D, ML_D)), per_seq((ML_H, ML_D)), per_seq((1, LANES))],
        out_shape=[jax.ShapeDtypeStruct((db, 1, ML_W), F32),
                   jax.ShapeDtypeStruct((db, ML_H, ML_D, ML_D), F32),
                   jax.ShapeDtypeStruct((db, ML_H, ML_D), F32),
                   jax.ShapeDtypeStruct((db, 1, LANES), F32)],
        compiler_params=_cparams(("arbitrary",), VMEM_LIMIT),
        name="mlstm_sample",
    )(ext, oraw, slab, state_c, state_n, m_in, *consts)


def _merge_kernel(x_ref, nm_ref, wg_ref, slab_ref, ex_ref, ocmp_ref, oslc_ref, owin_ref, oml_ref, omem_ref,
                  wb_ref, wout_ref, o_ref):
    x = x_ref[...]
    xn = _rms_rows(x, nm_ref[...]).astype(BF16)
    gparts = _split2(_sigmoid(slab_ref[...]))
    onsa = None
    for br, br_ref in enumerate((ocmp_ref, oslc_ref, owin_ref)):
        term = _dot_exact_rhs(gparts, ex_ref[br]) * br_ref[...]
        onsa = term if onsa is None else onsa + term
    z = None
    for n, on in enumerate((onsa, oml_ref[...], omem_ref[...])):
        gate = _sigmoid(_dot(xn, wg_ref[:, n * D_MODEL:(n + 1) * D_MODEL]))
        term = gate * _dot(on.astype(BF16), wb_ref[n])
        z = term if z is None else z + term
    o_ref[...] = x + _dot(z.astype(BF16), wout_ref[...])


def _merge(x2d, slab, ocmp, oslc, owin, oml, omem, pw, tm):
    t = x2d.shape[0]
    row = lambda w: pl.BlockSpec((tm, w), lambda i: (i, 0))
    const_spec = lambda a: _resident_spec(a)
    return pl.pallas_call(
        _merge_kernel,
        grid=(t // tm,),
        in_specs=[row(D_MODEL), const_spec(pw['norm_mix']), const_spec(pw['w_gate']), row(LANES),
                  const_spec(pw['gate_expand'])] + [row(BRANCH_W)] * 5
                 + [const_spec(pw['w_branch']), const_spec(pw['w_out'])],
        out_specs=row(D_MODEL),
        out_shape=jax.ShapeDtypeStruct((t, D_MODEL), F32),
        compiler_params=_cparams(("arbitrary",), VMEM_LIMIT),
        name="merge",
    )(x2d, pw['norm_mix'], pw['w_gate'], slab, pw['gate_expand'], ocmp, oslc, owin, oml, omem,
      pw['w_branch'], pw['w_out'])


def _ffn_kernel(x_ref, nf_ref, win_ref, wout_ref, o_ref):
    x = x_ref[...]
    xn = _rms_rows(x, nf_ref[...]).astype(BF16)
    a = _dot(xn, win_ref[:, 0:FFN_HID])
    b = _dot(xn, win_ref[:, FFN_HID:2 * FFN_HID])
    o_ref[...] = x + _dot((a * _sigmoid(a) * b).astype(BF16), wout_ref[...])


def _ffn(x2d, pw, tm):
    t = x2d.shape[0]
    consts = [pw['norm_ffn'], pw['w_ffn_in'], pw['w_ffn_out']]
    return pl.pallas_call(
        _ffn_kernel,
        grid=(t // tm,),
        in_specs=[pl.BlockSpec((tm, D_MODEL), lambda i: (i, 0))] + [_resident_spec(a) for a in consts],
        out_specs=pl.BlockSpec((tm, D_MODEL), lambda i: (i, 0)),
        out_shape=jax.ShapeDtypeStruct((t, D_MODEL), F32),
        compiler_params=_cparams(("arbitrary",), VMEM_LIMIT),
        name="ffn",
    )(x2d, *consts)


def _rope_tables(pos):
    half = ROT_DIM // 2
    freqs = ROPE_THETA ** (-jnp.arange(half, dtype=F32) / half)
    ang = pos.astype(F32)[:, None] * freqs
    cos, sin = jnp.cos(ang), jnp.sin(ang)
    n = pos.shape[0]
    ones = jnp.ones((n, HD - ROT_DIM), F32)
    zeros = jnp.zeros((n, HD - ROT_DIM), F32)
    zh = jnp.zeros((n, half), F32)
    ct = jnp.concatenate([cos, cos, ones], axis=1)
    sa = jnp.concatenate([-sin, zh, zeros], axis=1)
    sb = jnp.concatenate([zh, sin, zeros], axis=1)
    one64, zero64 = jnp.ones((n, HD), F32), jnp.zeros((n, HD), F32)
    q_tabs = [jnp.concatenate([t, t], axis=1) for t in (ct, sa, sb)]
    k_tabs = [jnp.concatenate([ct, one64], axis=1), jnp.concatenate([sa, zero64], axis=1),
              jnp.concatenate([sb, zero64], axis=1)]
    return q_tabs + k_tabs


def _prepare_weights(norm_mix, w_in, q_norm, k_norm_cmp, k_norm_slc, k_norm_win, cmp_pe, cmp_w1, cmp_b1, cmp_w2,
                     conv_w, conv_b, w_ml_q, w_ml_k, w_ml_v, b_igate, b_fgate, ml_norm, norm_mem, w_mem_kv,
                     mem_q_norm, mem_k_norm, w_branch, w_out, norm_ffn, w_ffn_in, w_ffn_out):
    widths = (NSA_H * HD, 3 * NSA_H, KV_W, KV_W, KV_W, ML_W, ML_W, ML_H, ML_H, MEM_W, N_BRANCH * D_MODEL)
    offs = np.concatenate([[0], np.cumsum(widths)])
    q, g, kvc, kvs, kvw, u, o, ig, fg, mq, mg = (w_in[:, int(offs[i]):int(offs[i + 1])] for i in range(len(widths)))
    slab_pad = jnp.zeros((D_MODEL, LANES - 2 * ML_H - 3 * NSA_H), F32)
    pw = {}
    pw['w_proj'] = jnp.concatenate([q, kvc, kvs, kvw, u, o, mq, ig, fg, g, slab_pad], axis=1).astype(BF16)
    pw['w_gate'] = mg.astype(BF16)
    pw['norm_mix'] = norm_mix.reshape(1, D_MODEL)
    pw['q_gain'] = jnp.tile(q_norm, NSA_H).reshape(1, NSA_H * HD)
    ones = jnp.ones((HD,), F32)
    pw['ks_gain'] = jnp.tile(jnp.concatenate([k_norm_slc, ones]), NSA_G).reshape(1, KV_W)
    pw['kw_gain'] = jnp.tile(jnp.concatenate([k_norm_win, ones]), NSA_G).reshape(1, KV_W)
    pw['kc_gain'] = jnp.concatenate([k_norm_cmp, ones]).reshape(1, LANES)
    pw['mq_gain'] = jnp.tile(mem_q_norm, MEM_H).reshape(1, MEM_W)
    head_of = np.arange(NSA_H * HD) // HD
    pw['seg64'] = jnp.asarray(head_of[:, None] == head_of[None, :], BF16)
    w1 = cmp_w1.reshape(2, CMP_BLOCK, HD, CMP_HID)
    eye = jnp.eye(2, dtype=F32)
    for name, part in (('cmp_wlo', w1[:, :CMP_STRIDE]), ('cmp_whi', w1[:, CMP_STRIDE:])):
        pw[name] = jnp.einsum('cjdh,ce->jcdeh', part, eye).reshape(CMP_STRIDE * 2 * HD, 2 * CMP_HID).astype(BF16)
    pw['cmp_w2'] = jnp.einsum('chd,ce->ched', cmp_w2, eye).reshape(2 * CMP_HID, 2 * HD).astype(BF16)
    pw['cmp_b1'] = cmp_b1.reshape(1, 2 * CMP_HID)
    pw['pe_lo'] = cmp_pe[:CMP_STRIDE].reshape(1, CMP_STRIDE * 2 * HD)
    pw['pe_hi'] = cmp_pe[CMP_STRIDE:].reshape(1, CMP_STRIDE * 2 * HD)
    pw['conv_w'] = conv_w
    pw['conv_b'] = conv_b.reshape(1, ML_W)
    pw['w_ml_q'], pw['w_ml_k'], pw['w_ml_v'] = (w.astype(BF16) for w in (w_ml_q, w_ml_k, w_ml_v))
    pw['w_ml_q_t'], pw['w_ml_k_t'], pw['w_ml_v_t'] = (jnp.swapaxes(w, 1, 2).astype(BF16) for w in (w_ml_q, w_ml_k, w_ml_v))
    pw['ml_gain_t'] = jnp.broadcast_to(ml_norm.reshape(ML_W, 1), (ML_W, LANES))
    gate_bias = jnp.concatenate([b_igate, b_fgate])
    pw['gate_bias_lanes'] = jnp.pad(gate_bias, (0, LANES - 2 * ML_H)).reshape(1, LANES)
    pw['gate_bias_rows'] = gate_bias.reshape(2 * ML_H, 1)
    pw['ml_gain'] = ml_norm.reshape(1, ML_W)
    pw['norm_mem'] = norm_mem.reshape(1, D_MODEL)
    pw['w_mem_kv'] = w_mem_kv.astype(BF16)
    pw['mem_k_gain'] = mem_k_norm.reshape(1, MEM_HD)
    ex = np.zeros((N_BRANCH, LANES, NSA_H * HD), np.float32)
    for br in range(N_BRANCH):
        for gh in range(NSA_H):
            ex[br, _SLAB_G + br * NSA_H + gh, gh * HD:(gh + 1) * HD] = 1.0
    pw['gate_expand'] = jnp.asarray(ex, BF16)
    pw['w_branch'] = w_branch.astype(BF16)
    pw['w_out'] = w_out.astype(BF16)
    pw['norm_ffn'] = norm_ffn.reshape(1, D_MODEL)
    pw['w_ffn_in'] = w_ffn_in.astype(BF16)
    pw['w_ffn_out'] = w_ffn_out.astype(BF16)
    return pw


def _overlap(n_cmp_rows, n_slc_cols):
    ci = np.arange(n_cmp_rows)[:, None] * CMP_STRIDE
    sj = np.arange(n_slc_cols)[None, :] * SLC_BLOCK
    return ((ci < sj + SLC_BLOCK) & (ci + CMP_BLOCK > sj)).astype(np.float32)


def _pick_tile(n, pref):
    t = min(n, pref)
    while n % t:
        t //= 2
    return t


def _prompt_group(x, mem, pw):
    b, s, _ = x.shape
    t = b * s
    x2d = x.reshape(t, D_MODEL)
    tm = _pick_tile(s, TOKEN_TILE)
    tabs = _rope_tables(jnp.arange(s, dtype=jnp.int32))
    (_, _, kvc, kvs, kvw, u, oraw, _, slab,
     qrot_t, kvc_t, kvs_t, kvw_t, qn_t, mq_t) = _project(x2d, tabs, pw, tm, seq_len=s)
    kvcmp, kvcmp_t = _compress_prompt(kvc.reshape(b, s, KV_W), pw)
    r = s // CMP_STRIDE
    ovt = jnp.asarray(_overlap(r, HD).T, BF16)
    ocmp, sel_t = _cmp_prompt(qn_t, kvcmp, kvcmp_t, ovt, b, s, _pick_tile(s, CMP_TQ))
    fq, fk = _pick_tile(s, FLASH_TQ), _pick_tile(s, FLASH_TK)
    oslc = _flash_prompt(qrot_t, kvs.reshape(b, s, KV_W), kvs_t, sel_t, b, s, fq, fk)
    owin = _flash_prompt(qrot_t, kvw.reshape(b, s, KV_W), kvw_t, None, b, s, fq, fk)
    chunk = _pick_tile(s, MLSTM_CHUNK)
    oml, c_t, n_new, m_new = _mlstm_prompt(u.reshape(b, s, ML_W), oraw.reshape(b, s, ML_W),
                                             slab.reshape(b, s, LANES), pw, chunk)
    t_mem = mem.shape[1]
    kvm, kvm_t = _mem_kv(mem, pw)
    omem = _mem_prompt(mq_t, kvm, kvm_t, b, s, _pick_tile(s, 512))
    x1 = _merge(x2d, slab, ocmp, oslc, owin, oml.reshape(t, ML_W), omem, pw, tm)
    y = _ffn(x1, pw, tm).reshape(b, s, D_MODEL)
    kv5 = lambda a_t: jnp.transpose(a_t.reshape(b, NSA_G, 2, HD, a_t.shape[-1]), (0, 4, 1, 2, 3))
    return dict(y=y, kv_cmp=kv5(kvc_t), kv_slc=kv5(kvs_t), kv_win_t=kvw_t, kv5=kv5,
                kv_mem=kvm.reshape(b, t_mem, MEM_H, 2, MEM_HD), c=jnp.swapaxes(c_t, 2, 3), n=n_new,
                m=m_new[:, 0, :ML_H],
                u=u.reshape(b, s, ML_W))


def _sample_group(x, cache_cmp, cache_slc, cache_win, cache_mem, state_c, state_n, state_m, state_conv,
                  page_table, pw):
    db, ds, _ = x.shape
    assert ds == 1, "the sample kernels handle one new token per sequence"
    n_pages = page_table.shape[1]
    past = n_pages * PAGE_SIZE
    pos = past
    x2d = x.reshape(db, D_MODEL)
    tabs = _rope_tables(jnp.full((db,), pos, jnp.int32))
    qn, qrot, kvc, kvs, kvw, u, oraw, mq, slab = _project(x2d, tabs, pw, db)
    kvcmp = _compress_paged(cache_cmp, page_table, kvc.reshape(db, 1, KV_W), pw)
    n_cmp = kvcmp.shape[2]
    n_slc = -(-(past + ds) // SLC_BLOCK)
    nslp = -(-n_slc // LANES) * LANES
    ov = jnp.asarray(_overlap(n_cmp, nslp), BF16)
    ocmp, sel = _cmp_sample(qn.reshape(db, NSA_H, HD), kvcmp, ov, pos, n_slc)
    q3 = qrot.reshape(db, NSA_G, NSA_HPG, 1, HD)
    qaug = (q3 * jnp.eye(NSA_G, dtype=BF16)[None, :, None, :, None]).reshape(db, NSA_H, NSA_G, HD)
    qaug = jnp.concatenate([qaug, jnp.zeros_like(qaug)], axis=-1).reshape(db, NSA_H, KV_W)
    oslc = _slc_sample(qaug, sel, cache_slc, page_table, kvs.reshape(db, 1, KV_W))
    w_buf = cache_win.shape[1]
    owin, win_new_t = _win_sample(qaug, _feature_major(cache_win), kvw.reshape(db, 1, KV_W), pos, past - w_buf)
    win_new = jnp.transpose(win_new_t.reshape(db, NSA_G, 2, HD, w_buf), (0, 4, 1, 2, 3))
    ext = jnp.concatenate([state_conv, u.reshape(db, 1, ML_W)], axis=1)
    oml, c_new, n_new, m_new = _mlstm_sample(ext, oraw.reshape(db, 1, ML_W), slab.reshape(db, 1, LANES),
                                             state_c, state_n, state_m, pw)
    q8 = jnp.pad(mq.reshape(db, MEM_H, MEM_HD), ((0, 0), (0, 8 - MEM_H), (0, 0)))
    omem = _mem_sample(q8, cache_mem)[:, :MEM_H].reshape(db, MEM_W)
    x1 = _merge(x2d, slab, ocmp.reshape(db, NSA_H * HD), oslc.reshape(db, NSA_H * HD), owin.reshape(db, NSA_H * HD),
                oml.reshape(db, ML_W), omem, pw, db)
    y = _ffn(x1, pw, db).reshape(db, ds, D_MODEL)
    kv5 = lambda a: a.reshape(db, ds, NSA_G, 2, HD)
    return dict(y=y, kv_cmp=kv5(kvc), kv_slc=kv5(kvs), win=win_new,
                c=c_new, n=n_new, m=m_new[:, 0, :ML_H], conv=ext[:, 1:])


def kernel(x_prompt, x_sample, cache_kv_cmp, cache_kv_slc, cache_kv_win, cache_kv_mem, state_C, state_n, state_m, state_conv, page_table, mem_prompt, norm_mix, w_in, q_norm, k_norm_cmp, k_norm_slc, k_norm_win, cmp_pe, cmp_w1, cmp_b1, cmp_w2, conv_w, conv_b, w_ml_q, w_ml_k, w_ml_v, b_igate, b_fgate, ml_norm, norm_mem, w_mem_kv, mem_q_norm, mem_k_norm, w_branch, w_out, norm_ffn, w_ffn_in, w_ffn_out):
    pw = _prepare_weights(norm_mix, w_in, q_norm, k_norm_cmp, k_norm_slc, k_norm_win, cmp_pe, cmp_w1, cmp_b1, cmp_w2,
                          conv_w, conv_b, w_ml_q, w_ml_k, w_ml_v, b_igate, b_fgate, ml_norm, norm_mem, w_mem_kv,
                          mem_q_norm, mem_k_norm, w_branch, w_out, norm_ffn, w_ffn_in, w_ffn_out)
    p = _prompt_group(x_prompt, mem_prompt, pw)
    s = _sample_group(x_sample, cache_kv_cmp, cache_kv_slc, cache_kv_win, cache_kv_mem, state_C, state_n, state_m,
                      state_conv, page_table, pw)
    b, seq = x_prompt.shape[:2]
    w_buf = cache_kv_win.shape[1]
    assert seq >= w_buf
    win_p = p['kv5'](p['kv_win_t'][:, :, seq - w_buf:])
    conv_p = p['u'][:, seq - (CONV_W - 1):]
    return (p['y'], s['y'], p['kv_cmp'], s['kv_cmp'], p['kv_slc'], s['kv_slc'], win_p, s['win'], p['kv_mem'],
            p['c'], s['c'], p['n'], s['n'], p['m'], s['m'], conv_p, s['conv'])
```

```python
import functools

import numpy as np
import jax
import jax.numpy as jnp
from jax import lax
from jax.experimental import pallas as pl
from jax.experimental.pallas import tpu as pltpu

F32 = jnp.float32
BF16 = jnp.bfloat16

D_MODEL = 1024
PAGE_SIZE = 128
NSA_H, NSA_G, NSA_HPG, HD = 8, 2, 4, 64
ROT_DIM = HD // 4
ROPE_THETA = 500000.0
CMP_BLOCK, CMP_STRIDE, CMP_HID = 32, 16, 128
SLC_BLOCK, N_SEL = 64, 16
WINDOW = 512
ML_H, ML_D, ML_W, CONV_W = 4, 128, 512, 4
MEM_H, MEM_HD, MEM_W = 4, 128, 512
N_BRANCH, BRANCH_W = 3, 512
FFN_HID = -(-8 * D_MODEL // (3 * 256)) * 256
EPS = 1e-6

KV_W = 2 * NSA_G * HD
CHUNK_W = CMP_STRIDE * KV_W
LANES = 128
MASKED = -1e30
UNSELECTED = -32768.0
VMEM_LIMIT = 56 * 1024 * 1024
TOKEN_TILE = 512
CMP_TQ = 512
FLASH_TQ, FLASH_TK = 512, 256
FLASH_COLS = 128
FLASH_ONES_ROWS = 64
MLSTM_CHUNK = 256
SAMPLE_SEQS_PER_STEP = 4

_Q0, _KVC0, _KVS0, _KVW0, _U0, _O0, _MQ0, _SLAB0, _PROJ_W = 0, 512, 768, 1024, 1280, 1792, 2304, 2816, 2944
_SLAB_I, _SLAB_F, _SLAB_G = 0, 4, 8


def _dot(a, b):
    return jnp.dot(a, b, preferred_element_type=F32)


def _dot_nt(a, b):
    return lax.dot_general(a, b, (((1,), (1,)), ((), ())), preferred_element_type=F32)


def _split2(x):
    hi = x.astype(BF16)
    return hi, (x - hi.astype(F32)).astype(BF16)


def _split3(x):
    hi = x.astype(BF16)
    r = x - hi.astype(F32)
    mid = r.astype(BF16)
    return hi, mid, (r - mid.astype(F32)).astype(BF16)


def _dot_exact_rhs(parts, m):
    out = _dot(parts[0], m)
    for p in parts[1:]:
        out = out + _dot(p, m)
    return out


def _dot_exact_rhs_left(m, parts):
    out = _dot(m, parts[0])
    for p in parts[1:]:
        out = out + _dot(m, p)
    return out


def _sigmoid(x):
    return 1.0 / (1.0 + jnp.exp(-x))


def _log_sigmoid(x):
    return jnp.minimum(x, 0.0) - jnp.log(1.0 + jnp.exp(-jnp.abs(x)))


def _gelu_tanh(x):
    return x * (0.5 * (1.0 + jnp.tanh(np.sqrt(2.0 / np.pi).astype(np.float32) * (x + 0.044715 * (x * x * x)))))


def _rms_rows(x, gain):
    ms = jnp.mean(x * x, axis=-1, keepdims=True)
    return x * lax.rsqrt(ms + EPS) * gain


def _iota(shape, dim):
    return lax.broadcasted_iota(jnp.int32, shape, dim)


def _cparams(sem, vmem=None):
    return pltpu.CompilerParams(dimension_semantics=sem, vmem_limit_bytes=vmem)


def _resident_spec(a):
    nd = a.ndim
    return pl.BlockSpec(a.shape, lambda *_: (0,) * nd, pipeline_mode=pl.Buffered(1))


def _masked_softmax(s, mask):
    s = jnp.where(mask, s, -jnp.inf)
    m = jnp.max(s, axis=-1, keepdims=True)
    m = jnp.where(m > -jnp.inf, m, 0.0)
    p = jnp.exp(s - m)
    return p / jnp.maximum(jnp.sum(p, axis=-1, keepdims=True), jnp.finfo(F32).tiny)


def _rs_heads64(a, seg_ref, width):
    ss = _dot_exact_rhs(_split2(a * a), seg_ref[0:width, 0:width])
    return lax.rsqrt(ss * (1.0 / HD) + EPS)


def _rope(a, ct, sa, sb, width):
    reps = width // LANES
    if reps > 1:
        ct, sa, sb = (jnp.concatenate([t] * reps, axis=1) for t in (ct, sa, sb))
    half = ROT_DIM // 2
    return a * ct + pltpu.roll(a, width - half, 1) * sa + pltpu.roll(a, half, 1) * sb


def _proj_kernel(x_ref, nm_ref, w_ref, qg_ref, ksg_ref, kwg_ref, mqg_ref, seg_ref,
                 ctq_ref, saq_ref, sbq_ref, ctk_ref, sak_ref, sbk_ref,
                 kvc_o, kvs_o, kvw_o, u_o, o_o, slab_o, *rest):
    fm_outs = rest if len(rest) == 6 else ()
    xn = _rms_rows(x_ref[...], nm_ref[...]).astype(BF16)

    def seg(lo, hi):
        return _dot(xn, w_ref[:, lo:hi])

    q = seg(_Q0, _KVC0)
    qn = q * _rs_heads64(q, seg_ref, NSA_H * HD) * qg_ref[...]
    scale = HD ** -0.5
    qr = _rope(qn, ctq_ref[...], saq_ref[...], sbq_ref[...], NSA_H * HD) * scale
    kvc = seg(_KVC0, _KVS0)
    kvc_o[...] = kvc
    if fm_outs:
        fm_outs[0][0] = qr.T.astype(BF16)
        fm_outs[1][0] = kvc.T
        fm_outs[4][0] = (qn * scale).T.astype(BF16)
    else:
        rest[0][...] = (qn * scale).astype(BF16)
        rest[1][...] = qr.astype(BF16)
    for n, (lo, g_ref, o_ref) in enumerate(((_KVS0, ksg_ref, kvs_o), (_KVW0, kwg_ref, kvw_o))):
        a = seg(lo, lo + KV_W)
        is_k = (_iota(a.shape, 1) & HD) == 0
        an = jnp.where(is_k, a * _rs_heads64(a, seg_ref, KV_W) * g_ref[...], a)
        ar = _rope(an, ctk_ref[...], sak_ref[...], sbk_ref[...], KV_W)
        o_ref[...] = ar
        if fm_outs:
            fm_outs[2 + n][0] = ar.T
    u_o[...] = seg(_U0, _O0)
    o_o[...] = seg(_O0, _MQ0)
    mq = seg(_MQ0, _SLAB0)
    heads = []
    for h in range(MEM_H):
        mh = mq[:, h * MEM_HD:(h + 1) * MEM_HD]
        heads.append(mh * lax.rsqrt(jnp.mean(mh * mh, axis=-1, keepdims=True) + EPS))
    mqn = jnp.concatenate(heads, axis=1) * mqg_ref[...] * (MEM_HD ** -0.5)
    if fm_outs:
        fm_outs[5][0] = mqn.T.astype(BF16)
    else:
        rest[2][...] = mqn.astype(BF16)
    slab_o[...] = seg(_SLAB0, _PROJ_W)


def _project(x2d, tabs, pw, tm, seq_len=None):
    t = x2d.shape[0]
    ntab = tabs[0].shape[0] // tm
    row = lambda i: (i, 0)
    const = lambda i: (0, 0)
    tab = lambda i: (i % ntab, 0)
    full = lambda a: _resident_spec(a)
    consts = [pw['norm_mix'], pw['w_proj'], pw['q_gain'], pw['ks_gain'], pw['kw_gain'], pw['mq_gain'],
              pw['seg64']]
    widths = [(KV_W, F32), (KV_W, F32), (KV_W, F32), (ML_W, F32), (ML_W, F32), (LANES, F32)]
    if seq_len is None:
        widths += [(NSA_H * HD, BF16), (NSA_H * HD, BF16), (MEM_W, BF16)]
    out_specs = [pl.BlockSpec((tm, w), row) for w, _ in widths]
    out_shape = [jax.ShapeDtypeStruct((t, w), d) for w, d in widths]
    if seq_len is not None:
        per_seq = seq_len // tm
        for w, d in ((NSA_H * HD, BF16), (KV_W, F32), (KV_W, F32), (KV_W, F32), (NSA_H * HD, BF16), (MEM_W, BF16)):
            out_specs.append(pl.BlockSpec((1, w, tm), lambda i: (i // per_seq, 0, i % per_seq)))
            out_shape.append(jax.ShapeDtypeStruct((t // seq_len, w, seq_len), d))
    return pl.pallas_call(
        _proj_kernel,
        grid=(t // tm,),
        in_specs=[pl.BlockSpec((tm, D_MODEL), row)] + [full(a) for a in consts]
                 + [pl.BlockSpec((tm, LANES), tab)] * 6,
        out_specs=out_specs,
        out_shape=out_shape,
        compiler_params=_cparams(("arbitrary",), VMEM_LIMIT),
        name="project",
    )(x2d, *consts, *tabs)


def _compress_group(y_lo, y_hi, tail_hi, wlo, whi, b1, w2, gain, n_valid):
    r = y_lo.shape[0]
    a = _dot(y_lo, wlo)
    bh = _dot(y_hi, whi)
    bs = pltpu.roll(bh, r - 1, 0)
    if tail_hi is not None:
        bt = _dot(tail_hi, whi)
        bs = jnp.where(_iota(bs.shape, 0) == r - 1, bt, bs)
    hid = _gelu_tanh(a + bs + b1)
    out = _dot(hid.astype(BF16), w2)
    is_k = _iota(out.shape, 1) < HD
    ss = jnp.sum(jnp.where(is_k, out * out, 0.0), axis=-1, keepdims=True) * (1.0 / HD)
    kv = jnp.where(is_k, out * lax.rsqrt(ss + EPS) * gain, out)
    return jnp.where(_iota(kv.shape, 0) < n_valid, kv, 0.0)


def _group_cols(load, g):
    return jnp.concatenate([load(j * KV_W + g * LANES, j * KV_W + (g + 1) * LANES) for j in range(CMP_STRIDE)], axis=1)


def _compress_prompt_kernel(y_ref, pe_lo, pe_hi, wlo, whi, b1, w2, gain, o_ref, ot_ref, *, n_valid):
    for g in range(NSA_G):
        yg = _group_cols(lambda lo, hi: y_ref[0, :, lo:hi], g)
        kv = _compress_group((yg + pe_lo[...]).astype(BF16), (yg + pe_hi[...]).astype(BF16), None,
                             wlo[...], whi[...], b1[...], w2[...], gain[...], n_valid)
        o_ref[0, g] = kv.astype(BF16)
        ot_ref[0, g] = kv.T.astype(BF16)


def _compress_paged_kernel(pt_ref, *refs, n_pages, n_valid):
    pages = refs[:n_pages]
    tail_ref, pe_lo, pe_hi, wlo, whi, b1, w2, gain, o_ref, y = refs[n_pages:]
    cpp = PAGE_SIZE // CMP_STRIDE
    for k, page in enumerate(pages):
        for g in range(NSA_G):
            nat = page[0, g * LANES:(g + 1) * LANES, :].T
            by_row = jnp.swapaxes(nat.reshape(cpp, CMP_STRIDE, LANES), 0, 1)
            for j in range(CMP_STRIDE):
                y[g, k * cpp:(k + 1) * cpp, j * LANES:(j + 1) * LANES] = by_row[j]
    for g in range(NSA_G):
        yg = y[g]
        tg = _group_cols(lambda lo, hi: tail_ref[0, :, lo:hi], g)
        o_ref[0, g] = _compress_group((yg + pe_lo[...]).astype(BF16), (yg + pe_hi[...]).astype(BF16),
                                      (tg + pe_hi[...]).astype(BF16), wlo[...], whi[...], b1[...], w2[...],
                                      gain[...], n_valid).astype(BF16)


def _compress_consts(pw):
    return [pw['cmp_wlo'], pw['cmp_whi'], pw['cmp_b1'], pw['cmp_w2'], pw['kc_gain']]


def _compress_prompt(kvc, pw):
    b, s, _ = kvc.shape
    r = s // CMP_STRIDE
    y = kvc.reshape(b, r, CHUNK_W)
    consts = [pw['pe_lo'], pw['pe_hi']] + _compress_consts(pw)
    return pl.pallas_call(
        functools.partial(_compress_prompt_kernel, n_valid=r - 1),
        grid=(b,),
        in_specs=[pl.BlockSpec((1, r, CHUNK_W), lambda i: (i, 0, 0))]
                 + [pl.BlockSpec(a.shape, lambda i: (0, 0)) for a in consts],
        out_specs=[pl.BlockSpec((1, NSA_G, r, LANES), lambda i: (i, 0, 0, 0)),
                   pl.BlockSpec((1, NSA_G, LANES, r), lambda i: (i, 0, 0, 0))],
        out_shape=[jax.ShapeDtypeStruct((b, NSA_G, r, LANES), BF16),
                   jax.ShapeDtypeStruct((b, NSA_G, LANES, r), BF16)],
        compiler_params=_cparams(("arbitrary",), VMEM_LIMIT),
        name="compress_prompt",
    )(y, *consts)


def _compress_paged(cache, page_table, kvc_new, pw):
    db, n_pages = page_table.shape
    chunks_per_page = PAGE_SIZE // CMP_STRIDE
    r = n_pages * chunks_per_page
    pool = _feature_major(cache)
    tail = jnp.pad(kvc_new, ((0, 0), (0, CMP_STRIDE - kvc_new.shape[1]), (0, 0))).reshape(db, 1, CHUNK_W)
    consts = [pw['pe_lo'], pw['pe_hi']] + _compress_consts(pw)
    page_specs = [pl.BlockSpec((1, KV_W, PAGE_SIZE), functools.partial(lambda i, pt, k: (pt[i, k], 0, 0), k=k))
                  for k in range(n_pages)]
    return pl.pallas_call(
        functools.partial(_compress_paged_kernel, n_pages=n_pages, n_valid=r),
        grid_spec=pltpu.PrefetchScalarGridSpec(
            num_scalar_prefetch=1,
            grid=(db,),
            in_specs=page_specs + [pl.BlockSpec((1, 1, CHUNK_W), lambda i, pt: (i, 0, 0))]
                     + [pl.BlockSpec(a.shape, lambda i, pt: (0, 0)) for a in consts],
            out_specs=pl.BlockSpec((1, NSA_G, r, LANES), lambda i, pt: (i, 0, 0, 0)),
            scratch_shapes=[pltpu.VMEM((NSA_G, r, CMP_STRIDE * LANES), F32)],
        ),
        out_shape=jax.ShapeDtypeStruct((db, NSA_G, r, LANES), BF16),
        compiler_params=_cparams(("arbitrary",), VMEM_LIMIT),
        name="compress_paged",
    )(page_table, *([pool] * n_pages), tail, *consts)


def _rank_select(score, idx, n_blocks):
    cnt = jnp.zeros(score.shape, jnp.int32)
    for i in range(n_blocks):
        other = score[:, i:i + 1]
        ge = jnp.where(other >= score, 1, 0)
        gt = jnp.where(other > score, 1, 0)
        cnt = cnt + jnp.where(idx > i, ge, gt)
    return jnp.where(score > -jnp.inf, jnp.where(cnt < N_SEL, 0.0, UNSELECTED), UNSELECTED)


def _rank_select_rows(score, n_blocks, last_pos, cnt_ref):
    sub = _iota((8, score.shape[1]), 0)
    n_groups = -(-n_blocks // 8)
    groups = [score[g * 8:(g + 1) * 8, :] for g in range(n_groups)]
    cnt_ref[...] = jnp.zeros(cnt_ref.shape, jnp.int32)
    for og in range(n_groups):
        @pl.when(og * 8 * SLC_BLOCK <= last_pos)
        def _():
            part = [None] * n_groups
            for i in range(og * 8, min(og * 8 + 8, n_blocks)):
                other = score[i:i + 1, :]
                for g, sg in enumerate(groups):
                    if g > og:
                        one = jnp.where(other >= sg, 1, 0)
                    elif g < og:
                        one = jnp.where(other > sg, 1, 0)
                    else:
                        one = jnp.where(sub > i % 8, jnp.where(other >= sg, 1, 0), jnp.where(other > sg, 1, 0))
                    part[g] = one if part[g] is None else part[g] + one
            for g in range(n_groups):
                cnt_ref[g * 8:(g + 1) * 8, :] += part[g]
    cnts = [cnt_ref[g * 8:(g + 1) * 8, :] for g in range(n_groups)]
    out = [jnp.where(sg > -jnp.inf, jnp.where(c < N_SEL, 0.0, UNSELECTED), UNSELECTED) for sg, c in zip(groups, cnts)]
    pad = score.shape[0] - 8 * n_groups
    if pad:
        out.append(jnp.full((pad, score.shape[1]), UNSELECTED, F32))
    return jnp.concatenate(out, axis=0)


def _cmp_prompt_kernel(qt_ref, kv_ref, kvt_ref, ovt_ref, o_ref, sel_ref, cnt_ref, *, tq, n_slc):
    i = pl.program_id(2)
    kv = kv_ref[0, 0]
    kvt = kvt_ref[0, 0]
    r = kv.shape[0]
    pos = i * tq + _iota((1, tq), 1)
    visible = (_iota((r, 1), 0) * CMP_STRIDE + (CMP_BLOCK - 1)) <= pos
    zeros = jnp.zeros((HD, tq), BF16)
    psum = jnp.zeros((r, tq), F32)
    outs = []
    for h in range(NSA_HPG):
        qa = jnp.concatenate([qt_ref[0, h * HD:(h + 1) * HD, :], zeros], axis=0)
        s = jnp.where(visible, _dot(kv, qa), -jnp.inf)
        m = jnp.max(s, axis=0, keepdims=True)
        e = jnp.exp(s - jnp.where(m > -jnp.inf, m, 0.0))
        inv_l = 1.0 / jnp.maximum(jnp.sum(e, axis=0, keepdims=True), jnp.finfo(F32).tiny)
        psum = psum + e * inv_l
        outs.append(_dot(kvt[HD:2 * HD, :], e.astype(BF16)) * inv_l)
    o_ref[...] = jnp.concatenate([jnp.concatenate(outs[0:2], axis=0).T, jnp.concatenate(outs[2:4], axis=0).T], axis=1)
    imp = _dot_exact_rhs_left(ovt_ref[...], _split2(psum))
    blk = _iota(imp.shape, 0)
    pos_row = i * tq + _iota(imp.shape, 1)
    forced = (blk == jnp.right_shift(pos_row, 6)) | (blk == 0)
    score = jnp.where(forced, jnp.inf, jnp.where(blk * SLC_BLOCK <= pos_row, imp, -jnp.inf))
    sel_ref[0, 0] = _rank_select_rows(score, n_slc, i * tq + tq - 1, cnt_ref).astype(BF16)


def _cmp_prompt(qn_t, kvcmp, kvcmp_t, ovt, b, s, tq):
    nq = s // tq
    r = kvcmp.shape[2]
    n_slc = s // SLC_BLOCK
    assert n_slc <= HD and ovt.shape == (HD, r) and SLC_BLOCK == 64
    return pl.pallas_call(
        functools.partial(_cmp_prompt_kernel, tq=tq, n_slc=n_slc),
        grid=(b, NSA_G, nq),
        in_specs=[pl.BlockSpec((1, NSA_HPG * HD, tq), lambda bi, g, i: (bi, g, i)),
                  pl.BlockSpec((1, 1, r, LANES), lambda bi, g, i: (bi, g, 0, 0)),
                  pl.BlockSpec((1, 1, LANES, r), lambda bi, g, i: (bi, g, 0, 0)),
                  pl.BlockSpec(ovt.shape, lambda bi, g, i: (0, 0))],
        out_specs=[pl.BlockSpec((tq, NSA_HPG * HD), lambda bi, g, i: (bi * nq + i, g)),
                   pl.BlockSpec((1, 1, HD, tq), lambda bi, g, i: (bi, g, 0, i))],
        out_shape=[jax.ShapeDtypeStruct((b * s, NSA_H * HD), F32),
                   jax.ShapeDtypeStruct((b, NSA_G, HD, s), BF16)],
        scratch_shapes=[pltpu.VMEM((HD, tq), jnp.int32)],
        compiler_params=_cparams(("arbitrary",) * 3, VMEM_LIMIT),
        name="cmp_attend_select_prompt",
    )(qn_t, kvcmp, kvcmp_t, ovt)


def _cmp_sample_kernel(q_ref, kv_ref, ov_ref, o_ref, sel_ref, *, pos, n_slc):
    r = kv_ref.shape[2]
    row = _iota((NSA_H, 1), 0)
    visible = (_iota((1, r), 1) * CMP_STRIDE + (CMP_BLOCK - 1)) <= pos
    n_seq = q_ref.shape[0]
    scores = []
    for bi in range(n_seq):
        q = q_ref[bi]
        kvs = [kv_ref[bi, g] for g in range(NSA_G)]
        s = jnp.where(row < NSA_HPG, _dot_nt(q, kvs[0][:, 0:HD]), _dot_nt(q, kvs[1][:, 0:HD]))
        p = _masked_softmax(s, visible)
        pb = p.astype(BF16)
        o = jnp.where(row < NSA_HPG, _dot(pb, kvs[0]), _dot(pb, kvs[1]))
        o_ref[bi] = o[:, HD:2 * HD]
        p0 = jnp.sum(p[0:NSA_HPG], axis=0, keepdims=True)
        p1 = jnp.sum(p[NSA_HPG:NSA_H], axis=0, keepdims=True)
        psum = jnp.where(row < NSA_HPG, p0, p1)
        imp = _dot_exact_rhs(_split2(psum), ov_ref[...])
        blk = _iota(imp.shape, 1)
        forced = (blk == pos // SLC_BLOCK) | (blk == 0)
        scores.append(jnp.where(forced, jnp.inf, jnp.where(blk * SLC_BLOCK <= pos, imp, -jnp.inf)))
    stacked = jnp.concatenate(scores, axis=0)
    sel = _rank_select(stacked, _iota(stacked.shape, 1), n_slc)
    for bi in range(n_seq):
        sel_ref[bi] = sel[bi * NSA_H:(bi + 1) * NSA_H]


def _cmp_sample(qn3, kvcmp, ov, pos, n_slc):
    db, _, r, _ = kvcmp.shape
    nslp = ov.shape[1]
    bb = _pick_tile(db, SAMPLE_SEQS_PER_STEP)
    return pl.pallas_call(
        functools.partial(_cmp_sample_kernel, pos=pos, n_slc=n_slc),
        grid=(db // bb,),
        in_specs=[pl.BlockSpec((bb, NSA_H, HD), lambda i: (i, 0, 0)),
                  pl.BlockSpec((bb, NSA_G, r, LANES), lambda i: (i, 0, 0, 0)),
                  pl.BlockSpec(ov.shape, lambda i: (0, 0))],
        out_specs=[pl.BlockSpec((bb, NSA_H, HD), lambda i: (i, 0, 0)),
                   pl.BlockSpec((bb, NSA_H, nslp), lambda i: (i, 0, 0))],
        out_shape=[jax.ShapeDtypeStruct((db, NSA_H, HD), F32),
                   jax.ShapeDtypeStruct((db, NSA_H, nslp), F32)],
        compiler_params=_cparams(("arbitrary",), VMEM_LIMIT),
        name="cmp_attend_select_sample",
    )(qn3, kvcmp, ov)


def _flash_prompt_kernel(*refs, tq, selected):
    if selected:
        qt_ref, kv_ref, kvt_ref, sel_ref, o_ref, kop, vop, qa_ref, m_ref, acc_ref = refs
    else:
        qt_ref, kv_ref, kvt_ref, o_ref, kop, vop, qa_ref, m_ref, acc_ref = refs
    i = pl.program_id(2)
    cols = NSA_HPG * tq
    n_tiles, v_rows, tk = vop.shape
    ones_rows = v_rows - HD
    wide = FLASH_COLS

    @pl.when(i == 0)
    def _():
        kvf = kv_ref[0]
        if selected:
            lane = _iota(kvf.shape, 1)
            onehot = (lane - HD) == jnp.right_shift(_iota(kvf.shape, 0), 6)
            kop[...] = jnp.where(lane < HD, kvf, onehot.astype(F32)).astype(BF16)
        else:
            kop[...] = kvf.astype(BF16)
        for t in range(n_tiles):
            vop[t, 0:ones_rows, :] = jnp.ones((ones_rows, tk), BF16)
            vop[t, ones_rows:v_rows, :] = kvt_ref[0, HD:2 * HD, t * tk:(t + 1) * tk].astype(BF16)

    extra = sel_ref[0, 0] if selected else jnp.zeros((HD, tq), BF16)
    for h in range(NSA_HPG):
        qa_ref[0:HD, h * tq:(h + 1) * tq] = qt_ref[0, h * HD:(h + 1) * HD, :]
        qa_ref[HD:2 * HD, h * tq:(h + 1) * tq] = extra
    m_ref[...] = jnp.full((1, cols), MASKED, F32)
    acc_ref[...] = jnp.zeros((v_rows, cols), F32)

    def tile_step(j, masked, blocks):
        start = pl.multiple_of(j * tk, tk)
        for cb in blocks:
            sl = slice(cb * wide, (cb + 1) * wide)
            s = _dot(kop[pl.ds(start, tk), :], qa_ref[:, sl])
            if masked:
                pos = i * tq + (cb * wide) % tq + _iota((1, wide), 1)
                kpos = j * tk + _iota((tk, 1), 0)
                valid = kpos <= pos
                if not selected:
                    valid = valid & ((pos - kpos) <= WINDOW)
                s = jnp.where(valid, s, MASKED)
            m_old = m_ref[:, sl]
            m_new = jnp.maximum(m_old, jnp.max(s, axis=0, keepdims=True))
            p = jnp.exp(s - m_new).astype(BF16)
            m_ref[:, sl] = m_new
            acc_ref[:, sl] = jnp.exp(m_old - m_new) * acc_ref[:, sl] + _dot(vop[j], p)

    per_q = tq // tk
    all_blocks = list(range(cols // wide))

    def reachable(back, u):
        first_key = u * tk - back * tq
        out = []
        for cb in all_blocks:
            q0 = (cb * wide) % tq
            causal = q0 + wide - 1 >= first_key
            in_window = selected or q0 - (first_key + tk - 1) <= WINDOW
            if causal and in_window:
                out.append(cb)
        return out

    def masked_unit(back):
        for u in range(per_q):
            tile_step((i - back) * per_q + u, True, reachable(back, u))

    if selected:
        def interior(t, carry):
            for u in range(per_q):
                tile_step(t * per_q + u, False, all_blocks)
            return carry
        lax.fori_loop(0, i, interior, 0)
    else:
        for back in range(WINDOW // tq, 0, -1):
            pl.when(i >= back)(functools.partial(masked_unit, back))
    masked_unit(0)
    normed = []
    for h in range(NSA_HPG):
        a = acc_ref[:, h * tq:(h + 1) * tq]
        normed.append(a[ones_rows:v_rows, :] * (1.0 / a[0:1, :]))
    o_ref[...] = jnp.concatenate([jnp.concatenate(normed[0:2], axis=0).T, jnp.concatenate(normed[2:4], axis=0).T], axis=1)


def _flash_prompt(qrot_t, kv, kv_t, sel_t, b, s, tq, tk):
    nq = s // tq
    assert WINDOW % tq == 0 and tq % tk == 0 and tk % LANES == 0 and s % tq == 0 and tq % FLASH_COLS == 0
    v_rows = FLASH_ONES_ROWS + HD
    selected = sel_t is not None
    in_specs = [pl.BlockSpec((1, NSA_HPG * HD, tq), lambda bi, g, i: (bi, g, i)),
                pl.BlockSpec((1, s, LANES), lambda bi, g, i: (bi, 0, g)),
                pl.BlockSpec((1, LANES, s), lambda bi, g, i: (bi, g, 0))]
    args = [qrot_t, kv, kv_t]
    cols = NSA_HPG * tq
    scratch = [pltpu.VMEM((s, LANES), BF16), pltpu.VMEM((s // tk, v_rows, tk), BF16),
               pltpu.VMEM((LANES, cols), BF16), pltpu.VMEM((1, cols), F32), pltpu.VMEM((v_rows, cols), F32)]
    if selected:
        in_specs.append(pl.BlockSpec((1, 1, HD, tq), lambda bi, g, i: (bi, g, 0, i)))
        args.append(sel_t)
    return pl.pallas_call(
        functools.partial(_flash_prompt_kernel, tq=tq, selected=selected),
        grid=(b, NSA_G, nq),
        in_specs=in_specs,
        out_specs=pl.BlockSpec((tq, NSA_HPG * HD), lambda bi, g, i: (bi * nq + i, g)),
        out_shape=jax.ShapeDtypeStruct((b * s, NSA_H * HD), F32),
        scratch_shapes=scratch,
        compiler_params=_cparams(("arbitrary",) * 3, VMEM_LIMIT),
        name="slc_attend_prompt" if selected else "win_attend_prompt",
    )(*args)


def _mem_kv_kernel(x_ref, nm_ref, w_ref, kg_ref, o_ref, ot_ref):
    xn = _rms_rows(x_ref[0], nm_ref[...]).astype(BF16)
    kv = _dot(xn, w_ref[...])
    cols = []
    for h in range(MEM_H):
        k = kv[:, 2 * h * MEM_HD:(2 * h + 1) * MEM_HD]
        cols.append(_rms_rows(k, kg_ref[...]))
        cols.append(kv[:, (2 * h + 1) * MEM_HD:(2 * h + 2) * MEM_HD])
    out = jnp.concatenate(cols, axis=1)
    o_ref[0] = out
    ot_ref[0] = out.T.astype(BF16)


def _mem_kv(mem, pw):
    b, t_mem, _ = mem.shape
    consts = [pw['norm_mem'], pw['w_mem_kv'], pw['mem_k_gain']]
    return pl.pallas_call(
        _mem_kv_kernel,
        grid=(b,),
        in_specs=[pl.BlockSpec((1, t_mem, D_MODEL), lambda i: (i, 0, 0))]
                 + [pl.BlockSpec(a.shape, lambda i: (0, 0)) for a in consts],
        out_specs=[pl.BlockSpec((1, t_mem, 2 * MEM_W), lambda i: (i, 0, 0)),
                   pl.BlockSpec((1, 2 * MEM_W, t_mem), lambda i: (i, 0, 0))],
        out_shape=[jax.ShapeDtypeStruct((b, t_mem, 2 * MEM_W), F32),
                   jax.ShapeDtypeStruct((b, 2 * MEM_W, t_mem), BF16)],
        compiler_params=_cparams(("arbitrary",), VMEM_LIMIT),
        name="mem_kv",
    )(mem, *consts)


def _mem_prompt_kernel(qt_ref, kv_ref, kvt_ref, o_ref):
    for h in range(MEM_H):
        k = kv_ref[0, :, 2 * h * MEM_HD:(2 * h + 1) * MEM_HD].astype(BF16)
        s = _dot(k, qt_ref[0, h * MEM_HD:(h + 1) * MEM_HD, :])
        p = jnp.exp(s - jnp.max(s, axis=0, keepdims=True))
        vt = kvt_ref[0, (2 * h + 1) * MEM_HD:(2 * h + 2) * MEM_HD, :]
        ot = _dot(vt, p.astype(BF16)) * (1.0 / jnp.sum(p, axis=0, keepdims=True))
        o_ref[:, h * MEM_HD:(h + 1) * MEM_HD] = ot.T


def _mem_prompt(mq_t, kvm, kvm_t, b, s, tq):
    nq = s // tq
    t_mem = kvm.shape[1]
    return pl.pallas_call(
        _mem_prompt_kernel,
        grid=(b, nq),
        in_specs=[pl.BlockSpec((1, MEM_W, tq), lambda bi, i: (bi, 0, i)),
                  pl.BlockSpec((1, t_mem, 2 * MEM_W), lambda bi, i: (bi, 0, 0)),
                  pl.BlockSpec((1, 2 * MEM_W, t_mem), lambda bi, i: (bi, 0, 0))],
        out_specs=pl.BlockSpec((tq, MEM_W), lambda bi, i: (bi * nq + i, 0)),
        out_shape=jax.ShapeDtypeStruct((b * s, MEM_W), F32),
        compiler_params=_cparams(("arbitrary",) * 2, VMEM_LIMIT),
        name="mem_attend_prompt",
    )(mq_t, kvm, kvm_t)


def _decode_attend(qa, blocks, new_row=None, new_bias=None, feature_major=False):
    qk, pv = (_dot, _dot_nt) if feature_major else (_dot_nt, _dot)
    scores = []
    for load, bias in blocks:
        s = qk(qa, load())
        scores.append(s if bias is None else s + bias)
    m = functools.reduce(jnp.maximum, [jnp.max(s, axis=-1, keepdims=True) for s in scores])
    if new_row is not None:
        nb = new_row.astype(BF16).astype(F32)
        s_new = jnp.sum(qa.astype(F32) * nb, axis=-1, keepdims=True) + new_bias
        m = jnp.maximum(m, s_new)
    ps = [jnp.exp(s - m) for s in scores]
    l = functools.reduce(lambda a, c: a + c, [jnp.sum(p, axis=-1, keepdims=True) for p in ps])
    acc = functools.reduce(lambda a, c: a + c, [pv(p.astype(BF16), load()) for p, (load, _) in zip(ps, blocks)])
    if new_row is not None:
        p_new = jnp.exp(s_new - m)
        l = l + p_new
        acc = acc + p_new.astype(BF16).astype(F32) * nb
    return acc / l


def _group_values(o):
    row = _iota((NSA_H, LANES), 0)
    both = jnp.where(row < NSA_HPG, o[:, 0:LANES], o[:, LANES:2 * LANES])
    return both[:, HD:2 * HD]


def _slc_sample_kernel(pt_ref, *refs, n_pages):
    pages = refs[:n_pages]
    q_ref, sel_ref, new_ref, o_ref = refs[n_pages:]
    sb = sel_ref[0]
    lane = _iota((NSA_H, PAGE_SIZE), 1)
    blocks = []
    for p, page in enumerate(pages):
        bias = jnp.where(lane < SLC_BLOCK, sb[:, 2 * p:2 * p + 1], sb[:, 2 * p + 1:2 * p + 2])
        blocks.append((lambda page=page: page[0].astype(BF16), bias))
    n_past = n_pages * (PAGE_SIZE // SLC_BLOCK)
    o = _decode_attend(q_ref[0], blocks, new_ref[0], sb[:, n_past:n_past + 1], feature_major=True)
    o_ref[0] = _group_values(o)


def _feature_major(cache):
    n, rows = cache.shape[:2]
    return jnp.transpose(cache, (0, 2, 3, 4, 1)).reshape(n, KV_W, rows)


def _slc_sample(qaug, sel, cache, page_table, kvs_new):
    db, n_pages = page_table.shape
    nslp = sel.shape[2]
    pool = _feature_major(cache)
    page_specs = [pl.BlockSpec((1, KV_W, PAGE_SIZE), functools.partial(lambda i, pt, k: (pt[i, k], 0, 0), k=k))
                  for k in range(n_pages)]
    return pl.pallas_call(
        functools.partial(_slc_sample_kernel, n_pages=n_pages),
        grid_spec=pltpu.PrefetchScalarGridSpec(
            num_scalar_prefetch=1,
            grid=(db,),
            in_specs=page_specs + [pl.BlockSpec((1, NSA_H, KV_W), lambda i, pt: (i, 0, 0)),
                                   pl.BlockSpec((1, NSA_H, nslp), lambda i, pt: (i, 0, 0)),
                                   pl.BlockSpec((1, 1, KV_W), lambda i, pt: (i, 0, 0))],
            out_specs=pl.BlockSpec((1, NSA_H, HD), lambda i, pt: (i, 0, 0)),
        ),
        out_shape=jax.ShapeDtypeStruct((db, NSA_H, HD), F32),
        compiler_params=_cparams(("arbitrary",), VMEM_LIMIT),
        name="slc_attend_sample",
    )(page_table, *([pool] * n_pages), qaug, sel, kvs_new)


def _win_sample_kernel(q_ref, buf_ref, new_ref, newcol_ref, o_ref, win_ref, *, pos, first_pos):
    w_buf = buf_ref.shape[2]
    d = pos - (first_pos + _iota((1, w_buf), 1))
    bias = jnp.where((d >= 0) & (d <= WINDOW), 0.0, MASKED)
    for bi in range(q_ref.shape[0]):
        buf = buf_ref[bi]
        o = _decode_attend(q_ref[bi], [(lambda buf=buf: buf.astype(BF16), bias)], new_ref[bi], 0.0, feature_major=True)
        o_ref[bi] = _group_values(o)
        shifted = pltpu.roll(buf, w_buf - 1, 1)
        win_ref[bi] = jnp.where(_iota(buf.shape, 1) == w_buf - 1, newcol_ref[bi], shifted)


def _win_sample(qaug, win_buf_t, kvw_new, pos, first_pos):
    db, _, w_buf = win_buf_t.shape
    bb = _pick_tile(db, SAMPLE_SEQS_PER_STEP)
    return pl.pallas_call(
        functools.partial(_win_sample_kernel, pos=pos, first_pos=first_pos),
        grid=(db // bb,),
        in_specs=[pl.BlockSpec((bb, NSA_H, KV_W), lambda i: (i, 0, 0)),
                  pl.BlockSpec((bb, KV_W, w_buf), lambda i: (i, 0, 0)),
                  pl.BlockSpec((bb, 1, KV_W), lambda i: (i, 0, 0)),
                  pl.BlockSpec((bb, KV_W, 1), lambda i: (i, 0, 0))],
        out_specs=[pl.BlockSpec((bb, NSA_H, HD), lambda i: (i, 0, 0)),
                   pl.BlockSpec((bb, KV_W, w_buf), lambda i: (i, 0, 0))],
        out_shape=[jax.ShapeDtypeStruct((db, NSA_H, HD), F32),
                   jax.ShapeDtypeStruct((db, KV_W, w_buf), F32)],
        compiler_params=_cparams(("arbitrary",), VMEM_LIMIT),
        name="win_attend_sample",
    )(qaug, win_buf_t, kvw_new, kvw_new.reshape(db, KV_W, 1))


def _mem_sample_kernel(q_ref, kv_ref, o_ref):
    row = _iota((8, MEM_HD), 0)
    for bi in range(q_ref.shape[0]):
        q = q_ref[bi]
        out = jnp.zeros((8, MEM_HD), F32)
        for h in range(MEM_H):
            k = kv_ref[bi, :, h, 0, :].astype(BF16)
            v = kv_ref[bi, :, h, 1, :].astype(BF16)
            s = _dot_nt(q, k)
            p = jnp.exp(s - jnp.max(s, axis=-1, keepdims=True))
            o = _dot(p.astype(BF16), v) / jnp.sum(p, axis=-1, keepdims=True)
            out = jnp.where(row == h, o, out)
        o_ref[bi] = out


def _mem_sample(q8, cache_mem):
    db, t_mem = cache_mem.shape[:2]
    bb = _pick_tile(db, SAMPLE_SEQS_PER_STEP)
    return pl.pallas_call(
        _mem_sample_kernel,
        grid=(db // bb,),
        in_specs=[pl.BlockSpec((bb, 8, MEM_HD), lambda i: (i, 0, 0)),
                  pl.BlockSpec((bb, t_mem, MEM_H, 2, MEM_HD), lambda i: (i, 0, 0, 0, 0))],
        out_specs=pl.BlockSpec((bb, 8, MEM_HD), lambda i: (i, 0, 0)),
        out_shape=jax.ShapeDtypeStruct((db, 8, MEM_HD), F32),
        compiler_params=_cparams(("arbitrary",), VMEM_LIMIT),
        name="mem_attend_sample",
    )(q8, cache_mem)


def _mlstm_prompt_kernel(u_ref, oraw_ref, slab_ref, slabt_ref, cw_ref, cb_ref, wqt_ref, wk_ref, wvt_ref,
                         bgc_ref, bgr_ref, mlgt_ref, tril_ref, triu_ref,
                         o_ref, ct_ref, n_ref, m_ref, ubuf, *, chunk):
    L = chunk
    pad = 8

    @pl.when(pl.program_id(1) == 0)
    def _():
        ct_ref[...] = jnp.zeros_like(ct_ref)
        n_ref[...] = jnp.zeros_like(n_ref)
        m_ref[...] = jnp.zeros_like(m_ref)
        ubuf[0:pad, :] = jnp.zeros((pad, ML_W), F32)

    u = u_ref[0]
    ubuf[pad:pad + L, :] = u
    conv = cb_ref[...] + cw_ref[CONV_W - 1:CONV_W, :] * u
    for w in range(CONV_W - 1):
        off = pad - (CONV_W - 1) + w
        conv = conv + cw_ref[w:w + 1, :] * ubuf[off:off + L, :]
    ubuf[0:pad, :] = u[L - pad:L, :]
    cact = conv * _sigmoid(conv)

    gl = slab_ref[0] + bgc_ref[...]
    gt = slabt_ref[0] + bgr_ref[...]
    b_c = None
    for part in _split3(_log_sigmoid(gl)):
        term = _dot(tril_ref[...], part)
        b_c = term if b_c is None else b_c + term
    b_r = _dot_exact_rhs(_split3(_log_sigmoid(gt)), triu_ref[...])
    m_prev = m_ref[0]
    not_after = _iota((L, L), 0) <= _iota((L, L), 1)
    lane1 = _iota((1, LANES), 1)
    gain_t = jnp.concatenate([mlgt_ref[...]] * (L // LANES), axis=1)
    m_out = m_prev
    outs = []
    for h in range(ML_H):
        sl = slice(h * ML_D, (h + 1) * ML_D)
        br = b_r[_SLAB_F + h:_SLAB_F + h + 1, :]
        ir = gt[_SLAB_I + h:_SLAB_I + h + 1, :]
        src = gl[:, _SLAB_I + h:_SLAB_I + h + 1] - b_c[:, _SLAB_F + h:_SLAB_F + h + 1]
        m0 = m_prev[:, h:h + 1]
        dm = jnp.where(not_after, br + src, -jnp.inf)
        m_new = jnp.maximum(br + m0, jnp.max(dm, axis=0, keepdims=True))
        ch = cact[:, sl].astype(BF16)
        kb = _dot(ch, wk_ref[h]).astype(BF16)
        qt = _dot_nt(wqt_ref[h], ch) * (ML_D ** -0.5)
        vt = _dot_nt(wvt_ref[h], u[:, sl].astype(BF16))
        qtb = qt.astype(BF16)
        wmat = jnp.exp(dm - m_new) * _dot(kb, qtb)
        inter = jnp.exp(br + m0 - m_new)
        ct_old = ct_ref[0, h]
        n_old = n_ref[0, h:h + 1, :]
        num = inter * _dot(ct_old.astype(BF16), qtb) + _dot(vt.astype(BF16), wmat.astype(BF16))
        nq = _dot_exact_rhs(_split2(jnp.broadcast_to(n_old, (8, ML_D))), qtb)[0:1]
        den = inter * nq + jnp.sum(wmat, axis=0, keepdims=True)
        hh = num / jnp.maximum(jnp.abs(den), jnp.exp(-m_new))
        m_end = m_new[:, L - 1:L]
        b_end = br[:, L - 1:L]
        decay = jnp.exp(b_end + m0 - m_end)
        wend = jnp.exp(b_end - br + ir - m_end)
        ct_ref[0, h] = decay * ct_old + _dot((vt * wend).astype(BF16), kb)
        n_ref[0, h:h + 1, :] = decay * n_old + _dot_exact_rhs(_split2(jnp.broadcast_to(wend, (8, L))), kb)[0:1]
        m_out = jnp.where(lane1 == h, m_end, m_out)
        hn = hh * lax.rsqrt(jnp.mean(hh * hh, axis=0, keepdims=True) + EPS) * gain_t[sl, :]
        outs.append(_sigmoid(oraw_ref[0, :, sl]) * hn.T)
    o_ref[0] = jnp.concatenate(outs, axis=1)
    m_ref[0] = m_out


def _mlstm_prompt(u, oraw, slab, pw, chunk):
    b, s, _ = u.shape
    nc = s // chunk
    slabt = jnp.swapaxes(slab[:, :, 0:8], 1, 2)
    tril = jnp.tril(jnp.ones((chunk, chunk), BF16))
    consts = [pw['conv_w'], pw['conv_b'], pw['w_ml_q_t'], pw['w_ml_k'], pw['w_ml_v_t'], pw['gate_bias_lanes'],
              pw['gate_bias_rows'], pw['ml_gain_t'], tril, tril.T]
    blk = lambda w: pl.BlockSpec((1, chunk, w), lambda bi, c: (bi, c, 0))

    def const_spec(a):
        nd = a.ndim
        return pl.BlockSpec(a.shape, lambda bi, c: (0,) * nd)

    return pl.pallas_call(
        functools.partial(_mlstm_prompt_kernel, chunk=chunk),
        grid=(b, nc),
        in_specs=[blk(ML_W), blk(ML_W), blk(LANES), pl.BlockSpec((1, 8, chunk), lambda bi, c: (bi, 0, c))]
                 + [const_spec(a) for a in consts],
        out_specs=[blk(ML_W),
                   pl.BlockSpec((1, ML_H, ML_D, ML_D), lambda bi, c: (bi, 0, 0, 0)),
                   pl.BlockSpec((1, ML_H, ML_D), lambda bi, c: (bi, 0, 0)),
                   pl.BlockSpec((1, 1, LANES), lambda bi, c: (bi, 0, 0))],
        out_shape=[jax.ShapeDtypeStruct((b, s, ML_W), F32),
                   jax.ShapeDtypeStruct((b, ML_H, ML_D, ML_D), F32),
                   jax.ShapeDtypeStruct((b, ML_H, ML_D), F32),
                   jax.ShapeDtypeStruct((b, 1, LANES), F32)],
        scratch_shapes=[pltpu.VMEM((chunk + 8, ML_W), F32)],
        compiler_params=_cparams(("arbitrary", "arbitrary"), VMEM_LIMIT),
        name="mlstm_prompt",
    )(u, oraw, slab, slabt, *consts)


def _mlstm_sample_kernel(ext_ref, oraw_ref, slab_ref, c_ref, n_ref, m_ref, cw_ref, cb_ref, wq_ref, wk_ref,
                         wv_ref, wkt_ref, bgc_ref, mlg_ref, o_ref, co_ref, no_ref, mo_ref):
    for bi in range(ext_ref.shape[0]):
        _mlstm_sample_step(bi, ext_ref, oraw_ref, slab_ref, c_ref, n_ref, m_ref, cw_ref, cb_ref, wq_ref, wk_ref,
                           wv_ref, wkt_ref, bgc_ref, mlg_ref, o_ref, co_ref, no_ref, mo_ref)


def _mlstm_sample_step(bi, ext_ref, oraw_ref, slab_ref, c_ref, n_ref, m_ref, cw_ref, cb_ref, wq_ref, wk_ref,
                       wv_ref, wkt_ref, bgc_ref, mlg_ref, o_ref, co_ref, no_ref, mo_ref):
    ext = ext_ref[bi]
    conv = cb_ref[...]
    for w in range(CONV_W):
        conv = conv + cw_ref[w:w + 1, :] * ext[w:w + 1, :]
    cact = conv * _sigmoid(conv)
    u = ext[CONV_W - 1:CONV_W, :]
    gl = slab_ref[bi] + bgc_ref[...]
    lf_all = _log_sigmoid(gl)
    m_prev = m_ref[bi]
    lane1 = _iota((1, LANES), 1)
    m_out = m_prev
    outs = []
    for h in range(ML_H):
        sl = slice(h * ML_D, (h + 1) * ML_D)
        ch8 = jnp.broadcast_to(cact[:, sl], (8, ML_D)).astype(BF16)
        u8 = jnp.broadcast_to(u[:, sl], (8, ML_D)).astype(BF16)
        q = (_dot(ch8, wq_ref[h]) * (ML_D ** -0.5))[0:1]
        k = _dot(ch8, wk_ref[h])[0:1]
        v = _dot(u8, wv_ref[h])[0:1]
        k_col = _dot_nt(wkt_ref[h], ch8)[:, 0:1]
        ig = gl[:, _SLAB_I + h:_SLAB_I + h + 1]
        lf = lf_all[:, _SLAB_F + h:_SLAB_F + h + 1]
        m0 = m_prev[:, h:h + 1]
        m_new = jnp.maximum(lf + m0, ig)
        qb = q.astype(BF16)
        qk = jnp.sum(qb.astype(F32) * k.astype(BF16).astype(F32), axis=-1, keepdims=True)
        w_in = jnp.exp(ig - m_new) * qk
        inter = jnp.exp(lf + m0 - m_new)
        c_old = c_ref[bi, h]
        n_old = n_ref[bi, h:h + 1, :]
        qc = _dot(jnp.broadcast_to(qb, (8, ML_D)), c_old.astype(BF16))[0:1]
        num = inter * qc + w_in * v
        den = inter * jnp.sum(q * n_old, axis=-1, keepdims=True) + w_in
        hh = num / jnp.maximum(jnp.abs(den), jnp.exp(-m_new))
        w_end = jnp.exp(ig - m_new)
        co_ref[bi, h] = inter * c_old + k_col * (w_end * v)
        no_ref[bi, h:h + 1, :] = inter * n_old + w_end * k
        m_out = jnp.where(lane1 == h, m_new, m_out)
        hn = hh * lax.rsqrt(jnp.mean(hh * hh, axis=-1, keepdims=True) + EPS) * mlg_ref[:, sl]
        outs.append(_sigmoid(oraw_ref[bi, :, sl]) * hn)
    o_ref[bi] = jnp.concatenate(outs, axis=1)
    mo_ref[bi] = m_out


def _mlstm_sample(ext, oraw, slab, state_c, state_n, state_m, pw):
    db = ext.shape[0]
    m_in = jnp.pad(state_m, ((0, 0), (0, LANES - ML_H))).reshape(db, 1, LANES)
    consts = [pw['conv_w'], pw['conv_b'], pw['w_ml_q'], pw['w_ml_k'], pw['w_ml_v'], pw['w_ml_k_t'],
              pw['gate_bias_lanes'], pw['ml_gain']]

    def const_spec(a):
        nd = a.ndim
        return pl.BlockSpec(a.shape, lambda i: (0,) * nd)

    bb = _pick_tile(db, SAMPLE_SEQS_PER_STEP)
    per_seq = lambda shape: pl.BlockSpec((bb,) + shape, lambda i: (i,) + (0,) * len(shape))
    return pl.pallas_call(
        _mlstm_sample_kernel,
        grid=(db // bb,),
        in_specs=[per_seq((CONV_W, ML_W)), per_seq((1, ML_W)), per_seq((1, LANES)), per_seq((ML_H, ML_D, ML_D)),
                  per_seq((ML_H, ML_D)), per_seq((1, LANES))] + [const_spec(a) for a in consts],
        out_specs=[per_seq((1, ML_W)), per_seq((ML_H, ML_D, ML_D)), per_seq((ML_H, ML_D)), per_seq((1, LANES))],
        out_shape=[jax.ShapeDtypeStruct((db, 1, ML_W), F32),
                   jax.ShapeDtypeStruct((db, ML_H, ML_D, ML_D), F32),
                   jax.ShapeDtypeStruct((db, ML_H, ML_D), F32),
                   jax.ShapeDtypeStruct((db, 1, LANES), F32)],
        compiler_params=_cparams(("arbitrary",), VMEM_LIMIT),
        name="mlstm_sample",
    )(ext, oraw, slab, state_c, state_n, m_in, *consts)


def _merge_kernel(x_ref, nm_ref, wg_ref, slab_ref, ex_ref, ocmp_ref, oslc_ref, owin_ref, oml_ref, omem_ref,
                  wb_ref, wout_ref, o_ref):
    x = x_ref[...]
    xn = _rms_rows(x, nm_ref[...]).astype(BF16)
    gparts = _split2(_sigmoid(slab_ref[...]))
    onsa = None
    for br, br_ref in enumerate((ocmp_ref, oslc_ref, owin_ref)):
        term = _dot_exact_rhs(gparts, ex_ref[br]) * br_ref[...]
        onsa = term if onsa is None else onsa + term
    z = None
    for n, on in enumerate((onsa, oml_ref[...], omem_ref[...])):
        gate = _sigmoid(_dot(xn, wg_ref[:, n * D_MODEL:(n + 1) * D_MODEL]))
        term = gate * _dot(on.astype(BF16), wb_ref[n])
        z = term if z is None else z + term
    o_ref[...] = x + _dot(z.astype(BF16), wout_ref[...])


def _merge(x2d, slab, ocmp, oslc, owin, oml, omem, pw, tm):
    t = x2d.shape[0]
    row = lambda w: pl.BlockSpec((tm, w), lambda i: (i, 0))
    const_spec = lambda a: _resident_spec(a)
    return pl.pallas_call(
        _merge_kernel,
        grid=(t // tm,),
        in_specs=[row(D_MODEL), const_spec(pw['norm_mix']), const_spec(pw['w_gate']), row(LANES),
                  const_spec(pw['gate_expand'])] + [row(BRANCH_W)] * 5
                 + [const_spec(pw['w_branch']), const_spec(pw['w_out'])],
        out_specs=row(D_MODEL),
        out_shape=jax.ShapeDtypeStruct((t, D_MODEL), F32),
        compiler_params=_cparams(("arbitrary",), VMEM_LIMIT),
        name="merge",
    )(x2d, pw['norm_mix'], pw['w_gate'], slab, pw['gate_expand'], ocmp, oslc, owin, oml, omem,
      pw['w_branch'], pw['w_out'])


def _ffn_kernel(x_ref, nf_ref, win_ref, wout_ref, o_ref):
    x = x_ref[...]
    xn = _rms_rows(x, nf_ref[...]).astype(BF16)
    a = _dot(xn, win_ref[:, 0:FFN_HID])
    b = _dot(xn, win_ref[:, FFN_HID:2 * FFN_HID])
    o_ref[...] = x + _dot((a * _sigmoid(a) * b).astype(BF16), wout_ref[...])


def _ffn(x2d, pw, tm):
    t = x2d.shape[0]
    consts = [pw['norm_ffn'], pw['w_ffn_in'], pw['w_ffn_out']]
    return pl.pallas_call(
        _ffn_kernel,
        grid=(t // tm,),
        in_specs=[pl.BlockSpec((tm, D_MODEL), lambda i: (i, 0))] + [_resident_spec(a) for a in consts],
        out_specs=pl.BlockSpec((tm, D_MODEL), lambda i: (i, 0)),
        out_shape=jax.ShapeDtypeStruct((t, D_MODEL), F32),
        compiler_params=_cparams(("arbitrary",), VMEM_LIMIT),
        name="ffn",
    )(x2d, *consts)


def _rope_tables(pos):
    half = ROT_DIM // 2
    f32 = np.float32
    freqs = ROPE_THETA ** (-np.arange(half, dtype=np.float64) / half)
    ang = pos.astype(np.float64)[:, None] * freqs
    cos, sin = np.cos(ang).astype(f32), np.sin(ang).astype(f32)
    n = pos.shape[0]
    ones = np.ones((n, HD - ROT_DIM), f32)
    zeros = np.zeros((n, HD - ROT_DIM), f32)
    zh = np.zeros((n, half), f32)
    ct = np.concatenate([cos, cos, ones], axis=1)
    sa = np.concatenate([-sin, zh, zeros], axis=1)
    sb = np.concatenate([zh, sin, zeros], axis=1)
    one64, zero64 = np.ones((n, HD), f32), np.zeros((n, HD), f32)
    q_tabs = [np.concatenate([t, t], axis=1) for t in (ct, sa, sb)]
    k_tabs = [np.concatenate([ct, one64], axis=1), np.concatenate([sa, zero64], axis=1),
              np.concatenate([sb, zero64], axis=1)]
    return [jnp.asarray(t) for t in q_tabs + k_tabs]


def _prepare_weights(norm_mix, w_in, q_norm, k_norm_cmp, k_norm_slc, k_norm_win, cmp_pe, cmp_w1, cmp_b1, cmp_w2,
                     conv_w, conv_b, w_ml_q, w_ml_k, w_ml_v, b_igate, b_fgate, ml_norm, norm_mem, w_mem_kv,
                     mem_q_norm, mem_k_norm, w_branch, w_out, norm_ffn, w_ffn_in, w_ffn_out):
    widths = (NSA_H * HD, 3 * NSA_H, KV_W, KV_W, KV_W, ML_W, ML_W, ML_H, ML_H, MEM_W, N_BRANCH * D_MODEL)
    offs = np.concatenate([[0], np.cumsum(widths)])
    q, g, kvc, kvs, kvw, u, o, ig, fg, mq, mg = (w_in[:, int(offs[i]):int(offs[i + 1])] for i in range(len(widths)))
    slab_pad = jnp.zeros((D_MODEL, LANES - 2 * ML_H - 3 * NSA_H), F32)
    pw = {}
    pw['w_proj'] = jnp.concatenate([q, kvc, kvs, kvw, u, o, mq, ig, fg, g, slab_pad], axis=1).astype(BF16)
    pw['w_gate'] = mg.astype(BF16)
    pw['norm_mix'] = norm_mix.reshape(1, D_MODEL)
    pw['q_gain'] = jnp.tile(q_norm, NSA_H).reshape(1, NSA_H * HD)
    ones = jnp.ones((HD,), F32)
    pw['ks_gain'] = jnp.tile(jnp.concatenate([k_norm_slc, ones]), NSA_G).reshape(1, KV_W)
    pw['kw_gain'] = jnp.tile(jnp.concatenate([k_norm_win, ones]), NSA_G).reshape(1, KV_W)
    pw['kc_gain'] = jnp.concatenate([k_norm_cmp, ones]).reshape(1, LANES)
    pw['mq_gain'] = jnp.tile(mem_q_norm, MEM_H).reshape(1, MEM_W)
    head_of = np.arange(NSA_H * HD) // HD
    pw['seg64'] = jnp.asarray(head_of[:, None] == head_of[None, :], BF16)
    w1 = cmp_w1.reshape(2, CMP_BLOCK, HD, CMP_HID)
    eye = jnp.eye(2, dtype=F32)
    for name, part in (('cmp_wlo', w1[:, :CMP_STRIDE]), ('cmp_whi', w1[:, CMP_STRIDE:])):
        pw[name] = jnp.einsum('cjdh,ce->jcdeh', part, eye).reshape(CMP_STRIDE * 2 * HD, 2 * CMP_HID).astype(BF16)
    pw['cmp_w2'] = jnp.einsum('chd,ce->ched', cmp_w2, eye).reshape(2 * CMP_HID, 2 * HD).astype(BF16)
    pw['cmp_b1'] = cmp_b1.reshape(1, 2 * CMP_HID)
    pw['pe_lo'] = cmp_pe[:CMP_STRIDE].reshape(1, CMP_STRIDE * 2 * HD)
    pw['pe_hi'] = cmp_pe[CMP_STRIDE:].reshape(1, CMP_STRIDE * 2 * HD)
    pw['conv_w'] = conv_w
    pw['conv_b'] = conv_b.reshape(1, ML_W)
    pw['w_ml_q'], pw['w_ml_k'], pw['w_ml_v'] = (w.astype(BF16) for w in (w_ml_q, w_ml_k, w_ml_v))
    pw['w_ml_q_t'], pw['w_ml_k_t'], pw['w_ml_v_t'] = (jnp.swapaxes(w, 1, 2).astype(BF16) for w in (w_ml_q, w_ml_k, w_ml_v))
    pw['ml_gain_t'] = jnp.broadcast_to(ml_norm.reshape(ML_W, 1), (ML_W, LANES))
    gate_bias = jnp.concatenate([b_igate, b_fgate])
    pw['gate_bias_lanes'] = jnp.pad(gate_bias, (0, LANES - 2 * ML_H)).reshape(1, LANES)
    pw['gate_bias_rows'] = gate_bias.reshape(2 * ML_H, 1)
    pw['ml_gain'] = ml_norm.reshape(1, ML_W)
    pw['norm_mem'] = norm_mem.reshape(1, D_MODEL)
    pw['w_mem_kv'] = w_mem_kv.astype(BF16)
    pw['mem_k_gain'] = mem_k_norm.reshape(1, MEM_HD)
    ex = np.zeros((N_BRANCH, LANES, NSA_H * HD), np.float32)
    for br in range(N_BRANCH):
        for gh in range(NSA_H):
            ex[br, _SLAB_G + br * NSA_H + gh, gh * HD:(gh + 1) * HD] = 1.0
    pw['gate_expand'] = jnp.asarray(ex, BF16)
    pw['w_branch'] = w_branch.astype(BF16)
    pw['w_out'] = w_out.astype(BF16)
    pw['norm_ffn'] = norm_ffn.reshape(1, D_MODEL)
    pw['w_ffn_in'] = w_ffn_in.astype(BF16)
    pw['w_ffn_out'] = w_ffn_out.astype(BF16)
    return pw


def _overlap(n_cmp_rows, n_slc_cols):
    ci = np.arange(n_cmp_rows)[:, None] * CMP_STRIDE
    sj = np.arange(n_slc_cols)[None, :] * SLC_BLOCK
    return ((ci < sj + SLC_BLOCK) & (ci + CMP_BLOCK > sj)).astype(np.float32)


def _pick_tile(n, pref):
    t = min(n, pref)
    while n % t:
        t //= 2
    return t


def _prompt_group(x, mem, pw):
    b, s, _ = x.shape
    t = b * s
    x2d = x.reshape(t, D_MODEL)
    tm = _pick_tile(s, TOKEN_TILE)
    tabs = _rope_tables(np.arange(s))
    (kvc, kvs, kvw, u, oraw, slab,
     qrot_t, kvc_t, kvs_t, kvw_t, qn_t, mq_t) = _project(x2d, tabs, pw, tm, seq_len=s)
    kvcmp, kvcmp_t = _compress_prompt(kvc.reshape(b, s, KV_W), pw)
    r = s // CMP_STRIDE
    ovt = jnp.asarray(_overlap(r, HD).T, BF16)
    ocmp, sel_t = _cmp_prompt(qn_t, kvcmp, kvcmp_t, ovt, b, s, _pick_tile(s, CMP_TQ))
    fq, fk = _pick_tile(s, FLASH_TQ), _pick_tile(s, FLASH_TK)
    oslc = _flash_prompt(qrot_t, kvs.reshape(b, s, KV_W), kvs_t, sel_t, b, s, fq, fk)
    owin = _flash_prompt(qrot_t, kvw.reshape(b, s, KV_W), kvw_t, None, b, s, fq, fk)
    chunk = _pick_tile(s, MLSTM_CHUNK)
    oml, c_t, n_new, m_new = _mlstm_prompt(u.reshape(b, s, ML_W), oraw.reshape(b, s, ML_W),
                                             slab.reshape(b, s, LANES), pw, chunk)
    t_mem = mem.shape[1]
    kvm, kvm_t = _mem_kv(mem, pw)
    omem = _mem_prompt(mq_t, kvm, kvm_t, b, s, _pick_tile(s, 512))
    x1 = _merge(x2d, slab, ocmp, oslc, owin, oml.reshape(t, ML_W), omem, pw, tm)
    y = _ffn(x1, pw, tm).reshape(b, s, D_MODEL)
    kv5 = lambda a_t: jnp.transpose(a_t.reshape(b, NSA_G, 2, HD, a_t.shape[-1]), (0, 4, 1, 2, 3))
    return dict(y=y, kv_cmp=kv5(kvc_t), kv_slc=kv5(kvs_t), kv_win_t=kvw_t, kv5=kv5,
                kv_mem=kvm.reshape(b, t_mem, MEM_H, 2, MEM_HD), c=jnp.swapaxes(c_t, 2, 3), n=n_new,
                m=m_new[:, 0, :ML_H],
                u=u.reshape(b, s, ML_W))


def _sample_group(x, cache_cmp, cache_slc, cache_win, cache_mem, state_c, state_n, state_m, state_conv,
                  page_table, pw):
    db, ds, _ = x.shape
    assert ds == 1, "the sample kernels handle one new token per sequence"
    n_pages = page_table.shape[1]
    past = n_pages * PAGE_SIZE
    pos = past
    x2d = x.reshape(db, D_MODEL)
    tabs = _rope_tables(np.full((db,), pos))
    kvc, kvs, kvw, u, oraw, slab, qn, qrot, mq = _project(x2d, tabs, pw, db)
    kvcmp = _compress_paged(cache_cmp, page_table, kvc.reshape(db, 1, KV_W), pw)
    n_cmp = kvcmp.shape[2]
    n_slc = -(-(past + ds) // SLC_BLOCK)
    nslp = -(-n_slc // LANES) * LANES
    ov = jnp.asarray(_overlap(n_cmp, nslp), BF16)
    ocmp, sel = _cmp_sample(qn.reshape(db, NSA_H, HD), kvcmp, ov, pos, n_slc)
    q3 = qrot.reshape(db, NSA_G, NSA_HPG, 1, HD)
    qaug = (q3 * jnp.eye(NSA_G, dtype=BF16)[None, :, None, :, None]).reshape(db, NSA_H, NSA_G, HD)
    qaug = jnp.concatenate([qaug, jnp.zeros_like(qaug)], axis=-1).reshape(db, NSA_H, KV_W)
    oslc = _slc_sample(qaug, sel, cache_slc, page_table, kvs.reshape(db, 1, KV_W))
    w_buf = cache_win.shape[1]
    owin, win_new_t = _win_sample(qaug, _feature_major(cache_win), kvw.reshape(db, 1, KV_W), pos, past - w_buf)
    win_new = jnp.transpose(win_new_t.reshape(db, NSA_G, 2, HD, w_buf), (0, 4, 1, 2, 3))
    ext = jnp.concatenate([state_conv, u.reshape(db, 1, ML_W)], axis=1)
    oml, c_new, n_new, m_new = _mlstm_sample(ext, oraw.reshape(db, 1, ML_W), slab.reshape(db, 1, LANES),
                                             state_c, state_n, state_m, pw)
    q8 = jnp.pad(mq.reshape(db, MEM_H, MEM_HD), ((0, 0), (0, 8 - MEM_H), (0, 0)))
    omem = _mem_sample(q8, cache_mem)[:, :MEM_H].reshape(db, MEM_W)
    x1 = _merge(x2d, slab, ocmp.reshape(db, NSA_H * HD), oslc.reshape(db, NSA_H * HD), owin.reshape(db, NSA_H * HD),
                oml.reshape(db, ML_W), omem, pw, db)
    y = _ffn(x1, pw, db).reshape(db, ds, D_MODEL)
    kv5 = lambda a: a.reshape(db, ds, NSA_G, 2, HD)
    return dict(y=y, kv_cmp=kv5(kvc), kv_slc=kv5(kvs), win=win_new,
                c=c_new, n=n_new, m=m_new[:, 0, :ML_H], conv=ext[:, 1:])


def kernel(x_prompt, x_sample, cache_kv_cmp, cache_kv_slc, cache_kv_win, cache_kv_mem, state_C, state_n, state_m, state_conv, page_table, mem_prompt, norm_mix, w_in, q_norm, k_norm_cmp, k_norm_slc, k_norm_win, cmp_pe, cmp_w1, cmp_b1, cmp_w2, conv_w, conv_b, w_ml_q, w_ml_k, w_ml_v, b_igate, b_fgate, ml_norm, norm_mem, w_mem_kv, mem_q_norm, mem_k_norm, w_branch, w_out, norm_ffn, w_ffn_in, w_ffn_out):
    pw = _prepare_weights(norm_mix, w_in, q_norm, k_norm_cmp, k_norm_slc, k_norm_win, cmp_pe, cmp_w1, cmp_b1, cmp_w2,
                          conv_w, conv_b, w_ml_q, w_ml_k, w_ml_v, b_igate, b_fgate, ml_norm, norm_mem, w_mem_kv,
                          mem_q_norm, mem_k_norm, w_branch, w_out, norm_ffn, w_ffn_in, w_ffn_out)
    p = _prompt_group(x_prompt, mem_prompt, pw)
    s = _sample_group(x_sample, cache_kv_cmp, cache_kv_slc, cache_kv_win, cache_kv_mem, state_C, state_n, state_m,
                      state_conv, page_table, pw)
    b, seq = x_prompt.shape[:2]
    w_buf = cache_kv_win.shape[1]
    assert seq >= w_buf
    win_p = p['kv5'](p['kv_win_t'][:, :, seq - w_buf:])
    conv_p = p['u'][:, seq - (CONV_W - 1):]
    return (p['y'], s['y'], p['kv_cmp'], s['kv_cmp'], p['kv_slc'], s['kv_slc'], win_p, s['win'], p['kv_mem'],
            p['c'], s['c'], p['n'], s['n'], p['m'], s['m'], conv_p, s['conv'])
```

```python
import functools

import numpy as np
import jax
import jax.numpy as jnp
from jax import lax
from jax.experimental import pallas as pl
from jax.experimental.pallas import tpu as pltpu

F32 = jnp.float32
BF16 = jnp.bfloat16

D_MODEL = 1024
PAGE_SIZE = 128
NSA_H, NSA_G, NSA_HPG, HD = 8, 2, 4, 64
ROT_DIM = HD // 4
ROPE_THETA = 500000.0
CMP_BLOCK, CMP_STRIDE, CMP_HID = 32, 16, 128
SLC_BLOCK, N_SEL = 64, 16
WINDOW = 512
ML_H, ML_D, ML_W, CONV_W = 4, 128, 512, 4
MEM_H, MEM_HD, MEM_W = 4, 128, 512
N_BRANCH, BRANCH_W = 3, 512
FFN_HID = -(-8 * D_MODEL // (3 * 256)) * 256
EPS = 1e-6

KV_W = 2 * NSA_G * HD
CHUNK_W = CMP_STRIDE * KV_W
LANES = 128
MASKED = -1e30
UNSELECTED = -32768.0
VMEM_LIMIT = 56 * 1024 * 1024
TOKEN_TILE = 512
CMP_TQ = 1024
MEM_TQ = 1024
FLASH_TQ, FLASH_TK = 512, 256
FLASH_COLS = 128
FLASH_ONES_ROWS = 64
MLSTM_CHUNK = 256
SAMPLE_SEQS_PER_STEP = 8

_Q0, _KVC0, _KVS0, _KVW0, _U0, _O0, _MQ0, _SLAB0, _PROJ_W = 0, 512, 768, 1024, 1280, 1792, 2304, 2816, 2944
_SLAB_I, _SLAB_F, _SLAB_G = 0, 4, 8


def _dot(a, b):
    return jnp.dot(a, b, preferred_element_type=F32)


def _dot_nt(a, b):
    return lax.dot_general(a, b, (((1,), (1,)), ((), ())), preferred_element_type=F32)


def _split2(x):
    hi = x.astype(BF16)
    return hi, (x - hi.astype(F32)).astype(BF16)


def _split3(x):
    hi = x.astype(BF16)
    r = x - hi.astype(F32)
    mid = r.astype(BF16)
    return hi, mid, (r - mid.astype(F32)).astype(BF16)


def _dot_exact_rhs(parts, m):
    out = _dot(parts[0], m)
    for p in parts[1:]:
        out = out + _dot(p, m)
    return out


def _dot_exact_rhs_left(m, parts):
    out = _dot(m, parts[0])
    for p in parts[1:]:
        out = out + _dot(m, p)
    return out


def _sigmoid(x):
    return 1.0 / (1.0 + jnp.exp(-x))


def _log_sigmoid(x):
    return jnp.minimum(x, 0.0) - jnp.log(1.0 + jnp.exp(-jnp.abs(x)))


def _gelu_tanh(x):
    return x * (0.5 * (1.0 + jnp.tanh(np.sqrt(2.0 / np.pi).astype(np.float32) * (x + 0.044715 * (x * x * x)))))


def _rms_rows(x, gain):
    ms = jnp.mean(x * x, axis=-1, keepdims=True)
    return x * lax.rsqrt(ms + EPS) * gain


def _iota(shape, dim):
    return lax.broadcasted_iota(jnp.int32, shape, dim)


def _cparams(sem, vmem=None):
    return pltpu.CompilerParams(dimension_semantics=sem, vmem_limit_bytes=vmem)


def _resident_spec(a):
    nd = a.ndim
    return pl.BlockSpec(a.shape, lambda *_: (0,) * nd, pipeline_mode=pl.Buffered(1))


def _masked_softmax(s, mask):
    s = jnp.where(mask, s, -jnp.inf)
    m = jnp.max(s, axis=-1, keepdims=True)
    m = jnp.where(m > -jnp.inf, m, 0.0)
    p = jnp.exp(s - m)
    return p / jnp.maximum(jnp.sum(p, axis=-1, keepdims=True), jnp.finfo(F32).tiny)


def _rs_heads64(a, seg_ref, width):
    ss = _dot_exact_rhs(_split2(a * a), seg_ref[0:width, 0:width])
    return lax.rsqrt(ss * (1.0 / HD) + EPS)


def _rope(a, ct, sa, sb, width):
    reps = width // LANES
    if reps > 1:
        ct, sa, sb = (jnp.concatenate([t] * reps, axis=1) for t in (ct, sa, sb))
    half = ROT_DIM // 2
    return a * ct + pltpu.roll(a, width - half, 1) * sa + pltpu.roll(a, half, 1) * sb


def _proj_kernel(x_ref, nm_ref, w_ref, qg_ref, ksg_ref, kwg_ref, mqg_ref, seg_ref,
                 ctq_ref, saq_ref, sbq_ref, ctk_ref, sak_ref, sbk_ref,
                 kvc_o, kvs_o, kvw_o, u_o, o_o, slab_o, *rest):
    fm_outs = rest if len(rest) == 6 else ()
    xn = _rms_rows(x_ref[...], nm_ref[...]).astype(BF16)

    def seg(lo, hi):
        return _dot(xn, w_ref[:, lo:hi])

    q = seg(_Q0, _KVC0)
    qn = q * _rs_heads64(q, seg_ref, NSA_H * HD) * qg_ref[...]
    scale = HD ** -0.5
    qr = _rope(qn, ctq_ref[...], saq_ref[...], sbq_ref[...], NSA_H * HD) * scale
    kvc = seg(_KVC0, _KVS0)
    kvc_o[...] = kvc
    if fm_outs:
        fm_outs[0][0] = qr.T.astype(BF16)
        fm_outs[1][0] = kvc.T
        fm_outs[4][0] = (qn * scale).T.astype(BF16)
    else:
        rest[0][...] = (qn * scale).astype(BF16)
        rest[1][...] = qr.astype(BF16)
    for n, (lo, g_ref, o_ref) in enumerate(((_KVS0, ksg_ref, kvs_o), (_KVW0, kwg_ref, kvw_o))):
        a = seg(lo, lo + KV_W)
        is_k = (_iota(a.shape, 1) & HD) == 0
        an = jnp.where(is_k, a * _rs_heads64(a, seg_ref, KV_W) * g_ref[...], a)
        ar = _rope(an, ctk_ref[...], sak_ref[...], sbk_ref[...], KV_W)
        o_ref[...] = ar
        if fm_outs:
            fm_outs[2 + n][0] = ar.T
    u_o[...] = seg(_U0, _O0)
    o_o[...] = seg(_O0, _MQ0)
    mq = seg(_MQ0, _SLAB0)
    heads = []
    for h in range(MEM_H):
        mh = mq[:, h * MEM_HD:(h + 1) * MEM_HD]
        heads.append(mh * lax.rsqrt(jnp.mean(mh * mh, axis=-1, keepdims=True) + EPS))
    mqn = jnp.concatenate(heads, axis=1) * mqg_ref[...] * (MEM_HD ** -0.5)
    if fm_outs:
        fm_outs[5][0] = mqn.T.astype(BF16)
    else:
        rest[2][...] = mqn.astype(BF16)
    slab_o[...] = seg(_SLAB0, _PROJ_W)


def _project(x2d, tabs, pw, tm, seq_len=None):
    t = x2d.shape[0]
    ntab = tabs[0].shape[0] // tm
    row = lambda i: (i, 0)
    const = lambda i: (0, 0)
    tab = lambda i: (i % ntab, 0)
    full = lambda a: _resident_spec(a)
    consts = [pw['norm_mix'], pw['w_proj'], pw['q_gain'], pw['ks_gain'], pw['kw_gain'], pw['mq_gain'],
              pw['seg64']]
    widths = [(KV_W, F32), (KV_W, F32), (KV_W, F32), (ML_W, F32), (ML_W, F32), (LANES, F32)]
    if seq_len is None:
        widths += [(NSA_H * HD, BF16), (NSA_H * HD, BF16), (MEM_W, BF16)]
    out_specs = [pl.BlockSpec((tm, w), row) for w, _ in widths]
    out_shape = [jax.ShapeDtypeStruct((t, w), d) for w, d in widths]
    if seq_len is not None:
        per_seq = seq_len // tm
        for w, d in ((NSA_H * HD, BF16), (KV_W, F32), (KV_W, F32), (KV_W, F32), (NSA_H * HD, BF16), (MEM_W, BF16)):
            out_specs.append(pl.BlockSpec((1, w, tm), lambda i: (i // per_seq, 0, i % per_seq)))
            out_shape.append(jax.ShapeDtypeStruct((t // seq_len, w, seq_len), d))
    return pl.pallas_call(
        _proj_kernel,
        grid=(t // tm,),
        in_specs=[pl.BlockSpec((tm, D_MODEL), row)] + [full(a) for a in consts]
                 + [pl.BlockSpec((tm, LANES), tab)] * 6,
        out_specs=out_specs,
        out_shape=out_shape,
        compiler_params=_cparams(("arbitrary",), VMEM_LIMIT),
        name="project",
    )(x2d, *consts, *tabs)


def _compress_group(y_lo, y_hi, tail_hi, wlo, whi, b1, w2, gain, n_valid):
    r = y_lo.shape[0]
    a = _dot(y_lo, wlo)
    bh = _dot(y_hi, whi)
    bs = pltpu.roll(bh, r - 1, 0)
    if tail_hi is not None:
        bt = _dot(tail_hi, whi)
        bs = jnp.where(_iota(bs.shape, 0) == r - 1, bt, bs)
    hid = _gelu_tanh(a + bs + b1)
    out = _dot(hid.astype(BF16), w2)
    is_k = _iota(out.shape, 1) < HD
    ss = jnp.sum(jnp.where(is_k, out * out, 0.0), axis=-1, keepdims=True) * (1.0 / HD)
    kv = jnp.where(is_k, out * lax.rsqrt(ss + EPS) * gain, out)
    return jnp.where(_iota(kv.shape, 0) < n_valid, kv, 0.0)


def _group_cols(load, g):
    return jnp.concatenate([load(j * KV_W + g * LANES, j * KV_W + (g + 1) * LANES) for j in range(CMP_STRIDE)], axis=1)


def _compress_prompt_kernel(y_ref, pe_lo, pe_hi, wlo, whi, b1, w2, gain, o_ref, ot_ref, *, n_valid):
    for g in range(NSA_G):
        yg = _group_cols(lambda lo, hi: y_ref[0, :, lo:hi], g)
        kv = _compress_group((yg + pe_lo[...]).astype(BF16), (yg + pe_hi[...]).astype(BF16), None,
                             wlo[...], whi[...], b1[...], w2[...], gain[...], n_valid)
        o_ref[0, g] = kv.astype(BF16)
        ot_ref[0, g] = kv.T.astype(BF16)


def _compress_paged_kernel(pt_ref, *refs, n_pages, n_valid):
    pages = refs[:n_pages]
    tail_ref, pe_lo, pe_hi, wlo, whi, b1, w2, gain, o_ref, y = refs[n_pages:]
    cpp = PAGE_SIZE // CMP_STRIDE
    for k, page in enumerate(pages):
        for g in range(NSA_G):
            nat = page[0, g * LANES:(g + 1) * LANES, :].T
            by_row = jnp.swapaxes(nat.reshape(cpp, CMP_STRIDE, LANES), 0, 1)
            for j in range(CMP_STRIDE):
                y[g, k * cpp:(k + 1) * cpp, j * LANES:(j + 1) * LANES] = by_row[j]
    for g in range(NSA_G):
        yg = y[g]
        tg = _group_cols(lambda lo, hi: tail_ref[0, :, lo:hi], g)
        o_ref[0, g] = _compress_group((yg + pe_lo[...]).astype(BF16), (yg + pe_hi[...]).astype(BF16),
                                      (tg + pe_hi[...]).astype(BF16), wlo[...], whi[...], b1[...], w2[...],
                                      gain[...], n_valid).astype(BF16)


def _compress_consts(pw):
    return [pw['cmp_wlo'], pw['cmp_whi'], pw['cmp_b1'], pw['cmp_w2'], pw['kc_gain']]


def _compress_prompt(kvc, pw):
    b, s, _ = kvc.shape
    r = s // CMP_STRIDE
    y = kvc.reshape(b, r, CHUNK_W)
    consts = [pw['pe_lo'], pw['pe_hi']] + _compress_consts(pw)
    return pl.pallas_call(
        functools.partial(_compress_prompt_kernel, n_valid=r - 1),
        grid=(b,),
        in_specs=[pl.BlockSpec((1, r, CHUNK_W), lambda i: (i, 0, 0))]
                 + [pl.BlockSpec(a.shape, lambda i: (0, 0)) for a in consts],
        out_specs=[pl.BlockSpec((1, NSA_G, r, LANES), lambda i: (i, 0, 0, 0)),
                   pl.BlockSpec((1, NSA_G, LANES, r), lambda i: (i, 0, 0, 0))],
        out_shape=[jax.ShapeDtypeStruct((b, NSA_G, r, LANES), BF16),
                   jax.ShapeDtypeStruct((b, NSA_G, LANES, r), BF16)],
        compiler_params=_cparams(("arbitrary",), VMEM_LIMIT),
        name="compress_prompt",
    )(y, *consts)


def _compress_paged(cache, page_table, kvc_new, pw):
    db, n_pages = page_table.shape
    chunks_per_page = PAGE_SIZE // CMP_STRIDE
    r = n_pages * chunks_per_page
    pool = _feature_major(cache)
    tail = jnp.pad(kvc_new, ((0, 0), (0, CMP_STRIDE - kvc_new.shape[1]), (0, 0))).reshape(db, 1, CHUNK_W)
    consts = [pw['pe_lo'], pw['pe_hi']] + _compress_consts(pw)
    page_specs = [pl.BlockSpec((1, KV_W, PAGE_SIZE), functools.partial(lambda i, pt, k: (pt[i, k], 0, 0), k=k))
                  for k in range(n_pages)]
    return pl.pallas_call(
        functools.partial(_compress_paged_kernel, n_pages=n_pages, n_valid=r),
        grid_spec=pltpu.PrefetchScalarGridSpec(
            num_scalar_prefetch=1,
            grid=(db,),
            in_specs=page_specs + [pl.BlockSpec((1, 1, CHUNK_W), lambda i, pt: (i, 0, 0))]
                     + [pl.BlockSpec(a.shape, lambda i, pt: (0, 0)) for a in consts],
            out_specs=pl.BlockSpec((1, NSA_G, r, LANES), lambda i, pt: (i, 0, 0, 0)),
            scratch_shapes=[pltpu.VMEM((NSA_G, r, CMP_STRIDE * LANES), F32)],
        ),
        out_shape=jax.ShapeDtypeStruct((db, NSA_G, r, LANES), BF16),
        compiler_params=_cparams(("arbitrary",), VMEM_LIMIT),
        name="compress_paged",
    )(page_table, *([pool] * n_pages), tail, *consts)


def _rank_select(score, idx, n_blocks):
    cnt = jnp.zeros(score.shape, jnp.int32)
    for i in range(n_blocks):
        other = score[:, i:i + 1]
        ge = jnp.where(other >= score, 1, 0)
        gt = jnp.where(other > score, 1, 0)
        cnt = cnt + jnp.where(idx > i, ge, gt)
    return jnp.where(score > -jnp.inf, jnp.where(cnt < N_SEL, 0.0, UNSELECTED), UNSELECTED)


def _rank_select_rows(score, n_blocks, last_pos, cnt_ref):
    sub = _iota((8, score.shape[1]), 0)
    n_groups = -(-n_blocks // 8)
    groups = [score[g * 8:(g + 1) * 8, :] for g in range(n_groups)]
    cnt_ref[...] = jnp.zeros(cnt_ref.shape, jnp.int32)
    for og in range(n_groups):
        @pl.when(og * 8 * SLC_BLOCK <= last_pos)
        def _():
            part = [None] * n_groups
            for i in range(og * 8, min(og * 8 + 8, n_blocks)):
                other = score[i:i + 1, :]
                for g, sg in enumerate(groups):
                    if g > og:
                        one = jnp.where(other >= sg, 1, 0)
                    elif g < og:
                        one = jnp.where(other > sg, 1, 0)
                    else:
                        one = jnp.where(sub > i % 8, jnp.where(other >= sg, 1, 0), jnp.where(other > sg, 1, 0))
                    part[g] = one if part[g] is None else part[g] + one
            for g in range(n_groups):
                cnt_ref[g * 8:(g + 1) * 8, :] += part[g]
    cnts = [cnt_ref[g * 8:(g + 1) * 8, :] for g in range(n_groups)]
    out = [jnp.where(sg > -jnp.inf, jnp.where(c < N_SEL, 0.0, UNSELECTED), UNSELECTED) for sg, c in zip(groups, cnts)]
    pad = score.shape[0] - 8 * n_groups
    if pad:
        out.append(jnp.full((pad, score.shape[1]), UNSELECTED, F32))
    return jnp.concatenate(out, axis=0)


def _cmp_prompt_kernel(qt_ref, kv_ref, kvt_ref, ovt_ref, o_ref, sel_ref, cnt_ref, *, tq, n_slc):
    i = pl.program_id(2)
    kv = kv_ref[0, 0]
    kvt = kvt_ref[0, 0]
    r = kv.shape[0]
    pos = i * tq + _iota((1, tq), 1)
    visible = (_iota((r, 1), 0) * CMP_STRIDE + (CMP_BLOCK - 1)) <= pos
    zeros = jnp.zeros((HD, tq), BF16)
    psum = jnp.zeros((r, tq), F32)
    outs = []
    for h in range(NSA_HPG):
        qa = jnp.concatenate([qt_ref[0, h * HD:(h + 1) * HD, :], zeros], axis=0)
        s = jnp.where(visible, _dot(kv, qa), -jnp.inf)
        m = jnp.max(s, axis=0, keepdims=True)
        e = jnp.exp(s - jnp.where(m > -jnp.inf, m, 0.0))
        inv_l = 1.0 / jnp.maximum(jnp.sum(e, axis=0, keepdims=True), jnp.finfo(F32).tiny)
        psum = psum + e * inv_l
        outs.append(_dot(kvt[HD:2 * HD, :], e.astype(BF16)) * inv_l)
    o_ref[...] = jnp.concatenate([jnp.concatenate(outs[0:2], axis=0).T, jnp.concatenate(outs[2:4], axis=0).T], axis=1)
    imp = _dot_exact_rhs_left(ovt_ref[...], _split2(psum))
    blk = _iota(imp.shape, 0)
    pos_row = i * tq + _iota(imp.shape, 1)
    forced = (blk == jnp.right_shift(pos_row, 6)) | (blk == 0)
    score = jnp.where(forced, jnp.inf, jnp.where(blk * SLC_BLOCK <= pos_row, imp, -jnp.inf))
    sel_ref[0, 0] = _rank_select_rows(score, n_slc, i * tq + tq - 1, cnt_ref).astype(BF16)


def _cmp_prompt(qn_t, kvcmp, kvcmp_t, ovt, b, s, tq):
    nq = s // tq
    r = kvcmp.shape[2]
    n_slc = s // SLC_BLOCK
    assert n_slc <= HD and ovt.shape == (HD, r) and SLC_BLOCK == 64
    return pl.pallas_call(
        functools.partial(_cmp_prompt_kernel, tq=tq, n_slc=n_slc),
        grid=(b, NSA_G, nq),
        in_specs=[pl.BlockSpec((1, NSA_HPG * HD, tq), lambda bi, g, i: (bi, g, i)),
                  pl.BlockSpec((1, 1, r, LANES), lambda bi, g, i: (bi, g, 0, 0)),
                  pl.BlockSpec((1, 1, LANES, r), lambda bi, g, i: (bi, g, 0, 0)),
                  pl.BlockSpec(ovt.shape, lambda bi, g, i: (0, 0))],
        out_specs=[pl.BlockSpec((tq, NSA_HPG * HD), lambda bi, g, i: (bi * nq + i, g)),
                   pl.BlockSpec((1, 1, HD, tq), lambda bi, g, i: (bi, g, 0, i))],
        out_shape=[jax.ShapeDtypeStruct((b * s, NSA_H * HD), F32),
                   jax.ShapeDtypeStruct((b, NSA_G, HD, s), BF16)],
        scratch_shapes=[pltpu.VMEM((HD, tq), jnp.int32)],
        compiler_params=_cparams(("arbitrary",) * 3, VMEM_LIMIT),
        name="cmp_attend_select_prompt",
    )(qn_t, kvcmp, kvcmp_t, ovt)


def _cmp_sample_kernel(q_ref, kv_ref, ov_ref, o_ref, sel_ref, *, pos, n_slc):
    r = kv_ref.shape[2]
    row = _iota((NSA_H, 1), 0)
    visible = (_iota((1, r), 1) * CMP_STRIDE + (CMP_BLOCK - 1)) <= pos
    n_seq = q_ref.shape[0]
    scores = []
    for bi in range(n_seq):
        q = q_ref[bi]
        kvs = [kv_ref[bi, g] for g in range(NSA_G)]
        s = jnp.where(row < NSA_HPG, _dot_nt(q, kvs[0][:, 0:HD]), _dot_nt(q, kvs[1][:, 0:HD]))
        p = _masked_softmax(s, visible)
        pb = p.astype(BF16)
        o = jnp.where(row < NSA_HPG, _dot(pb, kvs[0]), _dot(pb, kvs[1]))
        o_ref[bi] = o[:, HD:2 * HD]
        p0 = jnp.sum(p[0:NSA_HPG], axis=0, keepdims=True)
        p1 = jnp.sum(p[NSA_HPG:NSA_H], axis=0, keepdims=True)
        psum = jnp.where(row < NSA_HPG, p0, p1)
        imp = _dot_exact_rhs(_split2(psum), ov_ref[...])
        blk = _iota(imp.shape, 1)
        forced = (blk == pos // SLC_BLOCK) | (blk == 0)
        scores.append(jnp.where(forced, jnp.inf, jnp.where(blk * SLC_BLOCK <= pos, imp, -jnp.inf)))
    stacked = jnp.concatenate(scores, axis=0)
    sel = _rank_select(stacked, _iota(stacked.shape, 1), n_slc)
    for bi in range(n_seq):
        sel_ref[bi] = sel[bi * NSA_H:(bi + 1) * NSA_H]


def _cmp_sample(qn3, kvcmp, ov, pos, n_slc):
    db, _, r, _ = kvcmp.shape
    nslp = ov.shape[1]
    bb = _pick_tile(db, SAMPLE_SEQS_PER_STEP)
    return pl.pallas_call(
        functools.partial(_cmp_sample_kernel, pos=pos, n_slc=n_slc),
        grid=(db // bb,),
        in_specs=[pl.BlockSpec((bb, NSA_H, HD), lambda i: (i, 0, 0)),
                  pl.BlockSpec((bb, NSA_G, r, LANES), lambda i: (i, 0, 0, 0)),
                  pl.BlockSpec(ov.shape, lambda i: (0, 0))],
        out_specs=[pl.BlockSpec((bb, NSA_H, HD), lambda i: (i, 0, 0)),
                   pl.BlockSpec((bb, NSA_H, nslp), lambda i: (i, 0, 0))],
        out_shape=[jax.ShapeDtypeStruct((db, NSA_H, HD), F32),
                   jax.ShapeDtypeStruct((db, NSA_H, nslp), F32)],
        compiler_params=_cparams(("arbitrary",), VMEM_LIMIT),
        name="cmp_attend_select_sample",
    )(qn3, kvcmp, ov)


def _flash_prompt_kernel(*refs, tq, selected):
    if selected:
        qt_ref, kv_ref, kvt_ref, sel_ref, o_ref, kop, vop, qa_ref, m_ref, acc_ref = refs
    else:
        qt_ref, kv_ref, kvt_ref, o_ref, kop, vop, qa_ref, m_ref, acc_ref = refs
    i = pl.program_id(2)
    cols = NSA_HPG * tq
    n_tiles, v_rows, tk = vop.shape
    ones_rows = v_rows - HD
    wide = FLASH_COLS

    @pl.when(i == 0)
    def _():
        kvf = kv_ref[0]
        if selected:
            lane = _iota(kvf.shape, 1)
            onehot = (lane - HD) == jnp.right_shift(_iota(kvf.shape, 0), 6)
            kop[...] = jnp.where(lane < HD, kvf, onehot.astype(F32)).astype(BF16)
        else:
            kop[...] = kvf.astype(BF16)
        for t in range(n_tiles):
            vop[t, 0:ones_rows, :] = jnp.ones((ones_rows, tk), BF16)
            vop[t, ones_rows:v_rows, :] = kvt_ref[0, HD:2 * HD, t * tk:(t + 1) * tk].astype(BF16)

    extra = sel_ref[0, 0] if selected else jnp.zeros((HD, tq), BF16)
    for h in range(NSA_HPG):
        qa_ref[0:HD, h * tq:(h + 1) * tq] = qt_ref[0, h * HD:(h + 1) * HD, :]
        qa_ref[HD:2 * HD, h * tq:(h + 1) * tq] = extra
    m_ref[...] = jnp.full((1, cols), MASKED, F32)
    acc_ref[...] = jnp.zeros((v_rows, cols), F32)

    def tile_step(j, masked, blocks):
        start = pl.multiple_of(j * tk, tk)
        for cb in blocks:
            sl = slice(cb * wide, (cb + 1) * wide)
            s = _dot(kop[pl.ds(start, tk), :], qa_ref[:, sl])
            if masked:
                pos = i * tq + (cb * wide) % tq + _iota((1, wide), 1)
                kpos = j * tk + _iota((tk, 1), 0)
                valid = kpos <= pos
                if not selected:
                    valid = valid & ((pos - kpos) <= WINDOW)
                s = jnp.where(valid, s, MASKED)
            m_old = m_ref[:, sl]
            m_new = jnp.maximum(m_old, jnp.max(s, axis=0, keepdims=True))
            p = jnp.exp(s - m_new).astype(BF16)
            m_ref[:, sl] = m_new
            acc_ref[:, sl] = jnp.exp(m_old - m_new) * acc_ref[:, sl] + _dot(vop[j], p)

    per_q = tq // tk
    all_blocks = list(range(cols // wide))

    def reachable(back, u):
        first_key = u * tk - back * tq
        out = []
        for cb in all_blocks:
            q0 = (cb * wide) % tq
            causal = q0 + wide - 1 >= first_key
            in_window = selected or q0 - (first_key + tk - 1) <= WINDOW
            if causal and in_window:
                out.append(cb)
        return out

    def masked_unit(back):
        for u in range(per_q):
            tile_step((i - back) * per_q + u, True, reachable(back, u))

    if selected:
        def interior(t, carry):
            for u in range(per_q):
                tile_step(t * per_q + u, False, all_blocks)
            return carry
        lax.fori_loop(0, i, interior, 0)
    else:
        for back in range(WINDOW // tq, 0, -1):
            pl.when(i >= back)(functools.partial(masked_unit, back))
    masked_unit(0)
    normed = []
    for h in range(NSA_HPG):
        a = acc_ref[:, h * tq:(h + 1) * tq]
        normed.append(a[ones_rows:v_rows, :] * (1.0 / a[0:1, :]))
    o_ref[...] = jnp.concatenate([jnp.concatenate(normed[0:2], axis=0).T, jnp.concatenate(normed[2:4], axis=0).T], axis=1)


def _flash_prompt(qrot_t, kv, kv_t, sel_t, b, s, tq, tk):
    nq = s // tq
    assert WINDOW % tq == 0 and tq % tk == 0 and tk % LANES == 0 and s % tq == 0 and tq % FLASH_COLS == 0
    v_rows = FLASH_ONES_ROWS + HD
    selected = sel_t is not None
    in_specs = [pl.BlockSpec((1, NSA_HPG * HD, tq), lambda bi, g, i: (bi, g, i)),
                pl.BlockSpec((1, s, LANES), lambda bi, g, i: (bi, 0, g)),
                pl.BlockSpec((1, LANES, s), lambda bi, g, i: (bi, g, 0))]
    args = [qrot_t, kv, kv_t]
    cols = NSA_HPG * tq
    scratch = [pltpu.VMEM((s, LANES), BF16), pltpu.VMEM((s // tk, v_rows, tk), BF16),
               pltpu.VMEM((LANES, cols), BF16), pltpu.VMEM((1, cols), F32), pltpu.VMEM((v_rows, cols), F32)]
    if selected:
        in_specs.append(pl.BlockSpec((1, 1, HD, tq), lambda bi, g, i: (bi, g, 0, i)))
        args.append(sel_t)
    return pl.pallas_call(
        functools.partial(_flash_prompt_kernel, tq=tq, selected=selected),
        grid=(b, NSA_G, nq),
        in_specs=in_specs,
        out_specs=pl.BlockSpec((tq, NSA_HPG * HD), lambda bi, g, i: (bi * nq + i, g)),
        out_shape=jax.ShapeDtypeStruct((b * s, NSA_H * HD), F32),
        scratch_shapes=scratch,
        compiler_params=_cparams(("arbitrary",) * 3, VMEM_LIMIT),
        name="slc_attend_prompt" if selected else "win_attend_prompt",
    )(*args)


def _mem_kv_kernel(x_ref, nm_ref, w_ref, kg_ref, o_ref, ot_ref):
    xn = _rms_rows(x_ref[0], nm_ref[...]).astype(BF16)
    kv = _dot(xn, w_ref[...])
    cols = []
    for h in range(MEM_H):
        k = kv[:, 2 * h * MEM_HD:(2 * h + 1) * MEM_HD]
        cols.append(_rms_rows(k, kg_ref[...]))
        cols.append(kv[:, (2 * h + 1) * MEM_HD:(2 * h + 2) * MEM_HD])
    out = jnp.concatenate(cols, axis=1)
    o_ref[0] = out
    ot_ref[0] = out.T.astype(BF16)


def _mem_kv(mem, pw):
    b, t_mem, _ = mem.shape
    consts = [pw['norm_mem'], pw['w_mem_kv'], pw['mem_k_gain']]
    return pl.pallas_call(
        _mem_kv_kernel,
        grid=(b,),
        in_specs=[pl.BlockSpec((1, t_mem, D_MODEL), lambda i: (i, 0, 0))]
                 + [pl.BlockSpec(a.shape, lambda i: (0, 0)) for a in consts],
        out_specs=[pl.BlockSpec((1, t_mem, 2 * MEM_W), lambda i: (i, 0, 0)),
                   pl.BlockSpec((1, 2 * MEM_W, t_mem), lambda i: (i, 0, 0))],
        out_shape=[jax.ShapeDtypeStruct((b, t_mem, 2 * MEM_W), F32),
                   jax.ShapeDtypeStruct((b, 2 * MEM_W, t_mem), BF16)],
        compiler_params=_cparams(("arbitrary",), VMEM_LIMIT),
        name="mem_kv",
    )(mem, *consts)


def _mem_prompt_kernel(qt_ref, kv_ref, kvt_ref, o_ref):
    for h in range(MEM_H):
        k = kv_ref[0, :, 2 * h * MEM_HD:(2 * h + 1) * MEM_HD].astype(BF16)
        s = _dot(k, qt_ref[0, h * MEM_HD:(h + 1) * MEM_HD, :])
        p = jnp.exp(s - jnp.max(s, axis=0, keepdims=True))
        vt = kvt_ref[0, (2 * h + 1) * MEM_HD:(2 * h + 2) * MEM_HD, :]
        ot = _dot(vt, p.astype(BF16)) * (1.0 / jnp.sum(p, axis=0, keepdims=True))
        o_ref[:, h * MEM_HD:(h + 1) * MEM_HD] = ot.T


def _mem_prompt(mq_t, kvm, kvm_t, b, s, tq):
    nq = s // tq
    t_mem = kvm.shape[1]
    return pl.pallas_call(
        _mem_prompt_kernel,
        grid=(b, nq),
        in_specs=[pl.BlockSpec((1, MEM_W, tq), lambda bi, i: (bi, 0, i)),
                  pl.BlockSpec((1, t_mem, 2 * MEM_W), lambda bi, i: (bi, 0, 0)),
                  pl.BlockSpec((1, 2 * MEM_W, t_mem), lambda bi, i: (bi, 0, 0))],
        out_specs=pl.BlockSpec((tq, MEM_W), lambda bi, i: (bi * nq + i, 0)),
        out_shape=jax.ShapeDtypeStruct((b * s, MEM_W), F32),
        compiler_params=_cparams(("arbitrary",) * 2, VMEM_LIMIT),
        name="mem_attend_prompt",
    )(mq_t, kvm, kvm_t)


def _decode_attend(qa, blocks, new_row=None, new_bias=None, feature_major=False):
    qk, pv = (_dot, _dot_nt) if feature_major else (_dot_nt, _dot)
    scores = []
    for load, bias in blocks:
        s = qk(qa, load())
        scores.append(s if bias is None else s + bias)
    m = functools.reduce(jnp.maximum, [jnp.max(s, axis=-1, keepdims=True) for s in scores])
    if new_row is not None:
        nb = new_row.astype(BF16).astype(F32)
        s_new = jnp.sum(qa.astype(F32) * nb, axis=-1, keepdims=True) + new_bias
        m = jnp.maximum(m, s_new)
    ps = [jnp.exp(s - m) for s in scores]
    l = functools.reduce(lambda a, c: a + c, [jnp.sum(p, axis=-1, keepdims=True) for p in ps])
    acc = functools.reduce(lambda a, c: a + c, [pv(p.astype(BF16), load()) for p, (load, _) in zip(ps, blocks)])
    if new_row is not None:
        p_new = jnp.exp(s_new - m)
        l = l + p_new
        acc = acc + p_new.astype(BF16).astype(F32) * nb
    return acc / l


def _group_values(o):
    row = _iota((NSA_H, LANES), 0)
    both = jnp.where(row < NSA_HPG, o[:, 0:LANES], o[:, LANES:2 * LANES])
    return both[:, HD:2 * HD]


def _slc_sample_kernel(pt_ref, *refs, n_pages):
    pages = refs[:n_pages]
    q_ref, sel_ref, new_ref, o_ref = refs[n_pages:]
    sb = sel_ref[0]
    lane = _iota((NSA_H, PAGE_SIZE), 1)
    blocks = []
    for p, page in enumerate(pages):
        bias = jnp.where(lane < SLC_BLOCK, sb[:, 2 * p:2 * p + 1], sb[:, 2 * p + 1:2 * p + 2])
        blocks.append((lambda page=page: page[0].astype(BF16), bias))
    n_past = n_pages * (PAGE_SIZE // SLC_BLOCK)
    o = _decode_attend(q_ref[0], blocks, new_ref[0], sb[:, n_past:n_past + 1], feature_major=True)
    o_ref[0] = _group_values(o)


def _feature_major(cache):
    n, rows = cache.shape[:2]
    return jnp.transpose(cache, (0, 2, 3, 4, 1)).reshape(n, KV_W, rows)


def _slc_sample(qaug, sel, cache, page_table, kvs_new):
    db, n_pages = page_table.shape
    nslp = sel.shape[2]
    pool = _feature_major(cache)
    page_specs = [pl.BlockSpec((1, KV_W, PAGE_SIZE), functools.partial(lambda i, pt, k: (pt[i, k], 0, 0), k=k))
                  for k in range(n_pages)]
    return pl.pallas_call(
        functools.partial(_slc_sample_kernel, n_pages=n_pages),
        grid_spec=pltpu.PrefetchScalarGridSpec(
            num_scalar_prefetch=1,
            grid=(db,),
            in_specs=page_specs + [pl.BlockSpec((1, NSA_H, KV_W), lambda i, pt: (i, 0, 0)),
                                   pl.BlockSpec((1, NSA_H, nslp), lambda i, pt: (i, 0, 0)),
                                   pl.BlockSpec((1, 1, KV_W), lambda i, pt: (i, 0, 0))],
            out_specs=pl.BlockSpec((1, NSA_H, HD), lambda i, pt: (i, 0, 0)),
        ),
        out_shape=jax.ShapeDtypeStruct((db, NSA_H, HD), F32),
        compiler_params=_cparams(("arbitrary",), VMEM_LIMIT),
        name="slc_attend_sample",
    )(page_table, *([pool] * n_pages), qaug, sel, kvs_new)


def _win_sample_kernel(q_ref, buf_ref, new_ref, newcol_ref, o_ref, win_ref, *, pos, first_pos):
    w_buf = buf_ref.shape[2]
    d = pos - (first_pos + _iota((1, w_buf), 1))
    bias = jnp.where((d >= 0) & (d <= WINDOW), 0.0, MASKED)
    for bi in range(q_ref.shape[0]):
        buf = buf_ref[bi]
        o = _decode_attend(q_ref[bi], [(lambda buf=buf: buf.astype(BF16), bias)], new_ref[bi], 0.0, feature_major=True)
        o_ref[bi] = _group_values(o)
        shifted = pltpu.roll(buf, w_buf - 1, 1)
        win_ref[bi] = jnp.where(_iota(buf.shape, 1) == w_buf - 1, newcol_ref[bi], shifted)


def _win_sample(qaug, win_buf_t, kvw_new, pos, first_pos):
    db, _, w_buf = win_buf_t.shape
    bb = _pick_tile(db, SAMPLE_SEQS_PER_STEP)
    return pl.pallas_call(
        functools.partial(_win_sample_kernel, pos=pos, first_pos=first_pos),
        grid=(db // bb,),
        in_specs=[pl.BlockSpec((bb, NSA_H, KV_W), lambda i: (i, 0, 0)),
                  pl.BlockSpec((bb, KV_W, w_buf), lambda i: (i, 0, 0)),
                  pl.BlockSpec((bb, 1, KV_W), lambda i: (i, 0, 0)),
                  pl.BlockSpec((bb, KV_W, 1), lambda i: (i, 0, 0))],
        out_specs=[pl.BlockSpec((bb, NSA_H, HD), lambda i: (i, 0, 0)),
                   pl.BlockSpec((bb, KV_W, w_buf), lambda i: (i, 0, 0))],
        out_shape=[jax.ShapeDtypeStruct((db, NSA_H, HD), F32),
                   jax.ShapeDtypeStruct((db, KV_W, w_buf), F32)],
        compiler_params=_cparams(("arbitrary",), VMEM_LIMIT),
        name="win_attend_sample",
    )(qaug, win_buf_t, kvw_new, kvw_new.reshape(db, KV_W, 1))


def _mem_sample_kernel(q_ref, kv_ref, o_ref):
    row = _iota((8, MEM_HD), 0)
    for bi in range(q_ref.shape[0]):
        q = q_ref[bi]
        out = jnp.zeros((8, MEM_HD), F32)
        for h in range(MEM_H):
            k = kv_ref[bi, :, h, 0, :].astype(BF16)
            v = kv_ref[bi, :, h, 1, :].astype(BF16)
            s = _dot_nt(q, k)
            p = jnp.exp(s - jnp.max(s, axis=-1, keepdims=True))
            o = _dot(p.astype(BF16), v) / jnp.sum(p, axis=-1, keepdims=True)
            out = jnp.where(row == h, o, out)
        o_ref[bi] = out


def _mem_sample(q8, cache_mem):
    db, t_mem = cache_mem.shape[:2]
    bb = _pick_tile(db, SAMPLE_SEQS_PER_STEP)
    return pl.pallas_call(
        _mem_sample_kernel,
        grid=(db // bb,),
        in_specs=[pl.BlockSpec((bb, 8, MEM_HD), lambda i: (i, 0, 0)),
                  pl.BlockSpec((bb, t_mem, MEM_H, 2, MEM_HD), lambda i: (i, 0, 0, 0, 0))],
        out_specs=pl.BlockSpec((bb, 8, MEM_HD), lambda i: (i, 0, 0)),
        out_shape=jax.ShapeDtypeStruct((db, 8, MEM_HD), F32),
        compiler_params=_cparams(("arbitrary",), VMEM_LIMIT),
        name="mem_attend_sample",
    )(q8, cache_mem)


def _mlstm_prompt_kernel(u_ref, oraw_ref, slab_ref, slabt_ref, cw_ref, cb_ref, wqt_ref, wk_ref, wvt_ref,
                         bgc_ref, bgr_ref, mlgt_ref, tril_ref, triu_ref,
                         o_ref, ct_ref, n_ref, m_ref, ubuf, *, chunk):
    L = chunk
    pad = 8

    @pl.when(pl.program_id(1) == 0)
    def _():
        ct_ref[...] = jnp.zeros_like(ct_ref)
        n_ref[...] = jnp.zeros_like(n_ref)
        m_ref[...] = jnp.zeros_like(m_ref)
        ubuf[0:pad, :] = jnp.zeros((pad, ML_W), F32)

    u = u_ref[0]
    ubuf[pad:pad + L, :] = u
    conv = cb_ref[...] + cw_ref[CONV_W - 1:CONV_W, :] * u
    for w in range(CONV_W - 1):
        off = pad - (CONV_W - 1) + w
        conv = conv + cw_ref[w:w + 1, :] * ubuf[off:off + L, :]
    ubuf[0:pad, :] = u[L - pad:L, :]
    cact = conv * _sigmoid(conv)

    gl = slab_ref[0] + bgc_ref[...]
    gt = slabt_ref[0] + bgr_ref[...]
    b_c = None
    for part in _split3(_log_sigmoid(gl)):
        term = _dot(tril_ref[...], part)
        b_c = term if b_c is None else b_c + term
    b_r = _dot_exact_rhs(_split3(_log_sigmoid(gt)), triu_ref[...])
    m_prev = m_ref[0]
    not_after = _iota((L, L), 0) <= _iota((L, L), 1)
    lane1 = _iota((1, LANES), 1)
    gain_t = jnp.concatenate([mlgt_ref[...]] * (L // LANES), axis=1)
    m_out = m_prev
    outs = []
    for h in range(ML_H):
        sl = slice(h * ML_D, (h + 1) * ML_D)
        br = b_r[_SLAB_F + h:_SLAB_F + h + 1, :]
        ir = gt[_SLAB_I + h:_SLAB_I + h + 1, :]
        src = gl[:, _SLAB_I + h:_SLAB_I + h + 1] - b_c[:, _SLAB_F + h:_SLAB_F + h + 1]
        m0 = m_prev[:, h:h + 1]
        dm = jnp.where(not_after, br + src, -jnp.inf)
        m_new = jnp.maximum(br + m0, jnp.max(dm, axis=0, keepdims=True))
        ch = cact[:, sl].astype(BF16)
        kb = _dot(ch, wk_ref[h]).astype(BF16)
        qt = _dot_nt(wqt_ref[h], ch) * (ML_D ** -0.5)
        vt = _dot_nt(wvt_ref[h], u[:, sl].astype(BF16))
        qtb = qt.astype(BF16)
        wmat = jnp.exp(dm - m_new) * _dot(kb, qtb)
        inter = jnp.exp(br + m0 - m_new)
        ct_old = ct_ref[0, h]
        n_old = n_ref[0, h:h + 1, :]
        num = inter * _dot(ct_old.astype(BF16), qtb) + _dot(vt.astype(BF16), wmat.astype(BF16))
        nq = _dot_exact_rhs(_split2(jnp.broadcast_to(n_old, (8, ML_D))), qtb)[0:1]
        den = inter * nq + jnp.sum(wmat, axis=0, keepdims=True)
        hh = num / jnp.maximum(jnp.abs(den), jnp.exp(-m_new))
        m_end = m_new[:, L - 1:L]
        b_end = br[:, L - 1:L]
        decay = jnp.exp(b_end + m0 - m_end)
        wend = jnp.exp(b_end - br + ir - m_end)
        ct_ref[0, h] = decay * ct_old + _dot((vt * wend).astype(BF16), kb)
        n_ref[0, h:h + 1, :] = decay * n_old + _dot_exact_rhs(_split2(jnp.broadcast_to(wend, (8, L))), kb)[0:1]
        m_out = jnp.where(lane1 == h, m_end, m_out)
        hn = hh * lax.rsqrt(jnp.mean(hh * hh, axis=0, keepdims=True) + EPS) * gain_t[sl, :]
        outs.append(_sigmoid(oraw_ref[0, :, sl]) * hn.T)
    o_ref[0] = jnp.concatenate(outs, axis=1)
    m_ref[0] = m_out


def _mlstm_prompt(u, oraw, slab, pw, chunk):
    b, s, _ = u.shape
    nc = s // chunk
    slabt = jnp.swapaxes(slab[:, :, 0:8], 1, 2)
    tril = jnp.tril(jnp.ones((chunk, chunk), BF16))
    consts = [pw['conv_w'], pw['conv_b'], pw['w_ml_q_t'], pw['w_ml_k'], pw['w_ml_v_t'], pw['gate_bias_lanes'],
              pw['gate_bias_rows'], pw['ml_gain_t'], tril, tril.T]
    blk = lambda w: pl.BlockSpec((1, chunk, w), lambda bi, c: (bi, c, 0))

    def const_spec(a):
        nd = a.ndim
        return pl.BlockSpec(a.shape, lambda bi, c: (0,) * nd)

    return pl.pallas_call(
        functools.partial(_mlstm_prompt_kernel, chunk=chunk),
        grid=(b, nc),
        in_specs=[blk(ML_W), blk(ML_W), blk(LANES), pl.BlockSpec((1, 8, chunk), lambda bi, c: (bi, 0, c))]
                 + [const_spec(a) for a in consts],
        out_specs=[blk(ML_W),
                   pl.BlockSpec((1, ML_H, ML_D, ML_D), lambda bi, c: (bi, 0, 0, 0)),
                   pl.BlockSpec((1, ML_H, ML_D), lambda bi, c: (bi, 0, 0)),
                   pl.BlockSpec((1, 1, LANES), lambda bi, c: (bi, 0, 0))],
        out_shape=[jax.ShapeDtypeStruct((b, s, ML_W), F32),
                   jax.ShapeDtypeStruct((b, ML_H, ML_D, ML_D), F32),
                   jax.ShapeDtypeStruct((b, ML_H, ML_D), F32),
                   jax.ShapeDtypeStruct((b, 1, LANES), F32)],
        scratch_shapes=[pltpu.VMEM((chunk + 8, ML_W), F32)],
        compiler_params=_cparams(("arbitrary", "arbitrary"), VMEM_LIMIT),
        name="mlstm_prompt",
    )(u, oraw, slab, slabt, *consts)


def _mlstm_sample_kernel(ext_ref, oraw_ref, slab_ref, c_ref, n_ref, m_ref, cw_ref, cb_ref, wq_ref, wk_ref,
                         wv_ref, wkt_ref, bgc_ref, mlg_ref, o_ref, co_ref, no_ref, mo_ref):
    for bi in range(ext_ref.shape[0]):
        _mlstm_sample_step(bi, ext_ref, oraw_ref, slab_ref, c_ref, n_ref, m_ref, cw_ref, cb_ref, wq_ref, wk_ref,
                           wv_ref, wkt_ref, bgc_ref, mlg_ref, o_ref, co_ref, no_ref, mo_ref)


def _mlstm_sample_step(bi, ext_ref, oraw_ref, slab_ref, c_ref, n_ref, m_ref, cw_ref, cb_ref, wq_ref, wk_ref,
                       wv_ref, wkt_ref, bgc_ref, mlg_ref, o_ref, co_ref, no_ref, mo_ref):
    ext = ext_ref[bi]
    conv = cb_ref[...]
    for w in range(CONV_W):
        conv = conv + cw_ref[w:w + 1, :] * ext[w:w + 1, :]
    cact = conv * _sigmoid(conv)
    u = ext[CONV_W - 1:CONV_W, :]
    gl = slab_ref[bi] + bgc_ref[...]
    lf_all = _log_sigmoid(gl)
    m_prev = m_ref[bi]
    lane1 = _iota((1, LANES), 1)
    m_out = m_prev
    outs = []
    for h in range(ML_H):
        sl = slice(h * ML_D, (h + 1) * ML_D)
        ch8 = jnp.broadcast_to(cact[:, sl], (8, ML_D)).astype(BF16)
        u8 = jnp.broadcast_to(u[:, sl], (8, ML_D)).astype(BF16)
        q = (_dot(ch8, wq_ref[h]) * (ML_D ** -0.5))[0:1]
        k = _dot(ch8, wk_ref[h])[0:1]
        v = _dot(u8, wv_ref[h])[0:1]
        k_col = _dot_nt(wkt_ref[h], ch8)[:, 0:1]
        ig = gl[:, _SLAB_I + h:_SLAB_I + h + 1]
        lf = lf_all[:, _SLAB_F + h:_SLAB_F + h + 1]
        m0 = m_prev[:, h:h + 1]
        m_new = jnp.maximum(lf + m0, ig)
        qb = q.astype(BF16)
        qk = jnp.sum(qb.astype(F32) * k.astype(BF16).astype(F32), axis=-1, keepdims=True)
        w_in = jnp.exp(ig - m_new) * qk
        inter = jnp.exp(lf + m0 - m_new)
        c_old = c_ref[bi, h]
        n_old = n_ref[bi, h:h + 1, :]
        qc = _dot(jnp.broadcast_to(qb, (8, ML_D)), c_old.astype(BF16))[0:1]
        num = inter * qc + w_in * v
        den = inter * jnp.sum(q * n_old, axis=-1, keepdims=True) + w_in
        hh = num / jnp.maximum(jnp.abs(den), jnp.exp(-m_new))
        w_end = jnp.exp(ig - m_new)
        co_ref[bi, h] = inter * c_old + k_col * (w_end * v)
        no_ref[bi, h:h + 1, :] = inter * n_old + w_end * k
        m_out = jnp.where(lane1 == h, m_new, m_out)
        hn = hh * lax.rsqrt(jnp.mean(hh * hh, axis=-1, keepdims=True) + EPS) * mlg_ref[:, sl]
        outs.append(_sigmoid(oraw_ref[bi, :, sl]) * hn)
    o_ref[bi] = jnp.concatenate(outs, axis=1)
    mo_ref[bi] = m_out


def _mlstm_sample(ext, oraw, slab, state_c, state_n, state_m, pw):
    db = ext.shape[0]
    m_in = jnp.pad(state_m, ((0, 0), (0, LANES - ML_H))).reshape(db, 1, LANES)
    consts = [pw['conv_w'], pw['conv_b'], pw['w_ml_q'], pw['w_ml_k'], pw['w_ml_v'], pw['w_ml_k_t'],
              pw['gate_bias_lanes'], pw['ml_gain']]

    def const_spec(a):
        nd = a.ndim
        return pl.BlockSpec(a.shape, lambda i: (0,) * nd)

    bb = _pick_tile(db, SAMPLE_SEQS_PER_STEP)
    per_seq = lambda shape: pl.BlockSpec((bb,) + shape, lambda i: (i,) + (0,) * len(shape))
    return pl.pallas_call(
        _mlstm_sample_kernel,
        grid=(db // bb,),
        in_specs=[per_seq((CONV_W, ML_W)), per_seq((1, ML_W)), per_seq((1, LANES)), per_seq((ML_H, ML_D, ML_D)),
                  per_seq((ML_H, ML_D)), per_seq((1, LANES))] + [const_spec(a) for a in consts],
        out_specs=[per_seq((1, ML_W)), per_seq((ML_H, ML_D, ML_D)), per_seq((ML_H, ML_D)), per_seq((1, LANES))],
        out_shape=[jax.ShapeDtypeStruct((db, 1, ML_W), F32),
                   jax.ShapeDtypeStruct((db, ML_H, ML_D, ML_D), F32),
                   jax.ShapeDtypeStruct((db, ML_H, ML_D), F32),
                   jax.ShapeDtypeStruct((db, 1, LANES), F32)],
        compiler_params=_cparams(("arbitrary",), VMEM_LIMIT),
        name="mlstm_sample",
    )(ext, oraw, slab, state_c, state_n, m_in, *consts)


def _merge_kernel(x_ref, nm_ref, wg_ref, slab_ref, ex_ref, ocmp_ref, oslc_ref, owin_ref, oml_ref, omem_ref,
                  wb_ref, wout_ref, o_ref):
    x = x_ref[...]
    xn = _rms_rows(x, nm_ref[...]).astype(BF16)
    gparts = _split2(_sigmoid(slab_ref[...]))
    onsa = None
    for br, br_ref in enumerate((ocmp_ref, oslc_ref, owin_ref)):
        term = _dot_exact_rhs(gparts, ex_ref[br]) * br_ref[...]
        onsa = term if onsa is None else onsa + term
    z = None
    for n, on in enumerate((onsa, oml_ref[...], omem_ref[...])):
        gate = _sigmoid(_dot(xn, wg_ref[:, n * D_MODEL:(n + 1) * D_MODEL]))
        term = gate * _dot(on.astype(BF16), wb_ref[n])
        z = term if z is None else z + term
    o_ref[...] = x + _dot(z.astype(BF16), wout_ref[...])


def _merge(x2d, slab, ocmp, oslc, owin, oml, omem, pw, tm):
    t = x2d.shape[0]
    row = lambda w: pl.BlockSpec((tm, w), lambda i: (i, 0))
    const_spec = lambda a: _resident_spec(a)
    return pl.pallas_call(
        _merge_kernel,
        grid=(t // tm,),
        in_specs=[row(D_MODEL), const_spec(pw['norm_mix']), const_spec(pw['w_gate']), row(LANES),
                  const_spec(pw['gate_expand'])] + [row(BRANCH_W)] * 5
                 + [const_spec(pw['w_branch']), const_spec(pw['w_out'])],
        out_specs=row(D_MODEL),
        out_shape=jax.ShapeDtypeStruct((t, D_MODEL), F32),
        compiler_params=_cparams(("arbitrary",), VMEM_LIMIT),
        name="merge",
    )(x2d, pw['norm_mix'], pw['w_gate'], slab, pw['gate_expand'], ocmp, oslc, owin, oml, omem,
      pw['w_branch'], pw['w_out'])


def _ffn_kernel(x_ref, nf_ref, win_ref, wout_ref, o_ref):
    x = x_ref[...]
    xn = _rms_rows(x, nf_ref[...]).astype(BF16)
    a = _dot(xn, win_ref[:, 0:FFN_HID])
    b = _dot(xn, win_ref[:, FFN_HID:2 * FFN_HID])
    o_ref[...] = x + _dot((a * _sigmoid(a) * b).astype(BF16), wout_ref[...])


def _ffn(x2d, pw, tm):
    t = x2d.shape[0]
    consts = [pw['norm_ffn'], pw['w_ffn_in'], pw['w_ffn_out']]
    return pl.pallas_call(
        _ffn_kernel,
        grid=(t // tm,),
        in_specs=[pl.BlockSpec((tm, D_MODEL), lambda i: (i, 0))] + [_resident_spec(a) for a in consts],
        out_specs=pl.BlockSpec((tm, D_MODEL), lambda i: (i, 0)),
        out_shape=jax.ShapeDtypeStruct((t, D_MODEL), F32),
        compiler_params=_cparams(("arbitrary",), VMEM_LIMIT),
        name="ffn",
    )(x2d, *consts)


def _rope_tables(pos):
    half = ROT_DIM // 2
    f32 = np.float32
    freqs = ROPE_THETA ** (-np.arange(half, dtype=np.float64) / half)
    ang = pos.astype(np.float64)[:, None] * freqs
    cos, sin = np.cos(ang).astype(f32), np.sin(ang).astype(f32)
    n = pos.shape[0]
    ones = np.ones((n, HD - ROT_DIM), f32)
    zeros = np.zeros((n, HD - ROT_DIM), f32)
    zh = np.zeros((n, half), f32)
    ct = np.concatenate([cos, cos, ones], axis=1)
    sa = np.concatenate([-sin, zh, zeros], axis=1)
    sb = np.concatenate([zh, sin, zeros], axis=1)
    one64, zero64 = np.ones((n, HD), f32), np.zeros((n, HD), f32)
    q_tabs = [np.concatenate([t, t], axis=1) for t in (ct, sa, sb)]
    k_tabs = [np.concatenate([ct, one64], axis=1), np.concatenate([sa, zero64], axis=1),
              np.concatenate([sb, zero64], axis=1)]
    return [jnp.asarray(t) for t in q_tabs + k_tabs]


def _prepare_weights(norm_mix, w_in, q_norm, k_norm_cmp, k_norm_slc, k_norm_win, cmp_pe, cmp_w1, cmp_b1, cmp_w2,
                     conv_w, conv_b, w_ml_q, w_ml_k, w_ml_v, b_igate, b_fgate, ml_norm, norm_mem, w_mem_kv,
                     mem_q_norm, mem_k_norm, w_branch, w_out, norm_ffn, w_ffn_in, w_ffn_out):
    widths = (NSA_H * HD, 3 * NSA_H, KV_W, KV_W, KV_W, ML_W, ML_W, ML_H, ML_H, MEM_W, N_BRANCH * D_MODEL)
    offs = np.concatenate([[0], np.cumsum(widths)])
    q, g, kvc, kvs, kvw, u, o, ig, fg, mq, mg = (w_in[:, int(offs[i]):int(offs[i + 1])] for i in range(len(widths)))
    slab_pad = jnp.zeros((D_MODEL, LANES - 2 * ML_H - 3 * NSA_H), F32)
    pw = {}
    pw['w_proj'] = jnp.concatenate([q, kvc, kvs, kvw, u, o, mq, ig, fg, g, slab_pad], axis=1).astype(BF16)
    pw['w_gate'] = mg.astype(BF16)
    pw['norm_mix'] = norm_mix.reshape(1, D_MODEL)
    pw['q_gain'] = jnp.tile(q_norm, NSA_H).reshape(1, NSA_H * HD)
    ones = jnp.ones((HD,), F32)
    pw['ks_gain'] = jnp.tile(jnp.concatenate([k_norm_slc, ones]), NSA_G).reshape(1, KV_W)
    pw['kw_gain'] = jnp.tile(jnp.concatenate([k_norm_win, ones]), NSA_G).reshape(1, KV_W)
    pw['kc_gain'] = jnp.concatenate([k_norm_cmp, ones]).reshape(1, LANES)
    pw['mq_gain'] = jnp.tile(mem_q_norm, MEM_H).reshape(1, MEM_W)
    head_of = np.arange(NSA_H * HD) // HD
    pw['seg64'] = jnp.asarray(head_of[:, None] == head_of[None, :], BF16)
    w1 = cmp_w1.reshape(2, CMP_BLOCK, HD, CMP_HID)
    eye = jnp.eye(2, dtype=F32)
    for name, part in (('cmp_wlo', w1[:, :CMP_STRIDE]), ('cmp_whi', w1[:, CMP_STRIDE:])):
        pw[name] = jnp.einsum('cjdh,ce->jcdeh', part, eye).reshape(CMP_STRIDE * 2 * HD, 2 * CMP_HID).astype(BF16)
    pw['cmp_w2'] = jnp.einsum('chd,ce->ched', cmp_w2, eye).reshape(2 * CMP_HID, 2 * HD).astype(BF16)
    pw['cmp_b1'] = cmp_b1.reshape(1, 2 * CMP_HID)
    pw['pe_lo'] = cmp_pe[:CMP_STRIDE].reshape(1, CMP_STRIDE * 2 * HD)
    pw['pe_hi'] = cmp_pe[CMP_STRIDE:].reshape(1, CMP_STRIDE * 2 * HD)
    pw['conv_w'] = conv_w
    pw['conv_b'] = conv_b.reshape(1, ML_W)
    pw['w_ml_q'], pw['w_ml_k'], pw['w_ml_v'] = (w.astype(BF16) for w in (w_ml_q, w_ml_k, w_ml_v))
    pw['w_ml_q_t'], pw['w_ml_k_t'], pw['w_ml_v_t'] = (jnp.swapaxes(w, 1, 2).astype(BF16) for w in (w_ml_q, w_ml_k, w_ml_v))
    pw['ml_gain_t'] = jnp.broadcast_to(ml_norm.reshape(ML_W, 1), (ML_W, LANES))
    gate_bias = jnp.concatenate([b_igate, b_fgate])
    pw['gate_bias_lanes'] = jnp.pad(gate_bias, (0, LANES - 2 * ML_H)).reshape(1, LANES)
    pw['gate_bias_rows'] = gate_bias.reshape(2 * ML_H, 1)
    pw['ml_gain'] = ml_norm.reshape(1, ML_W)
    pw['norm_mem'] = norm_mem.reshape(1, D_MODEL)
    pw['w_mem_kv'] = w_mem_kv.astype(BF16)
    pw['mem_k_gain'] = mem_k_norm.reshape(1, MEM_HD)
    ex = np.zeros((N_BRANCH, LANES, NSA_H * HD), np.float32)
    for br in range(N_BRANCH):
        for gh in range(NSA_H):
            ex[br, _SLAB_G + br * NSA_H + gh, gh * HD:(gh + 1) * HD] = 1.0
    pw['gate_expand'] = jnp.asarray(ex, BF16)
    pw['w_branch'] = w_branch.astype(BF16)
    pw['w_out'] = w_out.astype(BF16)
    pw['norm_ffn'] = norm_ffn.reshape(1, D_MODEL)
    pw['w_ffn_in'] = w_ffn_in.astype(BF16)
    pw['w_ffn_out'] = w_ffn_out.astype(BF16)
    return pw


def _overlap(n_cmp_rows, n_slc_cols):
    ci = np.arange(n_cmp_rows)[:, None] * CMP_STRIDE
    sj = np.arange(n_slc_cols)[None, :] * SLC_BLOCK
    return ((ci < sj + SLC_BLOCK) & (ci + CMP_BLOCK > sj)).astype(np.float32)


def _pick_tile(n, pref):
    t = min(n, pref)
    while n % t:
        t //= 2
    return t


def _prompt_group(x, mem, pw):
    b, s, _ = x.shape
    t = b * s
    x2d = x.reshape(t, D_MODEL)
    tm = _pick_tile(s, TOKEN_TILE)
    tabs = _rope_tables(np.arange(s))
    (kvc, kvs, kvw, u, oraw, slab,
     qrot_t, kvc_t, kvs_t, kvw_t, qn_t, mq_t) = _project(x2d, tabs, pw, tm, seq_len=s)
    kvcmp, kvcmp_t = _compress_prompt(kvc.reshape(b, s, KV_W), pw)
    r = s // CMP_STRIDE
    ovt = jnp.asarray(_overlap(r, HD).T, BF16)
    ocmp, sel_t = _cmp_prompt(qn_t, kvcmp, kvcmp_t, ovt, b, s, _pick_tile(s, CMP_TQ))
    fq, fk = _pick_tile(s, FLASH_TQ), _pick_tile(s, FLASH_TK)
    oslc = _flash_prompt(qrot_t, kvs.reshape(b, s, KV_W), kvs_t, sel_t, b, s, fq, fk)
    owin = _flash_prompt(qrot_t, kvw.reshape(b, s, KV_W), kvw_t, None, b, s, fq, fk)
    chunk = _pick_tile(s, MLSTM_CHUNK)
    oml, c_t, n_new, m_new = _mlstm_prompt(u.reshape(b, s, ML_W), oraw.reshape(b, s, ML_W),
                                             slab.reshape(b, s, LANES), pw, chunk)
    t_mem = mem.shape[1]
    kvm, kvm_t = _mem_kv(mem, pw)
    omem = _mem_prompt(mq_t, kvm, kvm_t, b, s, _pick_tile(s, MEM_TQ))
    x1 = _merge(x2d, slab, ocmp, oslc, owin, oml.reshape(t, ML_W), omem, pw, tm)
    y = _ffn(x1, pw, tm).reshape(b, s, D_MODEL)
    kv5 = lambda a_t: jnp.transpose(a_t.reshape(b, NSA_G, 2, HD, a_t.shape[-1]), (0, 4, 1, 2, 3))
    return dict(y=y, kv_cmp=kv5(kvc_t), kv_slc=kv5(kvs_t), kv_win_t=kvw_t, kv5=kv5,
                kv_mem=kvm.reshape(b, t_mem, MEM_H, 2, MEM_HD), c=jnp.swapaxes(c_t, 2, 3), n=n_new,
                m=m_new[:, 0, :ML_H],
                u=u.reshape(b, s, ML_W))


def _sample_group(x, cache_cmp, cache_slc, cache_win, cache_mem, state_c, state_n, state_m, state_conv,
                  page_table, pw):
    db, ds, _ = x.shape
    assert ds == 1, "the sample kernels handle one new token per sequence"
    n_pages = page_table.shape[1]
    past = n_pages * PAGE_SIZE
    pos = past
    x2d = x.reshape(db, D_MODEL)
    tabs = _rope_tables(np.full((db,), pos))
    kvc, kvs, kvw, u, oraw, slab, qn, qrot, mq = _project(x2d, tabs, pw, db)
    kvcmp = _compress_paged(cache_cmp, page_table, kvc.reshape(db, 1, KV_W), pw)
    n_cmp = kvcmp.shape[2]
    n_slc = -(-(past + ds) // SLC_BLOCK)
    nslp = -(-n_slc // LANES) * LANES
    ov = jnp.asarray(_overlap(n_cmp, nslp), BF16)
    ocmp, sel = _cmp_sample(qn.reshape(db, NSA_H, HD), kvcmp, ov, pos, n_slc)
    q3 = qrot.reshape(db, NSA_G, NSA_HPG, 1, HD)
    qaug = (q3 * jnp.eye(NSA_G, dtype=BF16)[None, :, None, :, None]).reshape(db, NSA_H, NSA_G, HD)
    qaug = jnp.concatenate([qaug, jnp.zeros_like(qaug)], axis=-1).reshape(db, NSA_H, KV_W)
    oslc = _slc_sample(qaug, sel, cache_slc, page_table, kvs.reshape(db, 1, KV_W))
    w_buf = cache_win.shape[1]
    owin, win_new_t = _win_sample(qaug, _feature_major(cache_win), kvw.reshape(db, 1, KV_W), pos, past - w_buf)
    win_new = jnp.transpose(win_new_t.reshape(db, NSA_G, 2, HD, w_buf), (0, 4, 1, 2, 3))
    ext = jnp.concatenate([state_conv, u.reshape(db, 1, ML_W)], axis=1)
    oml, c_new, n_new, m_new = _mlstm_sample(ext, oraw.reshape(db, 1, ML_W), slab.reshape(db, 1, LANES),
                                             state_c, state_n, state_m, pw)
    q8 = jnp.pad(mq.reshape(db, MEM_H, MEM_HD), ((0, 0), (0, 8 - MEM_H), (0, 0)))
    omem = _mem_sample(q8, cache_mem)[:, :MEM_H].reshape(db, MEM_W)
    x1 = _merge(x2d, slab, ocmp.reshape(db, NSA_H * HD), oslc.reshape(db, NSA_H * HD), owin.reshape(db, NSA_H * HD),
                oml.reshape(db, ML_W), omem, pw, db)
    y = _ffn(x1, pw, db).reshape(db, ds, D_MODEL)
    kv5 = lambda a: a.reshape(db, ds, NSA_G, 2, HD)
    return dict(y=y, kv_cmp=kv5(kvc), kv_slc=kv5(kvs), win=win_new,
                c=c_new, n=n_new, m=m_new[:, 0, :ML_H], conv=ext[:, 1:])


def kernel(x_prompt, x_sample, cache_kv_cmp, cache_kv_slc, cache_kv_win, cache_kv_mem, state_C, state_n, state_m, state_conv, page_table, mem_prompt, norm_mix, w_in, q_norm, k_norm_cmp, k_norm_slc, k_norm_win, cmp_pe, cmp_w1, cmp_b1, cmp_w2, conv_w, conv_b, w_ml_q, w_ml_k, w_ml_v, b_igate, b_fgate, ml_norm, norm_mem, w_mem_kv, mem_q_norm, mem_k_norm, w_branch, w_out, norm_ffn, w_ffn_in, w_ffn_out):
    pw = _prepare_weights(norm_mix, w_in, q_norm, k_norm_cmp, k_norm_slc, k_norm_win, cmp_pe, cmp_w1, cmp_b1, cmp_w2,
                          conv_w, conv_b, w_ml_q, w_ml_k, w_ml_v, b_igate, b_fgate, ml_norm, norm_mem, w_mem_kv,
                          mem_q_norm, mem_k_norm, w_branch, w_out, norm_ffn, w_ffn_in, w_ffn_out)
    p = _prompt_group(x_prompt, mem_prompt, pw)
    s = _sample_group(x_sample, cache_kv_cmp, cache_kv_slc, cache_kv_win, cache_kv_mem, state_C, state_n, state_m,
                      state_conv, page_table, pw)
    b, seq = x_prompt.shape[:2]
    w_buf = cache_kv_win.shape[1]
    assert seq >= w_buf
    win_p = p['kv5'](p['kv_win_t'][:, :, seq - w_buf:])
    conv_p = p['u'][:, seq - (CONV_W - 1):]
    return (p['y'], s['y'], p['kv_cmp'], s['kv_cmp'], p['kv_slc'], s['kv_slc'], win_p, s['win'], p['kv_mem'],
            p['c'], s['c'], p['n'], s['n'], p['m'], s['m'], conv_p, s['conv'])
```

```python
import functools

import numpy as np
import jax
import jax.numpy as jnp
from jax import lax
from jax.experimental import pallas as pl
from jax.experimental.pallas import tpu as pltpu

F32 = jnp.float32
BF16 = jnp.bfloat16

D_MODEL = 1024
PAGE_SIZE = 128
NSA_H, NSA_G, NSA_HPG, HD = 8, 2, 4, 64
ROT_DIM = HD // 4
ROPE_THETA = 500000.0
CMP_BLOCK, CMP_STRIDE, CMP_HID = 32, 16, 128
SLC_BLOCK, N_SEL = 64, 16
WINDOW = 512
ML_H, ML_D, ML_W, CONV_W = 4, 128, 512, 4
MEM_H, MEM_HD, MEM_W = 4, 128, 512
N_BRANCH, BRANCH_W = 3, 512
FFN_HID = -(-8 * D_MODEL // (3 * 256)) * 256
EPS = 1e-6

KV_W = 2 * NSA_G * HD
CHUNK_W = CMP_STRIDE * KV_W
LANES = 128
MASKED = -1e30
UNSELECTED = -32768.0
VMEM_LIMIT = 56 * 1024 * 1024
TOKEN_TILE = 512
CMP_TQ = 1024
MEM_TQ = 1024
FLASH_TQ, FLASH_TK = 512, 256
FLASH_COLS = 128
FLASH_ONES_ROWS = 64
MLSTM_CHUNK = 512
SAMPLE_SEQS_PER_STEP = 8

_Q0, _KVC0, _KVS0, _KVW0, _U0, _O0, _MQ0, _SLAB0, _PROJ_W = 0, 512, 768, 1024, 1280, 1792, 2304, 2816, 2944
_SLAB_I, _SLAB_F, _SLAB_G = 0, 4, 8


def _dot(a, b):
    return jnp.dot(a, b, preferred_element_type=F32)


def _dot_nt(a, b):
    return lax.dot_general(a, b, (((1,), (1,)), ((), ())), preferred_element_type=F32)


def _split2(x):
    hi = x.astype(BF16)
    return hi, (x - hi.astype(F32)).astype(BF16)


def _split3(x):
    hi = x.astype(BF16)
    r = x - hi.astype(F32)
    mid = r.astype(BF16)
    return hi, mid, (r - mid.astype(F32)).astype(BF16)


def _dot_exact_rhs(parts, m):
    out = _dot(parts[0], m)
    for p in parts[1:]:
        out = out + _dot(p, m)
    return out


def _dot_exact_rhs_left(m, parts):
    out = _dot(m, parts[0])
    for p in parts[1:]:
        out = out + _dot(m, p)
    return out


def _sigmoid(x):
    return 1.0 / (1.0 + jnp.exp(-x))


def _log_sigmoid(x):
    return jnp.minimum(x, 0.0) - jnp.log(1.0 + jnp.exp(-jnp.abs(x)))


def _gelu_tanh(x):
    return x * (0.5 * (1.0 + jnp.tanh(np.sqrt(2.0 / np.pi).astype(np.float32) * (x + 0.044715 * (x * x * x)))))


def _rms_rows(x, gain):
    ms = jnp.mean(x * x, axis=-1, keepdims=True)
    return x * lax.rsqrt(ms + EPS) * gain


def _iota(shape, dim):
    return lax.broadcasted_iota(jnp.int32, shape, dim)


def _cparams(sem, vmem=None):
    return pltpu.CompilerParams(dimension_semantics=sem, vmem_limit_bytes=vmem)


def _resident_spec(a):
    nd = a.ndim
    return pl.BlockSpec(a.shape, lambda *_: (0,) * nd, pipeline_mode=pl.Buffered(1))


def _masked_softmax(s, mask):
    s = jnp.where(mask, s, -jnp.inf)
    m = jnp.max(s, axis=-1, keepdims=True)
    m = jnp.where(m > -jnp.inf, m, 0.0)
    p = jnp.exp(s - m)
    return p / jnp.maximum(jnp.sum(p, axis=-1, keepdims=True), jnp.finfo(F32).tiny)


def _rs_heads64(a, seg_ref, width):
    ss = _dot_exact_rhs(_split2(a * a), seg_ref[0:width, 0:width])
    return lax.rsqrt(ss * (1.0 / HD) + EPS)


def _rope(a, ct, sa, sb, width):
    reps = width // LANES
    if reps > 1:
        ct, sa, sb = (jnp.concatenate([t] * reps, axis=1) for t in (ct, sa, sb))
    half = ROT_DIM // 2
    return a * ct + pltpu.roll(a, width - half, 1) * sa + pltpu.roll(a, half, 1) * sb


def _proj_kernel(x_ref, nm_ref, w_ref, qg_ref, ksg_ref, kwg_ref, mqg_ref, seg_ref,
                 ctq_ref, saq_ref, sbq_ref, ctk_ref, sak_ref, sbk_ref,
                 kvc_o, kvs_o, kvw_o, u_o, o_o, slab_o, *rest):
    fm_outs = rest if len(rest) == 6 else ()
    xn = _rms_rows(x_ref[...], nm_ref[...]).astype(BF16)

    def seg(lo, hi):
        return _dot(xn, w_ref[:, lo:hi])

    q = seg(_Q0, _KVC0)
    qn = q * _rs_heads64(q, seg_ref, NSA_H * HD) * qg_ref[...]
    scale = HD ** -0.5
    qr = _rope(qn, ctq_ref[...], saq_ref[...], sbq_ref[...], NSA_H * HD) * scale
    kvc = seg(_KVC0, _KVS0)
    kvc_o[...] = kvc
    if fm_outs:
        fm_outs[0][0] = qr.T.astype(BF16)
        fm_outs[1][0] = kvc.T
        fm_outs[4][0] = (qn * scale).T.astype(BF16)
    else:
        rest[0][...] = (qn * scale).astype(BF16)
        rest[1][...] = qr.astype(BF16)
    for n, (lo, g_ref, o_ref) in enumerate(((_KVS0, ksg_ref, kvs_o), (_KVW0, kwg_ref, kvw_o))):
        a = seg(lo, lo + KV_W)
        is_k = (_iota(a.shape, 1) & HD) == 0
        an = jnp.where(is_k, a * _rs_heads64(a, seg_ref, KV_W) * g_ref[...], a)
        ar = _rope(an, ctk_ref[...], sak_ref[...], sbk_ref[...], KV_W)
        o_ref[...] = ar
        if fm_outs:
            fm_outs[2 + n][0] = ar.T
    u_o[...] = seg(_U0, _O0)
    o_o[...] = seg(_O0, _MQ0)
    mq = seg(_MQ0, _SLAB0)
    heads = []
    for h in range(MEM_H):
        mh = mq[:, h * MEM_HD:(h + 1) * MEM_HD]
        heads.append(mh * lax.rsqrt(jnp.mean(mh * mh, axis=-1, keepdims=True) + EPS))
    mqn = jnp.concatenate(heads, axis=1) * mqg_ref[...] * (MEM_HD ** -0.5)
    if fm_outs:
        fm_outs[5][0] = mqn.T.astype(BF16)
    else:
        rest[2][...] = mqn.astype(BF16)
    slab_o[...] = seg(_SLAB0, _PROJ_W)


def _project(x2d, tabs, pw, tm, seq_len=None):
    t = x2d.shape[0]
    ntab = tabs[0].shape[0] // tm
    row = lambda i: (i, 0)
    const = lambda i: (0, 0)
    tab = lambda i: (i % ntab, 0)
    full = lambda a: _resident_spec(a)
    consts = [pw['norm_mix'], pw['w_proj'], pw['q_gain'], pw['ks_gain'], pw['kw_gain'], pw['mq_gain'],
              pw['seg64']]
    widths = [(KV_W, F32), (KV_W, F32), (KV_W, F32), (ML_W, F32), (ML_W, F32), (LANES, F32)]
    if seq_len is None:
        widths += [(NSA_H * HD, BF16), (NSA_H * HD, BF16), (MEM_W, BF16)]
    out_specs = [pl.BlockSpec((tm, w), row) for w, _ in widths]
    out_shape = [jax.ShapeDtypeStruct((t, w), d) for w, d in widths]
    if seq_len is not None:
        per_seq = seq_len // tm
        for w, d in ((NSA_H * HD, BF16), (KV_W, F32), (KV_W, F32), (KV_W, F32), (NSA_H * HD, BF16), (MEM_W, BF16)):
            out_specs.append(pl.BlockSpec((1, w, tm), lambda i: (i // per_seq, 0, i % per_seq)))
            out_shape.append(jax.ShapeDtypeStruct((t // seq_len, w, seq_len), d))
    return pl.pallas_call(
        _proj_kernel,
        grid=(t // tm,),
        in_specs=[pl.BlockSpec((tm, D_MODEL), row)] + [full(a) for a in consts]
                 + [pl.BlockSpec((tm, LANES), tab)] * 6,
        out_specs=out_specs,
        out_shape=out_shape,
        compiler_params=_cparams(("arbitrary",), VMEM_LIMIT),
        name="project",
    )(x2d, *consts, *tabs)


def _compress_group(y_lo, y_hi, tail_hi, wlo, whi, b1, w2, gain, n_valid):
    r = y_lo.shape[0]
    a = _dot(y_lo, wlo)
    bh = _dot(y_hi, whi)
    bs = pltpu.roll(bh, r - 1, 0)
    if tail_hi is not None:
        bt = _dot(tail_hi, whi)
        bs = jnp.where(_iota(bs.shape, 0) == r - 1, bt, bs)
    hid = _gelu_tanh(a + bs + b1)
    out = _dot(hid.astype(BF16), w2)
    is_k = _iota(out.shape, 1) < HD
    ss = jnp.sum(jnp.where(is_k, out * out, 0.0), axis=-1, keepdims=True) * (1.0 / HD)
    kv = jnp.where(is_k, out * lax.rsqrt(ss + EPS) * gain, out)
    return jnp.where(_iota(kv.shape, 0) < n_valid, kv, 0.0)


def _group_cols(load, g):
    return jnp.concatenate([load(j * KV_W + g * LANES, j * KV_W + (g + 1) * LANES) for j in range(CMP_STRIDE)], axis=1)


def _compress_prompt_kernel(y_ref, pe_lo, pe_hi, wlo, whi, b1, w2, gain, o_ref, ot_ref, *, n_valid):
    for g in range(NSA_G):
        yg = _group_cols(lambda lo, hi: y_ref[0, :, lo:hi], g)
        kv = _compress_group((yg + pe_lo[...]).astype(BF16), (yg + pe_hi[...]).astype(BF16), None,
                             wlo[...], whi[...], b1[...], w2[...], gain[...], n_valid)
        o_ref[0, g] = kv.astype(BF16)
        ot_ref[0, g] = kv.T.astype(BF16)


def _compress_paged_kernel(pt_ref, *refs, n_pages, n_valid):
    pages = refs[:n_pages]
    tail_ref, pe_lo, pe_hi, wlo, whi, b1, w2, gain, o_ref, y = refs[n_pages:]
    cpp = PAGE_SIZE // CMP_STRIDE
    for k, page in enumerate(pages):
        for g in range(NSA_G):
            nat = page[0, g * LANES:(g + 1) * LANES, :].T
            by_row = jnp.swapaxes(nat.reshape(cpp, CMP_STRIDE, LANES), 0, 1)
            for j in range(CMP_STRIDE):
                y[g, k * cpp:(k + 1) * cpp, j * LANES:(j + 1) * LANES] = by_row[j]
    for g in range(NSA_G):
        yg = y[g]
        tg = _group_cols(lambda lo, hi: tail_ref[0, :, lo:hi], g)
        o_ref[0, g] = _compress_group((yg + pe_lo[...]).astype(BF16), (yg + pe_hi[...]).astype(BF16),
                                      (tg + pe_hi[...]).astype(BF16), wlo[...], whi[...], b1[...], w2[...],
                                      gain[...], n_valid).astype(BF16)


def _compress_consts(pw):
    return [pw['cmp_wlo'], pw['cmp_whi'], pw['cmp_b1'], pw['cmp_w2'], pw['kc_gain']]


def _compress_prompt(kvc, pw):
    b, s, _ = kvc.shape
    r = s // CMP_STRIDE
    y = kvc.reshape(b, r, CHUNK_W)
    consts = [pw['pe_lo'], pw['pe_hi']] + _compress_consts(pw)
    return pl.pallas_call(
        functools.partial(_compress_prompt_kernel, n_valid=r - 1),
        grid=(b,),
        in_specs=[pl.BlockSpec((1, r, CHUNK_W), lambda i: (i, 0, 0))]
                 + [pl.BlockSpec(a.shape, lambda i: (0, 0)) for a in consts],
        out_specs=[pl.BlockSpec((1, NSA_G, r, LANES), lambda i: (i, 0, 0, 0)),
                   pl.BlockSpec((1, NSA_G, LANES, r), lambda i: (i, 0, 0, 0))],
        out_shape=[jax.ShapeDtypeStruct((b, NSA_G, r, LANES), BF16),
                   jax.ShapeDtypeStruct((b, NSA_G, LANES, r), BF16)],
        compiler_params=_cparams(("arbitrary",), VMEM_LIMIT),
        name="compress_prompt",
    )(y, *consts)


def _compress_paged(cache, page_table, kvc_new, pw):
    db, n_pages = page_table.shape
    chunks_per_page = PAGE_SIZE // CMP_STRIDE
    r = n_pages * chunks_per_page
    pool = _feature_major(cache)
    tail = jnp.pad(kvc_new, ((0, 0), (0, CMP_STRIDE - kvc_new.shape[1]), (0, 0))).reshape(db, 1, CHUNK_W)
    consts = [pw['pe_lo'], pw['pe_hi']] + _compress_consts(pw)
    page_specs = [pl.BlockSpec((1, KV_W, PAGE_SIZE), functools.partial(lambda i, pt, k: (pt[i, k], 0, 0), k=k))
                  for k in range(n_pages)]
    return pl.pallas_call(
        functools.partial(_compress_paged_kernel, n_pages=n_pages, n_valid=r),
        grid_spec=pltpu.PrefetchScalarGridSpec(
            num_scalar_prefetch=1,
            grid=(db,),
            in_specs=page_specs + [pl.BlockSpec((1, 1, CHUNK_W), lambda i, pt: (i, 0, 0))]
                     + [pl.BlockSpec(a.shape, lambda i, pt: (0, 0)) for a in consts],
            out_specs=pl.BlockSpec((1, NSA_G, r, LANES), lambda i, pt: (i, 0, 0, 0)),
            scratch_shapes=[pltpu.VMEM((NSA_G, r, CMP_STRIDE * LANES), F32)],
        ),
        out_shape=jax.ShapeDtypeStruct((db, NSA_G, r, LANES), BF16),
        compiler_params=_cparams(("arbitrary",), VMEM_LIMIT),
        name="compress_paged",
    )(page_table, *([pool] * n_pages), tail, *consts)


def _rank_select(score, idx, n_blocks):
    cnt = jnp.zeros(score.shape, jnp.int32)
    for i in range(n_blocks):
        other = score[:, i:i + 1]
        ge = jnp.where(other >= score, 1, 0)
        gt = jnp.where(other > score, 1, 0)
        cnt = cnt + jnp.where(idx > i, ge, gt)
    return jnp.where(score > -jnp.inf, jnp.where(cnt < N_SEL, 0.0, UNSELECTED), UNSELECTED)


def _rank_select_rows(score, n_blocks, last_pos, cnt_ref):
    sub = _iota((8, score.shape[1]), 0)
    n_groups = -(-n_blocks // 8)
    groups = [score[g * 8:(g + 1) * 8, :] for g in range(n_groups)]
    cnt_ref[...] = jnp.zeros(cnt_ref.shape, jnp.int32)
    for og in range(n_groups):
        @pl.when(og * 8 * SLC_BLOCK <= last_pos)
        def _():
            part = [None] * n_groups
            for i in range(og * 8, min(og * 8 + 8, n_blocks)):
                other = score[i:i + 1, :]
                for g, sg in enumerate(groups):
                    if g > og:
                        one = jnp.where(other >= sg, 1, 0)
                    elif g < og:
                        one = jnp.where(other > sg, 1, 0)
                    else:
                        one = jnp.where(sub > i % 8, jnp.where(other >= sg, 1, 0), jnp.where(other > sg, 1, 0))
                    part[g] = one if part[g] is None else part[g] + one
            for g in range(n_groups):
                cnt_ref[g * 8:(g + 1) * 8, :] += part[g]
    cnts = [cnt_ref[g * 8:(g + 1) * 8, :] for g in range(n_groups)]
    out = [jnp.where(sg > -jnp.inf, jnp.where(c < N_SEL, 0.0, UNSELECTED), UNSELECTED) for sg, c in zip(groups, cnts)]
    pad = score.shape[0] - 8 * n_groups
    if pad:
        out.append(jnp.full((pad, score.shape[1]), UNSELECTED, F32))
    return jnp.concatenate(out, axis=0)


def _cmp_prompt_kernel(qt_ref, kv_ref, kvt_ref, ovt_ref, o_ref, sel_ref, cnt_ref, *, tq, n_slc):
    i = pl.program_id(2)
    kv = kv_ref[0, 0]
    kvt = kvt_ref[0, 0]
    r = kv.shape[0]
    pos = i * tq + _iota((1, tq), 1)
    visible = (_iota((r, 1), 0) * CMP_STRIDE + (CMP_BLOCK - 1)) <= pos
    zeros = jnp.zeros((HD, tq), BF16)
    psum = jnp.zeros((r, tq), F32)
    outs = []
    for h in range(NSA_HPG):
        qa = jnp.concatenate([qt_ref[0, h * HD:(h + 1) * HD, :], zeros], axis=0)
        s = jnp.where(visible, _dot(kv, qa), -jnp.inf)
        m = jnp.max(s, axis=0, keepdims=True)
        e = jnp.exp(s - jnp.where(m > -jnp.inf, m, 0.0))
        inv_l = 1.0 / jnp.maximum(jnp.sum(e, axis=0, keepdims=True), jnp.finfo(F32).tiny)
        psum = psum + e * inv_l
        outs.append(_dot(kvt[HD:2 * HD, :], e.astype(BF16)) * inv_l)
    o_ref[...] = jnp.concatenate([jnp.concatenate(outs[0:2], axis=0).T, jnp.concatenate(outs[2:4], axis=0).T], axis=1)
    imp = _dot_exact_rhs_left(ovt_ref[...], _split2(psum))
    blk = _iota(imp.shape, 0)
    pos_row = i * tq + _iota(imp.shape, 1)
    forced = (blk == jnp.right_shift(pos_row, 6)) | (blk == 0)
    score = jnp.where(forced, jnp.inf, jnp.where(blk * SLC_BLOCK <= pos_row, imp, -jnp.inf))
    sel_ref[0, 0] = _rank_select_rows(score, n_slc, i * tq + tq - 1, cnt_ref).astype(BF16)


def _cmp_prompt(qn_t, kvcmp, kvcmp_t, ovt, b, s, tq):
    nq = s // tq
    r = kvcmp.shape[2]
    n_slc = s // SLC_BLOCK
    assert n_slc <= HD and ovt.shape == (HD, r) and SLC_BLOCK == 64
    return pl.pallas_call(
        functools.partial(_cmp_prompt_kernel, tq=tq, n_slc=n_slc),
        grid=(b, NSA_G, nq),
        in_specs=[pl.BlockSpec((1, NSA_HPG * HD, tq), lambda bi, g, i: (bi, g, i)),
                  pl.BlockSpec((1, 1, r, LANES), lambda bi, g, i: (bi, g, 0, 0)),
                  pl.BlockSpec((1, 1, LANES, r), lambda bi, g, i: (bi, g, 0, 0)),
                  pl.BlockSpec(ovt.shape, lambda bi, g, i: (0, 0))],
        out_specs=[pl.BlockSpec((tq, NSA_HPG * HD), lambda bi, g, i: (bi * nq + i, g)),
                   pl.BlockSpec((1, 1, HD, tq), lambda bi, g, i: (bi, g, 0, i))],
        out_shape=[jax.ShapeDtypeStruct((b * s, NSA_H * HD), F32),
                   jax.ShapeDtypeStruct((b, NSA_G, HD, s), BF16)],
        scratch_shapes=[pltpu.VMEM((HD, tq), jnp.int32)],
        compiler_params=_cparams(("arbitrary",) * 3, VMEM_LIMIT),
        name="cmp_attend_select_prompt",
    )(qn_t, kvcmp, kvcmp_t, ovt)


def _cmp_sample_kernel(q_ref, kv_ref, ov_ref, o_ref, sel_ref, *, pos, n_slc):
    r = kv_ref.shape[2]
    row = _iota((NSA_H, 1), 0)
    visible = (_iota((1, r), 1) * CMP_STRIDE + (CMP_BLOCK - 1)) <= pos
    n_seq = q_ref.shape[0]
    scores = []
    for bi in range(n_seq):
        q = q_ref[bi]
        kvs = [kv_ref[bi, g] for g in range(NSA_G)]
        s = jnp.where(row < NSA_HPG, _dot_nt(q, kvs[0][:, 0:HD]), _dot_nt(q, kvs[1][:, 0:HD]))
        p = _masked_softmax(s, visible)
        pb = p.astype(BF16)
        o = jnp.where(row < NSA_HPG, _dot(pb, kvs[0]), _dot(pb, kvs[1]))
        o_ref[bi] = o[:, HD:2 * HD]
        p0 = jnp.sum(p[0:NSA_HPG], axis=0, keepdims=True)
        p1 = jnp.sum(p[NSA_HPG:NSA_H], axis=0, keepdims=True)
        psum = jnp.where(row < NSA_HPG, p0, p1)
        imp = _dot_exact_rhs(_split2(psum), ov_ref[...])
        blk = _iota(imp.shape, 1)
        forced = (blk == pos // SLC_BLOCK) | (blk == 0)
        scores.append(jnp.where(forced, jnp.inf, jnp.where(blk * SLC_BLOCK <= pos, imp, -jnp.inf)))
    stacked = jnp.concatenate(scores, axis=0)
    sel = _rank_select(stacked, _iota(stacked.shape, 1), n_slc)
    for bi in range(n_seq):
        sel_ref[bi] = sel[bi * NSA_H:(bi + 1) * NSA_H]


def _cmp_sample(qn3, kvcmp, ov, pos, n_slc):
    db, _, r, _ = kvcmp.shape
    nslp = ov.shape[1]
    bb = _pick_tile(db, SAMPLE_SEQS_PER_STEP)
    return pl.pallas_call(
        functools.partial(_cmp_sample_kernel, pos=pos, n_slc=n_slc),
        grid=(db // bb,),
        in_specs=[pl.BlockSpec((bb, NSA_H, HD), lambda i: (i, 0, 0)),
                  pl.BlockSpec((bb, NSA_G, r, LANES), lambda i: (i, 0, 0, 0)),
                  pl.BlockSpec(ov.shape, lambda i: (0, 0))],
        out_specs=[pl.BlockSpec((bb, NSA_H, HD), lambda i: (i, 0, 0)),
                   pl.BlockSpec((bb, NSA_H, nslp), lambda i: (i, 0, 0))],
        out_shape=[jax.ShapeDtypeStruct((db, NSA_H, HD), F32),
                   jax.ShapeDtypeStruct((db, NSA_H, nslp), F32)],
        compiler_params=_cparams(("arbitrary",), VMEM_LIMIT),
        name="cmp_attend_select_sample",
    )(qn3, kvcmp, ov)


def _flash_prompt_kernel(*refs, tq, selected):
    if selected:
        qt_ref, kv_ref, kvt_ref, sel_ref, o_ref, kop, vop, qa_ref, m_ref, acc_ref = refs
    else:
        qt_ref, kv_ref, kvt_ref, o_ref, kop, vop, qa_ref, m_ref, acc_ref = refs
    i = pl.program_id(2)
    cols = NSA_HPG * tq
    n_tiles, v_rows, tk = vop.shape
    ones_rows = v_rows - HD
    wide = FLASH_COLS

    @pl.when(i == 0)
    def _():
        kvf = kv_ref[0]
        if selected:
            lane = _iota(kvf.shape, 1)
            onehot = (lane - HD) == jnp.right_shift(_iota(kvf.shape, 0), 6)
            kop[...] = jnp.where(lane < HD, kvf, onehot.astype(F32)).astype(BF16)
        else:
            kop[...] = kvf.astype(BF16)
        for t in range(n_tiles):
            vop[t, 0:ones_rows, :] = jnp.ones((ones_rows, tk), BF16)
            vop[t, ones_rows:v_rows, :] = kvt_ref[0, HD:2 * HD, t * tk:(t + 1) * tk].astype(BF16)

    extra = sel_ref[0, 0] if selected else jnp.zeros((HD, tq), BF16)
    for h in range(NSA_HPG):
        qa_ref[0:HD, h * tq:(h + 1) * tq] = qt_ref[0, h * HD:(h + 1) * HD, :]
        qa_ref[HD:2 * HD, h * tq:(h + 1) * tq] = extra
    m_ref[...] = jnp.full((1, cols), MASKED, F32)
    acc_ref[...] = jnp.zeros((v_rows, cols), F32)

    def tile_step(j, masked, blocks):
        start = pl.multiple_of(j * tk, tk)
        for cb in blocks:
            sl = slice(cb * wide, (cb + 1) * wide)
            s = _dot(kop[pl.ds(start, tk), :], qa_ref[:, sl])
            if masked:
                pos = i * tq + (cb * wide) % tq + _iota((1, wide), 1)
                kpos = j * tk + _iota((tk, 1), 0)
                valid = kpos <= pos
                if not selected:
                    valid = valid & ((pos - kpos) <= WINDOW)
                s = jnp.where(valid, s, MASKED)
            m_old = m_ref[:, sl]
            m_new = jnp.maximum(m_old, jnp.max(s, axis=0, keepdims=True))
            p = jnp.exp(s - m_new).astype(BF16)
            m_ref[:, sl] = m_new
            acc_ref[:, sl] = jnp.exp(m_old - m_new) * acc_ref[:, sl] + _dot(vop[j], p)

    per_q = tq // tk
    all_blocks = list(range(cols // wide))

    def reachable(back, u):
        first_key = u * tk - back * tq
        out = []
        for cb in all_blocks:
            q0 = (cb * wide) % tq
            causal = q0 + wide - 1 >= first_key
            in_window = selected or q0 - (first_key + tk - 1) <= WINDOW
            if causal and in_window:
                out.append(cb)
        return out

    def masked_unit(back):
        for u in range(per_q):
            tile_step((i - back) * per_q + u, True, reachable(back, u))

    if selected:
        def interior(t, carry):
            for u in range(per_q):
                tile_step(t * per_q + u, False, all_blocks)
            return carry
        lax.fori_loop(0, i, interior, 0)
    else:
        for back in range(WINDOW // tq, 0, -1):
            pl.when(i >= back)(functools.partial(masked_unit, back))
    masked_unit(0)
    normed = []
    for h in range(NSA_HPG):
        a = acc_ref[:, h * tq:(h + 1) * tq]
        normed.append(a[ones_rows:v_rows, :] * (1.0 / a[0:1, :]))
    o_ref[...] = jnp.concatenate([jnp.concatenate(normed[0:2], axis=0).T, jnp.concatenate(normed[2:4], axis=0).T], axis=1)


def _flash_prompt(qrot_t, kv, kv_t, sel_t, b, s, tq, tk):
    nq = s // tq
    assert WINDOW % tq == 0 and tq % tk == 0 and tk % LANES == 0 and s % tq == 0 and tq % FLASH_COLS == 0
    v_rows = FLASH_ONES_ROWS + HD
    selected = sel_t is not None
    in_specs = [pl.BlockSpec((1, NSA_HPG * HD, tq), lambda bi, g, i: (bi, g, i)),
                pl.BlockSpec((1, s, LANES), lambda bi, g, i: (bi, 0, g)),
                pl.BlockSpec((1, LANES, s), lambda bi, g, i: (bi, g, 0))]
    args = [qrot_t, kv, kv_t]
    cols = NSA_HPG * tq
    scratch = [pltpu.VMEM((s, LANES), BF16), pltpu.VMEM((s // tk, v_rows, tk), BF16),
               pltpu.VMEM((LANES, cols), BF16), pltpu.VMEM((1, cols), F32), pltpu.VMEM((v_rows, cols), F32)]
    if selected:
        in_specs.append(pl.BlockSpec((1, 1, HD, tq), lambda bi, g, i: (bi, g, 0, i)))
        args.append(sel_t)
    return pl.pallas_call(
        functools.partial(_flash_prompt_kernel, tq=tq, selected=selected),
        grid=(b, NSA_G, nq),
        in_specs=in_specs,
        out_specs=pl.BlockSpec((tq, NSA_HPG * HD), lambda bi, g, i: (bi * nq + i, g)),
        out_shape=jax.ShapeDtypeStruct((b * s, NSA_H * HD), F32),
        scratch_shapes=scratch,
        compiler_params=_cparams(("arbitrary",) * 3, VMEM_LIMIT),
        name="slc_attend_prompt" if selected else "win_attend_prompt",
    )(*args)


def _mem_kv_kernel(x_ref, nm_ref, w_ref, kg_ref, o_ref, ot_ref):
    xn = _rms_rows(x_ref[0], nm_ref[...]).astype(BF16)
    kv = _dot(xn, w_ref[...])
    cols = []
    for h in range(MEM_H):
        k = kv[:, 2 * h * MEM_HD:(2 * h + 1) * MEM_HD]
        cols.append(_rms_rows(k, kg_ref[...]))
        cols.append(kv[:, (2 * h + 1) * MEM_HD:(2 * h + 2) * MEM_HD])
    out = jnp.concatenate(cols, axis=1)
    o_ref[0] = out
    ot_ref[0] = out.T.astype(BF16)


def _mem_kv(mem, pw):
    b, t_mem, _ = mem.shape
    consts = [pw['norm_mem'], pw['w_mem_kv'], pw['mem_k_gain']]
    return pl.pallas_call(
        _mem_kv_kernel,
        grid=(b,),
        in_specs=[pl.BlockSpec((1, t_mem, D_MODEL), lambda i: (i, 0, 0))]
                 + [pl.BlockSpec(a.shape, lambda i: (0, 0)) for a in consts],
        out_specs=[pl.BlockSpec((1, t_mem, 2 * MEM_W), lambda i: (i, 0, 0)),
                   pl.BlockSpec((1, 2 * MEM_W, t_mem), lambda i: (i, 0, 0))],
        out_shape=[jax.ShapeDtypeStruct((b, t_mem, 2 * MEM_W), F32),
                   jax.ShapeDtypeStruct((b, 2 * MEM_W, t_mem), BF16)],
        compiler_params=_cparams(("arbitrary",), VMEM_LIMIT),
        name="mem_kv",
    )(mem, *consts)


def _mem_prompt_kernel(qt_ref, kv_ref, kvt_ref, o_ref):
    for h in range(MEM_H):
        k = kv_ref[0, :, 2 * h * MEM_HD:(2 * h + 1) * MEM_HD].astype(BF16)
        s = _dot(k, qt_ref[0, h * MEM_HD:(h + 1) * MEM_HD, :])
        p = jnp.exp(s - jnp.max(s, axis=0, keepdims=True))
        vt = kvt_ref[0, (2 * h + 1) * MEM_HD:(2 * h + 2) * MEM_HD, :]
        ot = _dot(vt, p.astype(BF16)) * (1.0 / jnp.sum(p, axis=0, keepdims=True))
        o_ref[:, h * MEM_HD:(h + 1) * MEM_HD] = ot.T


def _mem_prompt(mq_t, kvm, kvm_t, b, s, tq):
    nq = s // tq
    t_mem = kvm.shape[1]
    return pl.pallas_call(
        _mem_prompt_kernel,
        grid=(b, nq),
        in_specs=[pl.BlockSpec((1, MEM_W, tq), lambda bi, i: (bi, 0, i)),
                  pl.BlockSpec((1, t_mem, 2 * MEM_W), lambda bi, i: (bi, 0, 0)),
                  pl.BlockSpec((1, 2 * MEM_W, t_mem), lambda bi, i: (bi, 0, 0))],
        out_specs=pl.BlockSpec((tq, MEM_W), lambda bi, i: (bi * nq + i, 0)),
        out_shape=jax.ShapeDtypeStruct((b * s, MEM_W), F32),
        compiler_params=_cparams(("arbitrary",) * 2, VMEM_LIMIT),
        name="mem_attend_prompt",
    )(mq_t, kvm, kvm_t)


def _decode_attend(qa, blocks, new_row=None, new_bias=None, feature_major=False):
    qk, pv = (_dot, _dot_nt) if feature_major else (_dot_nt, _dot)
    scores = []
    for load, bias in blocks:
        s = qk(qa, load())
        scores.append(s if bias is None else s + bias)
    m = functools.reduce(jnp.maximum, [jnp.max(s, axis=-1, keepdims=True) for s in scores])
    if new_row is not None:
        nb = new_row.astype(BF16).astype(F32)
        s_new = jnp.sum(qa.astype(F32) * nb, axis=-1, keepdims=True) + new_bias
        m = jnp.maximum(m, s_new)
    ps = [jnp.exp(s - m) for s in scores]
    l = functools.reduce(lambda a, c: a + c, [jnp.sum(p, axis=-1, keepdims=True) for p in ps])
    acc = functools.reduce(lambda a, c: a + c, [pv(p.astype(BF16), load()) for p, (load, _) in zip(ps, blocks)])
    if new_row is not None:
        p_new = jnp.exp(s_new - m)
        l = l + p_new
        acc = acc + p_new.astype(BF16).astype(F32) * nb
    return acc / l


def _group_values(o):
    row = _iota((NSA_H, LANES), 0)
    both = jnp.where(row < NSA_HPG, o[:, 0:LANES], o[:, LANES:2 * LANES])
    return both[:, HD:2 * HD]


def _slc_sample_kernel(pt_ref, *refs, n_pages):
    pages = refs[:n_pages]
    q_ref, sel_ref, new_ref, o_ref = refs[n_pages:]
    sb = sel_ref[0]
    lane = _iota((NSA_H, PAGE_SIZE), 1)
    blocks = []
    for p, page in enumerate(pages):
        bias = jnp.where(lane < SLC_BLOCK, sb[:, 2 * p:2 * p + 1], sb[:, 2 * p + 1:2 * p + 2])
        blocks.append((lambda page=page: page[0].astype(BF16), bias))
    n_past = n_pages * (PAGE_SIZE // SLC_BLOCK)
    o = _decode_attend(q_ref[0], blocks, new_ref[0], sb[:, n_past:n_past + 1], feature_major=True)
    o_ref[0] = _group_values(o)


def _feature_major(cache):
    n, rows = cache.shape[:2]
    return jnp.transpose(cache, (0, 2, 3, 4, 1)).reshape(n, KV_W, rows)


def _slc_sample(qaug, sel, cache, page_table, kvs_new):
    db, n_pages = page_table.shape
    nslp = sel.shape[2]
    pool = _feature_major(cache)
    page_specs = [pl.BlockSpec((1, KV_W, PAGE_SIZE), functools.partial(lambda i, pt, k: (pt[i, k], 0, 0), k=k))
                  for k in range(n_pages)]
    return pl.pallas_call(
        functools.partial(_slc_sample_kernel, n_pages=n_pages),
        grid_spec=pltpu.PrefetchScalarGridSpec(
            num_scalar_prefetch=1,
            grid=(db,),
            in_specs=page_specs + [pl.BlockSpec((1, NSA_H, KV_W), lambda i, pt: (i, 0, 0)),
                                   pl.BlockSpec((1, NSA_H, nslp), lambda i, pt: (i, 0, 0)),
                                   pl.BlockSpec((1, 1, KV_W), lambda i, pt: (i, 0, 0))],
            out_specs=pl.BlockSpec((1, NSA_H, HD), lambda i, pt: (i, 0, 0)),
        ),
        out_shape=jax.ShapeDtypeStruct((db, NSA_H, HD), F32),
        compiler_params=_cparams(("arbitrary",), VMEM_LIMIT),
        name="slc_attend_sample",
    )(page_table, *([pool] * n_pages), qaug, sel, kvs_new)


def _win_sample_kernel(q_ref, buf_ref, new_ref, newcol_ref, o_ref, win_ref, *, pos, first_pos):
    w_buf = buf_ref.shape[2]
    d = pos - (first_pos + _iota((1, w_buf), 1))
    bias = jnp.where((d >= 0) & (d <= WINDOW), 0.0, MASKED)
    for bi in range(q_ref.shape[0]):
        buf = buf_ref[bi]
        o = _decode_attend(q_ref[bi], [(lambda buf=buf: buf.astype(BF16), bias)], new_ref[bi], 0.0, feature_major=True)
        o_ref[bi] = _group_values(o)
        shifted = pltpu.roll(buf, w_buf - 1, 1)
        win_ref[bi] = jnp.where(_iota(buf.shape, 1) == w_buf - 1, newcol_ref[bi], shifted)


def _win_sample(qaug, win_buf_t, kvw_new, pos, first_pos):
    db, _, w_buf = win_buf_t.shape
    bb = _pick_tile(db, SAMPLE_SEQS_PER_STEP)
    return pl.pallas_call(
        functools.partial(_win_sample_kernel, pos=pos, first_pos=first_pos),
        grid=(db // bb,),
        in_specs=[pl.BlockSpec((bb, NSA_H, KV_W), lambda i: (i, 0, 0)),
                  pl.BlockSpec((bb, KV_W, w_buf), lambda i: (i, 0, 0)),
                  pl.BlockSpec((bb, 1, KV_W), lambda i: (i, 0, 0)),
                  pl.BlockSpec((bb, KV_W, 1), lambda i: (i, 0, 0))],
        out_specs=[pl.BlockSpec((bb, NSA_H, HD), lambda i: (i, 0, 0)),
                   pl.BlockSpec((bb, KV_W, w_buf), lambda i: (i, 0, 0))],
        out_shape=[jax.ShapeDtypeStruct((db, NSA_H, HD), F32),
                   jax.ShapeDtypeStruct((db, KV_W, w_buf), F32)],
        compiler_params=_cparams(("arbitrary",), VMEM_LIMIT),
        name="win_attend_sample",
    )(qaug, win_buf_t, kvw_new, kvw_new.reshape(db, KV_W, 1))


def _mem_sample_kernel(q_ref, kv_ref, o_ref):
    row = _iota((8, MEM_HD), 0)
    for bi in range(q_ref.shape[0]):
        q = q_ref[bi]
        out = jnp.zeros((8, MEM_HD), F32)
        for h in range(MEM_H):
            k = kv_ref[bi, :, h, 0, :].astype(BF16)
            v = kv_ref[bi, :, h, 1, :].astype(BF16)
            s = _dot_nt(q, k)
            p = jnp.exp(s - jnp.max(s, axis=-1, keepdims=True))
            o = _dot(p.astype(BF16), v) / jnp.sum(p, axis=-1, keepdims=True)
            out = jnp.where(row == h, o, out)
        o_ref[bi] = out


def _mem_sample(q8, cache_mem):
    db, t_mem = cache_mem.shape[:2]
    bb = _pick_tile(db, SAMPLE_SEQS_PER_STEP)
    return pl.pallas_call(
        _mem_sample_kernel,
        grid=(db // bb,),
        in_specs=[pl.BlockSpec((bb, 8, MEM_HD), lambda i: (i, 0, 0)),
                  pl.BlockSpec((bb, t_mem, MEM_H, 2, MEM_HD), lambda i: (i, 0, 0, 0, 0))],
        out_specs=pl.BlockSpec((bb, 8, MEM_HD), lambda i: (i, 0, 0)),
        out_shape=jax.ShapeDtypeStruct((db, 8, MEM_HD), F32),
        compiler_params=_cparams(("arbitrary",), VMEM_LIMIT),
        name="mem_attend_sample",
    )(q8, cache_mem)


def _mlstm_prompt_kernel(u_ref, oraw_ref, slab_ref, slabt_ref, cw_ref, cb_ref, wqt_ref, wk_ref, wvt_ref,
                         bgc_ref, bgr_ref, mlgt_ref, tril_ref, triu_ref,
                         o_ref, ct_ref, n_ref, m_ref, ubuf, *, chunk):
    L = chunk
    pad = 8

    @pl.when(pl.program_id(1) == 0)
    def _():
        ct_ref[...] = jnp.zeros_like(ct_ref)
        n_ref[...] = jnp.zeros_like(n_ref)
        m_ref[...] = jnp.zeros_like(m_ref)
        ubuf[0:pad, :] = jnp.zeros((pad, ML_W), F32)

    u = u_ref[0]
    ubuf[pad:pad + L, :] = u
    conv = cb_ref[...] + cw_ref[CONV_W - 1:CONV_W, :] * u
    for w in range(CONV_W - 1):
        off = pad - (CONV_W - 1) + w
        conv = conv + cw_ref[w:w + 1, :] * ubuf[off:off + L, :]
    ubuf[0:pad, :] = u[L - pad:L, :]
    cact = conv * _sigmoid(conv)

    gl = slab_ref[0] + bgc_ref[...]
    gt = slabt_ref[0] + bgr_ref[...]
    b_c = None
    for part in _split3(_log_sigmoid(gl)):
        term = _dot(tril_ref[...], part)
        b_c = term if b_c is None else b_c + term
    b_r = _dot_exact_rhs(_split3(_log_sigmoid(gt)), triu_ref[...])
    m_prev = m_ref[0]
    not_after = _iota((L, L), 0) <= _iota((L, L), 1)
    lane1 = _iota((1, LANES), 1)
    gain_t = jnp.concatenate([mlgt_ref[...]] * (L // LANES), axis=1)
    m_out = m_prev
    outs = []
    for h in range(ML_H):
        sl = slice(h * ML_D, (h + 1) * ML_D)
        br = b_r[_SLAB_F + h:_SLAB_F + h + 1, :]
        ir = gt[_SLAB_I + h:_SLAB_I + h + 1, :]
        src = gl[:, _SLAB_I + h:_SLAB_I + h + 1] - b_c[:, _SLAB_F + h:_SLAB_F + h + 1]
        m0 = m_prev[:, h:h + 1]
        dm = jnp.where(not_after, br + src, -jnp.inf)
        m_new = jnp.maximum(br + m0, jnp.max(dm, axis=0, keepdims=True))
        ch = cact[:, sl].astype(BF16)
        kb = _dot(ch, wk_ref[h]).astype(BF16)
        qt = _dot_nt(wqt_ref[h], ch) * (ML_D ** -0.5)
        vt = _dot_nt(wvt_ref[h], u[:, sl].astype(BF16))
        qtb = qt.astype(BF16)
        wmat = jnp.exp(dm - m_new) * _dot(kb, qtb)
        inter = jnp.exp(br + m0 - m_new)
        ct_old = ct_ref[0, h]
        n_old = n_ref[0, h:h + 1, :]
        num = inter * _dot(ct_old.astype(BF16), qtb) + _dot(vt.astype(BF16), wmat.astype(BF16))
        nq = _dot_exact_rhs(_split2(jnp.broadcast_to(n_old, (8, ML_D))), qtb)[0:1]
        den = inter * nq + jnp.sum(wmat, axis=0, keepdims=True)
        hh = num / jnp.maximum(jnp.abs(den), jnp.exp(-m_new))
        m_end = m_new[:, L - 1:L]
        b_end = br[:, L - 1:L]
        decay = jnp.exp(b_end + m0 - m_end)
        wend = jnp.exp(b_end - br + ir - m_end)
        ct_ref[0, h] = decay * ct_old + _dot((vt * wend).astype(BF16), kb)
        n_ref[0, h:h + 1, :] = decay * n_old + _dot_exact_rhs(_split2(jnp.broadcast_to(wend, (8, L))), kb)[0:1]
        m_out = jnp.where(lane1 == h, m_end, m_out)
        hn = hh * lax.rsqrt(jnp.mean(hh * hh, axis=0, keepdims=True) + EPS) * gain_t[sl, :]
        outs.append(_sigmoid(oraw_ref[0, :, sl]) * hn.T)
    o_ref[0] = jnp.concatenate(outs, axis=1)
    m_ref[0] = m_out


def _mlstm_prompt(u, oraw, slab, pw, chunk):
    b, s, _ = u.shape
    nc = s // chunk
    slabt = jnp.swapaxes(slab[:, :, 0:8], 1, 2)
    tril = jnp.tril(jnp.ones((chunk, chunk), BF16))
    consts = [pw['conv_w'], pw['conv_b'], pw['w_ml_q_t'], pw['w_ml_k'], pw['w_ml_v_t'], pw['gate_bias_lanes'],
              pw['gate_bias_rows'], pw['ml_gain_t'], tril, tril.T]
    blk = lambda w: pl.BlockSpec((1, chunk, w), lambda bi, c: (bi, c, 0))

    def const_spec(a):
        nd = a.ndim
        return pl.BlockSpec(a.shape, lambda bi, c: (0,) * nd)

    return pl.pallas_call(
        functools.partial(_mlstm_prompt_kernel, chunk=chunk),
        grid=(b, nc),
        in_specs=[blk(ML_W), blk(ML_W), blk(LANES), pl.BlockSpec((1, 8, chunk), lambda bi, c: (bi, 0, c))]
                 + [const_spec(a) for a in consts],
        out_specs=[blk(ML_W),
                   pl.BlockSpec((1, ML_H, ML_D, ML_D), lambda bi, c: (bi, 0, 0, 0)),
                   pl.BlockSpec((1, ML_H, ML_D), lambda bi, c: (bi, 0, 0)),
                   pl.BlockSpec((1, 1, LANES), lambda bi, c: (bi, 0, 0))],
        out_shape=[jax.ShapeDtypeStruct((b, s, ML_W), F32),
                   jax.ShapeDtypeStruct((b, ML_H, ML_D, ML_D), F32),
                   jax.ShapeDtypeStruct((b, ML_H, ML_D), F32),
                   jax.ShapeDtypeStruct((b, 1, LANES), F32)],
        scratch_shapes=[pltpu.VMEM((chunk + 8, ML_W), F32)],
        compiler_params=_cparams(("arbitrary", "arbitrary"), VMEM_LIMIT),
        name="mlstm_prompt",
    )(u, oraw, slab, slabt, *consts)


def _mlstm_sample_kernel(ext_ref, oraw_ref, slab_ref, c_ref, n_ref, m_ref, cw_ref, cb_ref, wq_ref, wk_ref,
                         wv_ref, wkt_ref, bgc_ref, mlg_ref, o_ref, co_ref, no_ref, mo_ref):
    for bi in range(ext_ref.shape[0]):
        _mlstm_sample_step(bi, ext_ref, oraw_ref, slab_ref, c_ref, n_ref, m_ref, cw_ref, cb_ref, wq_ref, wk_ref,
                           wv_ref, wkt_ref, bgc_ref, mlg_ref, o_ref, co_ref, no_ref, mo_ref)


def _mlstm_sample_step(bi, ext_ref, oraw_ref, slab_ref, c_ref, n_ref, m_ref, cw_ref, cb_ref, wq_ref, wk_ref,
                       wv_ref, wkt_ref, bgc_ref, mlg_ref, o_ref, co_ref, no_ref, mo_ref):
    ext = ext_ref[bi]
    conv = cb_ref[...]
    for w in range(CONV_W):
        conv = conv + cw_ref[w:w + 1, :] * ext[w:w + 1, :]
    cact = conv * _sigmoid(conv)
    u = ext[CONV_W - 1:CONV_W, :]
    gl = slab_ref[bi] + bgc_ref[...]
    lf_all = _log_sigmoid(gl)
    m_prev = m_ref[bi]
    lane1 = _iota((1, LANES), 1)
    m_out = m_prev
    outs = []
    for h in range(ML_H):
        sl = slice(h * ML_D, (h + 1) * ML_D)
        ch8 = jnp.broadcast_to(cact[:, sl], (8, ML_D)).astype(BF16)
        u8 = jnp.broadcast_to(u[:, sl], (8, ML_D)).astype(BF16)
        q = (_dot(ch8, wq_ref[h]) * (ML_D ** -0.5))[0:1]
        k = _dot(ch8, wk_ref[h])[0:1]
        v = _dot(u8, wv_ref[h])[0:1]
        k_col = _dot_nt(wkt_ref[h], ch8)[:, 0:1]
        ig = gl[:, _SLAB_I + h:_SLAB_I + h + 1]
        lf = lf_all[:, _SLAB_F + h:_SLAB_F + h + 1]
        m0 = m_prev[:, h:h + 1]
        m_new = jnp.maximum(lf + m0, ig)
        qb = q.astype(BF16)
        qk = jnp.sum(qb.astype(F32) * k.astype(BF16).astype(F32), axis=-1, keepdims=True)
        w_in = jnp.exp(ig - m_new) * qk
        inter = jnp.exp(lf + m0 - m_new)
        c_old = c_ref[bi, h]
        n_old = n_ref[bi, h:h + 1, :]
        qc = _dot(jnp.broadcast_to(qb, (8, ML_D)), c_old.astype(BF16))[0:1]
        num = inter * qc + w_in * v
        den = inter * jnp.sum(q * n_old, axis=-1, keepdims=True) + w_in
        hh = num / jnp.maximum(jnp.abs(den), jnp.exp(-m_new))
        w_end = jnp.exp(ig - m_new)
        co_ref[bi, h] = inter * c_old + k_col * (w_end * v)
        no_ref[bi, h:h + 1, :] = inter * n_old + w_end * k
        m_out = jnp.where(lane1 == h, m_new, m_out)
        hn = hh * lax.rsqrt(jnp.mean(hh * hh, axis=-1, keepdims=True) + EPS) * mlg_ref[:, sl]
        outs.append(_sigmoid(oraw_ref[bi, :, sl]) * hn)
    o_ref[bi] = jnp.concatenate(outs, axis=1)
    mo_ref[bi] = m_out


def _mlstm_sample(ext, oraw, slab, state_c, state_n, state_m, pw):
    db = ext.shape[0]
    m_in = jnp.pad(state_m, ((0, 0), (0, LANES - ML_H))).reshape(db, 1, LANES)
    consts = [pw['conv_w'], pw['conv_b'], pw['w_ml_q'], pw['w_ml_k'], pw['w_ml_v'], pw['w_ml_k_t'],
              pw['gate_bias_lanes'], pw['ml_gain']]

    def const_spec(a):
        nd = a.ndim
        return pl.BlockSpec(a.shape, lambda i: (0,) * nd)

    bb = _pick_tile(db, SAMPLE_SEQS_PER_STEP)
    per_seq = lambda shape: pl.BlockSpec((bb,) + shape, lambda i: (i,) + (0,) * len(shape))
    return pl.pallas_call(
        _mlstm_sample_kernel,
        grid=(db // bb,),
        in_specs=[per_seq((CONV_W, ML_W)), per_seq((1, ML_W)), per_seq((1, LANES)), per_seq((ML_H, ML_D, ML_D)),
                  per_seq((ML_H, ML_D)), per_seq((1, LANES))] + [const_spec(a) for a in consts],
        out_specs=[per_seq((1, ML_W)), per_seq((ML_H, ML_D, ML_D)), per_seq((ML_H, ML_D)), per_seq((1, LANES))],
        out_shape=[jax.ShapeDtypeStruct((db, 1, ML_W), F32),
                   jax.ShapeDtypeStruct((db, ML_H, ML_D, ML_D), F32),
                   jax.ShapeDtypeStruct((db, ML_H, ML_D), F32),
                   jax.ShapeDtypeStruct((db, 1, LANES), F32)],
        compiler_params=_cparams(("arbitrary",), VMEM_LIMIT),
        name="mlstm_sample",
    )(ext, oraw, slab, state_c, state_n, m_in, *consts)


def _merge_kernel(x_ref, nm_ref, wg_ref, slab_ref, ex_ref, ocmp_ref, oslc_ref, owin_ref, oml_ref, omem_ref,
                  wb_ref, wout_ref, o_ref):
    x = x_ref[...]
    xn = _rms_rows(x, nm_ref[...]).astype(BF16)
    gparts = _split2(_sigmoid(slab_ref[...]))
    onsa = None
    for br, br_ref in enumerate((ocmp_ref, oslc_ref, owin_ref)):
        term = _dot_exact_rhs(gparts, ex_ref[br]) * br_ref[...]
        onsa = term if onsa is None else onsa + term
    z = None
    for n, on in enumerate((onsa, oml_ref[...], omem_ref[...])):
        gate = _sigmoid(_dot(xn, wg_ref[:, n * D_MODEL:(n + 1) * D_MODEL]))
        term = gate * _dot(on.astype(BF16), wb_ref[n])
        z = term if z is None else z + term
    o_ref[...] = x + _dot(z.astype(BF16), wout_ref[...])


def _merge(x2d, slab, ocmp, oslc, owin, oml, omem, pw, tm):
    t = x2d.shape[0]
    row = lambda w: pl.BlockSpec((tm, w), lambda i: (i, 0))
    const_spec = lambda a: _resident_spec(a)
    return pl.pallas_call(
        _merge_kernel,
        grid=(t // tm,),
        in_specs=[row(D_MODEL), const_spec(pw['norm_mix']), const_spec(pw['w_gate']), row(LANES),
                  const_spec(pw['gate_expand'])] + [row(BRANCH_W)] * 5
                 + [const_spec(pw['w_branch']), const_spec(pw['w_out'])],
        out_specs=row(D_MODEL),
        out_shape=jax.ShapeDtypeStruct((t, D_MODEL), F32),
        compiler_params=_cparams(("arbitrary",), VMEM_LIMIT),
        name="merge",
    )(x2d, pw['norm_mix'], pw['w_gate'], slab, pw['gate_expand'], ocmp, oslc, owin, oml, omem,
      pw['w_branch'], pw['w_out'])


def _ffn_kernel(x_ref, nf_ref, win_ref, wout_ref, o_ref):
    x = x_ref[...]
    xn = _rms_rows(x, nf_ref[...]).astype(BF16)
    a = _dot(xn, win_ref[:, 0:FFN_HID])
    b = _dot(xn, win_ref[:, FFN_HID:2 * FFN_HID])
    o_ref[...] = x + _dot((a * _sigmoid(a) * b).astype(BF16), wout_ref[...])


def _ffn(x2d, pw, tm):
    t = x2d.shape[0]
    consts = [pw['norm_ffn'], pw['w_ffn_in'], pw['w_ffn_out']]
    return pl.pallas_call(
        _ffn_kernel,
        grid=(t // tm,),
        in_specs=[pl.BlockSpec((tm, D_MODEL), lambda i: (i, 0))] + [_resident_spec(a) for a in consts],
        out_specs=pl.BlockSpec((tm, D_MODEL), lambda i: (i, 0)),
        out_shape=jax.ShapeDtypeStruct((t, D_MODEL), F32),
        compiler_params=_cparams(("arbitrary",), VMEM_LIMIT),
        name="ffn",
    )(x2d, *consts)


def _rope_tables(pos):
    half = ROT_DIM // 2
    f32 = np.float32
    freqs = ROPE_THETA ** (-np.arange(half, dtype=np.float64) / half)
    ang = pos.astype(np.float64)[:, None] * freqs
    cos, sin = np.cos(ang).astype(f32), np.sin(ang).astype(f32)
    n = pos.shape[0]
    ones = np.ones((n, HD - ROT_DIM), f32)
    zeros = np.zeros((n, HD - ROT_DIM), f32)
    zh = np.zeros((n, half), f32)
    ct = np.concatenate([cos, cos, ones], axis=1)
    sa = np.concatenate([-sin, zh, zeros], axis=1)
    sb = np.concatenate([zh, sin, zeros], axis=1)
    one64, zero64 = np.ones((n, HD), f32), np.zeros((n, HD), f32)
    q_tabs = [np.concatenate([t, t], axis=1) for t in (ct, sa, sb)]
    k_tabs = [np.concatenate([ct, one64], axis=1), np.concatenate([sa, zero64], axis=1),
              np.concatenate([sb, zero64], axis=1)]
    return [jnp.asarray(t) for t in q_tabs + k_tabs]


def _prepare_weights(norm_mix, w_in, q_norm, k_norm_cmp, k_norm_slc, k_norm_win, cmp_pe, cmp_w1, cmp_b1, cmp_w2,
                     conv_w, conv_b, w_ml_q, w_ml_k, w_ml_v, b_igate, b_fgate, ml_norm, norm_mem, w_mem_kv,
                     mem_q_norm, mem_k_norm, w_branch, w_out, norm_ffn, w_ffn_in, w_ffn_out):
    widths = (NSA_H * HD, 3 * NSA_H, KV_W, KV_W, KV_W, ML_W, ML_W, ML_H, ML_H, MEM_W, N_BRANCH * D_MODEL)
    offs = np.concatenate([[0], np.cumsum(widths)])
    q, g, kvc, kvs, kvw, u, o, ig, fg, mq, mg = (w_in[:, int(offs[i]):int(offs[i + 1])] for i in range(len(widths)))
    slab_pad = jnp.zeros((D_MODEL, LANES - 2 * ML_H - 3 * NSA_H), F32)
    pw = {}
    pw['w_proj'] = jnp.concatenate([q, kvc, kvs, kvw, u, o, mq, ig, fg, g, slab_pad], axis=1).astype(BF16)
    pw['w_gate'] = mg.astype(BF16)
    pw['norm_mix'] = norm_mix.reshape(1, D_MODEL)
    pw['q_gain'] = jnp.tile(q_norm, NSA_H).reshape(1, NSA_H * HD)
    ones = jnp.ones((HD,), F32)
    pw['ks_gain'] = jnp.tile(jnp.concatenate([k_norm_slc, ones]), NSA_G).reshape(1, KV_W)
    pw['kw_gain'] = jnp.tile(jnp.concatenate([k_norm_win, ones]), NSA_G).reshape(1, KV_W)
    pw['kc_gain'] = jnp.concatenate([k_norm_cmp, ones]).reshape(1, LANES)
    pw['mq_gain'] = jnp.tile(mem_q_norm, MEM_H).reshape(1, MEM_W)
    head_of = np.arange(NSA_H * HD) // HD
    pw['seg64'] = jnp.asarray(head_of[:, None] == head_of[None, :], BF16)
    w1 = cmp_w1.reshape(2, CMP_BLOCK, HD, CMP_HID)
    eye = jnp.eye(2, dtype=F32)
    for name, part in (('cmp_wlo', w1[:, :CMP_STRIDE]), ('cmp_whi', w1[:, CMP_STRIDE:])):
        pw[name] = jnp.einsum('cjdh,ce->jcdeh', part, eye).reshape(CMP_STRIDE * 2 * HD, 2 * CMP_HID).astype(BF16)
    pw['cmp_w2'] = jnp.einsum('chd,ce->ched', cmp_w2, eye).reshape(2 * CMP_HID, 2 * HD).astype(BF16)
    pw['cmp_b1'] = cmp_b1.reshape(1, 2 * CMP_HID)
    pw['pe_lo'] = cmp_pe[:CMP_STRIDE].reshape(1, CMP_STRIDE * 2 * HD)
    pw['pe_hi'] = cmp_pe[CMP_STRIDE:].reshape(1, CMP_STRIDE * 2 * HD)
    pw['conv_w'] = conv_w
    pw['conv_b'] = conv_b.reshape(1, ML_W)
    pw['w_ml_q'], pw['w_ml_k'], pw['w_ml_v'] = (w.astype(BF16) for w in (w_ml_q, w_ml_k, w_ml_v))
    pw['w_ml_q_t'], pw['w_ml_k_t'], pw['w_ml_v_t'] = (jnp.swapaxes(w, 1, 2).astype(BF16) for w in (w_ml_q, w_ml_k, w_ml_v))
    pw['ml_gain_t'] = jnp.broadcast_to(ml_norm.reshape(ML_W, 1), (ML_W, LANES))
    gate_bias = jnp.concatenate([b_igate, b_fgate])
    pw['gate_bias_lanes'] = jnp.pad(gate_bias, (0, LANES - 2 * ML_H)).reshape(1, LANES)
    pw['gate_bias_rows'] = gate_bias.reshape(2 * ML_H, 1)
    pw['ml_gain'] = ml_norm.reshape(1, ML_W)
    pw['norm_mem'] = norm_mem.reshape(1, D_MODEL)
    pw['w_mem_kv'] = w_mem_kv.astype(BF16)
    pw['mem_k_gain'] = mem_k_norm.reshape(1, MEM_HD)
    ex = np.zeros((N_BRANCH, LANES, NSA_H * HD), np.float32)
    for br in range(N_BRANCH):
        for gh in range(NSA_H):
            ex[br, _SLAB_G + br * NSA_H + gh, gh * HD:(gh + 1) * HD] = 1.0
    pw['gate_expand'] = jnp.asarray(ex, BF16)
    pw['w_branch'] = w_branch.astype(BF16)
    pw['w_out'] = w_out.astype(BF16)
    pw['norm_ffn'] = norm_ffn.reshape(1, D_MODEL)
    pw['w_ffn_in'] = w_ffn_in.astype(BF16)
    pw['w_ffn_out'] = w_ffn_out.astype(BF16)
    return pw


def _overlap(n_cmp_rows, n_slc_cols):
    ci = np.arange(n_cmp_rows)[:, None] * CMP_STRIDE
    sj = np.arange(n_slc_cols)[None, :] * SLC_BLOCK
    return ((ci < sj + SLC_BLOCK) & (ci + CMP_BLOCK > sj)).astype(np.float32)


def _pick_tile(n, pref):
    t = min(n, pref)
    while n % t:
        t //= 2
    return t


def _prompt_group(x, mem, pw):
    b, s, _ = x.shape
    t = b * s
    x2d = x.reshape(t, D_MODEL)
    tm = _pick_tile(s, TOKEN_TILE)
    tabs = _rope_tables(np.arange(s))
    (kvc, kvs, kvw, u, oraw, slab,
     qrot_t, kvc_t, kvs_t, kvw_t, qn_t, mq_t) = _project(x2d, tabs, pw, tm, seq_len=s)
    kvcmp, kvcmp_t = _compress_prompt(kvc.reshape(b, s, KV_W), pw)
    r = s // CMP_STRIDE
    ovt = jnp.asarray(_overlap(r, HD).T, BF16)
    ocmp, sel_t = _cmp_prompt(qn_t, kvcmp, kvcmp_t, ovt, b, s, _pick_tile(s, CMP_TQ))
    fq, fk = _pick_tile(s, FLASH_TQ), _pick_tile(s, FLASH_TK)
    oslc = _flash_prompt(qrot_t, kvs.reshape(b, s, KV_W), kvs_t, sel_t, b, s, fq, fk)
    owin = _flash_prompt(qrot_t, kvw.reshape(b, s, KV_W), kvw_t, None, b, s, fq, fk)
    chunk = _pick_tile(s, MLSTM_CHUNK)
    oml, c_t, n_new, m_new = _mlstm_prompt(u.reshape(b, s, ML_W), oraw.reshape(b, s, ML_W),
                                             slab.reshape(b, s, LANES), pw, chunk)
    t_mem = mem.shape[1]
    kvm, kvm_t = _mem_kv(mem, pw)
    omem = _mem_prompt(mq_t, kvm, kvm_t, b, s, _pick_tile(s, MEM_TQ))
    x1 = _merge(x2d, slab, ocmp, oslc, owin, oml.reshape(t, ML_W), omem, pw, tm)
    y = _ffn(x1, pw, tm).reshape(b, s, D_MODEL)
    kv5 = lambda a_t: jnp.transpose(a_t.reshape(b, NSA_G, 2, HD, a_t.shape[-1]), (0, 4, 1, 2, 3))
    return dict(y=y, kv_cmp=kv5(kvc_t), kv_slc=kv5(kvs_t), kv_win_t=kvw_t, kv5=kv5,
                kv_mem=kvm.reshape(b, t_mem, MEM_H, 2, MEM_HD), c=jnp.swapaxes(c_t, 2, 3), n=n_new,
                m=m_new[:, 0, :ML_H],
                u=u.reshape(b, s, ML_W))


def _sample_group(x, cache_cmp, cache_slc, cache_win, cache_mem, state_c, state_n, state_m, state_conv,
                  page_table, pw):
    db, ds, _ = x.shape
    assert ds == 1, "the sample kernels handle one new token per sequence"
    n_pages = page_table.shape[1]
    past = n_pages * PAGE_SIZE
    pos = past
    x2d = x.reshape(db, D_MODEL)
    tabs = _rope_tables(np.full((db,), pos))
    kvc, kvs, kvw, u, oraw, slab, qn, qrot, mq = _project(x2d, tabs, pw, db)
    kvcmp = _compress_paged(cache_cmp, page_table, kvc.reshape(db, 1, KV_W), pw)
    n_cmp = kvcmp.shape[2]
    n_slc = -(-(past + ds) // SLC_BLOCK)
    nslp = -(-n_slc // LANES) * LANES
    ov = jnp.asarray(_overlap(n_cmp, nslp), BF16)
    ocmp, sel = _cmp_sample(qn.reshape(db, NSA_H, HD), kvcmp, ov, pos, n_slc)
    q3 = qrot.reshape(db, NSA_G, NSA_HPG, 1, HD)
    qaug = (q3 * jnp.eye(NSA_G, dtype=BF16)[None, :, None, :, None]).reshape(db, NSA_H, NSA_G, HD)
    qaug = jnp.concatenate([qaug, jnp.zeros_like(qaug)], axis=-1).reshape(db, NSA_H, KV_W)
    oslc = _slc_sample(qaug, sel, cache_slc, page_table, kvs.reshape(db, 1, KV_W))
    w_buf = cache_win.shape[1]
    owin, win_new_t = _win_sample(qaug, _feature_major(cache_win), kvw.reshape(db, 1, KV_W), pos, past - w_buf)
    win_new = jnp.transpose(win_new_t.reshape(db, NSA_G, 2, HD, w_buf), (0, 4, 1, 2, 3))
    ext = jnp.concatenate([state_conv, u.reshape(db, 1, ML_W)], axis=1)
    oml, c_new, n_new, m_new = _mlstm_sample(ext, oraw.reshape(db, 1, ML_W), slab.reshape(db, 1, LANES),
                                             state_c, state_n, state_m, pw)
    q8 = jnp.pad(mq.reshape(db, MEM_H, MEM_HD), ((0, 0), (0, 8 - MEM_H), (0, 0)))
    omem = _mem_sample(q8, cache_mem)[:, :MEM_H].reshape(db, MEM_W)
    x1 = _merge(x2d, slab, ocmp.reshape(db, NSA_H * HD), oslc.reshape(db, NSA_H * HD), owin.reshape(db, NSA_H * HD),
                oml.reshape(db, ML_W), omem, pw, db)
    y = _ffn(x1, pw, db).reshape(db, ds, D_MODEL)
    kv5 = lambda a: a.reshape(db, ds, NSA_G, 2, HD)
    return dict(y=y, kv_cmp=kv5(kvc), kv_slc=kv5(kvs), win=win_new,
                c=c_new, n=n_new, m=m_new[:, 0, :ML_H], conv=ext[:, 1:])


def kernel(x_prompt, x_sample, cache_kv_cmp, cache_kv_slc, cache_kv_win, cache_kv_mem, state_C, state_n, state_m, state_conv, page_table, mem_prompt, norm_mix, w_in, q_norm, k_norm_cmp, k_norm_slc, k_norm_win, cmp_pe, cmp_w1, cmp_b1, cmp_w2, conv_w, conv_b, w_ml_q, w_ml_k, w_ml_v, b_igate, b_fgate, ml_norm, norm_mem, w_mem_kv, mem_q_norm, mem_k_norm, w_branch, w_out, norm_ffn, w_ffn_in, w_ffn_out):
    pw = _prepare_weights(norm_mix, w_in, q_norm, k_norm_cmp, k_norm_slc, k_norm_win, cmp_pe, cmp_w1, cmp_b1, cmp_w2,
                          conv_w, conv_b, w_ml_q, w_ml_k, w_ml_v, b_igate, b_fgate, ml_norm, norm_mem, w_mem_kv,
                          mem_q_norm, mem_k_norm, w_branch, w_out, norm_ffn, w_ffn_in, w_ffn_out)
    p = _prompt_group(x_prompt, mem_prompt, pw)
    s = _sample_group(x_sample, cache_kv_cmp, cache_kv_slc, cache_kv_win, cache_kv_mem, state_C, state_n, state_m,
                      state_conv, page_table, pw)
    b, seq = x_prompt.shape[:2]
    w_buf = cache_kv_win.shape[1]
    assert seq >= w_buf
    win_p = p['kv5'](p['kv_win_t'][:, :, seq - w_buf:])
    conv_p = p['u'][:, seq - (CONV_W - 1):]
    return (p['y'], s['y'], p['kv_cmp'], s['kv_cmp'], p['kv_slc'], s['kv_slc'], win_p, s['win'], p['kv_mem'],
            p['c'], s['c'], p['n'], s['n'], p['m'], s['m'], conv_p, s['conv'])
```

```python
import functools

import numpy as np
import jax
import jax.numpy as jnp
from jax import lax
from jax.experimental import pallas as pl
from jax.experimental.pallas import tpu as pltpu

F32 = jnp.float32
BF16 = jnp.bfloat16

D_MODEL = 1024
PAGE_SIZE = 128
NSA_H, NSA_G, NSA_HPG, HD = 8, 2, 4, 64
ROT_DIM = HD // 4
ROPE_THETA = 500000.0
CMP_BLOCK, CMP_STRIDE, CMP_HID = 32, 16, 128
SLC_BLOCK, N_SEL = 64, 16
WINDOW = 512
ML_H, ML_D, ML_W, CONV_W = 4, 128, 512, 4
MEM_H, MEM_HD, MEM_W = 4, 128, 512
N_BRANCH, BRANCH_W = 3, 512
FFN_HID = -(-8 * D_MODEL // (3 * 256)) * 256
EPS = 1e-6

KV_W = 2 * NSA_G * HD
CHUNK_W = CMP_STRIDE * KV_W
LANES = 128
MASKED = -1e30
UNSELECTED = -32768.0
VMEM_LIMIT = 56 * 1024 * 1024
TOKEN_TILE = 512
CMP_TQ = 1024
MEM_TQ = 1024
FLASH_TQ, FLASH_TK = 512, 256
FLASH_COLS = 128
FLASH_ONES_ROWS = 64
MLSTM_CHUNK = 512
SAMPLE_SEQS_PER_STEP = 8

_Q0, _KVC0, _KVS0, _KVW0, _U0, _O0, _MQ0, _SLAB0, _PROJ_W = 0, 512, 768, 1024, 1280, 1792, 2304, 2816, 2944
_SLAB_I, _SLAB_F, _SLAB_G = 0, 4, 8


def _dot(a, b):
    return jnp.dot(a, b, preferred_element_type=F32)


def _dot_nt(a, b):
    return lax.dot_general(a, b, (((1,), (1,)), ((), ())), preferred_element_type=F32)


def _split2(x):
    hi = x.astype(BF16)
    return hi, (x - hi.astype(F32)).astype(BF16)


def _split3(x):
    hi = x.astype(BF16)
    r = x - hi.astype(F32)
    mid = r.astype(BF16)
    return hi, mid, (r - mid.astype(F32)).astype(BF16)


def _dot_exact_rhs(parts, m):
    out = _dot(parts[0], m)
    for p in parts[1:]:
        out = out + _dot(p, m)
    return out


def _dot_exact_rhs_left(m, parts):
    out = _dot(m, parts[0])
    for p in parts[1:]:
        out = out + _dot(m, p)
    return out


def _sigmoid(x):
    return 1.0 / (1.0 + jnp.exp(-x))


def _log_sigmoid(x):
    return jnp.minimum(x, 0.0) - jnp.log(1.0 + jnp.exp(-jnp.abs(x)))


def _gelu_tanh(x):
    return x * (0.5 * (1.0 + jnp.tanh(np.sqrt(2.0 / np.pi).astype(np.float32) * (x + 0.044715 * (x * x * x)))))


def _rms_rows(x, gain):
    ms = jnp.mean(x * x, axis=-1, keepdims=True)
    return x * lax.rsqrt(ms + EPS) * gain


def _iota(shape, dim):
    return lax.broadcasted_iota(jnp.int32, shape, dim)


def _cparams(sem, vmem=None):
    return pltpu.CompilerParams(dimension_semantics=sem, vmem_limit_bytes=vmem)


def _resident_spec(a):
    nd = a.ndim
    return pl.BlockSpec(a.shape, lambda *_: (0,) * nd, pipeline_mode=pl.Buffered(1))


def _masked_softmax(s, mask):
    s = jnp.where(mask, s, -jnp.inf)
    m = jnp.max(s, axis=-1, keepdims=True)
    m = jnp.where(m > -jnp.inf, m, 0.0)
    p = jnp.exp(s - m)
    return p / jnp.maximum(jnp.sum(p, axis=-1, keepdims=True), jnp.finfo(F32).tiny)


def _rs_heads64(a, seg_ref, width):
    ss = _dot_exact_rhs(_split2(a * a), seg_ref[0:width, 0:width])
    return lax.rsqrt(ss * (1.0 / HD) + EPS)


def _rope(a, ct, sa, sb, width):
    reps = width // LANES
    if reps > 1:
        ct, sa, sb = (jnp.concatenate([t] * reps, axis=1) for t in (ct, sa, sb))
    half = ROT_DIM // 2
    return a * ct + pltpu.roll(a, width - half, 1) * sa + pltpu.roll(a, half, 1) * sb


def _proj_kernel(x_ref, nm_ref, w_ref, qg_ref, ksg_ref, kwg_ref, mqg_ref, seg_ref,
                 ctq_ref, saq_ref, sbq_ref, ctk_ref, sak_ref, sbk_ref,
                 kvc_o, kvs_o, kvw_o, u_o, o_o, slab_o, *rest):
    fm_outs = rest if len(rest) == 6 else ()
    xn = _rms_rows(x_ref[...], nm_ref[...]).astype(BF16)

    def seg(lo, hi):
        return _dot(xn, w_ref[:, lo:hi])

    q = seg(_Q0, _KVC0)
    qn = q * _rs_heads64(q, seg_ref, NSA_H * HD) * qg_ref[...]
    scale = HD ** -0.5
    qr = _rope(qn, ctq_ref[...], saq_ref[...], sbq_ref[...], NSA_H * HD) * scale
    kvc = seg(_KVC0, _KVS0)
    kvc_o[...] = kvc
    if fm_outs:
        fm_outs[0][0] = qr.T.astype(BF16)
        fm_outs[1][0] = kvc.T
        fm_outs[4][0] = (qn * scale).T.astype(BF16)
    else:
        rest[0][...] = (qn * scale).astype(BF16)
        rest[1][...] = qr.astype(BF16)
    for n, (lo, g_ref, o_ref) in enumerate(((_KVS0, ksg_ref, kvs_o), (_KVW0, kwg_ref, kvw_o))):
        a = seg(lo, lo + KV_W)
        is_k = (_iota(a.shape, 1) & HD) == 0
        an = jnp.where(is_k, a * _rs_heads64(a, seg_ref, KV_W) * g_ref[...], a)
        ar = _rope(an, ctk_ref[...], sak_ref[...], sbk_ref[...], KV_W)
        o_ref[...] = ar
        if fm_outs:
            fm_outs[2 + n][0] = ar.T
    u_o[...] = seg(_U0, _O0)
    o_o[...] = seg(_O0, _MQ0)
    mq = seg(_MQ0, _SLAB0)
    heads = []
    for h in range(MEM_H):
        mh = mq[:, h * MEM_HD:(h + 1) * MEM_HD]
        heads.append(mh * lax.rsqrt(jnp.mean(mh * mh, axis=-1, keepdims=True) + EPS))
    mqn = jnp.concatenate(heads, axis=1) * mqg_ref[...] * (MEM_HD ** -0.5)
    if fm_outs:
        fm_outs[5][0] = mqn.T.astype(BF16)
    else:
        rest[2][...] = mqn.astype(BF16)
    slab_o[...] = seg(_SLAB0, _PROJ_W)


def _project(x2d, tabs, pw, tm, seq_len=None):
    t = x2d.shape[0]
    ntab = tabs[0].shape[0] // tm
    row = lambda i: (i, 0)
    const = lambda i: (0, 0)
    tab = lambda i: (i % ntab, 0)
    full = lambda a: _resident_spec(a)
    consts = [pw['norm_mix'], pw['w_proj'], pw['q_gain'], pw['ks_gain'], pw['kw_gain'], pw['mq_gain'],
              pw['seg64']]
    widths = [(KV_W, F32), (KV_W, F32), (KV_W, F32), (ML_W, F32), (ML_W, F32), (LANES, F32)]
    if seq_len is None:
        widths += [(NSA_H * HD, BF16), (NSA_H * HD, BF16), (MEM_W, BF16)]
    out_specs = [pl.BlockSpec((tm, w), row) for w, _ in widths]
    out_shape = [jax.ShapeDtypeStruct((t, w), d) for w, d in widths]
    if seq_len is not None:
        per_seq = seq_len // tm
        for w, d in ((NSA_H * HD, BF16), (KV_W, F32), (KV_W, F32), (KV_W, F32), (NSA_H * HD, BF16), (MEM_W, BF16)):
            out_specs.append(pl.BlockSpec((1, w, tm), lambda i: (i // per_seq, 0, i % per_seq)))
            out_shape.append(jax.ShapeDtypeStruct((t // seq_len, w, seq_len), d))
    return pl.pallas_call(
        _proj_kernel,
        grid=(t // tm,),
        in_specs=[pl.BlockSpec((tm, D_MODEL), row)] + [full(a) for a in consts]
                 + [pl.BlockSpec((tm, LANES), tab)] * 6,
        out_specs=out_specs,
        out_shape=out_shape,
        compiler_params=_cparams(("arbitrary",), VMEM_LIMIT),
        name="project",
    )(x2d, *consts, *tabs)


def _compress_group(y_lo, y_hi, tail_hi, wlo, whi, b1, w2, gain, n_valid):
    r = y_lo.shape[0]
    a = _dot(y_lo, wlo)
    bh = _dot(y_hi, whi)
    bs = pltpu.roll(bh, r - 1, 0)
    if tail_hi is not None:
        bt = _dot(tail_hi, whi)
        bs = jnp.where(_iota(bs.shape, 0) == r - 1, bt, bs)
    hid = _gelu_tanh(a + bs + b1)
    out = _dot(hid.astype(BF16), w2)
    is_k = _iota(out.shape, 1) < HD
    ss = jnp.sum(jnp.where(is_k, out * out, 0.0), axis=-1, keepdims=True) * (1.0 / HD)
    kv = jnp.where(is_k, out * lax.rsqrt(ss + EPS) * gain, out)
    return jnp.where(_iota(kv.shape, 0) < n_valid, kv, 0.0)


def _group_cols(load, g):
    return jnp.concatenate([load(j * KV_W + g * LANES, j * KV_W + (g + 1) * LANES) for j in range(CMP_STRIDE)], axis=1)


def _compress_prompt_kernel(y_ref, pe_lo, pe_hi, wlo, whi, b1, w2, gain, o_ref, ot_ref, *, n_valid):
    for g in range(NSA_G):
        yg = _group_cols(lambda lo, hi: y_ref[0, :, lo:hi], g)
        kv = _compress_group((yg + pe_lo[...]).astype(BF16), (yg + pe_hi[...]).astype(BF16), None,
                             wlo[...], whi[...], b1[...], w2[...], gain[...], n_valid)
        o_ref[0, g] = kv.astype(BF16)
        ot_ref[0, g] = kv.T.astype(BF16)


def _compress_paged_kernel(pt_ref, *refs, n_pages, n_valid):
    pages = refs[:n_pages]
    tail_ref, pe_lo, pe_hi, wlo, whi, b1, w2, gain, o_ref, y = refs[n_pages:]
    cpp = PAGE_SIZE // CMP_STRIDE
    for k, page in enumerate(pages):
        for g in range(NSA_G):
            nat = page[0, g * LANES:(g + 1) * LANES, :].T
            by_row = jnp.swapaxes(nat.reshape(cpp, CMP_STRIDE, LANES), 0, 1)
            for j in range(CMP_STRIDE):
                y[g, k * cpp:(k + 1) * cpp, j * LANES:(j + 1) * LANES] = by_row[j]
    for g in range(NSA_G):
        yg = y[g]
        tg = _group_cols(lambda lo, hi: tail_ref[0, :, lo:hi], g)
        o_ref[0, g] = _compress_group((yg + pe_lo[...]).astype(BF16), (yg + pe_hi[...]).astype(BF16),
                                      (tg + pe_hi[...]).astype(BF16), wlo[...], whi[...], b1[...], w2[...],
                                      gain[...], n_valid).astype(BF16)


def _compress_consts(pw):
    return [pw['cmp_wlo'], pw['cmp_whi'], pw['cmp_b1'], pw['cmp_w2'], pw['kc_gain']]


def _compress_prompt(kvc, pw):
    b, s, _ = kvc.shape
    r = s // CMP_STRIDE
    y = kvc.reshape(b, r, CHUNK_W)
    consts = [pw['pe_lo'], pw['pe_hi']] + _compress_consts(pw)
    return pl.pallas_call(
        functools.partial(_compress_prompt_kernel, n_valid=r - 1),
        grid=(b,),
        in_specs=[pl.BlockSpec((1, r, CHUNK_W), lambda i: (i, 0, 0))]
                 + [pl.BlockSpec(a.shape, lambda i: (0, 0)) for a in consts],
        out_specs=[pl.BlockSpec((1, NSA_G, r, LANES), lambda i: (i, 0, 0, 0)),
                   pl.BlockSpec((1, NSA_G, LANES, r), lambda i: (i, 0, 0, 0))],
        out_shape=[jax.ShapeDtypeStruct((b, NSA_G, r, LANES), BF16),
                   jax.ShapeDtypeStruct((b, NSA_G, LANES, r), BF16)],
        compiler_params=_cparams(("arbitrary",), VMEM_LIMIT),
        name="compress_prompt",
    )(y, *consts)


def _compress_paged(cache, page_table, kvc_new, pw):
    db, n_pages = page_table.shape
    chunks_per_page = PAGE_SIZE // CMP_STRIDE
    r = n_pages * chunks_per_page
    pool = _feature_major(cache)
    tail = jnp.pad(kvc_new, ((0, 0), (0, CMP_STRIDE - kvc_new.shape[1]), (0, 0))).reshape(db, 1, CHUNK_W)
    consts = [pw['pe_lo'], pw['pe_hi']] + _compress_consts(pw)
    page_specs = [pl.BlockSpec((1, KV_W, PAGE_SIZE), functools.partial(lambda i, pt, k: (pt[i, k], 0, 0), k=k))
                  for k in range(n_pages)]
    return pl.pallas_call(
        functools.partial(_compress_paged_kernel, n_pages=n_pages, n_valid=r),
        grid_spec=pltpu.PrefetchScalarGridSpec(
            num_scalar_prefetch=1,
            grid=(db,),
            in_specs=page_specs + [pl.BlockSpec((1, 1, CHUNK_W), lambda i, pt: (i, 0, 0))]
                     + [pl.BlockSpec(a.shape, lambda i, pt: (0, 0)) for a in consts],
            out_specs=pl.BlockSpec((1, NSA_G, r, LANES), lambda i, pt: (i, 0, 0, 0)),
            scratch_shapes=[pltpu.VMEM((NSA_G, r, CMP_STRIDE * LANES), F32)],
        ),
        out_shape=jax.ShapeDtypeStruct((db, NSA_G, r, LANES), BF16),
        compiler_params=_cparams(("arbitrary",), VMEM_LIMIT),
        name="compress_paged",
    )(page_table, *([pool] * n_pages), tail, *consts)


def _rank_select(score, idx, n_blocks):
    cnt = jnp.zeros(score.shape, jnp.int32)
    for i in range(n_blocks):
        other = score[:, i:i + 1]
        ge = jnp.where(other >= score, 1, 0)
        gt = jnp.where(other > score, 1, 0)
        cnt = cnt + jnp.where(idx > i, ge, gt)
    return jnp.where(score > -jnp.inf, jnp.where(cnt < N_SEL, 0.0, UNSELECTED), UNSELECTED)


def _rank_select_rows(score, n_blocks, last_pos, cnt_ref):
    sub = _iota((8, score.shape[1]), 0)
    n_groups = -(-n_blocks // 8)
    groups = [score[g * 8:(g + 1) * 8, :] for g in range(n_groups)]
    cnt_ref[...] = jnp.zeros(cnt_ref.shape, jnp.int32)
    for og in range(n_groups):
        @pl.when(og * 8 * SLC_BLOCK <= last_pos)
        def _():
            part = [None] * n_groups
            for i in range(og * 8, min(og * 8 + 8, n_blocks)):
                other = score[i:i + 1, :]
                for g, sg in enumerate(groups):
                    if g > og:
                        one = jnp.where(other >= sg, 1, 0)
                    elif g < og:
                        one = jnp.where(other > sg, 1, 0)
                    else:
                        one = jnp.where(sub > i % 8, jnp.where(other >= sg, 1, 0), jnp.where(other > sg, 1, 0))
                    part[g] = one if part[g] is None else part[g] + one
            for g in range(n_groups):
                cnt_ref[g * 8:(g + 1) * 8, :] += part[g]
    cnts = [cnt_ref[g * 8:(g + 1) * 8, :] for g in range(n_groups)]
    out = [jnp.where(sg > -jnp.inf, jnp.where(c < N_SEL, 0.0, UNSELECTED), UNSELECTED) for sg, c in zip(groups, cnts)]
    pad = score.shape[0] - 8 * n_groups
    if pad:
        out.append(jnp.full((pad, score.shape[1]), UNSELECTED, F32))
    return jnp.concatenate(out, axis=0)


def _cmp_prompt_kernel(qt_ref, kv_ref, kvt_ref, ovt_ref, o_ref, sel_ref, cnt_ref, *, tq, n_slc):
    i = pl.program_id(2)
    kv = kv_ref[0, 0]
    kvt = kvt_ref[0, 0]
    r = kv.shape[0]
    pos = i * tq + _iota((1, tq), 1)
    visible = (_iota((r, 1), 0) * CMP_STRIDE + (CMP_BLOCK - 1)) <= pos
    zeros = jnp.zeros((HD, tq), BF16)
    psum = jnp.zeros((r, tq), F32)
    outs = []
    for h in range(NSA_HPG):
        qa = jnp.concatenate([qt_ref[0, h * HD:(h + 1) * HD, :], zeros], axis=0)
        s = jnp.where(visible, _dot(kv, qa), -jnp.inf)
        m = jnp.max(s, axis=0, keepdims=True)
        e = jnp.exp(s - jnp.where(m > -jnp.inf, m, 0.0))
        inv_l = 1.0 / jnp.maximum(jnp.sum(e, axis=0, keepdims=True), jnp.finfo(F32).tiny)
        psum = psum + e * inv_l
        outs.append(_dot(kvt[HD:2 * HD, :], e.astype(BF16)) * inv_l)
    o_ref[...] = jnp.concatenate([jnp.concatenate(outs[0:2], axis=0).T, jnp.concatenate(outs[2:4], axis=0).T], axis=1)
    imp = _dot_exact_rhs_left(ovt_ref[...], _split2(psum))
    blk = _iota(imp.shape, 0)
    pos_row = i * tq + _iota(imp.shape, 1)
    forced = (blk == jnp.right_shift(pos_row, 6)) | (blk == 0)
    score = jnp.where(forced, jnp.inf, jnp.where(blk * SLC_BLOCK <= pos_row, imp, -jnp.inf))
    sel_ref[0, 0] = _rank_select_rows(score, n_slc, i * tq + tq - 1, cnt_ref).astype(BF16)


def _cmp_prompt(qn_t, kvcmp, kvcmp_t, ovt, b, s, tq):
    nq = s // tq
    r = kvcmp.shape[2]
    n_slc = s // SLC_BLOCK
    assert n_slc <= HD and ovt.shape == (HD, r) and SLC_BLOCK == 64
    return pl.pallas_call(
        functools.partial(_cmp_prompt_kernel, tq=tq, n_slc=n_slc),
        grid=(b, NSA_G, nq),
        in_specs=[pl.BlockSpec((1, NSA_HPG * HD, tq), lambda bi, g, i: (bi, g, i)),
                  pl.BlockSpec((1, 1, r, LANES), lambda bi, g, i: (bi, g, 0, 0)),
                  pl.BlockSpec((1, 1, LANES, r), lambda bi, g, i: (bi, g, 0, 0)),
                  pl.BlockSpec(ovt.shape, lambda bi, g, i: (0, 0))],
        out_specs=[pl.BlockSpec((tq, NSA_HPG * HD), lambda bi, g, i: (bi * nq + i, g)),
                   pl.BlockSpec((1, 1, HD, tq), lambda bi, g, i: (bi, g, 0, i))],
        out_shape=[jax.ShapeDtypeStruct((b * s, NSA_H * HD), F32),
                   jax.ShapeDtypeStruct((b, NSA_G, HD, s), BF16)],
        scratch_shapes=[pltpu.VMEM((HD, tq), jnp.int32)],
        compiler_params=_cparams(("arbitrary",) * 3, VMEM_LIMIT),
        name="cmp_attend_select_prompt",
    )(qn_t, kvcmp, kvcmp_t, ovt)


def _cmp_sample_kernel(q_ref, kv_ref, ov_ref, o_ref, sel_ref, *, pos, n_slc):
    r = kv_ref.shape[2]
    row = _iota((NSA_H, 1), 0)
    visible = (_iota((1, r), 1) * CMP_STRIDE + (CMP_BLOCK - 1)) <= pos
    n_seq = q_ref.shape[0]
    scores = []
    for bi in range(n_seq):
        q = q_ref[bi]
        kvs = [kv_ref[bi, g] for g in range(NSA_G)]
        s = jnp.where(row < NSA_HPG, _dot_nt(q, kvs[0][:, 0:HD]), _dot_nt(q, kvs[1][:, 0:HD]))
        p = _masked_softmax(s, visible)
        pb = p.astype(BF16)
        o = jnp.where(row < NSA_HPG, _dot(pb, kvs[0]), _dot(pb, kvs[1]))
        o_ref[bi] = o[:, HD:2 * HD]
        p0 = jnp.sum(p[0:NSA_HPG], axis=0, keepdims=True)
        p1 = jnp.sum(p[NSA_HPG:NSA_H], axis=0, keepdims=True)
        psum = jnp.where(row < NSA_HPG, p0, p1)
        imp = _dot_exact_rhs(_split2(psum), ov_ref[...])
        blk = _iota(imp.shape, 1)
        forced = (blk == pos // SLC_BLOCK) | (blk == 0)
        scores.append(jnp.where(forced, jnp.inf, jnp.where(blk * SLC_BLOCK <= pos, imp, -jnp.inf)))
    stacked = jnp.concatenate(scores, axis=0)
    sel = _rank_select(stacked, _iota(stacked.shape, 1), n_slc)
    for bi in range(n_seq):
        sel_ref[bi] = sel[bi * NSA_H:(bi + 1) * NSA_H]


def _cmp_sample(qn3, kvcmp, ov, pos, n_slc):
    db, _, r, _ = kvcmp.shape
    nslp = ov.shape[1]
    bb = _pick_tile(db, SAMPLE_SEQS_PER_STEP)
    return pl.pallas_call(
        functools.partial(_cmp_sample_kernel, pos=pos, n_slc=n_slc),
        grid=(db // bb,),
        in_specs=[pl.BlockSpec((bb, NSA_H, HD), lambda i: (i, 0, 0)),
                  pl.BlockSpec((bb, NSA_G, r, LANES), lambda i: (i, 0, 0, 0)),
                  pl.BlockSpec(ov.shape, lambda i: (0, 0))],
        out_specs=[pl.BlockSpec((bb, NSA_H, HD), lambda i: (i, 0, 0)),
                   pl.BlockSpec((bb, NSA_H, nslp), lambda i: (i, 0, 0))],
        out_shape=[jax.ShapeDtypeStruct((db, NSA_H, HD), F32),
                   jax.ShapeDtypeStruct((db, NSA_H, nslp), F32)],
        compiler_params=_cparams(("arbitrary",), VMEM_LIMIT),
        name="cmp_attend_select_sample",
    )(qn3, kvcmp, ov)


def _flash_prompt_kernel(*refs, tq, selected):
    if selected:
        qt_ref, kv_ref, kvt_ref, sel_ref, o_ref, kop, vop, qa_ref, m_ref, acc_ref = refs
    else:
        qt_ref, kv_ref, kvt_ref, o_ref, kop, vop, qa_ref, m_ref, acc_ref = refs
    i = pl.program_id(2)
    cols = NSA_HPG * tq
    n_tiles, v_rows, tk = vop.shape
    ones_rows = v_rows - HD
    wide = FLASH_COLS

    @pl.when(i == 0)
    def _():
        kvf = kv_ref[0]
        if selected:
            lane = _iota(kvf.shape, 1)
            onehot = (lane - HD) == jnp.right_shift(_iota(kvf.shape, 0), 6)
            kop[...] = jnp.where(lane < HD, kvf, onehot.astype(F32)).astype(BF16)
        else:
            kop[...] = kvf.astype(BF16)
        for t in range(n_tiles):
            vop[t, 0:ones_rows, :] = jnp.ones((ones_rows, tk), BF16)
            vop[t, ones_rows:v_rows, :] = kvt_ref[0, HD:2 * HD, t * tk:(t + 1) * tk].astype(BF16)

    extra = sel_ref[0, 0] if selected else jnp.zeros((HD, tq), BF16)
    for h in range(NSA_HPG):
        qa_ref[0:HD, h * tq:(h + 1) * tq] = qt_ref[0, h * HD:(h + 1) * HD, :]
        qa_ref[HD:2 * HD, h * tq:(h + 1) * tq] = extra
    m_ref[...] = jnp.full((1, cols), MASKED, F32)
    acc_ref[...] = jnp.zeros((v_rows, cols), F32)

    def tile_step(j, masked, blocks):
        start = pl.multiple_of(j * tk, tk)
        for cb in blocks:
            sl = slice(cb * wide, (cb + 1) * wide)
            s = _dot(kop[pl.ds(start, tk), :], qa_ref[:, sl])
            if masked:
                pos = i * tq + (cb * wide) % tq + _iota((1, wide), 1)
                kpos = j * tk + _iota((tk, 1), 0)
                valid = kpos <= pos
                if not selected:
                    valid = valid & ((pos - kpos) <= WINDOW)
                s = jnp.where(valid, s, MASKED)
            m_old = m_ref[:, sl]
            m_new = jnp.maximum(m_old, jnp.max(s, axis=0, keepdims=True))
            p = jnp.exp(s - m_new).astype(BF16)
            m_ref[:, sl] = m_new
            acc_ref[:, sl] = jnp.exp(m_old - m_new) * acc_ref[:, sl] + _dot(vop[j], p)

    per_q = tq // tk
    all_blocks = list(range(cols // wide))

    def reachable(back, u):
        first_key = u * tk - back * tq
        out = []
        for cb in all_blocks:
            q0 = (cb * wide) % tq
            causal = q0 + wide - 1 >= first_key
            in_window = selected or q0 - (first_key + tk - 1) <= WINDOW
            if causal and in_window:
                out.append(cb)
        return out

    def masked_unit(back):
        for u in range(per_q):
            tile_step((i - back) * per_q + u, True, reachable(back, u))

    if selected:
        def interior(t, carry):
            for u in range(per_q):
                tile_step(t * per_q + u, False, all_blocks)
            return carry
        lax.fori_loop(0, i, interior, 0)
    else:
        for back in range(WINDOW // tq, 0, -1):
            pl.when(i >= back)(functools.partial(masked_unit, back))
    masked_unit(0)
    normed = []
    for h in range(NSA_HPG):
        a = acc_ref[:, h * tq:(h + 1) * tq]
        normed.append(a[ones_rows:v_rows, :] * (1.0 / a[0:1, :]))
    o_ref[...] = jnp.concatenate([jnp.concatenate(normed[0:2], axis=0).T, jnp.concatenate(normed[2:4], axis=0).T], axis=1)


def _flash_prompt(qrot_t, kv, kv_t, sel_t, b, s, tq, tk):
    nq = s // tq
    assert WINDOW % tq == 0 and tq % tk == 0 and tk % LANES == 0 and s % tq == 0 and tq % FLASH_COLS == 0
    v_rows = FLASH_ONES_ROWS + HD
    selected = sel_t is not None
    in_specs = [pl.BlockSpec((1, NSA_HPG * HD, tq), lambda bi, g, i: (bi, g, i)),
                pl.BlockSpec((1, s, LANES), lambda bi, g, i: (bi, 0, g)),
                pl.BlockSpec((1, LANES, s), lambda bi, g, i: (bi, g, 0))]
    args = [qrot_t, kv, kv_t]
    cols = NSA_HPG * tq
    scratch = [pltpu.VMEM((s, LANES), BF16), pltpu.VMEM((s // tk, v_rows, tk), BF16),
               pltpu.VMEM((LANES, cols), BF16), pltpu.VMEM((1, cols), F32), pltpu.VMEM((v_rows, cols), F32)]
    if selected:
        in_specs.append(pl.BlockSpec((1, 1, HD, tq), lambda bi, g, i: (bi, g, 0, i)))
        args.append(sel_t)
    return pl.pallas_call(
        functools.partial(_flash_prompt_kernel, tq=tq, selected=selected),
        grid=(b, NSA_G, nq),
        in_specs=in_specs,
        out_specs=pl.BlockSpec((tq, NSA_HPG * HD), lambda bi, g, i: (bi * nq + i, g)),
        out_shape=jax.ShapeDtypeStruct((b * s, NSA_H * HD), F32),
        scratch_shapes=scratch,
        compiler_params=_cparams(("arbitrary",) * 3, VMEM_LIMIT),
        name="slc_attend_prompt" if selected else "win_attend_prompt",
    )(*args)


def _mem_kv_kernel(x_ref, nm_ref, w_ref, kg_ref, o_ref, ot_ref):
    xn = _rms_rows(x_ref[0], nm_ref[...]).astype(BF16)
    kv = _dot(xn, w_ref[...])
    cols = []
    for h in range(MEM_H):
        k = kv[:, 2 * h * MEM_HD:(2 * h + 1) * MEM_HD]
        cols.append(_rms_rows(k, kg_ref[...]))
        cols.append(kv[:, (2 * h + 1) * MEM_HD:(2 * h + 2) * MEM_HD])
    out = jnp.concatenate(cols, axis=1)
    o_ref[0] = out
    ot_ref[0] = out.T.astype(BF16)


def _mem_kv(mem, pw):
    b, t_mem, _ = mem.shape
    consts = [pw['norm_mem'], pw['w_mem_kv'], pw['mem_k_gain']]
    return pl.pallas_call(
        _mem_kv_kernel,
        grid=(b,),
        in_specs=[pl.BlockSpec((1, t_mem, D_MODEL), lambda i: (i, 0, 0))]
                 + [pl.BlockSpec(a.shape, lambda i: (0, 0)) for a in consts],
        out_specs=[pl.BlockSpec((1, t_mem, 2 * MEM_W), lambda i: (i, 0, 0)),
                   pl.BlockSpec((1, 2 * MEM_W, t_mem), lambda i: (i, 0, 0))],
        out_shape=[jax.ShapeDtypeStruct((b, t_mem, 2 * MEM_W), F32),
                   jax.ShapeDtypeStruct((b, 2 * MEM_W, t_mem), BF16)],
        compiler_params=_cparams(("arbitrary",), VMEM_LIMIT),
        name="mem_kv",
    )(mem, *consts)


def _mem_prompt_kernel(qt_ref, kv_ref, kvt_ref, o_ref):
    for h in range(MEM_H):
        k = kv_ref[0, :, 2 * h * MEM_HD:(2 * h + 1) * MEM_HD].astype(BF16)
        s = _dot(k, qt_ref[0, h * MEM_HD:(h + 1) * MEM_HD, :])
        p = jnp.exp(s - jnp.max(s, axis=0, keepdims=True))
        vt = kvt_ref[0, (2 * h + 1) * MEM_HD:(2 * h + 2) * MEM_HD, :]
        ot = _dot(vt, p.astype(BF16)) * (1.0 / jnp.sum(p, axis=0, keepdims=True))
        o_ref[:, h * MEM_HD:(h + 1) * MEM_HD] = ot.T


def _mem_prompt(mq_t, kvm, kvm_t, b, s, tq):
    nq = s // tq
    t_mem = kvm.shape[1]
    return pl.pallas_call(
        _mem_prompt_kernel,
        grid=(b, nq),
        in_specs=[pl.BlockSpec((1, MEM_W, tq), lambda bi, i: (bi, 0, i)),
                  pl.BlockSpec((1, t_mem, 2 * MEM_W), lambda bi, i: (bi, 0, 0)),
                  pl.BlockSpec((1, 2 * MEM_W, t_mem), lambda bi, i: (bi, 0, 0))],
        out_specs=pl.BlockSpec((tq, MEM_W), lambda bi, i: (bi * nq + i, 0)),
        out_shape=jax.ShapeDtypeStruct((b * s, MEM_W), F32),
        compiler_params=_cparams(("arbitrary",) * 2, VMEM_LIMIT),
        name="mem_attend_prompt",
    )(mq_t, kvm, kvm_t)


def _decode_attend(qa, blocks, new_row=None, new_bias=None, feature_major=False):
    qk, pv = (_dot, _dot_nt) if feature_major else (_dot_nt, _dot)
    scores = []
    for load, bias in blocks:
        s = qk(qa, load())
        scores.append(s if bias is None else s + bias)
    m = functools.reduce(jnp.maximum, [jnp.max(s, axis=-1, keepdims=True) for s in scores])
    if new_row is not None:
        nb = new_row.astype(BF16).astype(F32)
        s_new = jnp.sum(qa.astype(F32) * nb, axis=-1, keepdims=True) + new_bias
        m = jnp.maximum(m, s_new)
    ps = [jnp.exp(s - m) for s in scores]
    l = functools.reduce(lambda a, c: a + c, [jnp.sum(p, axis=-1, keepdims=True) for p in ps])
    acc = functools.reduce(lambda a, c: a + c, [pv(p.astype(BF16), load()) for p, (load, _) in zip(ps, blocks)])
    if new_row is not None:
        p_new = jnp.exp(s_new - m)
        l = l + p_new
        acc = acc + p_new.astype(BF16).astype(F32) * nb
    return acc / l


def _group_values(o):
    row = _iota((NSA_H, LANES), 0)
    both = jnp.where(row < NSA_HPG, o[:, 0:LANES], o[:, LANES:2 * LANES])
    return both[:, HD:2 * HD]


def _slc_sample_kernel(pt_ref, *refs, n_pages):
    pages = refs[:n_pages]
    q_ref, sel_ref, new_ref, o_ref = refs[n_pages:]
    sb = sel_ref[0]
    lane = _iota((NSA_H, PAGE_SIZE), 1)
    blocks = []
    for p, page in enumerate(pages):
        bias = jnp.where(lane < SLC_BLOCK, sb[:, 2 * p:2 * p + 1], sb[:, 2 * p + 1:2 * p + 2])
        blocks.append((lambda page=page: page[0].astype(BF16), bias))
    n_past = n_pages * (PAGE_SIZE // SLC_BLOCK)
    o = _decode_attend(q_ref[0], blocks, new_ref[0], sb[:, n_past:n_past + 1], feature_major=True)
    o_ref[0] = _group_values(o)


def _selected_pages(sel, page_table):
    db, n_pages = page_table.shape
    bpp = PAGE_SIZE // SLC_BLOCK
    n_past = n_pages * bpp
    n_slots = min(n_pages, NSA_G * N_SEL)
    picked = jnp.any(sel[:, :, :n_past] == 0.0, axis=1)
    page_any = jnp.any(picked.reshape(db, n_pages, bpp), axis=-1)
    order = jnp.argsort(jnp.logical_not(page_any), axis=1, stable=True)[:, :n_slots]
    slot_pages = jnp.take_along_axis(page_table, order, axis=1)
    blocks = (order[:, :, None] * bpp + jnp.arange(bpp)).reshape(db, 1, n_slots * bpp)
    bias = jnp.take_along_axis(sel, jnp.broadcast_to(blocks, (db, NSA_H, n_slots * bpp)), axis=2)
    bias = jnp.concatenate([bias, sel[:, :, n_past:n_past + 1]], axis=2)
    width = -(-bias.shape[2] // LANES) * LANES
    return slot_pages, jnp.pad(bias, ((0, 0), (0, 0), (0, width - bias.shape[2])))


def _feature_major(cache):
    n, rows = cache.shape[:2]
    return jnp.transpose(cache, (0, 2, 3, 4, 1)).reshape(n, KV_W, rows)


def _slc_sample(qaug, sel, cache, page_table, kvs_new):
    db, n_pages = page_table.shape
    nslp = sel.shape[2]
    pool = _feature_major(cache)
    page_specs = [pl.BlockSpec((1, KV_W, PAGE_SIZE), functools.partial(lambda i, pt, k: (pt[i, k], 0, 0), k=k))
                  for k in range(n_pages)]
    return pl.pallas_call(
        functools.partial(_slc_sample_kernel, n_pages=n_pages),
        grid_spec=pltpu.PrefetchScalarGridSpec(
            num_scalar_prefetch=1,
            grid=(db,),
            in_specs=page_specs + [pl.BlockSpec((1, NSA_H, KV_W), lambda i, pt: (i, 0, 0)),
                                   pl.BlockSpec((1, NSA_H, nslp), lambda i, pt: (i, 0, 0)),
                                   pl.BlockSpec((1, 1, KV_W), lambda i, pt: (i, 0, 0))],
            out_specs=pl.BlockSpec((1, NSA_H, HD), lambda i, pt: (i, 0, 0)),
        ),
        out_shape=jax.ShapeDtypeStruct((db, NSA_H, HD), F32),
        compiler_params=_cparams(("arbitrary",), VMEM_LIMIT),
        name="slc_attend_sample",
    )(page_table, *([pool] * n_pages), qaug, sel, kvs_new)


def _win_sample_kernel(q_ref, buf_ref, new_ref, newcol_ref, o_ref, win_ref, *, pos, first_pos):
    w_buf = buf_ref.shape[2]
    d = pos - (first_pos + _iota((1, w_buf), 1))
    bias = jnp.where((d >= 0) & (d <= WINDOW), 0.0, MASKED)
    for bi in range(q_ref.shape[0]):
        buf = buf_ref[bi]
        o = _decode_attend(q_ref[bi], [(lambda buf=buf: buf.astype(BF16), bias)], new_ref[bi], 0.0, feature_major=True)
        o_ref[bi] = _group_values(o)
        shifted = pltpu.roll(buf, w_buf - 1, 1)
        win_ref[bi] = jnp.where(_iota(buf.shape, 1) == w_buf - 1, newcol_ref[bi], shifted)


def _win_sample(qaug, win_buf_t, kvw_new, pos, first_pos):
    db, _, w_buf = win_buf_t.shape
    bb = _pick_tile(db, SAMPLE_SEQS_PER_STEP)
    return pl.pallas_call(
        functools.partial(_win_sample_kernel, pos=pos, first_pos=first_pos),
        grid=(db // bb,),
        in_specs=[pl.BlockSpec((bb, NSA_H, KV_W), lambda i: (i, 0, 0)),
                  pl.BlockSpec((bb, KV_W, w_buf), lambda i: (i, 0, 0)),
                  pl.BlockSpec((bb, 1, KV_W), lambda i: (i, 0, 0)),
                  pl.BlockSpec((bb, KV_W, 1), lambda i: (i, 0, 0))],
        out_specs=[pl.BlockSpec((bb, NSA_H, HD), lambda i: (i, 0, 0)),
                   pl.BlockSpec((bb, KV_W, w_buf), lambda i: (i, 0, 0))],
        out_shape=[jax.ShapeDtypeStruct((db, NSA_H, HD), F32),
                   jax.ShapeDtypeStruct((db, KV_W, w_buf), F32)],
        compiler_params=_cparams(("arbitrary",), VMEM_LIMIT),
        name="win_attend_sample",
    )(qaug, win_buf_t, kvw_new, kvw_new.reshape(db, KV_W, 1))


def _mem_sample_kernel(q_ref, kv_ref, o_ref):
    row = _iota((8, MEM_HD), 0)
    for bi in range(q_ref.shape[0]):
        q = q_ref[bi]
        out = jnp.zeros((8, MEM_HD), F32)
        for h in range(MEM_H):
            k = kv_ref[bi, :, h, 0, :].astype(BF16)
            v = kv_ref[bi, :, h, 1, :].astype(BF16)
            s = _dot_nt(q, k)
            p = jnp.exp(s - jnp.max(s, axis=-1, keepdims=True))
            o = _dot(p.astype(BF16), v) / jnp.sum(p, axis=-1, keepdims=True)
            out = jnp.where(row == h, o, out)
        o_ref[bi] = out


def _mem_sample(q8, cache_mem):
    db, t_mem = cache_mem.shape[:2]
    bb = _pick_tile(db, SAMPLE_SEQS_PER_STEP)
    return pl.pallas_call(
        _mem_sample_kernel,
        grid=(db // bb,),
        in_specs=[pl.BlockSpec((bb, 8, MEM_HD), lambda i: (i, 0, 0)),
                  pl.BlockSpec((bb, t_mem, MEM_H, 2, MEM_HD), lambda i: (i, 0, 0, 0, 0))],
        out_specs=pl.BlockSpec((bb, 8, MEM_HD), lambda i: (i, 0, 0)),
        out_shape=jax.ShapeDtypeStruct((db, 8, MEM_HD), F32),
        compiler_params=_cparams(("arbitrary",), VMEM_LIMIT),
        name="mem_attend_sample",
    )(q8, cache_mem)


def _mlstm_prompt_kernel(u_ref, oraw_ref, slab_ref, slabt_ref, cw_ref, cb_ref, wqt_ref, wk_ref, wvt_ref,
                         bgc_ref, bgr_ref, mlgt_ref, tril_ref, triu_ref,
                         o_ref, ct_ref, n_ref, m_ref, ubuf, *, chunk):
    L = chunk
    pad = 8

    @pl.when(pl.program_id(1) == 0)
    def _():
        ct_ref[...] = jnp.zeros_like(ct_ref)
        n_ref[...] = jnp.zeros_like(n_ref)
        m_ref[...] = jnp.zeros_like(m_ref)
        ubuf[0:pad, :] = jnp.zeros((pad, ML_W), F32)

    u = u_ref[0]
    ubuf[pad:pad + L, :] = u
    conv = cb_ref[...] + cw_ref[CONV_W - 1:CONV_W, :] * u
    for w in range(CONV_W - 1):
        off = pad - (CONV_W - 1) + w
        conv = conv + cw_ref[w:w + 1, :] * ubuf[off:off + L, :]
    ubuf[0:pad, :] = u[L - pad:L, :]
    cact = conv * _sigmoid(conv)

    gl = slab_ref[0] + bgc_ref[...]
    gt = slabt_ref[0] + bgr_ref[...]
    b_c = None
    for part in _split3(_log_sigmoid(gl)):
        term = _dot(tril_ref[...], part)
        b_c = term if b_c is None else b_c + term
    b_r = _dot_exact_rhs(_split3(_log_sigmoid(gt)), triu_ref[...])
    m_prev = m_ref[0]
    not_after = _iota((L, L), 0) <= _iota((L, L), 1)
    lane1 = _iota((1, LANES), 1)
    gain_t = jnp.concatenate([mlgt_ref[...]] * (L // LANES), axis=1)
    m_out = m_prev
    outs = []
    for h in range(ML_H):
        sl = slice(h * ML_D, (h + 1) * ML_D)
        br = b_r[_SLAB_F + h:_SLAB_F + h + 1, :]
        ir = gt[_SLAB_I + h:_SLAB_I + h + 1, :]
        src = gl[:, _SLAB_I + h:_SLAB_I + h + 1] - b_c[:, _SLAB_F + h:_SLAB_F + h + 1]
        m0 = m_prev[:, h:h + 1]
        dm = jnp.where(not_after, br + src, -jnp.inf)
        m_new = jnp.maximum(br + m0, jnp.max(dm, axis=0, keepdims=True))
        ch = cact[:, sl].astype(BF16)
        kb = _dot(ch, wk_ref[h]).astype(BF16)
        qt = _dot_nt(wqt_ref[h], ch) * (ML_D ** -0.5)
        vt = _dot_nt(wvt_ref[h], u[:, sl].astype(BF16))
        qtb = qt.astype(BF16)
        wmat = jnp.exp(dm - m_new) * _dot(kb, qtb)
        inter = jnp.exp(br + m0 - m_new)
        ct_old = ct_ref[0, h]
        n_old = n_ref[0, h:h + 1, :]
        num = inter * _dot(ct_old.astype(BF16), qtb) + _dot(vt.astype(BF16), wmat.astype(BF16))
        nq = _dot_exact_rhs(_split2(jnp.broadcast_to(n_old, (8, ML_D))), qtb)[0:1]
        den = inter * nq + jnp.sum(wmat, axis=0, keepdims=True)
        hh = num / jnp.maximum(jnp.abs(den), jnp.exp(-m_new))
        m_end = m_new[:, L - 1:L]
        b_end = br[:, L - 1:L]
        decay = jnp.exp(b_end + m0 - m_end)
        wend = jnp.exp(b_end - br + ir - m_end)
        ct_ref[0, h] = decay * ct_old + _dot((vt * wend).astype(BF16), kb)
        n_ref[0, h:h + 1, :] = decay * n_old + _dot_exact_rhs(_split2(jnp.broadcast_to(wend, (8, L))), kb)[0:1]
        m_out = jnp.where(lane1 == h, m_end, m_out)
        hn = hh * lax.rsqrt(jnp.mean(hh * hh, axis=0, keepdims=True) + EPS) * gain_t[sl, :]
        outs.append(_sigmoid(oraw_ref[0, :, sl]) * hn.T)
    o_ref[0] = jnp.concatenate(outs, axis=1)
    m_ref[0] = m_out


def _mlstm_prompt(u, oraw, slab, pw, chunk):
    b, s, _ = u.shape
    nc = s // chunk
    slabt = jnp.swapaxes(slab[:, :, 0:8], 1, 2)
    tril = jnp.tril(jnp.ones((chunk, chunk), BF16))
    consts = [pw['conv_w'], pw['conv_b'], pw['w_ml_q_t'], pw['w_ml_k'], pw['w_ml_v_t'], pw['gate_bias_lanes'],
              pw['gate_bias_rows'], pw['ml_gain_t'], tril, tril.T]
    blk = lambda w: pl.BlockSpec((1, chunk, w), lambda bi, c: (bi, c, 0))

    def const_spec(a):
        nd = a.ndim
        return pl.BlockSpec(a.shape, lambda bi, c: (0,) * nd)

    return pl.pallas_call(
        functools.partial(_mlstm_prompt_kernel, chunk=chunk),
        grid=(b, nc),
        in_specs=[blk(ML_W), blk(ML_W), blk(LANES), pl.BlockSpec((1, 8, chunk), lambda bi, c: (bi, 0, c))]
                 + [const_spec(a) for a in consts],
        out_specs=[blk(ML_W),
                   pl.BlockSpec((1, ML_H, ML_D, ML_D), lambda bi, c: (bi, 0, 0, 0)),
                   pl.BlockSpec((1, ML_H, ML_D), lambda bi, c: (bi, 0, 0)),
                   pl.BlockSpec((1, 1, LANES), lambda bi, c: (bi, 0, 0))],
        out_shape=[jax.ShapeDtypeStruct((b, s, ML_W), F32),
                   jax.ShapeDtypeStruct((b, ML_H, ML_D, ML_D), F32),
                   jax.ShapeDtypeStruct((b, ML_H, ML_D), F32),
                   jax.ShapeDtypeStruct((b, 1, LANES), F32)],
        scratch_shapes=[pltpu.VMEM((chunk + 8, ML_W), F32)],
        compiler_params=_cparams(("arbitrary", "arbitrary"), VMEM_LIMIT),
        name="mlstm_prompt",
    )(u, oraw, slab, slabt, *consts)


def _mlstm_sample_kernel(ext_ref, oraw_ref, slab_ref, c_ref, n_ref, m_ref, cw_ref, cb_ref, wq_ref, wk_ref,
                         wv_ref, wkt_ref, bgc_ref, mlg_ref, o_ref, co_ref, no_ref, mo_ref):
    for bi in range(ext_ref.shape[0]):
        _mlstm_sample_step(bi, ext_ref, oraw_ref, slab_ref, c_ref, n_ref, m_ref, cw_ref, cb_ref, wq_ref, wk_ref,
                           wv_ref, wkt_ref, bgc_ref, mlg_ref, o_ref, co_ref, no_ref, mo_ref)


def _mlstm_sample_step(bi, ext_ref, oraw_ref, slab_ref, c_ref, n_ref, m_ref, cw_ref, cb_ref, wq_ref, wk_ref,
                       wv_ref, wkt_ref, bgc_ref, mlg_ref, o_ref, co_ref, no_ref, mo_ref):
    ext = ext_ref[bi]
    conv = cb_ref[...]
    for w in range(CONV_W):
        conv = conv + cw_ref[w:w + 1, :] * ext[w:w + 1, :]
    cact = conv * _sigmoid(conv)
    u = ext[CONV_W - 1:CONV_W, :]
    gl = slab_ref[bi] + bgc_ref[...]
    lf_all = _log_sigmoid(gl)
    m_prev = m_ref[bi]
    lane1 = _iota((1, LANES), 1)
    m_out = m_prev
    outs = []
    for h in range(ML_H):
        sl = slice(h * ML_D, (h + 1) * ML_D)
        ch8 = jnp.broadcast_to(cact[:, sl], (8, ML_D)).astype(BF16)
        u8 = jnp.broadcast_to(u[:, sl], (8, ML_D)).astype(BF16)
        q = (_dot(ch8, wq_ref[h]) * (ML_D ** -0.5))[0:1]
        k = _dot(ch8, wk_ref[h])[0:1]
        v = _dot(u8, wv_ref[h])[0:1]
        k_col = _dot_nt(wkt_ref[h], ch8)[:, 0:1]
        ig = gl[:, _SLAB_I + h:_SLAB_I + h + 1]
        lf = lf_all[:, _SLAB_F + h:_SLAB_F + h + 1]
        m0 = m_prev[:, h:h + 1]
        m_new = jnp.maximum(lf + m0, ig)
        qb = q.astype(BF16)
        qk = jnp.sum(qb.astype(F32) * k.astype(BF16).astype(F32), axis=-1, keepdims=True)
        w_in = jnp.exp(ig - m_new) * qk
        inter = jnp.exp(lf + m0 - m_new)
        c_old = c_ref[bi, h]
        n_old = n_ref[bi, h:h + 1, :]
        qc = _dot(jnp.broadcast_to(qb, (8, ML_D)), c_old.astype(BF16))[0:1]
        num = inter * qc + w_in * v
        den = inter * jnp.sum(q * n_old, axis=-1, keepdims=True) + w_in
        hh = num / jnp.maximum(jnp.abs(den), jnp.exp(-m_new))
        w_end = jnp.exp(ig - m_new)
        co_ref[bi, h] = inter * c_old + k_col * (w_end * v)
        no_ref[bi, h:h + 1, :] = inter * n_old + w_end * k
        m_out = jnp.where(lane1 == h, m_new, m_out)
        hn = hh * lax.rsqrt(jnp.mean(hh * hh, axis=-1, keepdims=True) + EPS) * mlg_ref[:, sl]
        outs.append(_sigmoid(oraw_ref[bi, :, sl]) * hn)
    o_ref[bi] = jnp.concatenate(outs, axis=1)
    mo_ref[bi] = m_out


def _mlstm_sample(ext, oraw, slab, state_c, state_n, state_m, pw):
    db = ext.shape[0]
    m_in = jnp.pad(state_m, ((0, 0), (0, LANES - ML_H))).reshape(db, 1, LANES)
    consts = [pw['conv_w'], pw['conv_b'], pw['w_ml_q'], pw['w_ml_k'], pw['w_ml_v'], pw['w_ml_k_t'],
              pw['gate_bias_lanes'], pw['ml_gain']]

    def const_spec(a):
        nd = a.ndim
        return pl.BlockSpec(a.shape, lambda i: (0,) * nd)

    bb = _pick_tile(db, SAMPLE_SEQS_PER_STEP)
    per_seq = lambda shape: pl.BlockSpec((bb,) + shape, lambda i: (i,) + (0,) * len(shape))
    return pl.pallas_call(
        _mlstm_sample_kernel,
        grid=(db // bb,),
        in_specs=[per_seq((CONV_W, ML_W)), per_seq((1, ML_W)), per_seq((1, LANES)), per_seq((ML_H, ML_D, ML_D)),
                  per_seq((ML_H, ML_D)), per_seq((1, LANES))] + [const_spec(a) for a in consts],
        out_specs=[per_seq((1, ML_W)), per_seq((ML_H, ML_D, ML_D)), per_seq((ML_H, ML_D)), per_seq((1, LANES))],
        out_shape=[jax.ShapeDtypeStruct((db, 1, ML_W), F32),
                   jax.ShapeDtypeStruct((db, ML_H, ML_D, ML_D), F32),
                   jax.ShapeDtypeStruct((db, ML_H, ML_D), F32),
                   jax.ShapeDtypeStruct((db, 1, LANES), F32)],
        compiler_params=_cparams(("arbitrary",), VMEM_LIMIT),
        name="mlstm_sample",
    )(ext, oraw, slab, state_c, state_n, m_in, *consts)


def _merge_kernel(x_ref, nm_ref, wg_ref, slab_ref, ex_ref, ocmp_ref, oslc_ref, owin_ref, oml_ref, omem_ref,
                  wb_ref, wout_ref, o_ref):
    x = x_ref[...]
    xn = _rms_rows(x, nm_ref[...]).astype(BF16)
    gparts = _split2(_sigmoid(slab_ref[...]))
    onsa = None
    for br, br_ref in enumerate((ocmp_ref, oslc_ref, owin_ref)):
        term = _dot_exact_rhs(gparts, ex_ref[br]) * br_ref[...]
        onsa = term if onsa is None else onsa + term
    z = None
    for n, on in enumerate((onsa, oml_ref[...], omem_ref[...])):
        gate = _sigmoid(_dot(xn, wg_ref[:, n * D_MODEL:(n + 1) * D_MODEL]))
        term = gate * _dot(on.astype(BF16), wb_ref[n])
        z = term if z is None else z + term
    o_ref[...] = x + _dot(z.astype(BF16), wout_ref[...])


def _merge(x2d, slab, ocmp, oslc, owin, oml, omem, pw, tm):
    t = x2d.shape[0]
    row = lambda w: pl.BlockSpec((tm, w), lambda i: (i, 0))
    const_spec = lambda a: _resident_spec(a)
    return pl.pallas_call(
        _merge_kernel,
        grid=(t // tm,),
        in_specs=[row(D_MODEL), const_spec(pw['norm_mix']), const_spec(pw['w_gate']), row(LANES),
                  const_spec(pw['gate_expand'])] + [row(BRANCH_W)] * 5
                 + [const_spec(pw['w_branch']), const_spec(pw['w_out'])],
        out_specs=row(D_MODEL),
        out_shape=jax.ShapeDtypeStruct((t, D_MODEL), F32),
        compiler_params=_cparams(("arbitrary",), VMEM_LIMIT),
        name="merge",
    )(x2d, pw['norm_mix'], pw['w_gate'], slab, pw['gate_expand'], ocmp, oslc, owin, oml, omem,
      pw['w_branch'], pw['w_out'])


def _ffn_kernel(x_ref, nf_ref, win_ref, wout_ref, o_ref):
    x = x_ref[...]
    xn = _rms_rows(x, nf_ref[...]).astype(BF16)
    a = _dot(xn, win_ref[:, 0:FFN_HID])
    b = _dot(xn, win_ref[:, FFN_HID:2 * FFN_HID])
    o_ref[...] = x + _dot((a * _sigmoid(a) * b).astype(BF16), wout_ref[...])


def _ffn(x2d, pw, tm):
    t = x2d.shape[0]
    consts = [pw['norm_ffn'], pw['w_ffn_in'], pw['w_ffn_out']]
    return pl.pallas_call(
        _ffn_kernel,
        grid=(t // tm,),
        in_specs=[pl.BlockSpec((tm, D_MODEL), lambda i: (i, 0))] + [_resident_spec(a) for a in consts],
        out_specs=pl.BlockSpec((tm, D_MODEL), lambda i: (i, 0)),
        out_shape=jax.ShapeDtypeStruct((t, D_MODEL), F32),
        compiler_params=_cparams(("arbitrary",), VMEM_LIMIT),
        name="ffn",
    )(x2d, *consts)


def _rope_tables(pos):
    half = ROT_DIM // 2
    f32 = np.float32
    freqs = ROPE_THETA ** (-np.arange(half, dtype=np.float64) / half)
    ang = pos.astype(np.float64)[:, None] * freqs
    cos, sin = np.cos(ang).astype(f32), np.sin(ang).astype(f32)
    n = pos.shape[0]
    ones = np.ones((n, HD - ROT_DIM), f32)
    zeros = np.zeros((n, HD - ROT_DIM), f32)
    zh = np.zeros((n, half), f32)
    ct = np.concatenate([cos, cos, ones], axis=1)
    sa = np.concatenate([-sin, zh, zeros], axis=1)
    sb = np.concatenate([zh, sin, zeros], axis=1)
    one64, zero64 = np.ones((n, HD), f32), np.zeros((n, HD), f32)
    q_tabs = [np.concatenate([t, t], axis=1) for t in (ct, sa, sb)]
    k_tabs = [np.concatenate([ct, one64], axis=1), np.concatenate([sa, zero64], axis=1),
              np.concatenate([sb, zero64], axis=1)]
    return [jnp.asarray(t) for t in q_tabs + k_tabs]


def _prepare_weights(norm_mix, w_in, q_norm, k_norm_cmp, k_norm_slc, k_norm_win, cmp_pe, cmp_w1, cmp_b1, cmp_w2,
                     conv_w, conv_b, w_ml_q, w_ml_k, w_ml_v, b_igate, b_fgate, ml_norm, norm_mem, w_mem_kv,
                     mem_q_norm, mem_k_norm, w_branch, w_out, norm_ffn, w_ffn_in, w_ffn_out):
    widths = (NSA_H * HD, 3 * NSA_H, KV_W, KV_W, KV_W, ML_W, ML_W, ML_H, ML_H, MEM_W, N_BRANCH * D_MODEL)
    offs = np.concatenate([[0], np.cumsum(widths)])
    q, g, kvc, kvs, kvw, u, o, ig, fg, mq, mg = (w_in[:, int(offs[i]):int(offs[i + 1])] for i in range(len(widths)))
    slab_pad = jnp.zeros((D_MODEL, LANES - 2 * ML_H - 3 * NSA_H), F32)
    pw = {}
    pw['w_proj'] = jnp.concatenate([q, kvc, kvs, kvw, u, o, mq, ig, fg, g, slab_pad], axis=1).astype(BF16)
    pw['w_gate'] = mg.astype(BF16)
    pw['norm_mix'] = norm_mix.reshape(1, D_MODEL)
    pw['q_gain'] = jnp.tile(q_norm, NSA_H).reshape(1, NSA_H * HD)
    ones = jnp.ones((HD,), F32)
    pw['ks_gain'] = jnp.tile(jnp.concatenate([k_norm_slc, ones]), NSA_G).reshape(1, KV_W)
    pw['kw_gain'] = jnp.tile(jnp.concatenate([k_norm_win, ones]), NSA_G).reshape(1, KV_W)
    pw['kc_gain'] = jnp.concatenate([k_norm_cmp, ones]).reshape(1, LANES)
    pw['mq_gain'] = jnp.tile(mem_q_norm, MEM_H).reshape(1, MEM_W)
    head_of = np.arange(NSA_H * HD) // HD
    pw['seg64'] = jnp.asarray(head_of[:, None] == head_of[None, :], BF16)
    w1 = cmp_w1.reshape(2, CMP_BLOCK, HD, CMP_HID)
    eye = jnp.eye(2, dtype=F32)
    for name, part in (('cmp_wlo', w1[:, :CMP_STRIDE]), ('cmp_whi', w1[:, CMP_STRIDE:])):
        pw[name] = jnp.einsum('cjdh,ce->jcdeh', part, eye).reshape(CMP_STRIDE * 2 * HD, 2 * CMP_HID).astype(BF16)
    pw['cmp_w2'] = jnp.einsum('chd,ce->ched', cmp_w2, eye).reshape(2 * CMP_HID, 2 * HD).astype(BF16)
    pw['cmp_b1'] = cmp_b1.reshape(1, 2 * CMP_HID)
    pw['pe_lo'] = cmp_pe[:CMP_STRIDE].reshape(1, CMP_STRIDE * 2 * HD)
    pw['pe_hi'] = cmp_pe[CMP_STRIDE:].reshape(1, CMP_STRIDE * 2 * HD)
    pw['conv_w'] = conv_w
    pw['conv_b'] = conv_b.reshape(1, ML_W)
    pw['w_ml_q'], pw['w_ml_k'], pw['w_ml_v'] = (w.astype(BF16) for w in (w_ml_q, w_ml_k, w_ml_v))
    pw['w_ml_q_t'], pw['w_ml_k_t'], pw['w_ml_v_t'] = (jnp.swapaxes(w, 1, 2).astype(BF16) for w in (w_ml_q, w_ml_k, w_ml_v))
    pw['ml_gain_t'] = jnp.broadcast_to(ml_norm.reshape(ML_W, 1), (ML_W, LANES))
    gate_bias = jnp.concatenate([b_igate, b_fgate])
    pw['gate_bias_lanes'] = jnp.pad(gate_bias, (0, LANES - 2 * ML_H)).reshape(1, LANES)
    pw['gate_bias_rows'] = gate_bias.reshape(2 * ML_H, 1)
    pw['ml_gain'] = ml_norm.reshape(1, ML_W)
    pw['norm_mem'] = norm_mem.reshape(1, D_MODEL)
    pw['w_mem_kv'] = w_mem_kv.astype(BF16)
    pw['mem_k_gain'] = mem_k_norm.reshape(1, MEM_HD)
    ex = np.zeros((N_BRANCH, LANES, NSA_H * HD), np.float32)
    for br in range(N_BRANCH):
        for gh in range(NSA_H):
            ex[br, _SLAB_G + br * NSA_H + gh, gh * HD:(gh + 1) * HD] = 1.0
    pw['gate_expand'] = jnp.asarray(ex, BF16)
    pw['w_branch'] = w_branch.astype(BF16)
    pw['w_out'] = w_out.astype(BF16)
    pw['norm_ffn'] = norm_ffn.reshape(1, D_MODEL)
    pw['w_ffn_in'] = w_ffn_in.astype(BF16)
    pw['w_ffn_out'] = w_ffn_out.astype(BF16)
    return pw


def _overlap(n_cmp_rows, n_slc_cols):
    ci = np.arange(n_cmp_rows)[:, None] * CMP_STRIDE
    sj = np.arange(n_slc_cols)[None, :] * SLC_BLOCK
    return ((ci < sj + SLC_BLOCK) & (ci + CMP_BLOCK > sj)).astype(np.float32)


def _pick_tile(n, pref):
    t = min(n, pref)
    while n % t:
        t //= 2
    return t


def _prompt_group(x, mem, pw):
    b, s, _ = x.shape
    t = b * s
    x2d = x.reshape(t, D_MODEL)
    tm = _pick_tile(s, TOKEN_TILE)
    tabs = _rope_tables(np.arange(s))
    (kvc, kvs, kvw, u, oraw, slab,
     qrot_t, kvc_t, kvs_t, kvw_t, qn_t, mq_t) = _project(x2d, tabs, pw, tm, seq_len=s)
    kvcmp, kvcmp_t = _compress_prompt(kvc.reshape(b, s, KV_W), pw)
    r = s // CMP_STRIDE
    ovt = jnp.asarray(_overlap(r, HD).T, BF16)
    ocmp, sel_t = _cmp_prompt(qn_t, kvcmp, kvcmp_t, ovt, b, s, _pick_tile(s, CMP_TQ))
    fq, fk = _pick_tile(s, FLASH_TQ), _pick_tile(s, FLASH_TK)
    oslc = _flash_prompt(qrot_t, kvs.reshape(b, s, KV_W), kvs_t, sel_t, b, s, fq, fk)
    owin = _flash_prompt(qrot_t, kvw.reshape(b, s, KV_W), kvw_t, None, b, s, fq, fk)
    chunk = _pick_tile(s, MLSTM_CHUNK)
    oml, c_t, n_new, m_new = _mlstm_prompt(u.reshape(b, s, ML_W), oraw.reshape(b, s, ML_W),
                                             slab.reshape(b, s, LANES), pw, chunk)
    t_mem = mem.shape[1]
    kvm, kvm_t = _mem_kv(mem, pw)
    omem = _mem_prompt(mq_t, kvm, kvm_t, b, s, _pick_tile(s, MEM_TQ))
    x1 = _merge(x2d, slab, ocmp, oslc, owin, oml.reshape(t, ML_W), omem, pw, tm)
    y = _ffn(x1, pw, tm).reshape(b, s, D_MODEL)
    kv5 = lambda a_t: jnp.transpose(a_t.reshape(b, NSA_G, 2, HD, a_t.shape[-1]), (0, 4, 1, 2, 3))
    return dict(y=y, kv_cmp=kv5(kvc_t), kv_slc=kv5(kvs_t), kv_win_t=kvw_t, kv5=kv5,
                kv_mem=kvm.reshape(b, t_mem, MEM_H, 2, MEM_HD), c=jnp.swapaxes(c_t, 2, 3), n=n_new,
                m=m_new[:, 0, :ML_H],
                u=u.reshape(b, s, ML_W))


def _sample_group(x, cache_cmp, cache_slc, cache_win, cache_mem, state_c, state_n, state_m, state_conv,
                  page_table, pw):
    db, ds, _ = x.shape
    assert ds == 1, "the sample kernels handle one new token per sequence"
    n_pages = page_table.shape[1]
    past = n_pages * PAGE_SIZE
    pos = past
    x2d = x.reshape(db, D_MODEL)
    tabs = _rope_tables(np.full((db,), pos))
    kvc, kvs, kvw, u, oraw, slab, qn, qrot, mq = _project(x2d, tabs, pw, db)
    kvcmp = _compress_paged(cache_cmp, page_table, kvc.reshape(db, 1, KV_W), pw)
    n_cmp = kvcmp.shape[2]
    n_slc = -(-(past + ds) // SLC_BLOCK)
    nslp = -(-n_slc // LANES) * LANES
    ov = jnp.asarray(_overlap(n_cmp, nslp), BF16)
    ocmp, sel = _cmp_sample(qn.reshape(db, NSA_H, HD), kvcmp, ov, pos, n_slc)
    q3 = qrot.reshape(db, NSA_G, NSA_HPG, 1, HD)
    qaug = (q3 * jnp.eye(NSA_G, dtype=BF16)[None, :, None, :, None]).reshape(db, NSA_H, NSA_G, HD)
    qaug = jnp.concatenate([qaug, jnp.zeros_like(qaug)], axis=-1).reshape(db, NSA_H, KV_W)
    slot_pages, slot_bias = _selected_pages(sel, page_table)
    oslc = _slc_sample(qaug, slot_bias, cache_slc, slot_pages, kvs.reshape(db, 1, KV_W))
    w_buf = cache_win.shape[1]
    owin, win_new_t = _win_sample(qaug, _feature_major(cache_win), kvw.reshape(db, 1, KV_W), pos, past - w_buf)
    win_new = jnp.transpose(win_new_t.reshape(db, NSA_G, 2, HD, w_buf), (0, 4, 1, 2, 3))
    ext = jnp.concatenate([state_conv, u.reshape(db, 1, ML_W)], axis=1)
    oml, c_new, n_new, m_new = _mlstm_sample(ext, oraw.reshape(db, 1, ML_W), slab.reshape(db, 1, LANES),
                                             state_c, state_n, state_m, pw)
    q8 = jnp.pad(mq.reshape(db, MEM_H, MEM_HD), ((0, 0), (0, 8 - MEM_H), (0, 0)))
    omem = _mem_sample(q8, cache_mem)[:, :MEM_H].reshape(db, MEM_W)
    x1 = _merge(x2d, slab, ocmp.reshape(db, NSA_H * HD), oslc.reshape(db, NSA_H * HD), owin.reshape(db, NSA_H * HD),
                oml.reshape(db, ML_W), omem, pw, db)
    y = _ffn(x1, pw, db).reshape(db, ds, D_MODEL)
    kv5 = lambda a: a.reshape(db, ds, NSA_G, 2, HD)
    return dict(y=y, kv_cmp=kv5(kvc), kv_slc=kv5(kvs), win=win_new,
                c=c_new, n=n_new, m=m_new[:, 0, :ML_H], conv=ext[:, 1:])


def kernel(x_prompt, x_sample, cache_kv_cmp, cache_kv_slc, cache_kv_win, cache_kv_mem, state_C, state_n, state_m, state_conv, page_table, mem_prompt, norm_mix, w_in, q_norm, k_norm_cmp, k_norm_slc, k_norm_win, cmp_pe, cmp_w1, cmp_b1, cmp_w2, conv_w, conv_b, w_ml_q, w_ml_k, w_ml_v, b_igate, b_fgate, ml_norm, norm_mem, w_mem_kv, mem_q_norm, mem_k_norm, w_branch, w_out, norm_ffn, w_ffn_in, w_ffn_out):
    pw = _prepare_weights(norm_mix, w_in, q_norm, k_norm_cmp, k_norm_slc, k_norm_win, cmp_pe, cmp_w1, cmp_b1, cmp_w2,
                          conv_w, conv_b, w_ml_q, w_ml_k, w_ml_v, b_igate, b_fgate, ml_norm, norm_mem, w_mem_kv,
                          mem_q_norm, mem_k_norm, w_branch, w_out, norm_ffn, w_ffn_in, w_ffn_out)
    p = _prompt_group(x_prompt, mem_prompt, pw)
    s = _sample_group(x_sample, cache_kv_cmp, cache_kv_slc, cache_kv_win, cache_kv_mem, state_C, state_n, state_m,
                      state_conv, page_table, pw)
    b, seq = x_prompt.shape[:2]
    w_buf = cache_kv_win.shape[1]
    assert seq >= w_buf
    win_p = p['kv5'](p['kv_win_t'][:, :, seq - w_buf:])
    conv_p = p['u'][:, seq - (CONV_W - 1):]
    return (p['y'], s['y'], p['kv_cmp'], s['kv_cmp'], p['kv_slc'], s['kv_slc'], win_p, s['win'], p['kv_mem'],
            p['c'], s['c'], p['n'], s['n'], p['m'], s['m'], conv_p, s['conv'])
```

```python
import functools

import numpy as np
import jax
import jax.numpy as jnp
from jax import lax
from jax.experimental import pallas as pl
from jax.experimental.pallas import tpu as pltpu

F32 = jnp.float32
BF16 = jnp.bfloat16

D_MODEL = 1024
PAGE_SIZE = 128
NSA_H, NSA_G, NSA_HPG, HD = 8, 2, 4, 64
ROT_DIM = HD // 4
ROPE_THETA = 500000.0
CMP_BLOCK, CMP_STRIDE, CMP_HID = 32, 16, 128
SLC_BLOCK, N_SEL = 64, 16
WINDOW = 512
ML_H, ML_D, ML_W, CONV_W = 4, 128, 512, 4
MEM_H, MEM_HD, MEM_W = 4, 128, 512
N_BRANCH, BRANCH_W = 3, 512
FFN_HID = -(-8 * D_MODEL // (3 * 256)) * 256
EPS = 1e-6

KV_W = 2 * NSA_G * HD
CHUNK_W = CMP_STRIDE * KV_W
LANES = 128
MASKED = -1e30
UNSELECTED = -32768.0
VMEM_LIMIT = 56 * 1024 * 1024
TOKEN_TILE = 512
CMP_TQ = 1024
MEM_TQ = 1024
FLASH_TQ, FLASH_TK = 512, 256
SLC_TQ = 1024
FLASH_COLS = 128
FLASH_ONES_ROWS = 64
MLSTM_CHUNK = 512
SAMPLE_SEQS_PER_STEP = 8

_Q0, _KVC0, _KVS0, _KVW0, _U0, _O0, _MQ0, _SLAB0, _PROJ_W = 0, 512, 768, 1024, 1280, 1792, 2304, 2816, 2944
_SLAB_I, _SLAB_F, _SLAB_G = 0, 4, 8


def _dot(a, b):
    return jnp.dot(a, b, preferred_element_type=F32)


def _dot_nt(a, b):
    return lax.dot_general(a, b, (((1,), (1,)), ((), ())), preferred_element_type=F32)


def _split2(x):
    hi = x.astype(BF16)
    return hi, (x - hi.astype(F32)).astype(BF16)


def _split3(x):
    hi = x.astype(BF16)
    r = x - hi.astype(F32)
    mid = r.astype(BF16)
    return hi, mid, (r - mid.astype(F32)).astype(BF16)


def _dot_exact_rhs(parts, m):
    out = _dot(parts[0], m)
    for p in parts[1:]:
        out = out + _dot(p, m)
    return out


def _dot_exact_rhs_left(m, parts):
    out = _dot(m, parts[0])
    for p in parts[1:]:
        out = out + _dot(m, p)
    return out


def _sigmoid(x):
    return 1.0 / (1.0 + jnp.exp(-x))


def _log_sigmoid(x):
    return jnp.minimum(x, 0.0) - jnp.log(1.0 + jnp.exp(-jnp.abs(x)))


def _gelu_tanh(x):
    return x * (0.5 * (1.0 + jnp.tanh(np.sqrt(2.0 / np.pi).astype(np.float32) * (x + 0.044715 * (x * x * x)))))


def _rms_rows(x, gain):
    ms = jnp.mean(x * x, axis=-1, keepdims=True)
    return x * lax.rsqrt(ms + EPS) * gain


def _iota(shape, dim):
    return lax.broadcasted_iota(jnp.int32, shape, dim)


def _cparams(sem, vmem=None):
    return pltpu.CompilerParams(dimension_semantics=sem, vmem_limit_bytes=vmem)


def _resident_spec(a):
    nd = a.ndim
    return pl.BlockSpec(a.shape, lambda *_: (0,) * nd, pipeline_mode=pl.Buffered(1))


def _masked_softmax(s, mask):
    s = jnp.where(mask, s, -jnp.inf)
    m = jnp.max(s, axis=-1, keepdims=True)
    m = jnp.where(m > -jnp.inf, m, 0.0)
    p = jnp.exp(s - m)
    return p / jnp.maximum(jnp.sum(p, axis=-1, keepdims=True), jnp.finfo(F32).tiny)


def _rs_heads64(a, seg_ref, width):
    ss = _dot_exact_rhs(_split2(a * a), seg_ref[0:width, 0:width])
    return lax.rsqrt(ss * (1.0 / HD) + EPS)


def _rope(a, ct, sa, sb, width):
    reps = width // LANES
    if reps > 1:
        ct, sa, sb = (jnp.concatenate([t] * reps, axis=1) for t in (ct, sa, sb))
    half = ROT_DIM // 2
    return a * ct + pltpu.roll(a, width - half, 1) * sa + pltpu.roll(a, half, 1) * sb


def _proj_kernel(x_ref, nm_ref, w_ref, qg_ref, ksg_ref, kwg_ref, mqg_ref, seg_ref,
                 ctq_ref, saq_ref, sbq_ref, ctk_ref, sak_ref, sbk_ref,
                 kvc_o, kvs_o, kvw_o, u_o, o_o, slab_o, *rest):
    fm_outs = rest if len(rest) == 6 else ()
    xn = _rms_rows(x_ref[...], nm_ref[...]).astype(BF16)

    def seg(lo, hi):
        return _dot(xn, w_ref[:, lo:hi])

    q = seg(_Q0, _KVC0)
    qn = q * _rs_heads64(q, seg_ref, NSA_H * HD) * qg_ref[...]
    scale = HD ** -0.5
    qr = _rope(qn, ctq_ref[...], saq_ref[...], sbq_ref[...], NSA_H * HD) * scale
    kvc = seg(_KVC0, _KVS0)
    kvc_o[...] = kvc
    if fm_outs:
        fm_outs[0][0] = qr.T.astype(BF16)
        fm_outs[1][0] = kvc.T
        fm_outs[4][0] = (qn * scale).T.astype(BF16)
    else:
        rest[0][...] = (qn * scale).astype(BF16)
        rest[1][...] = qr.astype(BF16)
    for n, (lo, g_ref, o_ref) in enumerate(((_KVS0, ksg_ref, kvs_o), (_KVW0, kwg_ref, kvw_o))):
        a = seg(lo, lo + KV_W)
        is_k = (_iota(a.shape, 1) & HD) == 0
        an = jnp.where(is_k, a * _rs_heads64(a, seg_ref, KV_W) * g_ref[...], a)
        ar = _rope(an, ctk_ref[...], sak_ref[...], sbk_ref[...], KV_W)
        o_ref[...] = ar
        if fm_outs:
            fm_outs[2 + n][0] = ar.T
    u_o[...] = seg(_U0, _O0)
    o_o[...] = seg(_O0, _MQ0)
    mq = seg(_MQ0, _SLAB0)
    heads = []
    for h in range(MEM_H):
        mh = mq[:, h * MEM_HD:(h + 1) * MEM_HD]
        heads.append(mh * lax.rsqrt(jnp.mean(mh * mh, axis=-1, keepdims=True) + EPS))
    mqn = jnp.concatenate(heads, axis=1) * mqg_ref[...] * (MEM_HD ** -0.5)
    if fm_outs:
        fm_outs[5][0] = mqn.T.astype(BF16)
    else:
        rest[2][...] = mqn.astype(BF16)
    slab_o[...] = seg(_SLAB0, _PROJ_W)


def _project(x2d, tabs, pw, tm, seq_len=None):
    t = x2d.shape[0]
    ntab = tabs[0].shape[0] // tm
    row = lambda i: (i, 0)
    const = lambda i: (0, 0)
    tab = lambda i: (i % ntab, 0)
    full = lambda a: _resident_spec(a)
    consts = [pw['norm_mix'], pw['w_proj'], pw['q_gain'], pw['ks_gain'], pw['kw_gain'], pw['mq_gain'],
              pw['seg64']]
    widths = [(KV_W, F32), (KV_W, F32), (KV_W, F32), (ML_W, F32), (ML_W, F32), (LANES, F32)]
    if seq_len is None:
        widths += [(NSA_H * HD, BF16), (NSA_H * HD, BF16), (MEM_W, BF16)]
    out_specs = [pl.BlockSpec((tm, w), row) for w, _ in widths]
    out_shape = [jax.ShapeDtypeStruct((t, w), d) for w, d in widths]
    if seq_len is not None:
        per_seq = seq_len // tm
        for w, d in ((NSA_H * HD, BF16), (KV_W, F32), (KV_W, F32), (KV_W, F32), (NSA_H * HD, BF16), (MEM_W, BF16)):
            out_specs.append(pl.BlockSpec((1, w, tm), lambda i: (i // per_seq, 0, i % per_seq)))
            out_shape.append(jax.ShapeDtypeStruct((t // seq_len, w, seq_len), d))
    return pl.pallas_call(
        _proj_kernel,
        grid=(t // tm,),
        in_specs=[pl.BlockSpec((tm, D_MODEL), row)] + [full(a) for a in consts]
                 + [pl.BlockSpec((tm, LANES), tab)] * 6,
        out_specs=out_specs,
        out_shape=out_shape,
        compiler_params=_cparams(("arbitrary",), VMEM_LIMIT),
        name="project",
    )(x2d, *consts, *tabs)


def _compress_group(y_lo, y_hi, tail_hi, wlo, whi, b1, w2, gain, n_valid):
    r = y_lo.shape[0]
    a = _dot(y_lo, wlo)
    bh = _dot(y_hi, whi)
    bs = pltpu.roll(bh, r - 1, 0)
    if tail_hi is not None:
        bt = _dot(tail_hi, whi)
        bs = jnp.where(_iota(bs.shape, 0) == r - 1, bt, bs)
    hid = _gelu_tanh(a + bs + b1)
    out = _dot(hid.astype(BF16), w2)
    is_k = _iota(out.shape, 1) < HD
    ss = jnp.sum(jnp.where(is_k, out * out, 0.0), axis=-1, keepdims=True) * (1.0 / HD)
    kv = jnp.where(is_k, out * lax.rsqrt(ss + EPS) * gain, out)
    return jnp.where(_iota(kv.shape, 0) < n_valid, kv, 0.0)


def _group_cols(load, g):
    return jnp.concatenate([load(j * KV_W + g * LANES, j * KV_W + (g + 1) * LANES) for j in range(CMP_STRIDE)], axis=1)


def _compress_prompt_kernel(y_ref, pe_lo, pe_hi, wlo, whi, b1, w2, gain, o_ref, ot_ref, *, n_valid):
    for g in range(NSA_G):
        yg = _group_cols(lambda lo, hi: y_ref[0, :, lo:hi], g)
        kv = _compress_group((yg + pe_lo[...]).astype(BF16), (yg + pe_hi[...]).astype(BF16), None,
                             wlo[...], whi[...], b1[...], w2[...], gain[...], n_valid)
        o_ref[0, g] = kv.astype(BF16)
        ot_ref[0, g] = kv.T.astype(BF16)


def _compress_paged_kernel(pt_ref, *refs, n_pages, n_valid):
    pages = refs[:n_pages]
    tail_ref, pe_lo, pe_hi, wlo, whi, b1, w2, gain, o_ref, y = refs[n_pages:]
    cpp = PAGE_SIZE // CMP_STRIDE
    for k, page in enumerate(pages):
        for g in range(NSA_G):
            nat = page[0, g * LANES:(g + 1) * LANES, :].T
            by_row = jnp.swapaxes(nat.reshape(cpp, CMP_STRIDE, LANES), 0, 1)
            for j in range(CMP_STRIDE):
                y[g, k * cpp:(k + 1) * cpp, j * LANES:(j + 1) * LANES] = by_row[j]
    for g in range(NSA_G):
        yg = y[g]
        tg = _group_cols(lambda lo, hi: tail_ref[0, :, lo:hi], g)
        o_ref[0, g] = _compress_group((yg + pe_lo[...]).astype(BF16), (yg + pe_hi[...]).astype(BF16),
                                      (tg + pe_hi[...]).astype(BF16), wlo[...], whi[...], b1[...], w2[...],
                                      gain[...], n_valid).astype(BF16)


def _compress_consts(pw):
    return [pw['cmp_wlo'], pw['cmp_whi'], pw['cmp_b1'], pw['cmp_w2'], pw['kc_gain']]


def _compress_prompt(kvc, pw):
    b, s, _ = kvc.shape
    r = s // CMP_STRIDE
    y = kvc.reshape(b, r, CHUNK_W)
    consts = [pw['pe_lo'], pw['pe_hi']] + _compress_consts(pw)
    return pl.pallas_call(
        functools.partial(_compress_prompt_kernel, n_valid=r - 1),
        grid=(b,),
        in_specs=[pl.BlockSpec((1, r, CHUNK_W), lambda i: (i, 0, 0))]
                 + [pl.BlockSpec(a.shape, lambda i: (0, 0)) for a in consts],
        out_specs=[pl.BlockSpec((1, NSA_G, r, LANES), lambda i: (i, 0, 0, 0)),
                   pl.BlockSpec((1, NSA_G, LANES, r), lambda i: (i, 0, 0, 0))],
        out_shape=[jax.ShapeDtypeStruct((b, NSA_G, r, LANES), BF16),
                   jax.ShapeDtypeStruct((b, NSA_G, LANES, r), BF16)],
        compiler_params=_cparams(("arbitrary",), VMEM_LIMIT),
        name="compress_prompt",
    )(y, *consts)


def _compress_paged(cache, page_table, kvc_new, pw):
    db, n_pages = page_table.shape
    chunks_per_page = PAGE_SIZE // CMP_STRIDE
    r = n_pages * chunks_per_page
    pool = _feature_major(cache)
    tail = jnp.pad(kvc_new, ((0, 0), (0, CMP_STRIDE - kvc_new.shape[1]), (0, 0))).reshape(db, 1, CHUNK_W)
    consts = [pw['pe_lo'], pw['pe_hi']] + _compress_consts(pw)
    page_specs = [pl.BlockSpec((1, KV_W, PAGE_SIZE), functools.partial(lambda i, pt, k: (pt[i, k], 0, 0), k=k))
                  for k in range(n_pages)]
    return pl.pallas_call(
        functools.partial(_compress_paged_kernel, n_pages=n_pages, n_valid=r),
        grid_spec=pltpu.PrefetchScalarGridSpec(
            num_scalar_prefetch=1,
            grid=(db,),
            in_specs=page_specs + [pl.BlockSpec((1, 1, CHUNK_W), lambda i, pt: (i, 0, 0))]
                     + [pl.BlockSpec(a.shape, lambda i, pt: (0, 0)) for a in consts],
            out_specs=pl.BlockSpec((1, NSA_G, r, LANES), lambda i, pt: (i, 0, 0, 0)),
            scratch_shapes=[pltpu.VMEM((NSA_G, r, CMP_STRIDE * LANES), F32)],
        ),
        out_shape=jax.ShapeDtypeStruct((db, NSA_G, r, LANES), BF16),
        compiler_params=_cparams(("arbitrary",), VMEM_LIMIT),
        name="compress_paged",
    )(page_table, *([pool] * n_pages), tail, *consts)


def _rank_select(score, idx, n_blocks):
    cnt = jnp.zeros(score.shape, jnp.int32)
    for i in range(n_blocks):
        other = score[:, i:i + 1]
        ge = jnp.where(other >= score, 1, 0)
        gt = jnp.where(other > score, 1, 0)
        cnt = cnt + jnp.where(idx > i, ge, gt)
    return jnp.where(score > -jnp.inf, jnp.where(cnt < N_SEL, 0.0, UNSELECTED), UNSELECTED)


def _rank_select_rows(score, n_blocks, last_pos, cnt_ref):
    sub = _iota((8, score.shape[1]), 0)
    n_groups = -(-n_blocks // 8)
    groups = [score[g * 8:(g + 1) * 8, :] for g in range(n_groups)]
    cnt_ref[...] = jnp.zeros(cnt_ref.shape, jnp.int32)
    for og in range(n_groups):
        @pl.when(og * 8 * SLC_BLOCK <= last_pos)
        def _():
            part = [None] * n_groups
            for i in range(og * 8, min(og * 8 + 8, n_blocks)):
                other = score[i:i + 1, :]
                for g, sg in enumerate(groups):
                    if g > og:
                        one = jnp.where(other >= sg, 1, 0)
                    elif g < og:
                        one = jnp.where(other > sg, 1, 0)
                    else:
                        one = jnp.where(sub > i % 8, jnp.where(other >= sg, 1, 0), jnp.where(other > sg, 1, 0))
                    part[g] = one if part[g] is None else part[g] + one
            for g in range(n_groups):
                cnt_ref[g * 8:(g + 1) * 8, :] += part[g]
    cnts = [cnt_ref[g * 8:(g + 1) * 8, :] for g in range(n_groups)]
    out = [jnp.where(sg > -jnp.inf, jnp.where(c < N_SEL, 0.0, UNSELECTED), UNSELECTED) for sg, c in zip(groups, cnts)]
    pad = score.shape[0] - 8 * n_groups
    if pad:
        out.append(jnp.full((pad, score.shape[1]), UNSELECTED, F32))
    return jnp.concatenate(out, axis=0)


def _cmp_prompt_kernel(qt_ref, kv_ref, kvt_ref, ovt_ref, o_ref, sel_ref, cnt_ref, *, tq, n_slc):
    i = pl.program_id(2)
    kv = kv_ref[0, 0]
    kvt = kvt_ref[0, 0]
    r = kv.shape[0]
    pos = i * tq + _iota((1, tq), 1)
    visible = (_iota((r, 1), 0) * CMP_STRIDE + (CMP_BLOCK - 1)) <= pos
    zeros = jnp.zeros((HD, tq), BF16)
    psum = jnp.zeros((r, tq), F32)
    outs = []
    for h in range(NSA_HPG):
        qa = jnp.concatenate([qt_ref[0, h * HD:(h + 1) * HD, :], zeros], axis=0)
        s = jnp.where(visible, _dot(kv, qa), -jnp.inf)
        m = jnp.max(s, axis=0, keepdims=True)
        e = jnp.exp(s - jnp.where(m > -jnp.inf, m, 0.0))
        inv_l = 1.0 / jnp.maximum(jnp.sum(e, axis=0, keepdims=True), jnp.finfo(F32).tiny)
        psum = psum + e * inv_l
        outs.append(_dot(kvt[HD:2 * HD, :], e.astype(BF16)) * inv_l)
    o_ref[...] = jnp.concatenate([jnp.concatenate(outs[0:2], axis=0).T, jnp.concatenate(outs[2:4], axis=0).T], axis=1)
    imp = _dot_exact_rhs_left(ovt_ref[...], _split2(psum))
    blk = _iota(imp.shape, 0)
    pos_row = i * tq + _iota(imp.shape, 1)
    forced = (blk == jnp.right_shift(pos_row, 6)) | (blk == 0)
    score = jnp.where(forced, jnp.inf, jnp.where(blk * SLC_BLOCK <= pos_row, imp, -jnp.inf))
    sel_ref[0, 0] = _rank_select_rows(score, n_slc, i * tq + tq - 1, cnt_ref).astype(BF16)


def _cmp_prompt(qn_t, kvcmp, kvcmp_t, ovt, b, s, tq):
    nq = s // tq
    r = kvcmp.shape[2]
    n_slc = s // SLC_BLOCK
    assert n_slc <= HD and ovt.shape == (HD, r) and SLC_BLOCK == 64
    return pl.pallas_call(
        functools.partial(_cmp_prompt_kernel, tq=tq, n_slc=n_slc),
        grid=(b, NSA_G, nq),
        in_specs=[pl.BlockSpec((1, NSA_HPG * HD, tq), lambda bi, g, i: (bi, g, i)),
                  pl.BlockSpec((1, 1, r, LANES), lambda bi, g, i: (bi, g, 0, 0)),
                  pl.BlockSpec((1, 1, LANES, r), lambda bi, g, i: (bi, g, 0, 0)),
                  pl.BlockSpec(ovt.shape, lambda bi, g, i: (0, 0))],
        out_specs=[pl.BlockSpec((tq, NSA_HPG * HD), lambda bi, g, i: (bi * nq + i, g)),
                   pl.BlockSpec((1, 1, HD, tq), lambda bi, g, i: (bi, g, 0, i))],
        out_shape=[jax.ShapeDtypeStruct((b * s, NSA_H * HD), F32),
                   jax.ShapeDtypeStruct((b, NSA_G, HD, s), BF16)],
        scratch_shapes=[pltpu.VMEM((HD, tq), jnp.int32)],
        compiler_params=_cparams(("arbitrary",) * 3, VMEM_LIMIT),
        name="cmp_attend_select_prompt",
    )(qn_t, kvcmp, kvcmp_t, ovt)


def _cmp_sample_kernel(q_ref, kv_ref, ov_ref, o_ref, sel_ref, *, pos, n_slc):
    r = kv_ref.shape[2]
    row = _iota((NSA_H, 1), 0)
    visible = (_iota((1, r), 1) * CMP_STRIDE + (CMP_BLOCK - 1)) <= pos
    n_seq = q_ref.shape[0]
    scores = []
    for bi in range(n_seq):
        q = q_ref[bi]
        kvs = [kv_ref[bi, g] for g in range(NSA_G)]
        s = jnp.where(row < NSA_HPG, _dot_nt(q, kvs[0][:, 0:HD]), _dot_nt(q, kvs[1][:, 0:HD]))
        p = _masked_softmax(s, visible)
        pb = p.astype(BF16)
        o = jnp.where(row < NSA_HPG, _dot(pb, kvs[0]), _dot(pb, kvs[1]))
        o_ref[bi] = o[:, HD:2 * HD]
        p0 = jnp.sum(p[0:NSA_HPG], axis=0, keepdims=True)
        p1 = jnp.sum(p[NSA_HPG:NSA_H], axis=0, keepdims=True)
        psum = jnp.where(row < NSA_HPG, p0, p1)
        imp = _dot_exact_rhs(_split2(psum), ov_ref[...])
        blk = _iota(imp.shape, 1)
        forced = (blk == pos // SLC_BLOCK) | (blk == 0)
        scores.append(jnp.where(forced, jnp.inf, jnp.where(blk * SLC_BLOCK <= pos, imp, -jnp.inf)))
    stacked = jnp.concatenate(scores, axis=0)
    sel = _rank_select(stacked, _iota(stacked.shape, 1), n_slc)
    for bi in range(n_seq):
        sel_ref[bi] = sel[bi * NSA_H:(bi + 1) * NSA_H]


def _cmp_sample(qn3, kvcmp, ov, pos, n_slc):
    db, _, r, _ = kvcmp.shape
    nslp = ov.shape[1]
    bb = _pick_tile(db, SAMPLE_SEQS_PER_STEP)
    return pl.pallas_call(
        functools.partial(_cmp_sample_kernel, pos=pos, n_slc=n_slc),
        grid=(db // bb,),
        in_specs=[pl.BlockSpec((bb, NSA_H, HD), lambda i: (i, 0, 0)),
                  pl.BlockSpec((bb, NSA_G, r, LANES), lambda i: (i, 0, 0, 0)),
                  pl.BlockSpec(ov.shape, lambda i: (0, 0))],
        out_specs=[pl.BlockSpec((bb, NSA_H, HD), lambda i: (i, 0, 0)),
                   pl.BlockSpec((bb, NSA_H, nslp), lambda i: (i, 0, 0))],
        out_shape=[jax.ShapeDtypeStruct((db, NSA_H, HD), F32),
                   jax.ShapeDtypeStruct((db, NSA_H, nslp), F32)],
        compiler_params=_cparams(("arbitrary",), VMEM_LIMIT),
        name="cmp_attend_select_sample",
    )(qn3, kvcmp, ov)


def _flash_prompt_kernel(*refs, tq, selected):
    if selected:
        qt_ref, kv_ref, kvt_ref, sel_ref, o_ref, kop, vop, qa_ref, m_ref, acc_ref = refs
    else:
        qt_ref, kv_ref, kvt_ref, o_ref, kop, vop, qa_ref, m_ref, acc_ref = refs
    i = pl.program_id(2)
    cols = NSA_HPG * tq
    n_tiles, v_rows, tk = vop.shape
    ones_rows = v_rows - HD
    wide = FLASH_COLS

    @pl.when(i == 0)
    def _():
        kvf = kv_ref[0]
        if selected:
            lane = _iota(kvf.shape, 1)
            onehot = (lane - HD) == jnp.right_shift(_iota(kvf.shape, 0), 6)
            kop[...] = jnp.where(lane < HD, kvf, onehot.astype(F32)).astype(BF16)
        else:
            kop[...] = kvf.astype(BF16)
        for t in range(n_tiles):
            vop[t, 0:ones_rows, :] = jnp.ones((ones_rows, tk), BF16)
            vop[t, ones_rows:v_rows, :] = kvt_ref[0, HD:2 * HD, t * tk:(t + 1) * tk].astype(BF16)

    extra = sel_ref[0, 0] if selected else jnp.zeros((HD, tq), BF16)
    for h in range(NSA_HPG):
        qa_ref[0:HD, h * tq:(h + 1) * tq] = qt_ref[0, h * HD:(h + 1) * HD, :]
        qa_ref[HD:2 * HD, h * tq:(h + 1) * tq] = extra
    m_ref[...] = jnp.full((1, cols), MASKED, F32)
    acc_ref[...] = jnp.zeros((v_rows, cols), F32)

    def tile_step(j, masked, blocks):
        start = pl.multiple_of(j * tk, tk)
        for cb in blocks:
            sl = slice(cb * wide, (cb + 1) * wide)
            s = _dot(kop[pl.ds(start, tk), :], qa_ref[:, sl])
            if masked:
                pos = i * tq + (cb * wide) % tq + _iota((1, wide), 1)
                kpos = j * tk + _iota((tk, 1), 0)
                valid = kpos <= pos
                if not selected:
                    valid = valid & ((pos - kpos) <= WINDOW)
                s = jnp.where(valid, s, MASKED)
            m_old = m_ref[:, sl]
            m_new = jnp.maximum(m_old, jnp.max(s, axis=0, keepdims=True))
            p = jnp.exp(s - m_new).astype(BF16)
            m_ref[:, sl] = m_new
            acc_ref[:, sl] = jnp.exp(m_old - m_new) * acc_ref[:, sl] + _dot(vop[j], p)

    per_q = tq // tk
    all_blocks = list(range(cols // wide))

    def reachable(back, u):
        first_key = u * tk - back * tq
        out = []
        for cb in all_blocks:
            q0 = (cb * wide) % tq
            causal = q0 + wide - 1 >= first_key
            in_window = selected or q0 - (first_key + tk - 1) <= WINDOW
            if causal and in_window:
                out.append(cb)
        return out

    def masked_unit(back):
        for u in range(per_q):
            tile_step((i - back) * per_q + u, True, reachable(back, u))

    if selected:
        def interior(t, carry):
            for u in range(per_q):
                tile_step(t * per_q + u, False, all_blocks)
            return carry
        lax.fori_loop(0, i, interior, 0)
    else:
        for back in range(WINDOW // tq, 0, -1):
            pl.when(i >= back)(functools.partial(masked_unit, back))
    masked_unit(0)
    normed = []
    for h in range(NSA_HPG):
        a = acc_ref[:, h * tq:(h + 1) * tq]
        normed.append(a[ones_rows:v_rows, :] * (1.0 / a[0:1, :]))
    o_ref[...] = jnp.concatenate([jnp.concatenate(normed[0:2], axis=0).T, jnp.concatenate(normed[2:4], axis=0).T], axis=1)


def _flash_prompt(qrot_t, kv, kv_t, sel_t, b, s, tq, tk):
    nq = s // tq
    selected = sel_t is not None
    assert tq % tk == 0 and tk % LANES == 0 and s % tq == 0 and tq % FLASH_COLS == 0 and (selected or WINDOW % tq == 0)
    v_rows = FLASH_ONES_ROWS + HD
    in_specs = [pl.BlockSpec((1, NSA_HPG * HD, tq), lambda bi, g, i: (bi, g, i)),
                pl.BlockSpec((1, s, LANES), lambda bi, g, i: (bi, 0, g)),
                pl.BlockSpec((1, LANES, s), lambda bi, g, i: (bi, g, 0))]
    args = [qrot_t, kv, kv_t]
    cols = NSA_HPG * tq
    scratch = [pltpu.VMEM((s, LANES), BF16), pltpu.VMEM((s // tk, v_rows, tk), BF16),
               pltpu.VMEM((LANES, cols), BF16), pltpu.VMEM((1, cols), F32), pltpu.VMEM((v_rows, cols), F32)]
    if selected:
        in_specs.append(pl.BlockSpec((1, 1, HD, tq), lambda bi, g, i: (bi, g, 0, i)))
        args.append(sel_t)
    return pl.pallas_call(
        functools.partial(_flash_prompt_kernel, tq=tq, selected=selected),
        grid=(b, NSA_G, nq),
        in_specs=in_specs,
        out_specs=pl.BlockSpec((tq, NSA_HPG * HD), lambda bi, g, i: (bi * nq + i, g)),
        out_shape=jax.ShapeDtypeStruct((b * s, NSA_H * HD), F32),
        scratch_shapes=scratch,
        compiler_params=_cparams(("arbitrary",) * 3, VMEM_LIMIT),
        name="slc_attend_prompt" if selected else "win_attend_prompt",
    )(*args)


def _mem_kv_kernel(x_ref, nm_ref, w_ref, kg_ref, o_ref, ot_ref):
    xn = _rms_rows(x_ref[0], nm_ref[...]).astype(BF16)
    kv = _dot(xn, w_ref[...])
    cols = []
    for h in range(MEM_H):
        k = kv[:, 2 * h * MEM_HD:(2 * h + 1) * MEM_HD]
        cols.append(_rms_rows(k, kg_ref[...]))
        cols.append(kv[:, (2 * h + 1) * MEM_HD:(2 * h + 2) * MEM_HD])
    out = jnp.concatenate(cols, axis=1)
    o_ref[0] = out
    ot_ref[0] = out.T.astype(BF16)


def _mem_kv(mem, pw):
    b, t_mem, _ = mem.shape
    consts = [pw['norm_mem'], pw['w_mem_kv'], pw['mem_k_gain']]
    return pl.pallas_call(
        _mem_kv_kernel,
        grid=(b,),
        in_specs=[pl.BlockSpec((1, t_mem, D_MODEL), lambda i: (i, 0, 0))]
                 + [pl.BlockSpec(a.shape, lambda i: (0, 0)) for a in consts],
        out_specs=[pl.BlockSpec((1, t_mem, 2 * MEM_W), lambda i: (i, 0, 0)),
                   pl.BlockSpec((1, 2 * MEM_W, t_mem), lambda i: (i, 0, 0))],
        out_shape=[jax.ShapeDtypeStruct((b, t_mem, 2 * MEM_W), F32),
                   jax.ShapeDtypeStruct((b, 2 * MEM_W, t_mem), BF16)],
        compiler_params=_cparams(("arbitrary",), VMEM_LIMIT),
        name="mem_kv",
    )(mem, *consts)


def _mem_prompt_kernel(qt_ref, kv_ref, kvt_ref, o_ref):
    for h in range(MEM_H):
        k = kv_ref[0, :, 2 * h * MEM_HD:(2 * h + 1) * MEM_HD].astype(BF16)
        s = _dot(k, qt_ref[0, h * MEM_HD:(h + 1) * MEM_HD, :])
        p = jnp.exp(s - jnp.max(s, axis=0, keepdims=True))
        vt = kvt_ref[0, (2 * h + 1) * MEM_HD:(2 * h + 2) * MEM_HD, :]
        ot = _dot(vt, p.astype(BF16)) * (1.0 / jnp.sum(p, axis=0, keepdims=True))
        o_ref[:, h * MEM_HD:(h + 1) * MEM_HD] = ot.T


def _mem_prompt(mq_t, kvm, kvm_t, b, s, tq):
    nq = s // tq
    t_mem = kvm.shape[1]
    return pl.pallas_call(
        _mem_prompt_kernel,
        grid=(b, nq),
        in_specs=[pl.BlockSpec((1, MEM_W, tq), lambda bi, i: (bi, 0, i)),
                  pl.BlockSpec((1, t_mem, 2 * MEM_W), lambda bi, i: (bi, 0, 0)),
                  pl.BlockSpec((1, 2 * MEM_W, t_mem), lambda bi, i: (bi, 0, 0))],
        out_specs=pl.BlockSpec((tq, MEM_W), lambda bi, i: (bi * nq + i, 0)),
        out_shape=jax.ShapeDtypeStruct((b * s, MEM_W), F32),
        compiler_params=_cparams(("arbitrary",) * 2, VMEM_LIMIT),
        name="mem_attend_prompt",
    )(mq_t, kvm, kvm_t)


def _decode_attend(qa, blocks, new_row=None, new_bias=None, feature_major=False):
    qk, pv = (_dot, _dot_nt) if feature_major else (_dot_nt, _dot)
    scores = []
    for load, bias in blocks:
        s = qk(qa, load())
        scores.append(s if bias is None else s + bias)
    m = functools.reduce(jnp.maximum, [jnp.max(s, axis=-1, keepdims=True) for s in scores])
    if new_row is not None:
        nb = new_row.astype(BF16).astype(F32)
        s_new = jnp.sum(qa.astype(F32) * nb, axis=-1, keepdims=True) + new_bias
        m = jnp.maximum(m, s_new)
    ps = [jnp.exp(s - m) for s in scores]
    l = functools.reduce(lambda a, c: a + c, [jnp.sum(p, axis=-1, keepdims=True) for p in ps])
    acc = functools.reduce(lambda a, c: a + c, [pv(p.astype(BF16), load()) for p, (load, _) in zip(ps, blocks)])
    if new_row is not None:
        p_new = jnp.exp(s_new - m)
        l = l + p_new
        acc = acc + p_new.astype(BF16).astype(F32) * nb
    return acc / l


def _group_values(o):
    row = _iota((NSA_H, LANES), 0)
    both = jnp.where(row < NSA_HPG, o[:, 0:LANES], o[:, LANES:2 * LANES])
    return both[:, HD:2 * HD]


def _slc_sample_kernel(pt_ref, *refs, n_pages):
    pages = refs[:n_pages]
    q_ref, sel_ref, new_ref, o_ref = refs[n_pages:]
    sb = sel_ref[0]
    lane = _iota((NSA_H, PAGE_SIZE), 1)
    blocks = []
    for p, page in enumerate(pages):
        bias = jnp.where(lane < SLC_BLOCK, sb[:, 2 * p:2 * p + 1], sb[:, 2 * p + 1:2 * p + 2])
        blocks.append((lambda page=page: page[0].astype(BF16), bias))
    n_past = n_pages * (PAGE_SIZE // SLC_BLOCK)
    o = _decode_attend(q_ref[0], blocks, new_ref[0], sb[:, n_past:n_past + 1], feature_major=True)
    o_ref[0] = _group_values(o)


def _selected_pages(sel, page_table):
    db, n_pages = page_table.shape
    bpp = PAGE_SIZE // SLC_BLOCK
    n_past = n_pages * bpp
    n_slots = min(n_pages, NSA_G * N_SEL)
    picked = jnp.any(sel[:, :, :n_past] == 0.0, axis=1)
    page_any = jnp.any(picked.reshape(db, n_pages, bpp), axis=-1)
    order = jnp.argsort(jnp.logical_not(page_any), axis=1, stable=True)[:, :n_slots]
    slot_pages = jnp.take_along_axis(page_table, order, axis=1)
    blocks = (order[:, :, None] * bpp + jnp.arange(bpp)).reshape(db, 1, n_slots * bpp)
    bias = jnp.take_along_axis(sel, jnp.broadcast_to(blocks, (db, NSA_H, n_slots * bpp)), axis=2)
    bias = jnp.concatenate([bias, sel[:, :, n_past:n_past + 1]], axis=2)
    width = -(-bias.shape[2] // LANES) * LANES
    return slot_pages, jnp.pad(bias, ((0, 0), (0, 0), (0, width - bias.shape[2])))


def _feature_major(cache):
    n, rows = cache.shape[:2]
    return jnp.transpose(cache, (0, 2, 3, 4, 1)).reshape(n, KV_W, rows)


def _slc_sample(qaug, sel, cache, page_table, kvs_new):
    db, n_pages = page_table.shape
    nslp = sel.shape[2]
    pool = _feature_major(cache)
    page_specs = [pl.BlockSpec((1, KV_W, PAGE_SIZE), functools.partial(lambda i, pt, k: (pt[i, k], 0, 0), k=k))
                  for k in range(n_pages)]
    return pl.pallas_call(
        functools.partial(_slc_sample_kernel, n_pages=n_pages),
        grid_spec=pltpu.PrefetchScalarGridSpec(
            num_scalar_prefetch=1,
            grid=(db,),
            in_specs=page_specs + [pl.BlockSpec((1, NSA_H, KV_W), lambda i, pt: (i, 0, 0)),
                                   pl.BlockSpec((1, NSA_H, nslp), lambda i, pt: (i, 0, 0)),
                                   pl.BlockSpec((1, 1, KV_W), lambda i, pt: (i, 0, 0))],
            out_specs=pl.BlockSpec((1, NSA_H, HD), lambda i, pt: (i, 0, 0)),
        ),
        out_shape=jax.ShapeDtypeStruct((db, NSA_H, HD), F32),
        compiler_params=_cparams(("arbitrary",), VMEM_LIMIT),
        name="slc_attend_sample",
    )(page_table, *([pool] * n_pages), qaug, sel, kvs_new)


def _win_sample_kernel(q_ref, buf_ref, new_ref, newcol_ref, o_ref, win_ref, *, pos, first_pos):
    w_buf = buf_ref.shape[2]
    d = pos - (first_pos + _iota((1, w_buf), 1))
    bias = jnp.where((d >= 0) & (d <= WINDOW), 0.0, MASKED)
    for bi in range(q_ref.shape[0]):
        buf = buf_ref[bi]
        o = _decode_attend(q_ref[bi], [(lambda buf=buf: buf.astype(BF16), bias)], new_ref[bi], 0.0, feature_major=True)
        o_ref[bi] = _group_values(o)
        shifted = pltpu.roll(buf, w_buf - 1, 1)
        win_ref[bi] = jnp.where(_iota(buf.shape, 1) == w_buf - 1, newcol_ref[bi], shifted)


def _win_sample(qaug, win_buf_t, kvw_new, pos, first_pos):
    db, _, w_buf = win_buf_t.shape
    bb = _pick_tile(db, SAMPLE_SEQS_PER_STEP)
    return pl.pallas_call(
        functools.partial(_win_sample_kernel, pos=pos, first_pos=first_pos),
        grid=(db // bb,),
        in_specs=[pl.BlockSpec((bb, NSA_H, KV_W), lambda i: (i, 0, 0)),
                  pl.BlockSpec((bb, KV_W, w_buf), lambda i: (i, 0, 0)),
                  pl.BlockSpec((bb, 1, KV_W), lambda i: (i, 0, 0)),
                  pl.BlockSpec((bb, KV_W, 1), lambda i: (i, 0, 0))],
        out_specs=[pl.BlockSpec((bb, NSA_H, HD), lambda i: (i, 0, 0)),
                   pl.BlockSpec((bb, KV_W, w_buf), lambda i: (i, 0, 0))],
        out_shape=[jax.ShapeDtypeStruct((db, NSA_H, HD), F32),
                   jax.ShapeDtypeStruct((db, KV_W, w_buf), F32)],
        compiler_params=_cparams(("arbitrary",), VMEM_LIMIT),
        name="win_attend_sample",
    )(qaug, win_buf_t, kvw_new, kvw_new.reshape(db, KV_W, 1))


def _mem_sample_kernel(q_ref, kv_ref, o_ref):
    row = _iota((8, MEM_HD), 0)
    for bi in range(q_ref.shape[0]):
        q = q_ref[bi]
        out = jnp.zeros((8, MEM_HD), F32)
        for h in range(MEM_H):
            k = kv_ref[bi, :, h, 0, :].astype(BF16)
            v = kv_ref[bi, :, h, 1, :].astype(BF16)
            s = _dot_nt(q, k)
            p = jnp.exp(s - jnp.max(s, axis=-1, keepdims=True))
            o = _dot(p.astype(BF16), v) / jnp.sum(p, axis=-1, keepdims=True)
            out = jnp.where(row == h, o, out)
        o_ref[bi] = out


def _mem_sample(q8, cache_mem):
    db, t_mem = cache_mem.shape[:2]
    bb = _pick_tile(db, SAMPLE_SEQS_PER_STEP)
    return pl.pallas_call(
        _mem_sample_kernel,
        grid=(db // bb,),
        in_specs=[pl.BlockSpec((bb, 8, MEM_HD), lambda i: (i, 0, 0)),
                  pl.BlockSpec((bb, t_mem, MEM_H, 2, MEM_HD), lambda i: (i, 0, 0, 0, 0))],
        out_specs=pl.BlockSpec((bb, 8, MEM_HD), lambda i: (i, 0, 0)),
        out_shape=jax.ShapeDtypeStruct((db, 8, MEM_HD), F32),
        compiler_params=_cparams(("arbitrary",), VMEM_LIMIT),
        name="mem_attend_sample",
    )(q8, cache_mem)


def _mlstm_prompt_kernel(u_ref, oraw_ref, slab_ref, slabt_ref, cw_ref, cb_ref, wqt_ref, wk_ref, wvt_ref,
                         bgc_ref, bgr_ref, mlgt_ref, tril_ref, triu_ref,
                         o_ref, ct_ref, n_ref, m_ref, ubuf, *, chunk):
    L = chunk
    pad = 8

    @pl.when(pl.program_id(1) == 0)
    def _():
        ct_ref[...] = jnp.zeros_like(ct_ref)
        n_ref[...] = jnp.zeros_like(n_ref)
        m_ref[...] = jnp.zeros_like(m_ref)
        ubuf[0:pad, :] = jnp.zeros((pad, ML_W), F32)

    u = u_ref[0]
    ubuf[pad:pad + L, :] = u
    conv = cb_ref[...] + cw_ref[CONV_W - 1:CONV_W, :] * u
    for w in range(CONV_W - 1):
        off = pad - (CONV_W - 1) + w
        conv = conv + cw_ref[w:w + 1, :] * ubuf[off:off + L, :]
    ubuf[0:pad, :] = u[L - pad:L, :]
    cact = conv * _sigmoid(conv)

    gl = slab_ref[0] + bgc_ref[...]
    gt = slabt_ref[0] + bgr_ref[...]
    b_c = None
    for part in _split3(_log_sigmoid(gl)):
        term = _dot(tril_ref[...], part)
        b_c = term if b_c is None else b_c + term
    b_r = _dot_exact_rhs(_split3(_log_sigmoid(gt)), triu_ref[...])
    m_prev = m_ref[0]
    not_after = _iota((L, L), 0) <= _iota((L, L), 1)
    lane1 = _iota((1, LANES), 1)
    gain_t = jnp.concatenate([mlgt_ref[...]] * (L // LANES), axis=1)
    m_out = m_prev
    outs = []
    for h in range(ML_H):
        sl = slice(h * ML_D, (h + 1) * ML_D)
        br = b_r[_SLAB_F + h:_SLAB_F + h + 1, :]
        ir = gt[_SLAB_I + h:_SLAB_I + h + 1, :]
        src = gl[:, _SLAB_I + h:_SLAB_I + h + 1] - b_c[:, _SLAB_F + h:_SLAB_F + h + 1]
        m0 = m_prev[:, h:h + 1]
        dm = jnp.where(not_after, br + src, -jnp.inf)
        m_new = jnp.maximum(br + m0, jnp.max(dm, axis=0, keepdims=True))
        ch = cact[:, sl].astype(BF16)
        kb = _dot(ch, wk_ref[h]).astype(BF16)
        qt = _dot_nt(wqt_ref[h], ch) * (ML_D ** -0.5)
        vt = _dot_nt(wvt_ref[h], u[:, sl].astype(BF16))
        qtb = qt.astype(BF16)
        wmat = jnp.exp(dm - m_new) * _dot(kb, qtb)
        inter = jnp.exp(br + m0 - m_new)
        ct_old = ct_ref[0, h]
        n_old = n_ref[0, h:h + 1, :]
        num = inter * _dot(ct_old.astype(BF16), qtb) + _dot(vt.astype(BF16), wmat.astype(BF16))
        nq = _dot_exact_rhs(_split2(jnp.broadcast_to(n_old, (8, ML_D))), qtb)[0:1]
        den = inter * nq + jnp.sum(wmat, axis=0, keepdims=True)
        hh = num / jnp.maximum(jnp.abs(den), jnp.exp(-m_new))
        m_end = m_new[:, L - 1:L]
        b_end = br[:, L - 1:L]
        decay = jnp.exp(b_end + m0 - m_end)
        wend = jnp.exp(b_end - br + ir - m_end)
        ct_ref[0, h] = decay * ct_old + _dot((vt * wend).astype(BF16), kb)
        n_ref[0, h:h + 1, :] = decay * n_old + _dot_exact_rhs(_split2(jnp.broadcast_to(wend, (8, L))), kb)[0:1]
        m_out = jnp.where(lane1 == h, m_end, m_out)
        hn = hh * lax.rsqrt(jnp.mean(hh * hh, axis=0, keepdims=True) + EPS) * gain_t[sl, :]
        outs.append(_sigmoid(oraw_ref[0, :, sl]) * hn.T)
    o_ref[0] = jnp.concatenate(outs, axis=1)
    m_ref[0] = m_out


def _mlstm_prompt(u, oraw, slab, pw, chunk):
    b, s, _ = u.shape
    nc = s // chunk
    slabt = jnp.swapaxes(slab[:, :, 0:8], 1, 2)
    tril = jnp.tril(jnp.ones((chunk, chunk), BF16))
    consts = [pw['conv_w'], pw['conv_b'], pw['w_ml_q_t'], pw['w_ml_k'], pw['w_ml_v_t'], pw['gate_bias_lanes'],
              pw['gate_bias_rows'], pw['ml_gain_t'], tril, tril.T]
    blk = lambda w: pl.BlockSpec((1, chunk, w), lambda bi, c: (bi, c, 0))

    def const_spec(a):
        nd = a.ndim
        return pl.BlockSpec(a.shape, lambda bi, c: (0,) * nd)

    return pl.pallas_call(
        functools.partial(_mlstm_prompt_kernel, chunk=chunk),
        grid=(b, nc),
        in_specs=[blk(ML_W), blk(ML_W), blk(LANES), pl.BlockSpec((1, 8, chunk), lambda bi, c: (bi, 0, c))]
                 + [const_spec(a) for a in consts],
        out_specs=[blk(ML_W),
                   pl.BlockSpec((1, ML_H, ML_D, ML_D), lambda bi, c: (bi, 0, 0, 0)),
                   pl.BlockSpec((1, ML_H, ML_D), lambda bi, c: (bi, 0, 0)),
                   pl.BlockSpec((1, 1, LANES), lambda bi, c: (bi, 0, 0))],
        out_shape=[jax.ShapeDtypeStruct((b, s, ML_W), F32),
                   jax.ShapeDtypeStruct((b, ML_H, ML_D, ML_D), F32),
                   jax.ShapeDtypeStruct((b, ML_H, ML_D), F32),
                   jax.ShapeDtypeStruct((b, 1, LANES), F32)],
        scratch_shapes=[pltpu.VMEM((chunk + 8, ML_W), F32)],
        compiler_params=_cparams(("arbitrary", "arbitrary"), VMEM_LIMIT),
        name="mlstm_prompt",
    )(u, oraw, slab, slabt, *consts)


def _mlstm_sample_kernel(ext_ref, oraw_ref, slab_ref, c_ref, n_ref, m_ref, cw_ref, cb_ref, wq_ref, wk_ref,
                         wv_ref, wkt_ref, bgc_ref, mlg_ref, o_ref, co_ref, no_ref, mo_ref):
    for bi in range(ext_ref.shape[0]):
        _mlstm_sample_step(bi, ext_ref, oraw_ref, slab_ref, c_ref, n_ref, m_ref, cw_ref, cb_ref, wq_ref, wk_ref,
                           wv_ref, wkt_ref, bgc_ref, mlg_ref, o_ref, co_ref, no_ref, mo_ref)


def _mlstm_sample_step(bi, ext_ref, oraw_ref, slab_ref, c_ref, n_ref, m_ref, cw_ref, cb_ref, wq_ref, wk_ref,
                       wv_ref, wkt_ref, bgc_ref, mlg_ref, o_ref, co_ref, no_ref, mo_ref):
    ext = ext_ref[bi]
    conv = cb_ref[...]
    for w in range(CONV_W):
        conv = conv + cw_ref[w:w + 1, :] * ext[w:w + 1, :]
    cact = conv * _sigmoid(conv)
    u = ext[CONV_W - 1:CONV_W, :]
    gl = slab_ref[bi] + bgc_ref[...]
    lf_all = _log_sigmoid(gl)
    m_prev = m_ref[bi]
    lane1 = _iota((1, LANES), 1)
    m_out = m_prev
    outs = []
    for h in range(ML_H):
        sl = slice(h * ML_D, (h + 1) * ML_D)
        ch8 = jnp.broadcast_to(cact[:, sl], (8, ML_D)).astype(BF16)
        u8 = jnp.broadcast_to(u[:, sl], (8, ML_D)).astype(BF16)
        q = (_dot(ch8, wq_ref[h]) * (ML_D ** -0.5))[0:1]
        k = _dot(ch8, wk_ref[h])[0:1]
        v = _dot(u8, wv_ref[h])[0:1]
        k_col = _dot_nt(wkt_ref[h], ch8)[:, 0:1]
        ig = gl[:, _SLAB_I + h:_SLAB_I + h + 1]
        lf = lf_all[:, _SLAB_F + h:_SLAB_F + h + 1]
        m0 = m_prev[:, h:h + 1]
        m_new = jnp.maximum(lf + m0, ig)
        qb = q.astype(BF16)
        qk = jnp.sum(qb.astype(F32) * k.astype(BF16).astype(F32), axis=-1, keepdims=True)
        w_in = jnp.exp(ig - m_new) * qk
        inter = jnp.exp(lf + m0 - m_new)
        c_old = c_ref[bi, h]
        n_old = n_ref[bi, h:h + 1, :]
        qc = _dot(jnp.broadcast_to(qb, (8, ML_D)), c_old.astype(BF16))[0:1]
        num = inter * qc + w_in * v
        den = inter * jnp.sum(q * n_old, axis=-1, keepdims=True) + w_in
        hh = num / jnp.maximum(jnp.abs(den), jnp.exp(-m_new))
        w_end = jnp.exp(ig - m_new)
        co_ref[bi, h] = inter * c_old + k_col * (w_end * v)
        no_ref[bi, h:h + 1, :] = inter * n_old + w_end * k
        m_out = jnp.where(lane1 == h, m_new, m_out)
        hn = hh * lax.rsqrt(jnp.mean(hh * hh, axis=-1, keepdims=True) + EPS) * mlg_ref[:, sl]
        outs.append(_sigmoid(oraw_ref[bi, :, sl]) * hn)
    o_ref[bi] = jnp.concatenate(outs, axis=1)
    mo_ref[bi] = m_out


def _mlstm_sample(ext, oraw, slab, state_c, state_n, state_m, pw):
    db = ext.shape[0]
    m_in = jnp.pad(state_m, ((0, 0), (0, LANES - ML_H))).reshape(db, 1, LANES)
    consts = [pw['conv_w'], pw['conv_b'], pw['w_ml_q'], pw['w_ml_k'], pw['w_ml_v'], pw['w_ml_k_t'],
              pw['gate_bias_lanes'], pw['ml_gain']]

    def const_spec(a):
        nd = a.ndim
        return pl.BlockSpec(a.shape, lambda i: (0,) * nd)

    bb = _pick_tile(db, SAMPLE_SEQS_PER_STEP)
    per_seq = lambda shape: pl.BlockSpec((bb,) + shape, lambda i: (i,) + (0,) * len(shape))
    return pl.pallas_call(
        _mlstm_sample_kernel,
        grid=(db // bb,),
        in_specs=[per_seq((CONV_W, ML_W)), per_seq((1, ML_W)), per_seq((1, LANES)), per_seq((ML_H, ML_D, ML_D)),
                  per_seq((ML_H, ML_D)), per_seq((1, LANES))] + [const_spec(a) for a in consts],
        out_specs=[per_seq((1, ML_W)), per_seq((ML_H, ML_D, ML_D)), per_seq((ML_H, ML_D)), per_seq((1, LANES))],
        out_shape=[jax.ShapeDtypeStruct((db, 1, ML_W), F32),
                   jax.ShapeDtypeStruct((db, ML_H, ML_D, ML_D), F32),
                   jax.ShapeDtypeStruct((db, ML_H, ML_D), F32),
                   jax.ShapeDtypeStruct((db, 1, LANES), F32)],
        compiler_params=_cparams(("arbitrary",), VMEM_LIMIT),
        name="mlstm_sample",
    )(ext, oraw, slab, state_c, state_n, m_in, *consts)


def _merge_kernel(x_ref, nm_ref, wg_ref, slab_ref, ex_ref, ocmp_ref, oslc_ref, owin_ref, oml_ref, omem_ref,
                  wb_ref, wout_ref, o_ref):
    x = x_ref[...]
    xn = _rms_rows(x, nm_ref[...]).astype(BF16)
    gparts = _split2(_sigmoid(slab_ref[...]))
    onsa = None
    for br, br_ref in enumerate((ocmp_ref, oslc_ref, owin_ref)):
        term = _dot_exact_rhs(gparts, ex_ref[br]) * br_ref[...]
        onsa = term if onsa is None else onsa + term
    z = None
    for n, on in enumerate((onsa, oml_ref[...], omem_ref[...])):
        gate = _sigmoid(_dot(xn, wg_ref[:, n * D_MODEL:(n + 1) * D_MODEL]))
        term = gate * _dot(on.astype(BF16), wb_ref[n])
        z = term if z is None else z + term
    o_ref[...] = x + _dot(z.astype(BF16), wout_ref[...])


def _merge(x2d, slab, ocmp, oslc, owin, oml, omem, pw, tm):
    t = x2d.shape[0]
    row = lambda w: pl.BlockSpec((tm, w), lambda i: (i, 0))
    const_spec = lambda a: _resident_spec(a)
    return pl.pallas_call(
        _merge_kernel,
        grid=(t // tm,),
        in_specs=[row(D_MODEL), const_spec(pw['norm_mix']), const_spec(pw['w_gate']), row(LANES),
                  const_spec(pw['gate_expand'])] + [row(BRANCH_W)] * 5
                 + [const_spec(pw['w_branch']), const_spec(pw['w_out'])],
        out_specs=row(D_MODEL),
        out_shape=jax.ShapeDtypeStruct((t, D_MODEL), F32),
        compiler_params=_cparams(("arbitrary",), VMEM_LIMIT),
        name="merge",
    )(x2d, pw['norm_mix'], pw['w_gate'], slab, pw['gate_expand'], ocmp, oslc, owin, oml, omem,
      pw['w_branch'], pw['w_out'])


def _ffn_kernel(x_ref, nf_ref, win_ref, wout_ref, o_ref):
    x = x_ref[...]
    xn = _rms_rows(x, nf_ref[...]).astype(BF16)
    a = _dot(xn, win_ref[:, 0:FFN_HID])
    b = _dot(xn, win_ref[:, FFN_HID:2 * FFN_HID])
    o_ref[...] = x + _dot((a * _sigmoid(a) * b).astype(BF16), wout_ref[...])


def _ffn(x2d, pw, tm):
    t = x2d.shape[0]
    consts = [pw['norm_ffn'], pw['w_ffn_in'], pw['w_ffn_out']]
    return pl.pallas_call(
        _ffn_kernel,
        grid=(t // tm,),
        in_specs=[pl.BlockSpec((tm, D_MODEL), lambda i: (i, 0))] + [_resident_spec(a) for a in consts],
        out_specs=pl.BlockSpec((tm, D_MODEL), lambda i: (i, 0)),
        out_shape=jax.ShapeDtypeStruct((t, D_MODEL), F32),
        compiler_params=_cparams(("arbitrary",), VMEM_LIMIT),
        name="ffn",
    )(x2d, *consts)


def _rope_tables(pos):
    half = ROT_DIM // 2
    f32 = np.float32
    freqs = ROPE_THETA ** (-np.arange(half, dtype=np.float64) / half)
    ang = pos.astype(np.float64)[:, None] * freqs
    cos, sin = np.cos(ang).astype(f32), np.sin(ang).astype(f32)
    n = pos.shape[0]
    ones = np.ones((n, HD - ROT_DIM), f32)
    zeros = np.zeros((n, HD - ROT_DIM), f32)
    zh = np.zeros((n, half), f32)
    ct = np.concatenate([cos, cos, ones], axis=1)
    sa = np.concatenate([-sin, zh, zeros], axis=1)
    sb = np.concatenate([zh, sin, zeros], axis=1)
    one64, zero64 = np.ones((n, HD), f32), np.zeros((n, HD), f32)
    q_tabs = [np.concatenate([t, t], axis=1) for t in (ct, sa, sb)]
    k_tabs = [np.concatenate([ct, one64], axis=1), np.concatenate([sa, zero64], axis=1),
              np.concatenate([sb, zero64], axis=1)]
    return [jnp.asarray(t) for t in q_tabs + k_tabs]


def _prepare_weights(norm_mix, w_in, q_norm, k_norm_cmp, k_norm_slc, k_norm_win, cmp_pe, cmp_w1, cmp_b1, cmp_w2,
                     conv_w, conv_b, w_ml_q, w_ml_k, w_ml_v, b_igate, b_fgate, ml_norm, norm_mem, w_mem_kv,
                     mem_q_norm, mem_k_norm, w_branch, w_out, norm_ffn, w_ffn_in, w_ffn_out):
    widths = (NSA_H * HD, 3 * NSA_H, KV_W, KV_W, KV_W, ML_W, ML_W, ML_H, ML_H, MEM_W, N_BRANCH * D_MODEL)
    offs = np.concatenate([[0], np.cumsum(widths)])
    q, g, kvc, kvs, kvw, u, o, ig, fg, mq, mg = (w_in[:, int(offs[i]):int(offs[i + 1])] for i in range(len(widths)))
    slab_pad = jnp.zeros((D_MODEL, LANES - 2 * ML_H - 3 * NSA_H), F32)
    pw = {}
    pw['w_proj'] = jnp.concatenate([q, kvc, kvs, kvw, u, o, mq, ig, fg, g, slab_pad], axis=1).astype(BF16)
    pw['w_gate'] = mg.astype(BF16)
    pw['norm_mix'] = norm_mix.reshape(1, D_MODEL)
    pw['q_gain'] = jnp.tile(q_norm, NSA_H).reshape(1, NSA_H * HD)
    ones = jnp.ones((HD,), F32)
    pw['ks_gain'] = jnp.tile(jnp.concatenate([k_norm_slc, ones]), NSA_G).reshape(1, KV_W)
    pw['kw_gain'] = jnp.tile(jnp.concatenate([k_norm_win, ones]), NSA_G).reshape(1, KV_W)
    pw['kc_gain'] = jnp.concatenate([k_norm_cmp, ones]).reshape(1, LANES)
    pw['mq_gain'] = jnp.tile(mem_q_norm, MEM_H).reshape(1, MEM_W)
    head_of = np.arange(NSA_H * HD) // HD
    pw['seg64'] = jnp.asarray(head_of[:, None] == head_of[None, :], BF16)
    w1 = cmp_w1.reshape(2, CMP_BLOCK, HD, CMP_HID)
    eye = jnp.eye(2, dtype=F32)
    for name, part in (('cmp_wlo', w1[:, :CMP_STRIDE]), ('cmp_whi', w1[:, CMP_STRIDE:])):
        pw[name] = jnp.einsum('cjdh,ce->jcdeh', part, eye).reshape(CMP_STRIDE * 2 * HD, 2 * CMP_HID).astype(BF16)
    pw['cmp_w2'] = jnp.einsum('chd,ce->ched', cmp_w2, eye).reshape(2 * CMP_HID, 2 * HD).astype(BF16)
    pw['cmp_b1'] = cmp_b1.reshape(1, 2 * CMP_HID)
    pw['pe_lo'] = cmp_pe[:CMP_STRIDE].reshape(1, CMP_STRIDE * 2 * HD)
    pw['pe_hi'] = cmp_pe[CMP_STRIDE:].reshape(1, CMP_STRIDE * 2 * HD)
    pw['conv_w'] = conv_w
    pw['conv_b'] = conv_b.reshape(1, ML_W)
    pw['w_ml_q'], pw['w_ml_k'], pw['w_ml_v'] = (w.astype(BF16) for w in (w_ml_q, w_ml_k, w_ml_v))
    pw['w_ml_q_t'], pw['w_ml_k_t'], pw['w_ml_v_t'] = (jnp.swapaxes(w, 1, 2).astype(BF16) for w in (w_ml_q, w_ml_k, w_ml_v))
    pw['ml_gain_t'] = jnp.broadcast_to(ml_norm.reshape(ML_W, 1), (ML_W, LANES))
    gate_bias = jnp.concatenate([b_igate, b_fgate])
    pw['gate_bias_lanes'] = jnp.pad(gate_bias, (0, LANES - 2 * ML_H)).reshape(1, LANES)
    pw['gate_bias_rows'] = gate_bias.reshape(2 * ML_H, 1)
    pw['ml_gain'] = ml_norm.reshape(1, ML_W)
    pw['norm_mem'] = norm_mem.reshape(1, D_MODEL)
    pw['w_mem_kv'] = w_mem_kv.astype(BF16)
    pw['mem_k_gain'] = mem_k_norm.reshape(1, MEM_HD)
    ex = np.zeros((N_BRANCH, LANES, NSA_H * HD), np.float32)
    for br in range(N_BRANCH):
        for gh in range(NSA_H):
            ex[br, _SLAB_G + br * NSA_H + gh, gh * HD:(gh + 1) * HD] = 1.0
    pw['gate_expand'] = jnp.asarray(ex, BF16)
    pw['w_branch'] = w_branch.astype(BF16)
    pw['w_out'] = w_out.astype(BF16)
    pw['norm_ffn'] = norm_ffn.reshape(1, D_MODEL)
    pw['w_ffn_in'] = w_ffn_in.astype(BF16)
    pw['w_ffn_out'] = w_ffn_out.astype(BF16)
    return pw


def _overlap(n_cmp_rows, n_slc_cols):
    ci = np.arange(n_cmp_rows)[:, None] * CMP_STRIDE
    sj = np.arange(n_slc_cols)[None, :] * SLC_BLOCK
    return ((ci < sj + SLC_BLOCK) & (ci + CMP_BLOCK > sj)).astype(np.float32)


def _pick_tile(n, pref):
    t = min(n, pref)
    while n % t:
        t //= 2
    return t


def _prompt_group(x, mem, pw):
    b, s, _ = x.shape
    t = b * s
    x2d = x.reshape(t, D_MODEL)
    tm = _pick_tile(s, TOKEN_TILE)
    tabs = _rope_tables(np.arange(s))
    (kvc, kvs, kvw, u, oraw, slab,
     qrot_t, kvc_t, kvs_t, kvw_t, qn_t, mq_t) = _project(x2d, tabs, pw, tm, seq_len=s)
    kvcmp, kvcmp_t = _compress_prompt(kvc.reshape(b, s, KV_W), pw)
    r = s // CMP_STRIDE
    ovt = jnp.asarray(_overlap(r, HD).T, BF16)
    ocmp, sel_t = _cmp_prompt(qn_t, kvcmp, kvcmp_t, ovt, b, s, _pick_tile(s, CMP_TQ))
    fq, fk = _pick_tile(s, FLASH_TQ), _pick_tile(s, FLASH_TK)
    oslc = _flash_prompt(qrot_t, kvs.reshape(b, s, KV_W), kvs_t, sel_t, b, s, _pick_tile(s, SLC_TQ), fk)
    owin = _flash_prompt(qrot_t, kvw.reshape(b, s, KV_W), kvw_t, None, b, s, fq, fk)
    chunk = _pick_tile(s, MLSTM_CHUNK)
    oml, c_t, n_new, m_new = _mlstm_prompt(u.reshape(b, s, ML_W), oraw.reshape(b, s, ML_W),
                                             slab.reshape(b, s, LANES), pw, chunk)
    t_mem = mem.shape[1]
    kvm, kvm_t = _mem_kv(mem, pw)
    omem = _mem_prompt(mq_t, kvm, kvm_t, b, s, _pick_tile(s, MEM_TQ))
    x1 = _merge(x2d, slab, ocmp, oslc, owin, oml.reshape(t, ML_W), omem, pw, tm)
    y = _ffn(x1, pw, tm).reshape(b, s, D_MODEL)
    kv5 = lambda a_t: jnp.transpose(a_t.reshape(b, NSA_G, 2, HD, a_t.shape[-1]), (0, 4, 1, 2, 3))
    return dict(y=y, kv_cmp=kv5(kvc_t), kv_slc=kv5(kvs_t), kv_win_t=kvw_t, kv5=kv5,
                kv_mem=kvm.reshape(b, t_mem, MEM_H, 2, MEM_HD), c=jnp.swapaxes(c_t, 2, 3), n=n_new,
                m=m_new[:, 0, :ML_H],
                u=u.reshape(b, s, ML_W))


def _sample_group(x, cache_cmp, cache_slc, cache_win, cache_mem, state_c, state_n, state_m, state_conv,
                  page_table, pw):
    db, ds, _ = x.shape
    assert ds == 1, "the sample kernels handle one new token per sequence"
    n_pages = page_table.shape[1]
    past = n_pages * PAGE_SIZE
    pos = past
    x2d = x.reshape(db, D_MODEL)
    tabs = _rope_tables(np.full((db,), pos))
    kvc, kvs, kvw, u, oraw, slab, qn, qrot, mq = _project(x2d, tabs, pw, db)
    kvcmp = _compress_paged(cache_cmp, page_table, kvc.reshape(db, 1, KV_W), pw)
    n_cmp = kvcmp.shape[2]
    n_slc = -(-(past + ds) // SLC_BLOCK)
    nslp = -(-n_slc // LANES) * LANES
    ov = jnp.asarray(_overlap(n_cmp, nslp), BF16)
    ocmp, sel = _cmp_sample(qn.reshape(db, NSA_H, HD), kvcmp, ov, pos, n_slc)
    q3 = qrot.reshape(db, NSA_G, NSA_HPG, 1, HD)
    qaug = (q3 * jnp.eye(NSA_G, dtype=BF16)[None, :, None, :, None]).reshape(db, NSA_H, NSA_G, HD)
    qaug = jnp.concatenate([qaug, jnp.zeros_like(qaug)], axis=-1).reshape(db, NSA_H, KV_W)
    slot_pages, slot_bias = _selected_pages(sel, page_table)
    oslc = _slc_sample(qaug, slot_bias, cache_slc, slot_pages, kvs.reshape(db, 1, KV_W))
    w_buf = cache_win.shape[1]
    owin, win_new_t = _win_sample(qaug, _feature_major(cache_win), kvw.reshape(db, 1, KV_W), pos, past - w_buf)
    win_new = jnp.transpose(win_new_t.reshape(db, NSA_G, 2, HD, w_buf), (0, 4, 1, 2, 3))
    ext = jnp.concatenate([state_conv, u.reshape(db, 1, ML_W)], axis=1)
    oml, c_new, n_new, m_new = _mlstm_sample(ext, oraw.reshape(db, 1, ML_W), slab.reshape(db, 1, LANES),
                                             state_c, state_n, state_m, pw)
    q8 = jnp.pad(mq.reshape(db, MEM_H, MEM_HD), ((0, 0), (0, 8 - MEM_H), (0, 0)))
    omem = _mem_sample(q8, cache_mem)[:, :MEM_H].reshape(db, MEM_W)
    x1 = _merge(x2d, slab, ocmp.reshape(db, NSA_H * HD), oslc.reshape(db, NSA_H * HD), owin.reshape(db, NSA_H * HD),
                oml.reshape(db, ML_W), omem, pw, db)
    y = _ffn(x1, pw, db).reshape(db, ds, D_MODEL)
    kv5 = lambda a: a.reshape(db, ds, NSA_G, 2, HD)
    return dict(y=y, kv_cmp=kv5(kvc), kv_slc=kv5(kvs), win=win_new,
                c=c_new, n=n_new, m=m_new[:, 0, :ML_H], conv=ext[:, 1:])


def kernel(x_prompt, x_sample, cache_kv_cmp, cache_kv_slc, cache_kv_win, cache_kv_mem, state_C, state_n, state_m, state_conv, page_table, mem_prompt, norm_mix, w_in, q_norm, k_norm_cmp, k_norm_slc, k_norm_win, cmp_pe, cmp_w1, cmp_b1, cmp_w2, conv_w, conv_b, w_ml_q, w_ml_k, w_ml_v, b_igate, b_fgate, ml_norm, norm_mem, w_mem_kv, mem_q_norm, mem_k_norm, w_branch, w_out, norm_ffn, w_ffn_in, w_ffn_out):
    pw = _prepare_weights(norm_mix, w_in, q_norm, k_norm_cmp, k_norm_slc, k_norm_win, cmp_pe, cmp_w1, cmp_b1, cmp_w2,
                          conv_w, conv_b, w_ml_q, w_ml_k, w_ml_v, b_igate, b_fgate, ml_norm, norm_mem, w_mem_kv,
                          mem_q_norm, mem_k_norm, w_branch, w_out, norm_ffn, w_ffn_in, w_ffn_out)
    p = _prompt_group(x_prompt, mem_prompt, pw)
    s = _sample_group(x_sample, cache_kv_cmp, cache_kv_slc, cache_kv_win, cache_kv_mem, state_C, state_n, state_m,
                      state_conv, page_table, pw)
    b, seq = x_prompt.shape[:2]
    w_buf = cache_kv_win.shape[1]
    assert seq >= w_buf
    win_p = p['kv5'](p['kv_win_t'][:, :, seq - w_buf:])
    conv_p = p['u'][:, seq - (CONV_W - 1):]
    return (p['y'], s['y'], p['kv_cmp'], s['kv_cmp'], p['kv_slc'], s['kv_slc'], win_p, s['win'], p['kv_mem'],
            p['c'], s['c'], p['n'], s['n'], p['m'], s['m'], conv_p, s['conv'])
```
